```python
import jax, jax.numpy as jnp
from jax import lax
import numpy as np

D_MODEL = 2048
BATCH = 2
SEQ = 4096
DEPTH = 1
DEC_BATCH = 8
DEC_SEQ = 4
PAST_LEN = 16384
PAGE_SIZE = 128

A_HEADS = 8
A_KV_HEADS = 4
A_HEAD_DIM = 128
A_WIDTH = A_HEADS * A_HEAD_DIM
A_KV_WIDTH = A_KV_HEADS * A_HEAD_DIM
IDX_HEADS = 16
IDX_DIM = 64
TOPK_MAX = 256
Q_BLOCK = 128
ROPE_THETA = 10000.0
CHUNK = 128
B_GROUPS = 8
B_GROUP_DIM = 128
B_WIDTH = B_GROUPS * B_GROUP_DIM
N_MEM = 256
M_HEADS = 4
M_HEAD_DIM = 256
M_WIDTH = M_HEADS * M_HEAD_DIM
EPS = 1e-6

SPLIT_SIZES = (A_WIDTH, A_KV_WIDTH, A_KV_WIDTH, IDX_HEADS * IDX_DIM, IDX_DIM, IDX_HEADS, A_WIDTH,
               B_WIDTH, B_WIDTH, B_WIDTH,
               M_WIDTH, M_WIDTH,
               D_MODEL, D_MODEL, D_MODEL)
D_IN = sum(SPLIT_SIZES)

kernel_name = "hybrid_dsa_chunkmlp_memxattn_step"


def _rmsnorm(x, g):
    x32 = x.astype(jnp.float32)
    y = x32 * lax.rsqrt(jnp.mean(x32 * x32, axis=-1, keepdims=True) + EPS)
    return y.astype(x.dtype) * g


def _rope(x, pos):
    half = x.shape[-1] // 2
    freq = ROPE_THETA ** (-jnp.arange(half, dtype=jnp.float32) / half)
    ang = pos.astype(jnp.float32)[:, None] * freq[None, :]
    cos = jnp.cos(ang)[None, :, None, :]
    sin = jnp.sin(ang)[None, :, None, :]
    x32 = x.astype(jnp.float32)
    x1, x2 = x32[..., :half], x32[..., half:]
    return jnp.concatenate([x1 * cos - x2 * sin, x2 * cos + x1 * sin], axis=-1).astype(x.dtype)


def _project(x, g_pre, w_in):
    z = _rmsnorm(x, g_pre) @ w_in
    return jnp.split(z, list(np.cumsum(SPLIT_SIZES)[:-1]), axis=-1)


def _prep_a(q, k, v, qi, ki, wi, pos, g_q, g_k):
    Bn, T = q.shape[:2]
    q = _rope(_rmsnorm(q.reshape(Bn, T, A_HEADS, A_HEAD_DIM), g_q), pos)
    k = _rope(_rmsnorm(k.reshape(Bn, T, A_KV_HEADS, A_HEAD_DIM), g_k), pos)
    v = v.reshape(Bn, T, A_KV_HEADS, A_HEAD_DIM)
    qi = _rope(qi.reshape(Bn, T, IDX_HEADS, IDX_DIM), pos)
    ki = _rope(ki[:, :, None, :], pos)[:, :, 0, :]
    wi = wi * (IDX_HEADS ** -0.5)
    return q, k, v, qi, ki, wi


def _index_scores(qi, ki, wi):
    dots = jnp.einsum('bthd,bsd->bths', qi, ki).astype(jnp.float32) * (IDX_DIM ** -0.5)
    return jnp.einsum('bths,bth->bts', jax.nn.relu(dots), wi.astype(jnp.float32))


def _sparse_attend(q, kg, vg, valid):
    Bn, T = q.shape[:2]
    qg = q.reshape(Bn, T, A_KV_HEADS, A_HEADS // A_KV_HEADS, A_HEAD_DIM)
    s = jnp.einsum('btgrd,btkgd->btgrk', qg, kg).astype(jnp.float32) * (A_HEAD_DIM ** -0.5)
    s = jnp.where(valid[:, :, None, None, :], s, -jnp.inf)
    p = jax.nn.softmax(s, axis=-1).astype(vg.dtype)
    o = jnp.einsum('btgrk,btkgd->btgrd', p, vg)
    return o.reshape(Bn, T, A_WIDTH)


_gather_rows = jax.vmap(lambda rows, idx: rows[idx])


def _dsa_prompt(q, k, v, qi, ki, wi, topk):
    Bn, T = q.shape[:2]
    key_pos = jnp.arange(T)

    def block(i):
        start = i * Q_BLOCK
        qb = lax.dynamic_slice_in_dim(q, start, Q_BLOCK, axis=1)
        qib = lax.dynamic_slice_in_dim(qi, start, Q_BLOCK, axis=1)
        wib = lax.dynamic_slice_in_dim(wi, start, Q_BLOCK, axis=1)
        pos = start + jnp.arange(Q_BLOCK)
        sc = _index_scores(qib, ki, wib)
        sc = jnp.where((key_pos[None, :] <= pos[:, None])[None], sc, -jnp.inf)
        _, idx = lax.top_k(sc, topk)
        valid = idx <= pos[None, :, None]
        return _sparse_attend(qb, _gather_rows(k, idx), _gather_rows(v, idx), valid)

    out = lax.map(block, jnp.arange(T // Q_BLOCK))
    return out.transpose(1, 0, 2, 3).reshape(Bn, T, A_WIDTH)


def _dsa_sample(q, k_new, v_new, qi, ki_new, wi, cache_k, cache_v, cache_kidx, page_table, topk):
    Bn, T = q.shape[:2]
    n_past = page_table.shape[1] * PAGE_SIZE
    ki_past = cache_kidx[page_table].reshape(Bn, n_past, IDX_DIM)
    ki_all = jnp.concatenate([ki_past, ki_new], axis=1)
    L = n_past + T
    pos = n_past + jnp.arange(T)
    sc = _index_scores(qi, ki_all, wi)
    sc = jnp.where((jnp.arange(L)[None, :] <= pos[:, None])[None], sc, -jnp.inf)
    _, idx = lax.top_k(sc, topk)
    valid = idx <= pos[None, :, None]
    in_past = idx < n_past
    pidx = jnp.minimum(idx, n_past - 1)
    phys = _gather_rows(page_table, pidx // PAGE_SIZE)
    off = pidx % PAGE_SIZE
    nidx = jnp.clip(idx - n_past, 0, T - 1)
    sel = in_past[..., None, None]
    kg = jnp.where(sel, cache_k[phys, off], _gather_rows(k_new, nidx))
    vg = jnp.where(sel, cache_v[phys, off], _gather_rows(v_new, nidx))
    return _sparse_attend(q, kg, vg, valid)


def _chunk_mix(u, v, g_sgu, w_s, b_s):
    Bn, T, _ = v.shape
    vn = _rmsnorm(v, g_sgu).reshape(Bn, T, B_GROUPS, B_GROUP_DIM)
    n_chunks = -(-T // CHUNK)
    vp = jnp.pad(vn, ((0, 0), (0, n_chunks * CHUNK - T), (0, 0), (0, 0)))
    vp = vp.reshape(Bn, n_chunks, CHUNK, B_GROUPS, B_GROUP_DIM)
    tril = jnp.tril(jnp.ones((CHUNK, CHUNK), dtype=bool))
    ws = jnp.where(tril[None], w_s, 0)
    s = jnp.einsum('gpq,bnqgc->bnpgc', ws, vp) + b_s.T[None, None, :, :, None]
    s = s.reshape(Bn, n_chunks * CHUNK, B_WIDTH)[:, :T]
    return u * s, vn


def _mem_kv(mem, g_mem, w_mem_kv, g_mk):
    Bn, M, _ = mem.shape
    k, v = jnp.split(_rmsnorm(mem, g_mem) @ w_mem_kv, 2, axis=-1)
    k = _rmsnorm(k.reshape(Bn, M, M_HEADS, M_HEAD_DIM), g_mk)
    return k, v.reshape(Bn, M, M_HEADS, M_HEAD_DIM)


def _mem_attend(q, g_mq, mk, mv):
    Bn, T = q.shape[:2]
    q = _rmsnorm(q.reshape(Bn, T, M_HEADS, M_HEAD_DIM), g_mq)
    s = jnp.einsum('bthd,bmhd->bhtm', q, mk).astype(jnp.float32) * (M_HEAD_DIM ** -0.5)
    p = jax.nn.softmax(s, axis=-1).astype(mv.dtype)
    return jnp.einsum('bhtm,bmhd->bthd', p, mv).reshape(Bn, T, M_WIDTH)


def _merge(x, oa, ga, ob, gb, oc, gc, ra, rb, rc, w_pa, w_pb, w_pc, w_out):
    m = (jax.nn.sigmoid(ra) * ((oa * jax.nn.silu(ga)) @ w_pa)
         + jax.nn.sigmoid(rb) * ((ob * jax.nn.silu(gb)) @ w_pb)
         + jax.nn.sigmoid(rc) * ((oc * jax.nn.silu(gc)) @ w_pc))
    return x + m @ w_out


def setup_inputs(seed: int = 0) -> dict:
    key = jax.random.key(seed)
    ks = jax.random.split(key, 26)
    n_pages = PAST_LEN // PAGE_SIZE
    n_used = DEC_BATCH * n_pages
    n_pool = n_used + max(1, n_used // 4)
    nrm = lambda k, shape, s=1.0: jax.random.normal(k, shape, jnp.float32) * s
    gain = lambda k, n: 1.0 + 0.01 * jax.random.normal(k, (n,), jnp.float32)
    page_table = jax.random.permutation(ks[0], n_pool)[:n_used].reshape(DEC_BATCH, n_pages).astype(jnp.int32)
    return {
        "x_prompt": nrm(ks[1], (BATCH, SEQ, D_MODEL)),
        "x_sample": nrm(ks[2], (DEC_BATCH, DEC_SEQ, D_MODEL)),
        "cache_k": nrm(ks[3], (n_pool, PAGE_SIZE, A_KV_HEADS, A_HEAD_DIM)),
        "cache_v": nrm(ks[4], (n_pool, PAGE_SIZE, A_KV_HEADS, A_HEAD_DIM)),
        "cache_kidx": nrm(ks[5], (n_pool, PAGE_SIZE, IDX_DIM)),
        "cache_mem_k": nrm(ks[6], (DEC_BATCH, N_MEM, M_HEADS, M_HEAD_DIM)),
        "cache_mem_v": nrm(ks[7], (DEC_BATCH, N_MEM, M_HEADS, M_HEAD_DIM)),
        "page_table": page_table,
        "mem_prompt": nrm(ks[8], (BATCH, N_MEM, D_MODEL)),
        "g_pre": gain(ks[9], D_MODEL),
        "w_in": nrm(ks[10], (D_MODEL, D_IN), D_MODEL ** -0.5),
        "g_q": gain(ks[11], A_HEAD_DIM),
        "g_k": gain(ks[12], A_HEAD_DIM),
        "g_mq": gain(ks[13], M_HEAD_DIM),
        "g_mk": gain(ks[14], M_HEAD_DIM),
        "g_mem": gain(ks[15], D_MODEL),
        "w_mem_kv": nrm(ks[16], (D_MODEL, 2 * M_WIDTH), D_MODEL ** -0.5),
        "g_sgu": gain(ks[17], B_WIDTH),
        "w_s": nrm(ks[18], (B_GROUPS, CHUNK, CHUNK), CHUNK ** -0.5),
        "b_s": 1.0 + 0.02 * jax.random.normal(ks[19], (B_GROUPS, CHUNK), jnp.float32),
        "w_pa": nrm(ks[20], (A_WIDTH, D_MODEL), A_WIDTH ** -0.5),
        "w_pb": nrm(ks[21], (B_WIDTH, D_MODEL), B_WIDTH ** -0.5),
        "w_pc": nrm(ks[22], (M_WIDTH, D_MODEL), M_WIDTH ** -0.5),
        "w_out": nrm(ks[23], (D_MODEL, D_MODEL), D_MODEL ** -0.5),
    }


def reference(x_prompt, x_sample, cache_k, cache_v, cache_kidx, cache_mem_k, cache_mem_v, page_table,
              mem_prompt, g_pre, w_in, g_q, g_k, g_mq, g_mk, g_mem, w_mem_kv, g_sgu, w_s, b_s,
              w_pa, w_pb, w_pc, w_out):
    T = x_prompt.shape[1]
    y_prompt = x_prompt
    for _ in range(DEPTH):
        (aq, ak, av, aqi, aki, awi, ag, bu, bv, bg, cq, cg, ra, rb, rc) = _project(y_prompt, g_pre, w_in)
        q, k_p, v_p, qi, ki_p, wi = _prep_a(aq, ak, av, aqi, aki, awi, jnp.arange(T), g_q, g_k)
        oa = _dsa_prompt(q, k_p, v_p, qi, ki_p, wi, min(TOPK_MAX, T // 4))
        ob, _ = _chunk_mix(bu, bv, g_sgu, w_s, b_s)
        mk_p, mv_p = _mem_kv(mem_prompt, g_mem, w_mem_kv, g_mk)
        oc = _mem_attend(cq, g_mq, mk_p, mv_p)
        y_prompt = _merge(y_prompt, oa, ag, ob, bg, oc, cg, ra, rb, rc, w_pa, w_pb, w_pc, w_out)

    Ts = x_sample.shape[1]
    n_past = page_table.shape[1] * PAGE_SIZE
    y_sample = x_sample
    for _ in range(DEPTH):
        (aq, ak, av, aqi, aki, awi, ag, bu, bv, bg, cq, cg, ra, rb, rc) = _project(y_sample, g_pre, w_in)
        q, k_s, v_s, qi, ki_s, wi = _prep_a(aq, ak, av, aqi, aki, awi, n_past + jnp.arange(Ts), g_q, g_k)
        oa = _dsa_sample(q, k_s, v_s, qi, ki_s, wi, cache_k, cache_v, cache_kidx, page_table,
                         min(TOPK_MAX, (n_past + Ts) // 4))
        ob, chunk_v_s = _chunk_mix(bu, bv, g_sgu, w_s, b_s)
        oc = _mem_attend(cq, g_mq, cache_mem_k, cache_mem_v)
        y_sample = _merge(y_sample, oa, ag, ob, bg, oc, cg, ra, rb, rc, w_pa, w_pb, w_pc, w_out)

    return (y_prompt, y_sample, k_p, v_p, ki_p, mk_p, mv_p, k_s, v_s, ki_s, chunk_v_s)
```

```python
import functools

import numpy as np
import jax
import jax.numpy as jnp
from jax import lax
from jax.experimental import pallas as pl
from jax.experimental.pallas import tpu as pltpu

F32 = jnp.float32
BF16 = jnp.bfloat16
I32 = jnp.int32

D_MODEL = 2048
PAGE = 128
A_HEADS = 8
A_KV_HEADS = 4
A_HEAD_DIM = 128
A_WIDTH = A_HEADS * A_HEAD_DIM
A_KV_WIDTH = A_KV_HEADS * A_HEAD_DIM
IDX_HEADS = 16
IDX_DIM = 64
TOPK_MAX = 256
Q_BLOCK = 128
ROPE_THETA = 10000.0
CHUNK = 128
B_GROUPS = 8
B_GROUP_DIM = 128
B_WIDTH = B_GROUPS * B_GROUP_DIM
N_MEM = 256
M_HEADS = 4
M_HEAD_DIM = 256
M_WIDTH = M_HEADS * M_HEAD_DIM
EPS = 1e-6

OFF_K = A_WIDTH
OFF_V = OFF_K + A_KV_WIDTH
OFF_QI = OFF_V + A_KV_WIDTH
OFF_KI = OFF_QI + IDX_HEADS * IDX_DIM
OFF_WI = OFF_KI + IDX_DIM
OFF_REST = OFF_WI + IDX_HEADS
A_COLS = 3200
REST_COLS = A_WIDTH + 3 * B_WIDTH + 2 * M_WIDTH + 3 * D_MODEL

S_PAD = 16
INT_MIN = np.int32(-2 ** 31)
INT_MAX = np.int32(2 ** 31 - 1)
NEG_INF = float("-inf")

V7X_VMEM_LIMIT = 56 * 1024 * 1024


def _cparams(sem):
    return pltpu.CompilerParams(dimension_semantics=sem, vmem_limit_bytes=V7X_VMEM_LIMIT)


def _dot(a, b):
    return jnp.dot(a, b, preferred_element_type=F32)


def _dot_nt(a, b):
    return lax.dot_general(a, b, (((1,), (1,)), ((), ())), preferred_element_type=F32)


def _rms(x, g):
    return x * lax.rsqrt(jnp.mean(x * x, axis=-1, keepdims=True) + EPS) * g


def _const_spec(shape):
    nd = len(shape)
    return pl.BlockSpec(shape, lambda *_: (0,) * nd, pipeline_mode=pl.Buffered(1))


def _proj_a_kernel(x_ref, g_ref, w_ref, gq_ref, gk_ref, cq_ref, sq_ref, ci_ref, sia_ref, sib_ref,
                   q_ref, k_ref, v_ref, kb_ref, vb_ref, qi_ref, ki_ref, kib_ref, wi_ref):
    h = _rms(x_ref[...], g_ref[...]).astype(BF16)
    z = _dot(h, w_ref[...])
    cq, sq = cq_ref[...], sq_ref[...]
    ci, sia, sib = ci_ref[...], sia_ref[...], sib_ref[...]

    def norm_rope(zz, g):
        n = _rms(zz, g)
        return n * cq + pltpu.roll(n, A_HEAD_DIM // 2, 1) * sq

    def rope_idx(zz):
        return zz * ci + pltpu.roll(zz, 96, 1) * sia + pltpu.roll(zz, 32, 1) * sib

    for hh in range(A_HEADS):
        sl = slice(hh * A_HEAD_DIM, (hh + 1) * A_HEAD_DIM)
        q_ref[:, sl] = (norm_rope(z[:, sl], gq_ref[...]) * (A_HEAD_DIM ** -0.5)).astype(BF16)
    for hh in range(A_KV_HEADS):
        sl = slice(hh * A_HEAD_DIM, (hh + 1) * A_HEAD_DIM)
        kh = norm_rope(z[:, OFF_K + hh * A_HEAD_DIM:OFF_K + (hh + 1) * A_HEAD_DIM], gk_ref[...])
        k_ref[:, sl] = kh
        kb_ref[:, sl] = kh.astype(BF16)
    v = z[:, OFF_V:OFF_V + A_KV_WIDTH]
    v_ref[...] = v
    vb_ref[...] = v.astype(BF16)
    for t in range(IDX_HEADS * IDX_DIM // 128):
        sl = slice(t * 128, (t + 1) * 128)
        qi_ref[:, sl] = rope_idx(z[:, OFF_QI + t * 128:OFF_QI + (t + 1) * 128]).astype(BF16)
    last = z[:, OFF_KI:OFF_KI + 128]
    ki = rope_idx(last)[:, :IDX_DIM]
    ki_ref[...] = ki
    kib_ref[...] = ki.astype(BF16)
    wi_ref[...] = last[:, IDX_DIM:IDX_DIM + IDX_HEADS] * ((IDX_HEADS ** -0.5) * (IDX_DIM ** -0.5))


def _proj_a(x, g_pre, w_a, g_q, g_k, tabs, tm, tab_map):
    n = x.shape[0]
    row = lambda w: pl.BlockSpec((tm, w), lambda i: (i, 0))
    tab = pl.BlockSpec((tm, 128), tab_map)
    outs = [(A_WIDTH, BF16), (A_KV_WIDTH, F32), (A_KV_WIDTH, F32), (A_KV_WIDTH, BF16), (A_KV_WIDTH, BF16),
            (IDX_HEADS * IDX_DIM, BF16), (IDX_DIM, F32), (IDX_DIM, BF16), (IDX_HEADS, F32)]
    return pl.pallas_call(
        _proj_a_kernel,
        grid=(n // tm,),
        in_specs=[row(D_MODEL), _const_spec((1, D_MODEL)), _const_spec((D_MODEL, A_COLS)),
                  _const_spec((1, A_HEAD_DIM)), _const_spec((1, A_HEAD_DIM)), tab, tab, tab, tab, tab],
        out_specs=[row(w) for w, _ in outs],
        out_shape=[jax.ShapeDtypeStruct((n, w), dt) for w, dt in outs],
        compiler_params=_cparams(("arbitrary",)),
        name="proj_a",
    )(x, g_pre, w_a, g_q, g_k, *tabs)


def _proj_rest_kernel(x_ref, g_ref, w_ref, o_ref, h_ref):
    @pl.when(pl.program_id(1) == 0)
    def _():
        h_ref[...] = _rms(x_ref[...], g_ref[...]).astype(BF16)

    o_ref[...] = _dot(h_ref[...], w_ref[...]).astype(BF16)


def _proj_rest(x, g_pre, w_r, tm, tn):
    n = x.shape[0]
    return pl.pallas_call(
        _proj_rest_kernel,
        grid=(n // tm, REST_COLS // tn),
        in_specs=[pl.BlockSpec((tm, D_MODEL), lambda i, j: (i, 0)),
                  pl.BlockSpec((1, D_MODEL), lambda i, j: (0, 0)),
                  pl.BlockSpec((D_MODEL, tn), lambda i, j: (0, j))],
        out_specs=pl.BlockSpec((tm, tn), lambda i, j: (i, j)),
        out_shape=jax.ShapeDtypeStruct((n, REST_COLS), BF16),
        scratch_shapes=[pltpu.VMEM((tm, D_MODEL), BF16)],
        compiler_params=_cparams(("arbitrary", "arbitrary")),
        name="proj_rest",
    )(x, g_pre, w_r)


def _mem_kv_kernel(x_ref, g_ref, w_ref, gk_ref, k_ref, v_ref, kb_ref, vb_ref):
    h = _rms(x_ref[...], g_ref[...]).astype(BF16)
    z = _dot(h, w_ref[...])
    for hh in range(M_HEADS):
        sl = slice(hh * M_HEAD_DIM, (hh + 1) * M_HEAD_DIM)
        kh = _rms(z[:, sl], gk_ref[...])
        k_ref[:, sl] = kh
        kb_ref[:, sl] = kh.astype(BF16)
    v = z[:, M_WIDTH:]
    v_ref[...] = v
    vb_ref[...] = v.astype(BF16)


def _mem_kv(mem, g_mem, w_mem, g_mk):
    n = mem.shape[0]
    blk = pl.BlockSpec((N_MEM, M_WIDTH), lambda i: (i, 0))
    return pl.pallas_call(
        _mem_kv_kernel,
        grid=(n // N_MEM,),
        in_specs=[pl.BlockSpec((N_MEM, D_MODEL), lambda i: (i, 0)), _const_spec((1, D_MODEL)),
                  _const_spec((D_MODEL, 2 * M_WIDTH)), _const_spec((1, M_HEAD_DIM))],
        out_specs=[blk, blk, blk, blk],
        out_shape=[jax.ShapeDtypeStruct((n, M_WIDTH), dt) for dt in (F32, F32, BF16, BF16)],
        compiler_params=_cparams(("arbitrary",)),
        name="mem_kv",
    )(mem, g_mem, w_mem, g_mk)


def _order_key(score):
    bits = pltpu.bitcast(score, I32)
    return jnp.where(bits < 0, -(bits & INT_MAX), bits)


def _select_params(count, topk, pos_bits, j_ref):
    kf = float(topk)
    t0 = jnp.where(count(lambda k, p: k >= 0) >= kf, jnp.int32(0), INT_MIN)

    def bit_body(b, t):
        cand = t + lax.shift_left(jnp.int32(1), 30 - b)
        return jnp.where(count(lambda k, p: k >= cand) >= kf, cand, t)

    t = lax.fori_loop(0, 31, bit_body, t0)
    tie = (count(lambda k, p: k >= t) > kf) & (t > INT_MIN)
    j_ref[...] = jnp.full(j_ref.shape, INT_MAX, I32)

    @pl.when(jnp.max(tie.astype(I32)) > 0)
    def _():
        n_gt = count(lambda k, p: k > t)

        def pos_body(b, p_lo):
            cand = p_lo + lax.shift_left(jnp.int32(1), pos_bits - 1 - b)
            n_eq = count(lambda k, p: (k == t) & (p < cand))
            return jnp.where(n_gt + n_eq < kf, cand, p_lo)

        p_sel = lax.fori_loop(0, pos_bits, pos_body, jnp.zeros(t.shape, I32))
        j_ref[...] = jnp.where(tie, p_sel, INT_MAX)

    return t, j_ref[...]


def _dsa_prompt_kernel(qi_ref, wi_ref, q_ref, ki_ref, k_ref, v_ref, o_ref, key_ref, bias_ref, j_ref,
                       *, topk, kc, pos_bits):
    i = pl.program_id(1)
    nck = (i * Q_BLOCK + Q_BLOCK + kc - 1) // kc
    qi = qi_ref[...]
    wi = wi_ref[...]
    row_pos = i * Q_BLOCK + lax.broadcasted_iota(I32, (Q_BLOCK, 1), 0)
    lane = lax.broadcasted_iota(I32, (1, kc), 1)

    def chunk_start(c):
        return pl.multiple_of(c * kc, kc)

    def score_body(c, carry):
        kic = ki_ref[pl.ds(chunk_start(c), kc), :]
        acc = jnp.zeros((Q_BLOCK, kc), F32)
        for j in range(IDX_HEADS):
            d = _dot_nt(qi[:, j * IDX_DIM:(j + 1) * IDX_DIM], kic)
            acc = acc + jnp.maximum(d, 0.0) * wi[:, j:j + 1]
        pos = c * kc + lane
        key_ref[c] = jnp.where(pos <= row_pos, _order_key(acc), INT_MIN)
        return carry

    lax.fori_loop(0, nck, score_body, 0)

    def count(pred):
        def body(c, acc):
            part = jnp.where(pred(key_ref[c], c * kc + lane), 1.0, 0.0)
            for t in range(kc // 128):
                acc = acc + part[:, t * 128:(t + 1) * 128]
            return acc

        acc = lax.fori_loop(0, nck, body, jnp.zeros((Q_BLOCK, 128), F32))
        return jnp.sum(acc, axis=-1, keepdims=True)

    thr, jsel = _select_params(count, topk, pos_bits, j_ref)

    def bias_body(c, carry):
        k = key_ref[c]
        pos = c * kc + lane
        sel = ((k > thr) | ((k == thr) & (pos <= jsel))) & (pos <= row_pos)
        bias_ref[c] = jnp.where(sel, 0.0, NEG_INF)
        return carry

    lax.fori_loop(0, nck, bias_body, 0)

    rep = A_HEADS // A_KV_HEADS
    for g in range(A_KV_HEADS):
        qg = jnp.concatenate(
            [q_ref[:, (g * rep + r) * A_HEAD_DIM:(g * rep + r + 1) * A_HEAD_DIM] for r in range(rep)], axis=0)
        gsl = slice(g * A_HEAD_DIM, (g + 1) * A_HEAD_DIM)

        def att_body(c, carry, qg=qg, gsl=gsl):
            m, l, acc = carry
            rows = pl.ds(chunk_start(c), kc)
            s = _dot_nt(qg, k_ref[rows, gsl])
            b = bias_ref[c]
            s = s + jnp.concatenate([b] * rep, axis=0)
            m_new = jnp.maximum(m, jnp.max(s, axis=-1, keepdims=True))
            m_safe = jnp.where(m_new == NEG_INF, 0.0, m_new)
            alpha = jnp.exp(m - m_safe)
            p = jnp.exp(s - m_safe)
            l = alpha * l + jnp.sum(p, axis=-1, keepdims=True)
            acc = alpha * acc + _dot(p.astype(BF16), v_ref[rows, gsl])
            return m_new, l, acc

        init = (jnp.full((rep * Q_BLOCK, 1), NEG_INF, F32), jnp.zeros((rep * Q_BLOCK, 1), F32),
                jnp.zeros((rep * Q_BLOCK, A_HEAD_DIM), F32))
        _, l, acc = lax.fori_loop(0, nck, att_body, init)
        o = acc / l
        for r in range(rep):
            o_ref[:, (g * rep + r) * A_HEAD_DIM:(g * rep + r + 1) * A_HEAD_DIM] = (
                o[r * Q_BLOCK:(r + 1) * Q_BLOCK].astype(BF16))


def _dsa_prompt(qi, wi, q, ki_b, k_b, v_b, batch, seq, topk):
    nqb = seq // Q_BLOCK
    kc = min(512, seq)
    nchunks = seq // kc
    pos_bits = max(1, int(np.ceil(np.log2(seq))))
    qrow = lambda w: pl.BlockSpec((Q_BLOCK, w), lambda b, i: (b * nqb + i, 0))
    seqblk = lambda w: pl.BlockSpec((seq, w), lambda b, i: (b, 0))
    return pl.pallas_call(
        functools.partial(_dsa_prompt_kernel, topk=topk, kc=kc, pos_bits=pos_bits),
        grid=(batch, nqb),
        in_specs=[qrow(IDX_HEADS * IDX_DIM), qrow(IDX_HEADS), qrow(A_WIDTH),
                  seqblk(IDX_DIM), seqblk(A_KV_WIDTH), seqblk(A_KV_WIDTH)],
        out_specs=qrow(A_WIDTH),
        out_shape=jax.ShapeDtypeStruct((batch * seq, A_WIDTH), BF16),
        scratch_shapes=[pltpu.VMEM((nchunks, Q_BLOCK, kc), I32), pltpu.VMEM((nchunks, Q_BLOCK, kc), F32),
                        pltpu.VMEM((Q_BLOCK, 1), I32)],
        compiler_params=_cparams(("arbitrary", "arbitrary")),
        name="dsa_prompt",
    )(qi, wi, q, ki_b, k_b, v_b)


def _fold_heads(e, nq):
    acc = e[0:8]
    for t in range(1, e.shape[0] // 8):
        acc = acc + e[8 * t:8 * (t + 1)]
    return acc + pltpu.roll(acc, nq, 0)


def _sample_scores_kernel(pt_ref, qp_ref, wc_ref, kin_ref, *refs, pages, nq):
    page_refs, (sc_ref, scn_ref) = refs[:pages], refs[pages:]
    qp = qp_ref[...]
    wc = wc_ref[...]
    kp = jnp.concatenate([r[...] for r in page_refs], axis=0).astype(BF16)
    sc_ref[...] = _fold_heads(jnp.maximum(_dot_nt(qp, kp), 0.0) * wc, nq)

    @pl.when(pl.program_id(1) == 0)
    def _():
        kin = jnp.concatenate([kin_ref[...], jnp.zeros((PAGE - S_PAD, IDX_DIM), BF16)], axis=0)
        scn_ref[...] = _fold_heads(jnp.maximum(_dot_nt(qp, kin), 0.0) * wc, nq)


def _sample_scores(page_flat, qp, wcol, ki_new_b, cache_kidx, nseq, npages, pages, nq):
    steps = npages // pages
    rows = IDX_HEADS * nq
    page_specs = [pl.BlockSpec((None, PAGE, IDX_DIM),
                               lambda b, s, pt, p=p: (pt[b * npages + s * pages + p], 0, 0))
                  for p in range(pages)]
    grid_spec = pltpu.PrefetchScalarGridSpec(
        num_scalar_prefetch=1,
        grid=(nseq, steps),
        in_specs=[pl.BlockSpec((None, rows, IDX_DIM), lambda b, s, pt: (b, 0, 0)),
                  pl.BlockSpec((None, rows, 1), lambda b, s, pt: (b, 0, 0)),
                  pl.BlockSpec((S_PAD, IDX_DIM), lambda b, s, pt: (b, 0))] + page_specs,
        out_specs=[pl.BlockSpec((None, 8, pages * PAGE), lambda b, s, pt: (b, 0, s)),
                   pl.BlockSpec((None, 8, PAGE), lambda b, s, pt: (b, 0, 0))],
    )
    return pl.pallas_call(
        functools.partial(_sample_scores_kernel, pages=pages, nq=nq),
        grid_spec=grid_spec,
        out_shape=[jax.ShapeDtypeStruct((nseq, 8, npages * PAGE), F32),
                   jax.ShapeDtypeStruct((nseq, 8, PAGE), F32)],
        compiler_params=_cparams(("arbitrary", "arbitrary")),
        name="dsa_sample_scores",
    )(page_flat, qp, wcol, ki_new_b, *([cache_kidx] * pages))


def _sample_attn_kernel(pt_ref, sc_ref, scn_ref, qs_ref, kn_ref, vn_ref, *refs,
                        pages, nq, n_past, topk, pos_bits):
    k_pages, v_pages = refs[:pages], refs[pages:2 * pages]
    o_ref = refs[2 * pages]
    keyp_ref, keyn_ref, bias_ref, biasn_ref, j_ref, m_ref, l_ref, acc_ref, kbuf_ref, vbuf_ref = refs[2 * pages + 1:]
    s_idx = pl.program_id(1)
    kw = pages * PAGE
    rows_q = qs_ref.shape[1]

    @pl.when(s_idx == 0)
    def _():
        qrow = lax.broadcasted_iota(I32, (8, 1), 0) % nq
        lane_p = lax.broadcasted_iota(I32, (1, n_past), 1)
        lane_n = lax.broadcasted_iota(I32, (1, PAGE), 1)
        keyp_ref[...] = _order_key(sc_ref[...])
        keyn_ref[...] = jnp.where(lane_n <= qrow, _order_key(scn_ref[...]), INT_MIN)

        def count(pred):
            cp = jnp.sum(jnp.where(pred(keyp_ref[...], lane_p), 1.0, 0.0), axis=-1, keepdims=True)
            cn = jnp.sum(jnp.where(pred(keyn_ref[...], n_past + lane_n), 1.0, 0.0), axis=-1, keepdims=True)
            return cp + cn

        thr, jsel = _select_params(count, topk, pos_bits, j_ref)
        kp = keyp_ref[...]
        bias_p = jnp.where((kp > thr) | ((kp == thr) & (lane_p <= jsel)), 0.0, NEG_INF)
        for st in range(n_past // kw):
            bias_ref[st] = bias_p[:, st * kw:(st + 1) * kw]
        kn = keyn_ref[...]
        seln = ((kn > thr) | ((kn == thr) & (n_past + lane_n <= jsel))) & (lane_n <= qrow)
        biasn_ref[...] = jnp.where(seln, 0.0, NEG_INF)
        m_ref[...] = jnp.full(m_ref.shape, NEG_INF, F32)
        l_ref[...] = jnp.zeros(l_ref.shape, F32)
        acc_ref[...] = jnp.zeros(acc_ref.shape, F32)

    for p in range(pages):
        kbuf_ref[p * PAGE:(p + 1) * PAGE, :] = k_pages[p][...].astype(BF16)
        vbuf_ref[p * PAGE:(p + 1) * PAGE, :] = v_pages[p][...].astype(BF16)

    def update(g, s, vv):
        m = m_ref[g]
        m_new = jnp.maximum(m, jnp.max(s, axis=-1, keepdims=True))
        m_safe = jnp.where(m_new == NEG_INF, 0.0, m_new)
        alpha = jnp.exp(m - m_safe)
        p = jnp.exp(s - m_safe)
        l_ref[g] = alpha * l_ref[g] + jnp.sum(p, axis=-1, keepdims=True)
        acc_ref[g] = alpha * acc_ref[g] + _dot(p.astype(BF16), vv)
        m_ref[g] = m_new

    bias = jnp.concatenate([bias_ref[s_idx]] * (rows_q // 8), axis=0)
    for g in range(A_KV_HEADS):
        gsl = slice(g * A_HEAD_DIM, (g + 1) * A_HEAD_DIM)
        update(g, _dot_nt(qs_ref[g], kbuf_ref[:, gsl]) + bias, vbuf_ref[:, gsl])

    @pl.when(s_idx == pl.num_programs(1) - 1)
    def _():
        bn = jnp.concatenate([biasn_ref[:, :S_PAD]] * (rows_q // 8), axis=0)
        for g in range(A_KV_HEADS):
            gsl = slice(g * A_HEAD_DIM, (g + 1) * A_HEAD_DIM)
            update(g, _dot_nt(qs_ref[g], kn_ref[:, gsl]) + bn, vn_ref[:, gsl])
            o_ref[g] = acc_ref[g] / l_ref[g]


def _sample_attn(page_flat, sc, scn, qs, k_new_b, v_new_b, cache_k, cache_v, nseq, npages, pages, nq, topk):
    steps = npages // pages
    n_past = npages * PAGE
    kw = pages * PAGE
    rows_q = qs.shape[2]
    pos_bits = int(np.ceil(np.log2(n_past + PAGE)))
    page_spec = lambda p: pl.BlockSpec((None, PAGE, A_KV_WIDTH),
                                       lambda b, s, pt, p=p: (pt[b * npages + s * pages + p], 0, 0))
    grid_spec = pltpu.PrefetchScalarGridSpec(
        num_scalar_prefetch=1,
        grid=(nseq, steps),
        in_specs=[pl.BlockSpec((None, 8, n_past), lambda b, s, pt: (b, 0, 0)),
                  pl.BlockSpec((None, 8, PAGE), lambda b, s, pt: (b, 0, 0)),
                  pl.BlockSpec((None, A_KV_HEADS, rows_q, A_HEAD_DIM), lambda b, s, pt: (b, 0, 0, 0)),
                  pl.BlockSpec((S_PAD, A_KV_WIDTH), lambda b, s, pt: (b, 0)),
                  pl.BlockSpec((S_PAD, A_KV_WIDTH), lambda b, s, pt: (b, 0))]
                 + [page_spec(p) for p in range(pages)] + [page_spec(p) for p in range(pages)],
        out_specs=pl.BlockSpec((None, A_KV_HEADS, rows_q, A_HEAD_DIM), lambda b, s, pt: (b, 0, 0, 0)),
        scratch_shapes=[pltpu.VMEM((8, n_past), I32), pltpu.VMEM((8, PAGE), I32),
                        pltpu.VMEM((steps, 8, kw), F32), pltpu.VMEM((8, PAGE), F32),
                        pltpu.VMEM((8, 1), I32),
                        pltpu.VMEM((A_KV_HEADS, rows_q, 1), F32), pltpu.VMEM((A_KV_HEADS, rows_q, 1), F32),
                        pltpu.VMEM((A_KV_HEADS, rows_q, A_HEAD_DIM), F32),
                        pltpu.VMEM((kw, A_KV_WIDTH), BF16), pltpu.VMEM((kw, A_KV_WIDTH), BF16)],
    )
    return pl.pallas_call(
        functools.partial(_sample_attn_kernel, pages=pages, nq=nq, n_past=n_past, topk=topk, pos_bits=pos_bits),
        grid_spec=grid_spec,
        out_shape=jax.ShapeDtypeStruct((nseq, A_KV_HEADS, rows_q, A_HEAD_DIM), F32),
        compiler_params=_cparams(("arbitrary", "arbitrary")),
        name="dsa_sample_attn",
    )(page_flat, sc, scn, qs, k_new_b, v_new_b, *([cache_k] * pages), *([cache_v] * pages))


def _merge_kernel(x_ref, oa_ref, ag_ref, bu_ref, bv_ref, bg_ref, cq_ref, cg_ref, ra_ref, rb_ref, rc_ref,
                  mk_ref, mv_ref, ws_ref, bs_ref, gsgu_ref, gmq_ref, wpa_ref, wpb_ref, wpc_ref, wout_ref,
                  y_ref, *maybe_vn_ref, tm, chunk):
    f32 = lambda r: r[...].astype(F32)
    silu = lambda t: t * jax.nn.sigmoid(t)

    vn = _rms(f32(bv_ref), gsgu_ref[...])
    if maybe_vn_ref:
        maybe_vn_ref[0][...] = vn
    vnb = vn.astype(BF16)
    bu = f32(bu_ref)
    tril = (lax.broadcasted_iota(I32, (chunk, chunk), 1) <= lax.broadcasted_iota(I32, (chunk, chunk), 0))
    ob_cols = []
    for g in range(B_GROUPS):
        wg = jnp.where(tril, ws_ref[g], 0.0).astype(BF16)
        gsl = slice(g * B_GROUP_DIM, (g + 1) * B_GROUP_DIM)
        parts = [_dot(wg, vnb[c * chunk:(c + 1) * chunk, gsl]) + bs_ref[:, g:g + 1] for c in range(tm // chunk)]
        ob_cols.append(parts[0] if len(parts) == 1 else jnp.concatenate(parts, axis=0))
    ob = bu * jnp.concatenate(ob_cols, axis=1)
    pb = _dot((ob * silu(f32(bg_ref))).astype(BF16), wpb_ref[...])

    cq = f32(cq_ref)
    oc_cols = []
    for hh in range(M_HEADS):
        hsl = slice(hh * M_HEAD_DIM, (hh + 1) * M_HEAD_DIM)
        qn = (_rms(cq[:, hsl], gmq_ref[...]) * (M_HEAD_DIM ** -0.5)).astype(BF16)
        s = _dot_nt(qn, mk_ref[:, hsl])
        p = jnp.exp(s - jnp.max(s, axis=-1, keepdims=True))
        oc_cols.append(_dot(p.astype(BF16), mv_ref[:, hsl]) / jnp.sum(p, axis=-1, keepdims=True))
    oc = jnp.concatenate(oc_cols, axis=1)
    pc = _dot((oc * silu(f32(cg_ref))).astype(BF16), wpc_ref[...])

    pa = _dot((f32(oa_ref) * silu(f32(ag_ref))).astype(BF16), wpa_ref[...])
    sig = jax.nn.sigmoid
    m = sig(f32(ra_ref)) * pa + sig(f32(rb_ref)) * pb + sig(f32(rc_ref)) * pc
    y_ref[...] = x_ref[...] + _dot(m.astype(BF16), wout_ref[...])


def _merge(x, oa, zr, mk_b, mv_b, ws, bs_t, g_sgu, g_mq, w_pa, w_pb, w_pc, w_out, tm, chunk, mem_map, emit_vn):
    n = x.shape[0]
    col = lambda w, j: pl.BlockSpec((tm, w), lambda i, j=j: (i, j))
    mem = pl.BlockSpec((N_MEM, M_WIDTH), mem_map)
    in_specs = [col(D_MODEL, 0), col(A_WIDTH, 0),
                col(1024, 0), col(1024, 1), col(1024, 2), col(1024, 3), col(1024, 4), col(1024, 5),
                col(2048, 3), col(2048, 4), col(2048, 5),
                mem, mem,
                _const_spec((B_GROUPS, chunk, chunk)), _const_spec((chunk, B_GROUPS)),
                _const_spec((1, B_WIDTH)), _const_spec((1, M_HEAD_DIM)),
                _const_spec((A_WIDTH, D_MODEL)), _const_spec((B_WIDTH, D_MODEL)),
                _const_spec((M_WIDTH, D_MODEL)), _const_spec((D_MODEL, D_MODEL))]
    out_specs = [col(D_MODEL, 0)]
    out_shape = [jax.ShapeDtypeStruct((n, D_MODEL), F32)]
    if emit_vn:
        out_specs.append(col(B_WIDTH, 0))
        out_shape.append(jax.ShapeDtypeStruct((n, B_WIDTH), F32))
    return pl.pallas_call(
        functools.partial(_merge_kernel, tm=tm, chunk=chunk),
        grid=(n // tm,),
        in_specs=in_specs,
        out_specs=out_specs,
        out_shape=out_shape,
        compiler_params=_cparams(("arbitrary",)),
        name="merge",
    )(x, oa, *([zr] * 9), mk_b, mv_b, ws, bs_t, g_sgu, g_mq, w_pa, w_pb, w_pc, w_out)


def _rope_tables(pos):
    pos = pos.astype(F32)[:, None]

    def cs(half):
        freq = ROPE_THETA ** (-jnp.arange(half, dtype=F32) / half)
        ang = pos * freq[None, :]
        return jnp.cos(ang), jnp.sin(ang)

    c, s = cs(A_HEAD_DIM // 2)
    ci, si = cs(IDX_DIM // 2)
    z = jnp.zeros_like(si)
    return (jnp.concatenate([c, c], axis=1), jnp.concatenate([-s, s], axis=1),
            jnp.concatenate([ci] * 4, axis=1), jnp.concatenate([-si, z, -si, z], axis=1),
            jnp.concatenate([z, si, z, si], axis=1))


def kernel(x_prompt, x_sample, cache_k, cache_v, cache_kidx, cache_mem_k, cache_mem_v, page_table,
           mem_prompt, g_pre, w_in, g_q, g_k, g_mq, g_mk, g_mem, w_mem_kv, g_sgu, w_s, b_s,
           w_pa, w_pb, w_pc, w_out):
    batch, seq, _ = x_prompt.shape
    nseq, nq, _ = x_sample.shape
    npages = page_table.shape[1]
    n_past = npages * PAGE
    n_pool = cache_k.shape[0]
    row2 = lambda a: a.reshape(1, -1)

    w_a = w_in[:, :A_COLS].astype(BF16)
    w_r = w_in[:, OFF_REST:].astype(BF16)
    w_pa_b, w_pb_b, w_pc_b, w_out_b = (w.astype(BF16) for w in (w_pa, w_pb, w_pc, w_out))
    w_mem_b = w_mem_kv.astype(BF16)
    g_pre2, g_q2, g_k2, g_mq2, g_mk2, g_mem2, g_sgu2 = map(row2, (g_pre, g_q, g_k, g_mq, g_mk, g_mem, g_sgu))

    xp = x_prompt.reshape(batch * seq, D_MODEL)
    tm_a = min(256, seq)
    nblk = seq // tm_a
    tabs_p = _rope_tables(jnp.arange(seq))
    q, k_p, v_p, k_b, v_b, qi, ki_p, ki_b, wi = _proj_a(
        xp, g_pre2, w_a, g_q2, g_k2, tabs_p, tm_a, lambda i: (i % nblk, 0))
    zr = _proj_rest(xp, g_pre2, w_r, min(1024, seq), 512)
    mk_p, mv_p, mk_b, mv_b = _mem_kv(mem_prompt.reshape(batch * N_MEM, D_MODEL), g_mem2, w_mem_b, g_mk2)
    oa = _dsa_prompt(qi, wi, q, ki_b, k_b, v_b, batch, seq, min(TOPK_MAX, seq // 4))
    tm_m = min(256, seq)
    nblk_m = seq // tm_m
    (y_p,) = _merge(xp, oa, zr, mk_b, mv_b, w_s, b_s.T, g_sgu2, g_mq2, w_pa_b, w_pb_b, w_pc_b, w_out_b,
                    tm_m, CHUNK, lambda i: (i // nblk_m, 0), False)

    xs = jnp.pad(x_sample, ((0, 0), (0, S_PAD - nq), (0, 0))).reshape(nseq * S_PAD, D_MODEL)
    tabs_s = tuple(jnp.tile(t, (nseq, 1)) for t in _rope_tables(n_past + jnp.arange(S_PAD)))
    rows_s = nseq * S_PAD
    q_s, k_s, v_s, k_sb, v_sb, qi_s, ki_s, ki_sb, wi_s = _proj_a(
        xs, g_pre2, w_a, g_q2, g_k2, tabs_s, rows_s, lambda i: (0, 0))
    zr_s = _proj_rest(xs, g_pre2, w_r, rows_s, 512)

    qp = (qi_s.reshape(nseq, S_PAD, IDX_HEADS, IDX_DIM)[:, :nq]
          .transpose(0, 2, 1, 3).reshape(nseq, IDX_HEADS * nq, IDX_DIM))
    wcol = wi_s.reshape(nseq, S_PAD, IDX_HEADS)[:, :nq].transpose(0, 2, 1).reshape(nseq, IDX_HEADS * nq, 1)
    rep = A_HEADS // A_KV_HEADS
    qs = (q_s.reshape(nseq, S_PAD, A_KV_HEADS, rep, A_HEAD_DIM)[:, :nq]
          .transpose(0, 2, 3, 1, 4).reshape(nseq, A_KV_HEADS, rep * nq, A_HEAD_DIM))
    qs = jnp.concatenate([qs, qs], axis=2)
    page_flat = page_table.reshape(-1)
    pages = 8
    sc, scn = _sample_scores(page_flat, qp, wcol, ki_sb, cache_kidx, nseq, npages, pages, nq)
    o_s = _sample_attn(page_flat, sc, scn, qs, k_sb, v_sb,
                       cache_k.reshape(n_pool, PAGE, A_KV_WIDTH), cache_v.reshape(n_pool, PAGE, A_KV_WIDTH),
                       nseq, npages, pages, nq, min(TOPK_MAX, (n_past + nq) // 4))
    oa_s = (o_s[:, :, :rep * nq].reshape(nseq, A_KV_HEADS, rep, nq, A_HEAD_DIM)
            .transpose(0, 3, 1, 2, 4).reshape(nseq, nq, A_WIDTH))
    oa_s = jnp.pad(oa_s, ((0, 0), (0, S_PAD - nq), (0, 0))).reshape(rows_s, A_WIDTH).astype(BF16)
    mk_s = cache_mem_k.reshape(nseq * N_MEM, M_WIDTH).astype(BF16)
    mv_s = cache_mem_v.reshape(nseq * N_MEM, M_WIDTH).astype(BF16)
    y_s, vn_s = _merge(xs, oa_s, zr_s, mk_s, mv_s, w_s[:, :S_PAD, :S_PAD], b_s[:, :S_PAD].T, g_sgu2, g_mq2,
                       w_pa_b, w_pb_b, w_pc_b, w_out_b, S_PAD, S_PAD, lambda i: (i, 0), True)

    take = lambda a, shape: a.reshape(nseq, S_PAD, -1)[:, :nq].reshape(shape)
    return (y_p.reshape(batch, seq, D_MODEL),
            take(y_s, (nseq, nq, D_MODEL)),
            k_p.reshape(batch, seq, A_KV_HEADS, A_HEAD_DIM),
            v_p.reshape(batch, seq, A_KV_HEADS, A_HEAD_DIM),
            ki_p.reshape(batch, seq, IDX_DIM),
            mk_p.reshape(batch, N_MEM, M_HEADS, M_HEAD_DIM),
            mv_p.reshape(batch, N_MEM, M_HEADS, M_HEAD_DIM),
            take(k_s, (nseq, nq, A_KV_HEADS, A_HEAD_DIM)),
            take(v_s, (nseq, nq, A_KV_HEADS, A_HEAD_DIM)),
            take(ki_s, (nseq, nq, IDX_DIM)),
            take(vn_s, (nseq, nq, B_GROUPS, B_GROUP_DIM)))
```

```python
import functools

import numpy as np
import jax
import jax.numpy as jnp
from jax import lax
from jax.experimental import pallas as pl
from jax.experimental.pallas import tpu as pltpu

F32 = jnp.float32
BF16 = jnp.bfloat16
I32 = jnp.int32

D_MODEL = 2048
PAGE = 128
A_HEADS = 8
A_KV_HEADS = 4
A_HEAD_DIM = 128
A_WIDTH = A_HEADS * A_HEAD_DIM
A_KV_WIDTH = A_KV_HEADS * A_HEAD_DIM
IDX_HEADS = 16
IDX_DIM = 64
TOPK_MAX = 256
Q_BLOCK = 128
ROPE_THETA = 10000.0
CHUNK = 128
B_GROUPS = 8
B_GROUP_DIM = 128
B_WIDTH = B_GROUPS * B_GROUP_DIM
N_MEM = 256
M_HEADS = 4
M_HEAD_DIM = 256
M_WIDTH = M_HEADS * M_HEAD_DIM
EPS = 1e-6

OFF_K = A_WIDTH
OFF_V = OFF_K + A_KV_WIDTH
OFF_QI = OFF_V + A_KV_WIDTH
OFF_KI = OFF_QI + IDX_HEADS * IDX_DIM
OFF_WI = OFF_KI + IDX_DIM
OFF_REST = OFF_WI + IDX_HEADS
A_COLS = 3200
REST_COLS = A_WIDTH + 3 * B_WIDTH + 2 * M_WIDTH + 3 * D_MODEL

Q_SCALE = float(np.log2(np.e)) * A_HEAD_DIM ** -0.5
S_PAD = 16
INT_MIN = np.int32(-2 ** 31)
INT_MAX = np.int32(2 ** 31 - 1)
NEG_INF = float("-inf")

V7X_VMEM_LIMIT = 56 * 1024 * 1024


def _cparams(sem):
    return pltpu.CompilerParams(dimension_semantics=sem, vmem_limit_bytes=V7X_VMEM_LIMIT)


def _dot(a, b):
    return jnp.dot(a, b, preferred_element_type=F32)


def _dot_nt(a, b):
    return lax.dot_general(a, b, (((1,), (1,)), ((), ())), preferred_element_type=F32)


def _rms(x, g):
    return x * lax.rsqrt(jnp.mean(x * x, axis=-1, keepdims=True) + EPS) * g


def _const_spec(shape):
    nd = len(shape)
    return pl.BlockSpec(shape, lambda *_: (0,) * nd, pipeline_mode=pl.Buffered(1))


def _proj_a_kernel(x_ref, g_ref, w_ref, gq_ref, gk_ref, cq_ref, sq_ref, ci_ref, sia_ref, sib_ref,
                   q_ref, k_ref, v_ref, kb_ref, vb_ref, qi_ref, ki_ref, kib_ref, wi_ref):
    h = _rms(x_ref[...], g_ref[...]).astype(BF16)
    z = _dot(h, w_ref[...])
    cq, sq = cq_ref[...], sq_ref[...]
    ci, sia, sib = ci_ref[...], sia_ref[...], sib_ref[...]

    def norm_rope(zz, g):
        n = _rms(zz, g)
        return n * cq + pltpu.roll(n, A_HEAD_DIM // 2, 1) * sq

    def rope_idx(zz):
        return zz * ci + pltpu.roll(zz, 96, 1) * sia + pltpu.roll(zz, 32, 1) * sib

    for hh in range(A_HEADS):
        sl = slice(hh * A_HEAD_DIM, (hh + 1) * A_HEAD_DIM)
        q_ref[:, sl] = (norm_rope(z[:, sl], gq_ref[...]) * Q_SCALE).astype(BF16)
    for hh in range(A_KV_HEADS):
        sl = slice(hh * A_HEAD_DIM, (hh + 1) * A_HEAD_DIM)
        kh = norm_rope(z[:, OFF_K + hh * A_HEAD_DIM:OFF_K + (hh + 1) * A_HEAD_DIM], gk_ref[...])
        k_ref[:, sl] = kh
        kb_ref[:, sl] = kh.astype(BF16)
    v = z[:, OFF_V:OFF_V + A_KV_WIDTH]
    v_ref[...] = v
    vb_ref[...] = v.astype(BF16)
    for t in range(IDX_HEADS * IDX_DIM // 128):
        sl = slice(t * 128, (t + 1) * 128)
        qi_ref[:, sl] = rope_idx(z[:, OFF_QI + t * 128:OFF_QI + (t + 1) * 128]).astype(BF16)
    last = z[:, OFF_KI:OFF_KI + 128]
    ki = rope_idx(last)[:, :IDX_DIM]
    ki_ref[...] = ki
    kib_ref[...] = ki.astype(BF16)
    wi_ref[...] = last[:, IDX_DIM:IDX_DIM + IDX_HEADS] * ((IDX_HEADS ** -0.5) * (IDX_DIM ** -0.5))


def _proj_a(x, g_pre, w_a, g_q, g_k, tabs, tm, tab_map):
    n = x.shape[0]
    row = lambda w: pl.BlockSpec((tm, w), lambda i: (i, 0))
    tab = pl.BlockSpec((tm, 128), tab_map)
    outs = [(A_WIDTH, BF16), (A_KV_WIDTH, F32), (A_KV_WIDTH, F32), (A_KV_WIDTH, BF16), (A_KV_WIDTH, BF16),
            (IDX_HEADS * IDX_DIM, BF16), (IDX_DIM, F32), (IDX_DIM, BF16), (IDX_HEADS, F32)]
    return pl.pallas_call(
        _proj_a_kernel,
        grid=(n // tm,),
        in_specs=[row(D_MODEL), _const_spec((1, D_MODEL)), _const_spec((D_MODEL, A_COLS)),
                  _const_spec((1, A_HEAD_DIM)), _const_spec((1, A_HEAD_DIM)), tab, tab, tab, tab, tab],
        out_specs=[row(w) for w, _ in outs],
        out_shape=[jax.ShapeDtypeStruct((n, w), dt) for w, dt in outs],
        compiler_params=_cparams(("arbitrary",)),
        name="proj_a",
    )(x, g_pre, w_a, g_q, g_k, *tabs)


def _proj_rest_kernel(x_ref, g_ref, w_ref, o_ref, h_ref):
    @pl.when(pl.program_id(1) == 0)
    def _():
        h_ref[...] = _rms(x_ref[...], g_ref[...]).astype(BF16)

    o_ref[...] = _dot(h_ref[...], w_ref[...]).astype(BF16)


def _proj_rest(x, g_pre, w_r, tm, tn):
    n = x.shape[0]
    return pl.pallas_call(
        _proj_rest_kernel,
        grid=(n // tm, REST_COLS // tn),
        in_specs=[pl.BlockSpec((tm, D_MODEL), lambda i, j: (i, 0)),
                  pl.BlockSpec((1, D_MODEL), lambda i, j: (0, 0)),
                  pl.BlockSpec((D_MODEL, tn), lambda i, j: (0, j))],
        out_specs=pl.BlockSpec((tm, tn), lambda i, j: (i, j)),
        out_shape=jax.ShapeDtypeStruct((n, REST_COLS), BF16),
        scratch_shapes=[pltpu.VMEM((tm, D_MODEL), BF16)],
        compiler_params=_cparams(("arbitrary", "arbitrary")),
        name="proj_rest",
    )(x, g_pre, w_r)


def _mem_kv_kernel(x_ref, g_ref, w_ref, gk_ref, k_ref, v_ref, kb_ref, vb_ref):
    h = _rms(x_ref[...], g_ref[...]).astype(BF16)
    z = _dot(h, w_ref[...])
    for hh in range(M_HEADS):
        sl = slice(hh * M_HEAD_DIM, (hh + 1) * M_HEAD_DIM)
        kh = _rms(z[:, sl], gk_ref[...])
        k_ref[:, sl] = kh
        kb_ref[:, sl] = kh.astype(BF16)
    v = z[:, M_WIDTH:]
    v_ref[...] = v
    vb_ref[...] = v.astype(BF16)


def _mem_kv(mem, g_mem, w_mem, g_mk):
    n = mem.shape[0]
    blk = pl.BlockSpec((N_MEM, M_WIDTH), lambda i: (i, 0))
    return pl.pallas_call(
        _mem_kv_kernel,
        grid=(n // N_MEM,),
        in_specs=[pl.BlockSpec((N_MEM, D_MODEL), lambda i: (i, 0)), _const_spec((1, D_MODEL)),
                  _const_spec((D_MODEL, 2 * M_WIDTH)), _const_spec((1, M_HEAD_DIM))],
        out_specs=[blk, blk, blk, blk],
        out_shape=[jax.ShapeDtypeStruct((n, M_WIDTH), dt) for dt in (F32, F32, BF16, BF16)],
        compiler_params=_cparams(("arbitrary",)),
        name="mem_kv",
    )(mem, g_mem, w_mem, g_mk)


KEY_NEG_INF = np.int32(-0x7F800000)


def _key_to_f32(key):
    return pltpu.bitcast(jnp.where(key >= 0, key, INT_MIN - key), F32)


def _select_params(count, topk, pos_bits, j_ref):
    kf = float(topk)
    t0 = jnp.where(count(lambda s, p: s >= 0.0) >= kf, jnp.int32(0), INT_MIN)

    def bit_body(b, t):
        cand = t + lax.shift_left(jnp.int32(1), 30 - b)
        cand_f = _key_to_f32(cand)
        return jnp.where(count(lambda s, p: s >= cand_f) >= kf, cand, t)

    t = _key_to_f32(jnp.maximum(lax.fori_loop(0, 31, bit_body, t0), KEY_NEG_INF))
    tie = (count(lambda s, p: s >= t) > kf) & (t > NEG_INF)
    j_ref[...] = jnp.full(j_ref.shape, INT_MAX, I32)

    @pl.when(jnp.max(tie.astype(I32)) > 0)
    def _():
        n_gt = count(lambda s, p: s > t)

        def pos_body(b, p_lo):
            cand = p_lo + lax.shift_left(jnp.int32(1), pos_bits - 1 - b)
            n_eq = count(lambda s, p: (s == t) & (p < cand))
            return jnp.where(n_gt + n_eq < kf, cand, p_lo)

        p_sel = lax.fori_loop(0, pos_bits, pos_body, jnp.zeros(t.shape, I32))
        j_ref[...] = jnp.where(tie, p_sel, INT_MAX)

    return t, j_ref[...]


def _dsa_prompt_kernel(qi_ref, wit_ref, q_ref, ki_ref, k_ref, v_ref, o_ref,
                       sc_ref, bias_ref, vt_ref, j_ref, s_ref, acc_ref, *, topk, kc, pos_bits):
    i = pl.program_id(1)
    nck = (i * Q_BLOCK + Q_BLOCK + kc - 1) // kc
    nchunks = vt_ref.shape[0]
    rep = A_HEADS // A_KV_HEADS
    q_pos = i * Q_BLOCK + lax.broadcasted_iota(I32, (1, Q_BLOCK), 1)
    sub = lax.broadcasted_iota(I32, (kc, 1), 0)

    @pl.when(i == 0)
    def _():
        for c in range(nchunks):
            for g in range(A_KV_HEADS):
                gsl = slice(g * A_HEAD_DIM, (g + 1) * A_HEAD_DIM)
                vt_ref[c, gsl, :] = v_ref[c * kc:(c + 1) * kc, gsl].astype(F32).T.astype(BF16)

    def chunk_rows(c):
        return pl.ds(pl.multiple_of(c * kc, kc), kc)

    qi = qi_ref[...]
    wit = wit_ref[...]
    qi_pairs = [jnp.concatenate([qi[:, (2 * j) * IDX_DIM:(2 * j + 1) * IDX_DIM],
                                 qi[:, (2 * j + 1) * IDX_DIM:(2 * j + 2) * IDX_DIM]], axis=0)
                for j in range(IDX_HEADS // 2)]

    def score_body(c, carry):
        kic = ki_ref[chunk_rows(c), :]
        acc = jnp.zeros((kc, Q_BLOCK), F32)
        for j in range(IDX_HEADS // 2):
            d = _dot_nt(kic, qi_pairs[j])
            acc = acc + jnp.maximum(d[:, :Q_BLOCK], 0.0) * wit[2 * j:2 * j + 1, :]
            acc = acc + jnp.maximum(d[:, Q_BLOCK:], 0.0) * wit[2 * j + 1:2 * j + 2, :]
        sc_ref[c] = jnp.where(c * kc + sub <= q_pos, acc, NEG_INF)
        return carry

    lax.fori_loop(0, nck, score_body, 0)

    def count(pred):
        def body(c, acc):
            part = jnp.where(pred(sc_ref[c], c * kc + sub), 1.0, 0.0)
            return acc + jnp.sum(part.reshape(kc // 64, 64, Q_BLOCK), axis=0)

        acc = lax.fori_loop(0, nck, body, jnp.zeros((64, Q_BLOCK), F32))
        return jnp.sum(acc, axis=0, keepdims=True)

    thr, jsel = _select_params(count, topk, pos_bits, j_ref)

    def bias_body(c, carry):
        s = sc_ref[c]
        pos = c * kc + sub
        sel = ((s > thr) | ((s == thr) & (pos <= jsel))) & (pos <= q_pos)
        bias_ref[c] = jnp.where(sel, 0.0, NEG_INF)
        return carry

    lax.fori_loop(0, nck, bias_body, 0)

    gsl = lambda g: slice(g * A_HEAD_DIM, (g + 1) * A_HEAD_DIM)
    row0 = lambda v: jnp.full((1, rep * Q_BLOCK), v, F32)
    for gs in [(g0, g0 + 1) for g0 in range(0, A_KV_HEADS, 2)]:
        qgs = [jnp.concatenate([q_ref[:, gsl(g * rep + r)] for r in range(rep)], axis=0) for g in gs]

        def qk_body(c, ms, gs=gs, qgs=qgs):
            rows = chunk_rows(c)
            b = bias_ref[c]
            bias = jnp.concatenate([b] * rep, axis=1)
            out = []
            for u, g in enumerate(gs):
                s = _dot_nt(k_ref[rows, gsl(g)], qgs[u]) + bias
                s_ref[c, u] = s
                out.append(jnp.maximum(ms[u], jnp.max(s, axis=0, keepdims=True)))
            return tuple(out)

        ms = lax.fori_loop(0, nck, qk_body, (row0(NEG_INF),) * len(gs))
        acc_ref[...] = jnp.zeros(acc_ref.shape, F32)

        def pv_body(c, ls, gs=gs, ms=ms):
            out = []
            for u, g in enumerate(gs):
                p = jnp.exp2(s_ref[c, u] - ms[u])
                out.append(ls[u] + jnp.sum(p, axis=0, keepdims=True))
                acc_ref[u] += _dot(vt_ref[c, gsl(g), :], p.astype(BF16))
            return tuple(out)

        ls = lax.fori_loop(0, nck, pv_body, (row0(0.0),) * len(gs))
        for u, g in enumerate(gs):
            o = acc_ref[u] / ls[u]
            for r in range(rep):
                o_ref[:, gsl(g * rep + r)] = o[:, r * Q_BLOCK:(r + 1) * Q_BLOCK].T.astype(BF16)


def _dsa_prompt(qi, wit, q, ki_b, k_b, v_b, batch, seq, topk):
    nqb = seq // Q_BLOCK
    rep = A_HEADS // A_KV_HEADS
    kc = min(512, seq)
    nchunks = seq // kc
    pos_bits = max(1, int(np.ceil(np.log2(seq))))
    qrow = lambda w: pl.BlockSpec((Q_BLOCK, w), lambda b, i: (b * nqb + i, 0))
    seqblk = lambda w: pl.BlockSpec((seq, w), lambda b, i: (b, 0))
    return pl.pallas_call(
        functools.partial(_dsa_prompt_kernel, topk=topk, kc=kc, pos_bits=pos_bits),
        grid=(batch, nqb),
        in_specs=[qrow(IDX_HEADS * IDX_DIM), pl.BlockSpec((IDX_HEADS, Q_BLOCK), lambda b, i: (0, b * nqb + i)),
                  qrow(A_WIDTH), seqblk(IDX_DIM), seqblk(A_KV_WIDTH), seqblk(A_KV_WIDTH)],
        out_specs=qrow(A_WIDTH),
        out_shape=jax.ShapeDtypeStruct((batch * seq, A_WIDTH), BF16),
        scratch_shapes=[pltpu.VMEM((nchunks, kc, Q_BLOCK), F32), pltpu.VMEM((nchunks, kc, Q_BLOCK), F32),
                        pltpu.VMEM((nchunks, A_KV_WIDTH, kc), BF16), pltpu.VMEM((1, Q_BLOCK), I32),
                        pltpu.VMEM((nchunks, 2, kc, rep * Q_BLOCK), F32),
                        pltpu.VMEM((2, A_HEAD_DIM, rep * Q_BLOCK), F32)],
        compiler_params=_cparams(("arbitrary", "arbitrary")),
        name="dsa_prompt",
    )(qi, wit, q, ki_b, k_b, v_b)


def _fold_heads(e, nq):
    acc = e[0:8]
    for t in range(1, e.shape[0] // 8):
        acc = acc + e[8 * t:8 * (t + 1)]
    return acc + pltpu.roll(acc, nq, 0)


def _sample_scores_kernel(pt_ref, qp_ref, wc_ref, kin_ref, *refs, pages, nq):
    page_refs, (sc_ref, scn_ref) = refs[:pages], refs[pages:]
    qp = qp_ref[...]
    wc = wc_ref[...]
    kp = jnp.concatenate([r[...] for r in page_refs], axis=0).astype(BF16)
    sc_ref[...] = _fold_heads(jnp.maximum(_dot_nt(qp, kp), 0.0) * wc, nq)

    @pl.when(pl.program_id(1) == 0)
    def _():
        kin = jnp.concatenate([kin_ref[...], jnp.zeros((PAGE - S_PAD, IDX_DIM), BF16)], axis=0)
        scn_ref[...] = _fold_heads(jnp.maximum(_dot_nt(qp, kin), 0.0) * wc, nq)


def _sample_scores(page_flat, qp, wcol, ki_new_b, cache_kidx, nseq, npages, pages, nq):
    steps = npages // pages
    rows = IDX_HEADS * nq
    page_specs = [pl.BlockSpec((None, PAGE, IDX_DIM),
                               lambda b, s, pt, p=p: (pt[b * npages + s * pages + p], 0, 0))
                  for p in range(pages)]
    grid_spec = pltpu.PrefetchScalarGridSpec(
        num_scalar_prefetch=1,
        grid=(nseq, steps),
        in_specs=[pl.BlockSpec((None, rows, IDX_DIM), lambda b, s, pt: (b, 0, 0)),
                  pl.BlockSpec((None, rows, 1), lambda b, s, pt: (b, 0, 0)),
                  pl.BlockSpec((S_PAD, IDX_DIM), lambda b, s, pt: (b, 0))] + page_specs,
        out_specs=[pl.BlockSpec((None, 8, pages * PAGE), lambda b, s, pt: (b, 0, s)),
                   pl.BlockSpec((None, 8, PAGE), lambda b, s, pt: (b, 0, 0))],
    )
    return pl.pallas_call(
        functools.partial(_sample_scores_kernel, pages=pages, nq=nq),
        grid_spec=grid_spec,
        out_shape=[jax.ShapeDtypeStruct((nseq, 8, npages * PAGE), F32),
                   jax.ShapeDtypeStruct((nseq, 8, PAGE), F32)],
        compiler_params=_cparams(("arbitrary", "arbitrary")),
        name="dsa_sample_scores",
    )(page_flat, qp, wcol, ki_new_b, *([cache_kidx] * pages))


def _sample_attn_kernel(pt_ref, sc_ref, scn_ref, qs_ref, kn_ref, vn_ref, *refs,
                        pages, nq, n_past, topk, pos_bits):
    k_pages, v_pages = refs[:pages], refs[pages:2 * pages]
    o_ref = refs[2 * pages]
    scm_ref, bias_ref, biasn_ref, j_ref, m_ref, l_ref, acc_ref, kbuf_ref, vbuf_ref = refs[2 * pages + 1:]
    s_idx = pl.program_id(1)
    kw = pages * PAGE
    rows_q = qs_ref.shape[1]

    @pl.when(s_idx == 0)
    def _():
        qrow = lax.broadcasted_iota(I32, (8, 1), 0) % nq
        lane_p = lax.broadcasted_iota(I32, (1, n_past), 1)
        lane_n = lax.broadcasted_iota(I32, (1, PAGE), 1)
        scm_ref[...] = jnp.where(lane_n <= qrow, scn_ref[...], NEG_INF)

        def count(pred):
            cp = jnp.sum(jnp.where(pred(sc_ref[...], lane_p), 1.0, 0.0), axis=-1, keepdims=True)
            cn = jnp.sum(jnp.where(pred(scm_ref[...], n_past + lane_n), 1.0, 0.0), axis=-1, keepdims=True)
            return cp + cn

        thr, jsel = _select_params(count, topk, pos_bits, j_ref)
        kp = sc_ref[...]
        bias_p = jnp.where((kp > thr) | ((kp == thr) & (lane_p <= jsel)), 0.0, NEG_INF)
        for st in range(n_past // kw):
            bias_ref[st] = bias_p[:, st * kw:(st + 1) * kw]
        kn = scm_ref[...]
        seln = ((kn > thr) | ((kn == thr) & (n_past + lane_n <= jsel))) & (lane_n <= qrow)
        biasn_ref[...] = jnp.where(seln, 0.0, NEG_INF)
        m_ref[...] = jnp.full(m_ref.shape, NEG_INF, F32)
        l_ref[...] = jnp.zeros(l_ref.shape, F32)
        acc_ref[...] = jnp.zeros(acc_ref.shape, F32)

    for p in range(pages):
        for g in range(A_KV_HEADS):
            gsl = slice(g * A_HEAD_DIM, (g + 1) * A_HEAD_DIM)
            rows = pl.ds(g, PAGE, stride=A_KV_HEADS)
            kbuf_ref[p * PAGE:(p + 1) * PAGE, gsl] = k_pages[p][rows, :].astype(BF16)
            vbuf_ref[p * PAGE:(p + 1) * PAGE, gsl] = v_pages[p][rows, :].astype(BF16)

    def update(g, s, vv):
        m = m_ref[g]
        m_new = jnp.maximum(m, jnp.max(s, axis=-1, keepdims=True))
        m_safe = jnp.where(m_new == NEG_INF, 0.0, m_new)
        alpha = jnp.exp2(m - m_safe)
        p = jnp.exp2(s - m_safe)
        l_ref[g] = alpha * l_ref[g] + jnp.sum(p, axis=-1, keepdims=True)
        acc_ref[g] = alpha * acc_ref[g] + _dot(p.astype(BF16), vv)
        m_ref[g] = m_new

    bias = jnp.concatenate([bias_ref[s_idx]] * (rows_q // 8), axis=0)
    for g in range(A_KV_HEADS):
        gsl = slice(g * A_HEAD_DIM, (g + 1) * A_HEAD_DIM)
        update(g, _dot_nt(qs_ref[g], kbuf_ref[:, gsl]) + bias, vbuf_ref[:, gsl])

    @pl.when(s_idx == pl.num_programs(1) - 1)
    def _():
        bn = jnp.concatenate([biasn_ref[:, :S_PAD]] * (rows_q // 8), axis=0)
        for g in range(A_KV_HEADS):
            gsl = slice(g * A_HEAD_DIM, (g + 1) * A_HEAD_DIM)
            update(g, _dot_nt(qs_ref[g], kn_ref[:, gsl]) + bn, vn_ref[:, gsl])
            o_ref[g] = acc_ref[g] / l_ref[g]


def _sample_attn(page_flat, sc, scn, qs, k_new_b, v_new_b, cache_k, cache_v, nseq, npages, pages, nq, topk):
    steps = npages // pages
    n_past = npages * PAGE
    kw = pages * PAGE
    rows_q = qs.shape[2]
    pos_bits = int(np.ceil(np.log2(n_past + PAGE)))
    page_spec = lambda p: pl.BlockSpec((None, PAGE * A_KV_HEADS, A_HEAD_DIM),
                                       lambda b, s, pt, p=p: (pt[b * npages + s * pages + p], 0, 0))
    grid_spec = pltpu.PrefetchScalarGridSpec(
        num_scalar_prefetch=1,
        grid=(nseq, steps),
        in_specs=[pl.BlockSpec((None, 8, n_past), lambda b, s, pt: (b, 0, 0)),
                  pl.BlockSpec((None, 8, PAGE), lambda b, s, pt: (b, 0, 0)),
                  pl.BlockSpec((None, A_KV_HEADS, rows_q, A_HEAD_DIM), lambda b, s, pt: (b, 0, 0, 0)),
                  pl.BlockSpec((S_PAD, A_KV_WIDTH), lambda b, s, pt: (b, 0)),
                  pl.BlockSpec((S_PAD, A_KV_WIDTH), lambda b, s, pt: (b, 0))]
                 + [page_spec(p) for p in range(pages)] + [page_spec(p) for p in range(pages)],
        out_specs=pl.BlockSpec((None, A_KV_HEADS, rows_q, A_HEAD_DIM), lambda b, s, pt: (b, 0, 0, 0)),
        scratch_shapes=[pltpu.VMEM((8, PAGE), F32),
                        pltpu.VMEM((steps, 8, kw), F32), pltpu.VMEM((8, PAGE), F32),
                        pltpu.VMEM((8, 1), I32),
                        pltpu.VMEM((A_KV_HEADS, rows_q, 1), F32), pltpu.VMEM((A_KV_HEADS, rows_q, 1), F32),
                        pltpu.VMEM((A_KV_HEADS, rows_q, A_HEAD_DIM), F32),
                        pltpu.VMEM((kw, A_KV_WIDTH), BF16), pltpu.VMEM((kw, A_KV_WIDTH), BF16)],
    )
    return pl.pallas_call(
        functools.partial(_sample_attn_kernel, pages=pages, nq=nq, n_past=n_past, topk=topk, pos_bits=pos_bits),
        grid_spec=grid_spec,
        out_shape=jax.ShapeDtypeStruct((nseq, A_KV_HEADS, rows_q, A_HEAD_DIM), F32),
        compiler_params=_cparams(("arbitrary", "arbitrary")),
        name="dsa_sample_attn",
    )(page_flat, sc, scn, qs, k_new_b, v_new_b, *([cache_k] * pages), *([cache_v] * pages))


def _merge_kernel(x_ref, oa_ref, ag_ref, bu_ref, bv_ref, bg_ref, cq_ref, cg_ref, ra_ref, rb_ref, rc_ref,
                  mk_ref, mv_ref, ws_ref, bs_ref, gsgu_ref, gmq_ref, wpa_ref, wpb_ref, wpc_ref, wout_ref,
                  y_ref, *maybe_vn_ref, tm, chunk):
    f32 = lambda r: r[...].astype(F32)
    silu = lambda t: t * jax.nn.sigmoid(t)

    vn = _rms(f32(bv_ref), gsgu_ref[...])
    if maybe_vn_ref:
        maybe_vn_ref[0][...] = vn
    vnb = vn.astype(BF16)
    bu = f32(bu_ref)
    tril = (lax.broadcasted_iota(I32, (chunk, chunk), 1) <= lax.broadcasted_iota(I32, (chunk, chunk), 0))
    ob_cols = []
    for g in range(B_GROUPS):
        wg = jnp.where(tril, ws_ref[g], 0.0).astype(BF16)
        gsl = slice(g * B_GROUP_DIM, (g + 1) * B_GROUP_DIM)
        parts = [_dot(wg, vnb[c * chunk:(c + 1) * chunk, gsl]) + bs_ref[:, g:g + 1] for c in range(tm // chunk)]
        ob_cols.append(parts[0] if len(parts) == 1 else jnp.concatenate(parts, axis=0))
    ob = bu * jnp.concatenate(ob_cols, axis=1)
    pb = _dot((ob * silu(f32(bg_ref))).astype(BF16), wpb_ref[...])

    cq = f32(cq_ref)
    oc_cols = []
    for hh in range(M_HEADS):
        hsl = slice(hh * M_HEAD_DIM, (hh + 1) * M_HEAD_DIM)
        qn = (_rms(cq[:, hsl], gmq_ref[...]) * (M_HEAD_DIM ** -0.5)).astype(BF16)
        s = _dot_nt(qn, mk_ref[:, hsl])
        p = jnp.exp(s - jnp.max(s, axis=-1, keepdims=True))
        oc_cols.append(_dot(p.astype(BF16), mv_ref[:, hsl]) / jnp.sum(p, axis=-1, keepdims=True))
    oc = jnp.concatenate(oc_cols, axis=1)
    pc = _dot((oc * silu(f32(cg_ref))).astype(BF16), wpc_ref[...])

    pa = _dot((f32(oa_ref) * silu(f32(ag_ref))).astype(BF16), wpa_ref[...])
    sig = jax.nn.sigmoid
    m = sig(f32(ra_ref)) * pa + sig(f32(rb_ref)) * pb + sig(f32(rc_ref)) * pc
    y_ref[...] = x_ref[...] + _dot(m.astype(BF16), wout_ref[...])


def _merge(x, oa, zr, mk_b, mv_b, ws, bs_t, g_sgu, g_mq, w_pa, w_pb, w_pc, w_out, tm, chunk, mem_map, emit_vn):
    n = x.shape[0]
    col = lambda w, j: pl.BlockSpec((tm, w), lambda i, j=j: (i, j))
    mem = pl.BlockSpec((N_MEM, M_WIDTH), mem_map)
    in_specs = [col(D_MODEL, 0), col(A_WIDTH, 0),
                col(1024, 0), col(1024, 1), col(1024, 2), col(1024, 3), col(1024, 4), col(1024, 5),
                col(2048, 3), col(2048, 4), col(2048, 5),
                mem, mem,
                _const_spec((B_GROUPS, chunk, chunk)), _const_spec((chunk, B_GROUPS)),
                _const_spec((1, B_WIDTH)), _const_spec((1, M_HEAD_DIM)),
                _const_spec((A_WIDTH, D_MODEL)), _const_spec((B_WIDTH, D_MODEL)),
                _const_spec((M_WIDTH, D_MODEL)), _const_spec((D_MODEL, D_MODEL))]
    out_specs = [col(D_MODEL, 0)]
    out_shape = [jax.ShapeDtypeStruct((n, D_MODEL), F32)]
    if emit_vn:
        out_specs.append(col(B_WIDTH, 0))
        out_shape.append(jax.ShapeDtypeStruct((n, B_WIDTH), F32))
    return pl.pallas_call(
        functools.partial(_merge_kernel, tm=tm, chunk=chunk),
        grid=(n // tm,),
        in_specs=in_specs,
        out_specs=out_specs,
        out_shape=out_shape,
        compiler_params=_cparams(("arbitrary",)),
        name="merge",
    )(x, oa, *([zr] * 9), mk_b, mv_b, ws, bs_t, g_sgu, g_mq, w_pa, w_pb, w_pc, w_out)


def _rope_tables(pos):
    pos = pos.astype(F32)[:, None]

    def cs(half):
        freq = ROPE_THETA ** (-jnp.arange(half, dtype=F32) / half)
        ang = pos * freq[None, :]
        return jnp.cos(ang), jnp.sin(ang)

    c, s = cs(A_HEAD_DIM // 2)
    ci, si = cs(IDX_DIM // 2)
    z = jnp.zeros_like(si)
    return (jnp.concatenate([c, c], axis=1), jnp.concatenate([-s, s], axis=1),
            jnp.concatenate([ci] * 4, axis=1), jnp.concatenate([-si, z, -si, z], axis=1),
            jnp.concatenate([z, si, z, si], axis=1))


def kernel(x_prompt, x_sample, cache_k, cache_v, cache_kidx, cache_mem_k, cache_mem_v, page_table,
           mem_prompt, g_pre, w_in, g_q, g_k, g_mq, g_mk, g_mem, w_mem_kv, g_sgu, w_s, b_s,
           w_pa, w_pb, w_pc, w_out):
    batch, seq, _ = x_prompt.shape
    nseq, nq, _ = x_sample.shape
    assert nq == 4 and nq <= S_PAD
    npages = page_table.shape[1]
    n_past = npages * PAGE
    n_pool = cache_k.shape[0]
    row2 = lambda a: a.reshape(1, -1)

    w_a = w_in[:, :A_COLS].astype(BF16)
    w_r = w_in[:, OFF_REST:].astype(BF16)
    w_pa_b, w_pb_b, w_pc_b, w_out_b = (w.astype(BF16) for w in (w_pa, w_pb, w_pc, w_out))
    w_mem_b = w_mem_kv.astype(BF16)
    g_pre2, g_q2, g_k2, g_mq2, g_mk2, g_mem2, g_sgu2 = map(row2, (g_pre, g_q, g_k, g_mq, g_mk, g_mem, g_sgu))

    xp = x_prompt.reshape(batch * seq, D_MODEL)
    tm_a = min(256, seq)
    nblk = seq // tm_a
    tabs_p = _rope_tables(jnp.arange(seq))
    q, k_p, v_p, k_b, v_b, qi, ki_p, ki_b, wi = _proj_a(
        xp, g_pre2, w_a, g_q2, g_k2, tabs_p, tm_a, lambda i: (i % nblk, 0))
    zr = _proj_rest(xp, g_pre2, w_r, min(1024, seq), 512)
    mk_p, mv_p, mk_b, mv_b = _mem_kv(mem_prompt.reshape(batch * N_MEM, D_MODEL), g_mem2, w_mem_b, g_mk2)
    oa = _dsa_prompt(qi, wi.T, q, ki_b, k_b, v_b, batch, seq, min(TOPK_MAX, seq // 4))
    tm_m = min(256, seq)
    nblk_m = seq // tm_m
    (y_p,) = _merge(xp, oa, zr, mk_b, mv_b, w_s, b_s.T, g_sgu2, g_mq2, w_pa_b, w_pb_b, w_pc_b, w_out_b,
                    tm_m, CHUNK, lambda i: (i // nblk_m, 0), False)

    xs = jnp.pad(x_sample, ((0, 0), (0, S_PAD - nq), (0, 0))).reshape(nseq * S_PAD, D_MODEL)
    tabs_s = tuple(jnp.tile(t, (nseq, 1)) for t in _rope_tables(n_past + jnp.arange(S_PAD)))
    rows_s = nseq * S_PAD
    q_s, k_s, v_s, k_sb, v_sb, qi_s, ki_s, ki_sb, wi_s = _proj_a(
        xs, g_pre2, w_a, g_q2, g_k2, tabs_s, rows_s, lambda i: (0, 0))
    zr_s = _proj_rest(xs, g_pre2, w_r, rows_s, 512)

    qp = (qi_s.reshape(nseq, S_PAD, IDX_HEADS, IDX_DIM)[:, :nq]
          .transpose(0, 2, 1, 3).reshape(nseq, IDX_HEADS * nq, IDX_DIM))
    wcol = wi_s.reshape(nseq, S_PAD, IDX_HEADS)[:, :nq].transpose(0, 2, 1).reshape(nseq, IDX_HEADS * nq, 1)
    rep = A_HEADS // A_KV_HEADS
    qs = (q_s.reshape(nseq, S_PAD, A_KV_HEADS, rep, A_HEAD_DIM)[:, :nq]
          .transpose(0, 2, 3, 1, 4).reshape(nseq, A_KV_HEADS, rep * nq, A_HEAD_DIM))
    qs = jnp.concatenate([qs, qs], axis=2)
    page_flat = page_table.reshape(-1)
    pages = 8
    sc, scn = _sample_scores(page_flat, qp, wcol, ki_sb, cache_kidx, nseq, npages, pages, nq)
    o_s = _sample_attn(page_flat, sc, scn, qs, k_sb, v_sb,
                       cache_k.reshape(n_pool, PAGE * A_KV_HEADS, A_HEAD_DIM),
                       cache_v.reshape(n_pool, PAGE * A_KV_HEADS, A_HEAD_DIM),
                       nseq, npages, pages, nq, min(TOPK_MAX, (n_past + nq) // 4))
    oa_s = (o_s[:, :, :rep * nq].reshape(nseq, A_KV_HEADS, rep, nq, A_HEAD_DIM)
            .transpose(0, 3, 1, 2, 4).reshape(nseq, nq, A_WIDTH))
    oa_s = jnp.pad(oa_s, ((0, 0), (0, S_PAD - nq), (0, 0))).reshape(rows_s, A_WIDTH).astype(BF16)
    mk_s = cache_mem_k.reshape(nseq * N_MEM, M_WIDTH).astype(BF16)
    mv_s = cache_mem_v.reshape(nseq * N_MEM, M_WIDTH).astype(BF16)
    y_s, vn_s = _merge(xs, oa_s, zr_s, mk_s, mv_s, w_s[:, :S_PAD, :S_PAD], b_s[:, :S_PAD].T, g_sgu2, g_mq2,
                       w_pa_b, w_pb_b, w_pc_b, w_out_b, S_PAD, S_PAD, lambda i: (i, 0), True)

    take = lambda a, shape: a.reshape(nseq, S_PAD, -1)[:, :nq].reshape(shape)
    return (y_p.reshape(batch, seq, D_MODEL),
            take(y_s, (nseq, nq, D_MODEL)),
            k_p.reshape(batch, seq, A_KV_HEADS, A_HEAD_DIM),
            v_p.reshape(batch, seq, A_KV_HEADS, A_HEAD_DIM),
            ki_p.reshape(batch, seq, IDX_DIM),
            mk_p.reshape(batch, N_MEM, M_HEADS, M_HEAD_DIM),
            mv_p.reshape(batch, N_MEM, M_HEADS, M_HEAD_DIM),
            take(k_s, (nseq, nq, A_KV_HEADS, A_HEAD_DIM)),
            take(v_s, (nseq, nq, A_KV_HEADS, A_HEAD_DIM)),
            take(ki_s, (nseq, nq, IDX_DIM)),
            take(vn_s, (nseq, nq, B_GROUPS, B_GROUP_DIM)))
```

```python
import functools

import numpy as np
import jax
import jax.numpy as jnp
from jax import lax
from jax.experimental import pallas as pl
from jax.experimental.pallas import tpu as pltpu

F32 = jnp.float32
BF16 = jnp.bfloat16
I32 = jnp.int32

D_MODEL = 2048
PAGE = 128
A_HEADS = 8
A_KV_HEADS = 4
A_HEAD_DIM = 128
A_WIDTH = A_HEADS * A_HEAD_DIM
A_KV_WIDTH = A_KV_HEADS * A_HEAD_DIM
IDX_HEADS = 16
IDX_DIM = 64
TOPK_MAX = 256
Q_BLOCK = 128
ROPE_THETA = 10000.0
CHUNK = 128
B_GROUPS = 8
B_GROUP_DIM = 128
B_WIDTH = B_GROUPS * B_GROUP_DIM
N_MEM = 256
M_HEADS = 4
M_HEAD_DIM = 256
M_WIDTH = M_HEADS * M_HEAD_DIM
EPS = 1e-6

OFF_K = A_WIDTH
OFF_V = OFF_K + A_KV_WIDTH
OFF_QI = OFF_V + A_KV_WIDTH
OFF_KI = OFF_QI + IDX_HEADS * IDX_DIM
OFF_WI = OFF_KI + IDX_DIM
OFF_REST = OFF_WI + IDX_HEADS
A_COLS = 3200
REST_COLS = A_WIDTH + 3 * B_WIDTH + 2 * M_WIDTH + 3 * D_MODEL

Q_SCALE = float(np.log2(np.e)) * A_HEAD_DIM ** -0.5
S_PAD = 16
PROJ_TN = 1024
INT_MIN = np.int32(-2 ** 31)
INT_MAX = np.int32(2 ** 31 - 1)
NEG_INF = float("-inf")

V7X_VMEM_LIMIT = 56 * 1024 * 1024


def _cparams(sem):
    return pltpu.CompilerParams(dimension_semantics=sem, vmem_limit_bytes=V7X_VMEM_LIMIT)


def _dot(a, b):
    return jnp.dot(a, b, preferred_element_type=F32)


def _dot_nt(a, b):
    return lax.dot_general(a, b, (((1,), (1,)), ((), ())), preferred_element_type=F32)


def _rms(x, g):
    return x * lax.rsqrt(jnp.mean(x * x, axis=-1, keepdims=True) + EPS) * g


def _const_spec(shape):
    nd = len(shape)
    return pl.BlockSpec(shape, lambda *_: (0,) * nd, pipeline_mode=pl.Buffered(1))


def _proj_a_kernel(x_ref, g_ref, w_ref, gq_ref, gk_ref, cq_ref, sq_ref, ci_ref, sia_ref, sib_ref,
                   q_ref, k_ref, v_ref, kb_ref, vb_ref, qi_ref, ki_ref, kib_ref, wi_ref):
    h = _rms(x_ref[...], g_ref[...]).astype(BF16)
    z = _dot_nt(h, w_ref[...])
    tm = z.shape[0]
    cq, sq = cq_ref[...], sq_ref[...]
    ci, sia, sib = ci_ref[...], sia_ref[...], sib_ref[...]

    def norm_rope(zz, g):
        n = _rms(zz, g)
        return n * cq + pltpu.roll(n, A_HEAD_DIM // 2, 1) * sq

    def rope_idx(zz):
        return zz * ci + pltpu.roll(zz, 96, 1) * sia + pltpu.roll(zz, 32, 1) * sib

    for hh in range(A_HEADS):
        sl = slice(hh * A_HEAD_DIM, (hh + 1) * A_HEAD_DIM)
        q_ref[:, sl] = (norm_rope(z[:, sl], gq_ref[...]) * Q_SCALE).astype(BF16)
    for hh in range(A_KV_HEADS):
        sl = slice(hh * A_HEAD_DIM, (hh + 1) * A_HEAD_DIM)
        kh = norm_rope(z[:, OFF_K + hh * A_HEAD_DIM:OFF_K + (hh + 1) * A_HEAD_DIM], gk_ref[...])
        vh = z[:, OFF_V + hh * A_HEAD_DIM:OFF_V + (hh + 1) * A_HEAD_DIM]
        head_rows = pl.ds(hh, tm, stride=A_KV_HEADS)
        k_ref[head_rows, :] = kh
        v_ref[head_rows, :] = vh
        kb_ref[:, sl] = kh.astype(BF16)
        vb_ref[:, sl] = vh.astype(BF16)
    for t in range(IDX_HEADS * IDX_DIM // 128):
        sl = slice(t * 128, (t + 1) * 128)
        qi_ref[:, sl] = rope_idx(z[:, OFF_QI + t * 128:OFF_QI + (t + 1) * 128]).astype(BF16)
    last = z[:, OFF_KI:OFF_KI + 128]
    ki = rope_idx(last)[:, :IDX_DIM]
    ki_ref[...] = ki
    kib_ref[...] = ki.astype(BF16)
    wi_ref[...] = last[:, IDX_DIM:IDX_DIM + IDX_HEADS] * ((IDX_HEADS ** -0.5) * (IDX_DIM ** -0.5))


def _proj_a(x, g_pre, w_a, g_q, g_k, tabs, tm, tab_map):
    n = x.shape[0]
    row = lambda w: pl.BlockSpec((tm, w), lambda i: (i, 0))
    tab = pl.BlockSpec((tm, 128), tab_map)
    outs = [(1, A_WIDTH, BF16), (A_KV_HEADS, A_HEAD_DIM, F32), (A_KV_HEADS, A_HEAD_DIM, F32),
            (1, A_KV_WIDTH, BF16), (1, A_KV_WIDTH, BF16),
            (1, IDX_HEADS * IDX_DIM, BF16), (1, IDX_DIM, F32), (1, IDX_DIM, BF16), (1, IDX_HEADS, F32)]
    return pl.pallas_call(
        _proj_a_kernel,
        grid=(n // tm,),
        in_specs=[row(D_MODEL), _const_spec((1, D_MODEL)), _const_spec((A_COLS, D_MODEL)),
                  _const_spec((1, A_HEAD_DIM)), _const_spec((1, A_HEAD_DIM)), tab, tab, tab, tab, tab],
        out_specs=[pl.BlockSpec((tm * r, w), lambda i: (i, 0)) for r, w, _ in outs],
        out_shape=[jax.ShapeDtypeStruct((n * r, w), dt) for r, w, dt in outs],
        compiler_params=_cparams(("arbitrary",)),
        name="proj_a",
    )(x, g_pre, w_a, g_q, g_k, *tabs)


def _proj_rest_kernel(x_ref, g_ref, w_ref, o_ref, h_ref):
    @pl.when(pl.program_id(1) == 0)
    def _():
        h_ref[...] = _rms(x_ref[...], g_ref[...]).astype(BF16)

    o_ref[...] = _dot_nt(h_ref[...], w_ref[...]).astype(BF16)


def _proj_rest(x, g_pre, w_r, tm, tn):
    n = x.shape[0]
    return pl.pallas_call(
        _proj_rest_kernel,
        grid=(n // tm, REST_COLS // tn),
        in_specs=[pl.BlockSpec((tm, D_MODEL), lambda i, j: (i, 0)),
                  pl.BlockSpec((1, D_MODEL), lambda i, j: (0, 0)),
                  pl.BlockSpec((tn, D_MODEL), lambda i, j: (j, 0))],
        out_specs=pl.BlockSpec((tm, tn), lambda i, j: (i, j)),
        out_shape=jax.ShapeDtypeStruct((n, REST_COLS), BF16),
        scratch_shapes=[pltpu.VMEM((tm, D_MODEL), BF16)],
        compiler_params=_cparams(("arbitrary", "arbitrary")),
        name="proj_rest",
    )(x, g_pre, w_r)


def _mem_kv_kernel(x_ref, g_ref, w_ref, gk_ref, k_ref, v_ref, kb_ref, vb_ref):
    h = _rms(x_ref[...], g_ref[...]).astype(BF16)
    z = _dot(h, w_ref[...])
    for hh in range(M_HEADS):
        sl = slice(hh * M_HEAD_DIM, (hh + 1) * M_HEAD_DIM)
        kh = _rms(z[:, sl], gk_ref[...])
        k_ref[:, sl] = kh
        kb_ref[:, sl] = kh.astype(BF16)
    v = z[:, M_WIDTH:]
    v_ref[...] = v
    vb_ref[...] = v.astype(BF16)


def _mem_kv(mem, g_mem, w_mem, g_mk):
    n = mem.shape[0]
    blk = pl.BlockSpec((N_MEM, M_WIDTH), lambda i: (i, 0))
    return pl.pallas_call(
        _mem_kv_kernel,
        grid=(n // N_MEM,),
        in_specs=[pl.BlockSpec((N_MEM, D_MODEL), lambda i: (i, 0)), _const_spec((1, D_MODEL)),
                  _const_spec((D_MODEL, 2 * M_WIDTH)), _const_spec((1, M_HEAD_DIM))],
        out_specs=[blk, blk, blk, blk],
        out_shape=[jax.ShapeDtypeStruct((n, M_WIDTH), dt) for dt in (F32, F32, BF16, BF16)],
        compiler_params=_cparams(("arbitrary",)),
        name="mem_kv",
    )(mem, g_mem, w_mem, g_mk)


KEY_NEG_INF = np.int32(-0x7F800000)


def _key_to_f32(key):
    return pltpu.bitcast(jnp.where(key >= 0, key, INT_MIN - key), F32)


def _select_params(count, topk, pos_bits, j_ref):
    kf = float(topk)
    t0 = jnp.where(count(lambda s, p: s >= 0.0) >= kf, jnp.int32(0), INT_MIN)

    def bit_body(b, t):
        cand = t + lax.shift_left(jnp.int32(1), 30 - b)
        cand_f = _key_to_f32(cand)
        return jnp.where(count(lambda s, p: s >= cand_f) >= kf, cand, t)

    t = _key_to_f32(jnp.maximum(lax.fori_loop(0, 31, bit_body, t0), KEY_NEG_INF))
    tie = (count(lambda s, p: s >= t) > kf) & (t > NEG_INF)
    j_ref[...] = jnp.full(j_ref.shape, INT_MAX, I32)

    @pl.when(jnp.max(tie.astype(I32)) > 0)
    def _():
        n_gt = count(lambda s, p: s > t)

        def pos_body(b, p_lo):
            cand = p_lo + lax.shift_left(jnp.int32(1), pos_bits - 1 - b)
            n_eq = count(lambda s, p: (s == t) & (p < cand))
            return jnp.where(n_gt + n_eq < kf, cand, p_lo)

        p_sel = lax.fori_loop(0, pos_bits, pos_body, jnp.zeros(t.shape, I32))
        j_ref[...] = jnp.where(tie, p_sel, INT_MAX)

    return t, j_ref[...]


def _dsa_prompt_kernel(qi_ref, wit_ref, q_ref, ki_ref, k_ref, v_ref, o_ref,
                       sc_ref, bias_ref, vt_ref, j_ref, s_ref, acc_ref, *, topk, kc, pos_bits):
    i = pl.program_id(1)
    nck = (i * Q_BLOCK + Q_BLOCK + kc - 1) // kc
    nchunks = vt_ref.shape[0]
    rep = A_HEADS // A_KV_HEADS
    q_pos = i * Q_BLOCK + lax.broadcasted_iota(I32, (1, Q_BLOCK), 1)
    sub = lax.broadcasted_iota(I32, (kc, 1), 0)

    @pl.when(i == 0)
    def _():
        for c in range(nchunks):
            for g in range(A_KV_HEADS):
                gsl = slice(g * A_HEAD_DIM, (g + 1) * A_HEAD_DIM)
                vt_ref[c, gsl, :] = v_ref[c * kc:(c + 1) * kc, gsl].astype(F32).T.astype(BF16)

    def chunk_rows(c):
        return pl.ds(pl.multiple_of(c * kc, kc), kc)

    qi = qi_ref[...]
    wit = wit_ref[...]
    qi_pairs = [jnp.concatenate([qi[:, (2 * j) * IDX_DIM:(2 * j + 1) * IDX_DIM],
                                 qi[:, (2 * j + 1) * IDX_DIM:(2 * j + 2) * IDX_DIM]], axis=0)
                for j in range(IDX_HEADS // 2)]

    def score_body(c, carry):
        kic = ki_ref[chunk_rows(c), :]
        acc = jnp.zeros((kc, Q_BLOCK), F32)
        for j in range(IDX_HEADS // 2):
            d = _dot_nt(kic, qi_pairs[j])
            acc = acc + jnp.maximum(d[:, :Q_BLOCK], 0.0) * wit[2 * j:2 * j + 1, :]
            acc = acc + jnp.maximum(d[:, Q_BLOCK:], 0.0) * wit[2 * j + 1:2 * j + 2, :]
        sc_ref[c] = jnp.where(c * kc + sub <= q_pos, acc, NEG_INF)
        return carry

    lax.fori_loop(0, nck, score_body, 0)

    def count(pred):
        def body(c, acc):
            part = jnp.where(pred(sc_ref[c], c * kc + sub), 1.0, 0.0)
            return acc + jnp.sum(part.reshape(kc // 64, 64, Q_BLOCK), axis=0)

        acc = lax.fori_loop(0, nck, body, jnp.zeros((64, Q_BLOCK), F32))
        return jnp.sum(acc, axis=0, keepdims=True)

    thr, jsel = _select_params(count, topk, pos_bits, j_ref)

    def bias_body(c, carry):
        s = sc_ref[c]
        pos = c * kc + sub
        sel = ((s > thr) | ((s == thr) & (pos <= jsel))) & (pos <= q_pos)
        bias_ref[c] = jnp.where(sel, 0.0, NEG_INF)
        return carry

    lax.fori_loop(0, nck, bias_body, 0)

    gsl = lambda g: slice(g * A_HEAD_DIM, (g + 1) * A_HEAD_DIM)
    row0 = lambda v: jnp.full((1, rep * Q_BLOCK), v, F32)
    for gs in [(g0, g0 + 1) for g0 in range(0, A_KV_HEADS, 2)]:
        qgs = [jnp.concatenate([q_ref[:, gsl(g * rep + r)] for r in range(rep)], axis=0) for g in gs]

        def qk_body(c, ms, gs=gs, qgs=qgs):
            rows = chunk_rows(c)
            b = bias_ref[c]
            bias = jnp.concatenate([b] * rep, axis=1)
            out = []
            for u, g in enumerate(gs):
                s = _dot_nt(k_ref[rows, gsl(g)], qgs[u]) + bias
                s_ref[c, u] = s
                out.append(jnp.maximum(ms[u], jnp.max(s, axis=0, keepdims=True)))
            return tuple(out)

        ms = lax.fori_loop(0, nck, qk_body, (row0(NEG_INF),) * len(gs))
        acc_ref[...] = jnp.zeros(acc_ref.shape, F32)

        def pv_body(c, ls, gs=gs, ms=ms):
            out = []
            for u, g in enumerate(gs):
                p = jnp.exp2(s_ref[c, u] - ms[u])
                out.append(ls[u] + jnp.sum(p, axis=0, keepdims=True))
                acc_ref[u] += _dot(vt_ref[c, gsl(g), :], p.astype(BF16))
            return tuple(out)

        ls = lax.fori_loop(0, nck, pv_body, (row0(0.0),) * len(gs))
        for u, g in enumerate(gs):
            o = acc_ref[u] / ls[u]
            for r in range(rep):
                o_ref[:, gsl(g * rep + r)] = o[:, r * Q_BLOCK:(r + 1) * Q_BLOCK].T.astype(BF16)


def _dsa_prompt(qi, wit, q, ki_b, k_b, v_b, batch, seq, topk):
    nqb = seq // Q_BLOCK
    rep = A_HEADS // A_KV_HEADS
    kc = min(512, seq)
    nchunks = seq // kc
    pos_bits = max(1, int(np.ceil(np.log2(seq))))
    qrow = lambda w: pl.BlockSpec((Q_BLOCK, w), lambda b, i: (b * nqb + i, 0))
    seqblk = lambda w: pl.BlockSpec((seq, w), lambda b, i: (b, 0))
    return pl.pallas_call(
        functools.partial(_dsa_prompt_kernel, topk=topk, kc=kc, pos_bits=pos_bits),
        grid=(batch, nqb),
        in_specs=[qrow(IDX_HEADS * IDX_DIM), pl.BlockSpec((IDX_HEADS, Q_BLOCK), lambda b, i: (0, b * nqb + i)),
                  qrow(A_WIDTH), seqblk(IDX_DIM), seqblk(A_KV_WIDTH), seqblk(A_KV_WIDTH)],
        out_specs=qrow(A_WIDTH),
        out_shape=jax.ShapeDtypeStruct((batch * seq, A_WIDTH), BF16),
        scratch_shapes=[pltpu.VMEM((nchunks, kc, Q_BLOCK), F32), pltpu.VMEM((nchunks, kc, Q_BLOCK), F32),
                        pltpu.VMEM((nchunks, A_KV_WIDTH, kc), BF16), pltpu.VMEM((1, Q_BLOCK), I32),
                        pltpu.VMEM((nchunks, 2, kc, rep * Q_BLOCK), F32),
                        pltpu.VMEM((2, A_HEAD_DIM, rep * Q_BLOCK), F32)],
        compiler_params=_cparams(("arbitrary", "arbitrary")),
        name="dsa_prompt",
    )(qi, wit, q, ki_b, k_b, v_b)


def _fold_heads(e, nq):
    acc = e[0:8]
    for t in range(1, e.shape[0] // 8):
        acc = acc + e[8 * t:8 * (t + 1)]
    return acc + pltpu.roll(acc, nq, 0)


def _sample_scores_kernel(pt_ref, qp_ref, wc_ref, kin_ref, *refs, pages, nq):
    page_refs, (sc_ref, scn_ref) = refs[:pages], refs[pages:]
    qp = qp_ref[...]
    wc = wc_ref[...]
    kpt = jnp.concatenate([r[...] for r in page_refs], axis=1).astype(BF16)
    sc_ref[...] = _fold_heads(jnp.maximum(_dot(qp, kpt), 0.0) * wc, nq)

    @pl.when(pl.program_id(1) == 0)
    def _():
        kin = jnp.concatenate([kin_ref[...], jnp.zeros((PAGE - S_PAD, IDX_DIM), BF16)], axis=0)
        scn_ref[...] = _fold_heads(jnp.maximum(_dot_nt(qp, kin), 0.0) * wc, nq)


def _sample_scores(page_flat, qp, wcol, ki_new_b, cache_kidx, nseq, npages, pages, nq):
    steps = npages // pages
    rows = IDX_HEADS * nq
    page_specs = [pl.BlockSpec((None, IDX_DIM, PAGE),
                               lambda b, s, pt, p=p: (pt[b * npages + s * pages + p], 0, 0))
                  for p in range(pages)]
    grid_spec = pltpu.PrefetchScalarGridSpec(
        num_scalar_prefetch=1,
        grid=(nseq, steps),
        in_specs=[pl.BlockSpec((None, rows, IDX_DIM), lambda b, s, pt: (b, 0, 0)),
                  pl.BlockSpec((None, rows, 1), lambda b, s, pt: (b, 0, 0)),
                  pl.BlockSpec((S_PAD, IDX_DIM), lambda b, s, pt: (b, 0))] + page_specs,
        out_specs=[pl.BlockSpec((None, 8, pages * PAGE), lambda b, s, pt: (b, 0, s)),
                   pl.BlockSpec((None, 8, PAGE), lambda b, s, pt: (b, 0, 0))],
    )
    return pl.pallas_call(
        functools.partial(_sample_scores_kernel, pages=pages, nq=nq),
        grid_spec=grid_spec,
        out_shape=[jax.ShapeDtypeStruct((nseq, 8, npages * PAGE), F32),
                   jax.ShapeDtypeStruct((nseq, 8, PAGE), F32)],
        compiler_params=_cparams(("arbitrary", "arbitrary")),
        name="dsa_sample_scores",
    )(page_flat, qp, wcol, ki_new_b, *([cache_kidx] * pages))


def _sample_attn_kernel(pt_ref, sc_ref, scn_ref, qs_ref, kn_ref, vn_ref, *refs,
                        pages, nq, n_past, topk, pos_bits):
    k_pages, v_pages = refs[:pages], refs[pages:2 * pages]
    o_ref = refs[2 * pages]
    scm_ref, bias_ref, biasn_ref, j_ref, m_ref, l_ref, acc_ref, kbuf_ref, vbuf_ref = refs[2 * pages + 1:]
    s_idx = pl.program_id(1)
    kw = pages * PAGE
    rows_q = qs_ref.shape[1]

    @pl.when(s_idx == 0)
    def _():
        qrow = lax.broadcasted_iota(I32, (8, 1), 0) % nq
        lane_p = lax.broadcasted_iota(I32, (1, n_past), 1)
        lane_n = lax.broadcasted_iota(I32, (1, PAGE), 1)
        scm_ref[...] = jnp.where(lane_n <= qrow, scn_ref[...], NEG_INF)

        def count(pred):
            cp = jnp.sum(jnp.where(pred(sc_ref[...], lane_p), 1.0, 0.0), axis=-1, keepdims=True)
            cn = jnp.sum(jnp.where(pred(scm_ref[...], n_past + lane_n), 1.0, 0.0), axis=-1, keepdims=True)
            return cp + cn

        thr, jsel = _select_params(count, topk, pos_bits, j_ref)
        kp = sc_ref[...]
        bias_p = jnp.where((kp > thr) | ((kp == thr) & (lane_p <= jsel)), 0.0, NEG_INF)
        for st in range(n_past // kw):
            bias_ref[st] = bias_p[:, st * kw:(st + 1) * kw]
        kn = scm_ref[...]
        seln = ((kn > thr) | ((kn == thr) & (n_past + lane_n <= jsel))) & (lane_n <= qrow)
        biasn_ref[...] = jnp.where(seln, 0.0, NEG_INF)
        m_ref[...] = jnp.full(m_ref.shape, NEG_INF, F32)
        l_ref[...] = jnp.zeros(l_ref.shape, F32)
        acc_ref[...] = jnp.zeros(acc_ref.shape, F32)

    for p in range(pages):
        for g in range(A_KV_HEADS):
            gsl = slice(g * A_HEAD_DIM, (g + 1) * A_HEAD_DIM)
            rows = pl.ds(g, PAGE, stride=A_KV_HEADS)
            kbuf_ref[p * PAGE:(p + 1) * PAGE, gsl] = k_pages[p][rows, :].astype(BF16)
            vbuf_ref[p * PAGE:(p + 1) * PAGE, gsl] = v_pages[p][rows, :].astype(BF16)

    def update(g, s, vv):
        m = m_ref[g]
        m_new = jnp.maximum(m, jnp.max(s, axis=-1, keepdims=True))
        m_safe = jnp.where(m_new == NEG_INF, 0.0, m_new)
        alpha = jnp.exp2(m - m_safe)
        p = jnp.exp2(s - m_safe)
        l_ref[g] = alpha * l_ref[g] + jnp.sum(p, axis=-1, keepdims=True)
        acc_ref[g] = alpha * acc_ref[g] + _dot(p.astype(BF16), vv)
        m_ref[g] = m_new

    bias = jnp.concatenate([bias_ref[s_idx]] * (rows_q // 8), axis=0)
    for g in range(A_KV_HEADS):
        gsl = slice(g * A_HEAD_DIM, (g + 1) * A_HEAD_DIM)
        update(g, _dot_nt(qs_ref[g], kbuf_ref[:, gsl]) + bias, vbuf_ref[:, gsl])

    @pl.when(s_idx == pl.num_programs(1) - 1)
    def _():
        bn = jnp.concatenate([biasn_ref[:, :S_PAD]] * (rows_q // 8), axis=0)
        for g in range(A_KV_HEADS):
            gsl = slice(g * A_HEAD_DIM, (g + 1) * A_HEAD_DIM)
            update(g, _dot_nt(qs_ref[g], kn_ref[:, gsl]) + bn, vn_ref[:, gsl])
            o_ref[g] = acc_ref[g] / l_ref[g]


def _sample_attn(page_flat, sc, scn, qs, k_new_b, v_new_b, cache_k, cache_v, nseq, npages, pages, nq, topk):
    steps = npages // pages
    n_past = npages * PAGE
    kw = pages * PAGE
    rows_q = qs.shape[2]
    pos_bits = int(np.ceil(np.log2(n_past + PAGE)))
    page_spec = lambda p: pl.BlockSpec((None, PAGE * A_KV_HEADS, A_HEAD_DIM),
                                       lambda b, s, pt, p=p: (pt[b * npages + s * pages + p], 0, 0))
    grid_spec = pltpu.PrefetchScalarGridSpec(
        num_scalar_prefetch=1,
        grid=(nseq, steps),
        in_specs=[pl.BlockSpec((None, 8, n_past), lambda b, s, pt: (b, 0, 0)),
                  pl.BlockSpec((None, 8, PAGE), lambda b, s, pt: (b, 0, 0)),
                  pl.BlockSpec((None, A_KV_HEADS, rows_q, A_HEAD_DIM), lambda b, s, pt: (b, 0, 0, 0)),
                  pl.BlockSpec((S_PAD, A_KV_WIDTH), lambda b, s, pt: (b, 0)),
                  pl.BlockSpec((S_PAD, A_KV_WIDTH), lambda b, s, pt: (b, 0))]
                 + [page_spec(p) for p in range(pages)] + [page_spec(p) for p in range(pages)],
        out_specs=pl.BlockSpec((None, A_KV_HEADS, rows_q, A_HEAD_DIM), lambda b, s, pt: (b, 0, 0, 0)),
        scratch_shapes=[pltpu.VMEM((8, PAGE), F32),
                        pltpu.VMEM((steps, 8, kw), F32), pltpu.VMEM((8, PAGE), F32),
                        pltpu.VMEM((8, 1), I32),
                        pltpu.VMEM((A_KV_HEADS, rows_q, 1), F32), pltpu.VMEM((A_KV_HEADS, rows_q, 1), F32),
                        pltpu.VMEM((A_KV_HEADS, rows_q, A_HEAD_DIM), F32),
                        pltpu.VMEM((kw, A_KV_WIDTH), BF16), pltpu.VMEM((kw, A_KV_WIDTH), BF16)],
    )
    return pl.pallas_call(
        functools.partial(_sample_attn_kernel, pages=pages, nq=nq, n_past=n_past, topk=topk, pos_bits=pos_bits),
        grid_spec=grid_spec,
        out_shape=jax.ShapeDtypeStruct((nseq, A_KV_HEADS, rows_q, A_HEAD_DIM), F32),
        compiler_params=_cparams(("arbitrary", "arbitrary")),
        name="dsa_sample_attn",
    )(page_flat, sc, scn, qs, k_new_b, v_new_b, *([cache_k] * pages), *([cache_v] * pages))


def _merge_kernel(x_ref, oa_ref, ag_ref, bu_ref, bv_ref, bg_ref, cq_ref, cg_ref, ra_ref, rb_ref, rc_ref,
                  mk_ref, mv_ref, ws_ref, bs_ref, gsgu_ref, gmq_ref, wpa_ref, wpb_ref, wpc_ref, wout_ref,
                  y_ref, *maybe_vn_ref, tm, chunk):
    f32 = lambda r: r[...].astype(F32)
    silu = lambda t: t * jax.nn.sigmoid(t)

    vn = _rms(f32(bv_ref), gsgu_ref[...])
    if maybe_vn_ref:
        maybe_vn_ref[0][...] = vn
    vnb = vn.astype(BF16)
    bu = f32(bu_ref)
    tril = (lax.broadcasted_iota(I32, (chunk, chunk), 1) <= lax.broadcasted_iota(I32, (chunk, chunk), 0))
    ob_cols = []
    for g in range(B_GROUPS):
        wg = jnp.where(tril, ws_ref[g], 0.0).astype(BF16)
        gsl = slice(g * B_GROUP_DIM, (g + 1) * B_GROUP_DIM)
        parts = [_dot(wg, vnb[c * chunk:(c + 1) * chunk, gsl]) + bs_ref[:, g:g + 1] for c in range(tm // chunk)]
        ob_cols.append(parts[0] if len(parts) == 1 else jnp.concatenate(parts, axis=0))
    ob = bu * jnp.concatenate(ob_cols, axis=1)
    pb = _dot((ob * silu(f32(bg_ref))).astype(BF16), wpb_ref[...])

    cq = f32(cq_ref)
    oc_cols = []
    for hh in range(M_HEADS):
        hsl = slice(hh * M_HEAD_DIM, (hh + 1) * M_HEAD_DIM)
        qn = (_rms(cq[:, hsl], gmq_ref[...]) * (M_HEAD_DIM ** -0.5)).astype(BF16)
        s = _dot_nt(qn, mk_ref[:, hsl])
        p = jnp.exp(s - jnp.max(s, axis=-1, keepdims=True))
        oc_cols.append(_dot(p.astype(BF16), mv_ref[:, hsl]) / jnp.sum(p, axis=-1, keepdims=True))
    oc = jnp.concatenate(oc_cols, axis=1)
    pc = _dot((oc * silu(f32(cg_ref))).astype(BF16), wpc_ref[...])

    pa = _dot((f32(oa_ref) * silu(f32(ag_ref))).astype(BF16), wpa_ref[...])
    sig = jax.nn.sigmoid
    m = sig(f32(ra_ref)) * pa + sig(f32(rb_ref)) * pb + sig(f32(rc_ref)) * pc
    y_ref[...] = x_ref[...] + _dot(m.astype(BF16), wout_ref[...])


def _merge(x, oa, zr, mk_b, mv_b, ws, bs_t, g_sgu, g_mq, w_pa, w_pb, w_pc, w_out, tm, chunk, mem_map, emit_vn):
    n = x.shape[0]
    col = lambda w, j: pl.BlockSpec((tm, w), lambda i, j=j: (i, j))
    mem = pl.BlockSpec((N_MEM, M_WIDTH), mem_map)
    in_specs = [col(D_MODEL, 0), col(A_WIDTH, 0),
                col(1024, 0), col(1024, 1), col(1024, 2), col(1024, 3), col(1024, 4), col(1024, 5),
                col(2048, 3), col(2048, 4), col(2048, 5),
                mem, mem,
                _const_spec((B_GROUPS, chunk, chunk)), _const_spec((chunk, B_GROUPS)),
                _const_spec((1, B_WIDTH)), _const_spec((1, M_HEAD_DIM)),
                _const_spec((A_WIDTH, D_MODEL)), _const_spec((B_WIDTH, D_MODEL)),
                _const_spec((M_WIDTH, D_MODEL)), _const_spec((D_MODEL, D_MODEL))]
    out_specs = [col(D_MODEL, 0)]
    out_shape = [jax.ShapeDtypeStruct((n, D_MODEL), F32)]
    if emit_vn:
        out_specs.append(col(B_WIDTH, 0))
        out_shape.append(jax.ShapeDtypeStruct((n, B_WIDTH), F32))
    return pl.pallas_call(
        functools.partial(_merge_kernel, tm=tm, chunk=chunk),
        grid=(n // tm,),
        in_specs=in_specs,
        out_specs=out_specs,
        out_shape=out_shape,
        compiler_params=_cparams(("arbitrary",)),
        name="merge",
    )(x, oa, *([zr] * 9), mk_b, mv_b, ws, bs_t, g_sgu, g_mq, w_pa, w_pb, w_pc, w_out)


def _rope_tables(pos):
    pos = pos.astype(F32)[:, None]

    def cs(half):
        freq = ROPE_THETA ** (-jnp.arange(half, dtype=F32) / half)
        ang = pos * freq[None, :]
        return jnp.cos(ang), jnp.sin(ang)

    c, s = cs(A_HEAD_DIM // 2)
    ci, si = cs(IDX_DIM // 2)
    z = jnp.zeros_like(si)
    return (jnp.concatenate([c, c], axis=1), jnp.concatenate([-s, s], axis=1),
            jnp.concatenate([ci] * 4, axis=1), jnp.concatenate([-si, z, -si, z], axis=1),
            jnp.concatenate([z, si, z, si], axis=1))


def kernel(x_prompt, x_sample, cache_k, cache_v, cache_kidx, cache_mem_k, cache_mem_v, page_table,
           mem_prompt, g_pre, w_in, g_q, g_k, g_mq, g_mk, g_mem, w_mem_kv, g_sgu, w_s, b_s,
           w_pa, w_pb, w_pc, w_out):
    batch, seq, _ = x_prompt.shape
    nseq, nq, _ = x_sample.shape
    assert nq == 4 and nq <= S_PAD
    npages = page_table.shape[1]
    n_past = npages * PAGE
    n_pool = cache_k.shape[0]
    row2 = lambda a: a.reshape(1, -1)

    w_in_t = w_in.T
    w_a = w_in_t[:A_COLS].astype(BF16)
    w_r = w_in_t[OFF_REST:].astype(BF16)
    w_pa_b, w_pb_b, w_pc_b, w_out_b = (w.astype(BF16) for w in (w_pa, w_pb, w_pc, w_out))
    w_mem_b = w_mem_kv.astype(BF16)
    g_pre2, g_q2, g_k2, g_mq2, g_mk2, g_mem2, g_sgu2 = map(row2, (g_pre, g_q, g_k, g_mq, g_mk, g_mem, g_sgu))

    xp = x_prompt.reshape(batch * seq, D_MODEL)
    tm_a = min(256, seq)
    nblk = seq // tm_a
    tabs_p = _rope_tables(jnp.arange(seq))
    q, k_p, v_p, k_b, v_b, qi, ki_p, ki_b, wi = _proj_a(
        xp, g_pre2, w_a, g_q2, g_k2, tabs_p, tm_a, lambda i: (i % nblk, 0))
    zr = _proj_rest(xp, g_pre2, w_r, min(1024, seq), PROJ_TN)
    mk_p, mv_p, mk_b, mv_b = _mem_kv(mem_prompt.reshape(batch * N_MEM, D_MODEL), g_mem2, w_mem_b, g_mk2)
    oa = _dsa_prompt(qi, wi.T, q, ki_b, k_b, v_b, batch, seq, min(TOPK_MAX, seq // 4))
    tm_m = min(256, seq)
    nblk_m = seq // tm_m
    (y_p,) = _merge(xp, oa, zr, mk_b, mv_b, w_s, b_s.T, g_sgu2, g_mq2, w_pa_b, w_pb_b, w_pc_b, w_out_b,
                    tm_m, CHUNK, lambda i: (i // nblk_m, 0), False)

    xs = jnp.pad(x_sample, ((0, 0), (0, S_PAD - nq), (0, 0))).reshape(nseq * S_PAD, D_MODEL)
    tabs_s = tuple(jnp.tile(t, (nseq, 1)) for t in _rope_tables(n_past + jnp.arange(S_PAD)))
    rows_s = nseq * S_PAD
    q_s, k_s, v_s, k_sb, v_sb, qi_s, ki_s, ki_sb, wi_s = _proj_a(
        xs, g_pre2, w_a, g_q2, g_k2, tabs_s, rows_s, lambda i: (0, 0))
    zr_s = _proj_rest(xs, g_pre2, w_r, rows_s, PROJ_TN)

    qp = (qi_s.reshape(nseq, S_PAD, IDX_HEADS, IDX_DIM)[:, :nq]
          .transpose(0, 2, 1, 3).reshape(nseq, IDX_HEADS * nq, IDX_DIM))
    wcol = wi_s.reshape(nseq, S_PAD, IDX_HEADS)[:, :nq].transpose(0, 2, 1).reshape(nseq, IDX_HEADS * nq, 1)
    rep = A_HEADS // A_KV_HEADS
    qs = (q_s.reshape(nseq, S_PAD, A_KV_HEADS, rep, A_HEAD_DIM)[:, :nq]
          .transpose(0, 2, 3, 1, 4).reshape(nseq, A_KV_HEADS, rep * nq, A_HEAD_DIM))
    qs = jnp.concatenate([qs, qs], axis=2)
    page_flat = page_table.reshape(-1)
    pages = 8
    sc, scn = _sample_scores(page_flat, qp, wcol, ki_sb, jnp.swapaxes(cache_kidx, 1, 2), nseq, npages, pages, nq)
    o_s = _sample_attn(page_flat, sc, scn, qs, k_sb, v_sb,
                       cache_k.reshape(n_pool, PAGE * A_KV_HEADS, A_HEAD_DIM),
                       cache_v.reshape(n_pool, PAGE * A_KV_HEADS, A_HEAD_DIM),
                       nseq, npages, pages, nq, min(TOPK_MAX, (n_past + nq) // 4))
    oa_s = (o_s[:, :, :rep * nq].reshape(nseq, A_KV_HEADS, rep, nq, A_HEAD_DIM)
            .transpose(0, 3, 1, 2, 4).reshape(nseq, nq, A_WIDTH))
    oa_s = jnp.pad(oa_s, ((0, 0), (0, S_PAD - nq), (0, 0))).reshape(rows_s, A_WIDTH).astype(BF16)
    mk_s = cache_mem_k.reshape(nseq * N_MEM, M_WIDTH).astype(BF16)
    mv_s = cache_mem_v.reshape(nseq * N_MEM, M_WIDTH).astype(BF16)
    y_s, vn_s = _merge(xs, oa_s, zr_s, mk_s, mv_s, w_s[:, :S_PAD, :S_PAD], b_s[:, :S_PAD].T, g_sgu2, g_mq2,
                       w_pa_b, w_pb_b, w_pc_b, w_out_b, S_PAD, S_PAD, lambda i: (i, 0), True)

    take = lambda a, shape: a.reshape(nseq, S_PAD, -1)[:, :nq].reshape(shape)
    return (y_p.reshape(batch, seq, D_MODEL),
            take(y_s, (nseq, nq, D_MODEL)),
            k_p.reshape(batch, seq, A_KV_HEADS, A_HEAD_DIM),
            v_p.reshape(batch, seq, A_KV_HEADS, A_HEAD_DIM),
            ki_p.reshape(batch, seq, IDX_DIM),
            mk_p.reshape(batch, N_MEM, M_HEADS, M_HEAD_DIM),
            mv_p.reshape(batch, N_MEM, M_HEADS, M_HEAD_DIM),
            take(k_s, (nseq, nq, A_KV_HEADS, A_HEAD_DIM)),
            take(v_s, (nseq, nq, A_KV_HEADS, A_HEAD_DIM)),
            take(ki_s, (nseq, nq, IDX_DIM)),
            take(vn_s, (nseq, nq, B_GROUPS, B_GROUP_DIM)))
```

```python
import functools

import numpy as np
import jax
import jax.numpy as jnp
from jax import lax
from jax.experimental import pallas as pl
from jax.experimental.pallas import tpu as pltpu

F32 = jnp.float32
BF16 = jnp.bfloat16
I32 = jnp.int32

D_MODEL = 2048
PAGE = 128
A_HEADS = 8
A_KV_HEADS = 4
A_HEAD_DIM = 128
A_WIDTH = A_HEADS * A_HEAD_DIM
A_KV_WIDTH = A_KV_HEADS * A_HEAD_DIM
IDX_HEADS = 16
IDX_DIM = 64
TOPK_MAX = 256
Q_BLOCK = 128
ROPE_THETA = 10000.0
CHUNK = 128
B_GROUPS = 8
B_GROUP_DIM = 128
B_WIDTH = B_GROUPS * B_GROUP_DIM
N_MEM = 256
M_HEADS = 4
M_HEAD_DIM = 256
M_WIDTH = M_HEADS * M_HEAD_DIM
EPS = 1e-6

OFF_K = A_WIDTH
OFF_V = OFF_K + A_KV_WIDTH
OFF_QI = OFF_V + A_KV_WIDTH
OFF_KI = OFF_QI + IDX_HEADS * IDX_DIM
OFF_WI = OFF_KI + IDX_DIM
OFF_REST = OFF_WI + IDX_HEADS
A_COLS = 3200
REST_COLS = A_WIDTH + 3 * B_WIDTH + 2 * M_WIDTH + 3 * D_MODEL

Q_SCALE = float(np.log2(np.e)) * A_HEAD_DIM ** -0.5
S_PAD = 16
PROJ_TN = 1024
ATT_GROUPS = 4
INT_MIN = np.int32(-2 ** 31)
INT_MAX = np.int32(2 ** 31 - 1)
NEG_INF = float("-inf")

V7X_VMEM_LIMIT = 56 * 1024 * 1024


def _cparams(sem):
    return pltpu.CompilerParams(dimension_semantics=sem, vmem_limit_bytes=V7X_VMEM_LIMIT)


def _dot(a, b):
    return jnp.dot(a, b, preferred_element_type=F32)


def _dot_nt(a, b):
    return lax.dot_general(a, b, (((1,), (1,)), ((), ())), preferred_element_type=F32)


def _rms(x, g):
    return x * lax.rsqrt(jnp.mean(x * x, axis=-1, keepdims=True) + EPS) * g


def _const_spec(shape):
    nd = len(shape)
    return pl.BlockSpec(shape, lambda *_: (0,) * nd, pipeline_mode=pl.Buffered(1))


def _proj_a_kernel(x_ref, g_ref, w_ref, gq_ref, gk_ref, cq_ref, sq_ref, ci_ref, sia_ref, sib_ref,
                   q_ref, k_ref, v_ref, kb_ref, vb_ref, qi_ref, ki_ref, kib_ref, wi_ref):
    h = _rms(x_ref[...], g_ref[...]).astype(BF16)
    z = _dot_nt(h, w_ref[...])
    tm = z.shape[0]
    cq, sq = cq_ref[...], sq_ref[...]
    ci, sia, sib = ci_ref[...], sia_ref[...], sib_ref[...]

    def norm_rope(zz, g):
        n = _rms(zz, g)
        return n * cq + pltpu.roll(n, A_HEAD_DIM // 2, 1) * sq

    def rope_idx(zz):
        return zz * ci + pltpu.roll(zz, 96, 1) * sia + pltpu.roll(zz, 32, 1) * sib

    for hh in range(A_HEADS):
        sl = slice(hh * A_HEAD_DIM, (hh + 1) * A_HEAD_DIM)
        q_ref[:, sl] = (norm_rope(z[:, sl], gq_ref[...]) * Q_SCALE).astype(BF16)
    for hh in range(A_KV_HEADS):
        sl = slice(hh * A_HEAD_DIM, (hh + 1) * A_HEAD_DIM)
        kh = norm_rope(z[:, OFF_K + hh * A_HEAD_DIM:OFF_K + (hh + 1) * A_HEAD_DIM], gk_ref[...])
        vh = z[:, OFF_V + hh * A_HEAD_DIM:OFF_V + (hh + 1) * A_HEAD_DIM]
        head_rows = pl.ds(hh, tm, stride=A_KV_HEADS)
        k_ref[head_rows, :] = kh
        v_ref[head_rows, :] = vh
        kb_ref[:, sl] = kh.astype(BF16)
        vb_ref[:, sl] = vh.astype(BF16)
    for t in range(IDX_HEADS * IDX_DIM // 128):
        sl = slice(t * 128, (t + 1) * 128)
        qi_ref[:, sl] = rope_idx(z[:, OFF_QI + t * 128:OFF_QI + (t + 1) * 128]).astype(BF16)
    last = z[:, OFF_KI:OFF_KI + 128]
    ki = rope_idx(last)[:, :IDX_DIM]
    ki_ref[...] = ki
    kib_ref[...] = ki.astype(BF16)
    wi_ref[...] = last[:, IDX_DIM:IDX_DIM + IDX_HEADS] * ((IDX_HEADS ** -0.5) * (IDX_DIM ** -0.5))


def _proj_a(x, g_pre, w_a, g_q, g_k, tabs, tm, tab_map):
    n = x.shape[0]
    row = lambda w: pl.BlockSpec((tm, w), lambda i: (i, 0))
    tab = pl.BlockSpec((tm, 128), tab_map)
    outs = [(1, A_WIDTH, BF16), (A_KV_HEADS, A_HEAD_DIM, F32), (A_KV_HEADS, A_HEAD_DIM, F32),
            (1, A_KV_WIDTH, BF16), (1, A_KV_WIDTH, BF16),
            (1, IDX_HEADS * IDX_DIM, BF16), (1, IDX_DIM, F32), (1, IDX_DIM, BF16), (1, IDX_HEADS, F32)]
    return pl.pallas_call(
        _proj_a_kernel,
        grid=(n // tm,),
        in_specs=[row(D_MODEL), _const_spec((1, D_MODEL)), _const_spec((A_COLS, D_MODEL)),
                  _const_spec((1, A_HEAD_DIM)), _const_spec((1, A_HEAD_DIM)), tab, tab, tab, tab, tab],
        out_specs=[pl.BlockSpec((tm * r, w), lambda i: (i, 0)) for r, w, _ in outs],
        out_shape=[jax.ShapeDtypeStruct((n * r, w), dt) for r, w, dt in outs],
        compiler_params=_cparams(("arbitrary",)),
        name="proj_a",
    )(x, g_pre, w_a, g_q, g_k, *tabs)


def _proj_rest_kernel(x_ref, g_ref, w_ref, o_ref, h_ref):
    @pl.when(pl.program_id(1) == 0)
    def _():
        h_ref[...] = _rms(x_ref[...], g_ref[...]).astype(BF16)

    o_ref[...] = _dot_nt(h_ref[...], w_ref[...].astype(BF16)).astype(BF16)


def _proj_rest(x, g_pre, w_in_t, tm, tn):
    n = x.shape[0]
    return pl.pallas_call(
        _proj_rest_kernel,
        grid=(n // tm, REST_COLS // tn),
        in_specs=[pl.BlockSpec((tm, D_MODEL), lambda i, j: (i, 0), pipeline_mode=pl.Buffered(1)),
                  pl.BlockSpec((1, D_MODEL), lambda i, j: (0, 0)),
                  pl.BlockSpec((pl.Element(tn), pl.Element(D_MODEL)), lambda i, j: (pl.multiple_of(OFF_REST + j * tn, 16), 0))],
        out_specs=pl.BlockSpec((tm, tn), lambda i, j: (i, j)),
        out_shape=jax.ShapeDtypeStruct((n, REST_COLS), BF16),
        scratch_shapes=[pltpu.VMEM((tm, D_MODEL), BF16)],
        compiler_params=_cparams(("arbitrary", "arbitrary")),
        name="proj_rest",
    )(x, g_pre, w_in_t)


def _mem_kv_kernel(x_ref, g_ref, w_ref, gk_ref, k_ref, v_ref, kb_ref, vb_ref):
    h = _rms(x_ref[...], g_ref[...]).astype(BF16)
    z = _dot(h, w_ref[...])
    for hh in range(M_HEADS):
        sl = slice(hh * M_HEAD_DIM, (hh + 1) * M_HEAD_DIM)
        kh = _rms(z[:, sl], gk_ref[...])
        k_ref[:, sl] = kh
        kb_ref[:, sl] = kh.astype(BF16)
    v = z[:, M_WIDTH:]
    v_ref[...] = v
    vb_ref[...] = v.astype(BF16)


def _mem_kv(mem, g_mem, w_mem, g_mk):
    n = mem.shape[0]
    blk = pl.BlockSpec((N_MEM, M_WIDTH), lambda i: (i, 0))
    return pl.pallas_call(
        _mem_kv_kernel,
        grid=(n // N_MEM,),
        in_specs=[pl.BlockSpec((N_MEM, D_MODEL), lambda i: (i, 0)), _const_spec((1, D_MODEL)),
                  _const_spec((D_MODEL, 2 * M_WIDTH)), _const_spec((1, M_HEAD_DIM))],
        out_specs=[blk, blk, blk, blk],
        out_shape=[jax.ShapeDtypeStruct((n, M_WIDTH), dt) for dt in (F32, F32, BF16, BF16)],
        compiler_params=_cparams(("arbitrary",)),
        name="mem_kv",
    )(mem, g_mem, w_mem, g_mk)


KEY_NEG_INF = np.int32(-0x7F800000)


def _key_to_f32(key):
    return pltpu.bitcast(jnp.where(key >= 0, key, INT_MIN - key), F32)


def _select_params(count, topk, pos_bits, j_ref):
    kf = float(topk)
    t0 = jnp.where(count(lambda s, p: s >= 0.0) >= kf, jnp.int32(0), INT_MIN)

    def bit_body(b, t):
        cand = t + lax.shift_left(jnp.int32(1), 30 - b)
        cand_f = _key_to_f32(cand)
        return jnp.where(count(lambda s, p: s >= cand_f) >= kf, cand, t)

    t = _key_to_f32(jnp.maximum(lax.fori_loop(0, 31, bit_body, t0), KEY_NEG_INF))
    tie = (count(lambda s, p: s >= t) > kf) & (t > NEG_INF)
    j_ref[...] = jnp.full(j_ref.shape, INT_MAX, I32)

    @pl.when(jnp.max(tie.astype(I32)) > 0)
    def _():
        n_gt = count(lambda s, p: s > t)

        def pos_body(b, p_lo):
            cand = p_lo + lax.shift_left(jnp.int32(1), pos_bits - 1 - b)
            n_eq = count(lambda s, p: (s == t) & (p < cand))
            return jnp.where(n_gt + n_eq < kf, cand, p_lo)

        p_sel = lax.fori_loop(0, pos_bits, pos_body, jnp.zeros(t.shape, I32))
        j_ref[...] = jnp.where(tie, p_sel, INT_MAX)

    return t, j_ref[...]


def _dsa_prompt_kernel(qi_ref, wit_ref, q_ref, ki_ref, k_ref, v_ref, o_ref,
                       sc_ref, bias_ref, vt_ref, j_ref, s_ref, acc_ref, *, topk, kc, pos_bits):
    i = pl.program_id(1)
    nck = (i * Q_BLOCK + Q_BLOCK + kc - 1) // kc
    nchunks = vt_ref.shape[0]
    rep = A_HEADS // A_KV_HEADS
    q_pos = i * Q_BLOCK + lax.broadcasted_iota(I32, (1, Q_BLOCK), 1)
    sub = lax.broadcasted_iota(I32, (kc, 1), 0)

    @pl.when(i == 0)
    def _():
        for c in range(nchunks):
            for g in range(A_KV_HEADS):
                gsl = slice(g * A_HEAD_DIM, (g + 1) * A_HEAD_DIM)
                vt_ref[c, gsl, :] = v_ref[c * kc:(c + 1) * kc, gsl].astype(F32).T.astype(BF16)

    def chunk_rows(c):
        return pl.ds(pl.multiple_of(c * kc, kc), kc)

    qi = qi_ref[...]
    wit = wit_ref[...]
    qi_pairs = [jnp.concatenate([qi[:, (2 * j) * IDX_DIM:(2 * j + 1) * IDX_DIM],
                                 qi[:, (2 * j + 1) * IDX_DIM:(2 * j + 2) * IDX_DIM]], axis=0)
                for j in range(IDX_HEADS // 2)]

    def score_body(c, carry):
        kic = ki_ref[chunk_rows(c), :]
        acc = jnp.zeros((kc, Q_BLOCK), F32)
        for j in range(IDX_HEADS // 2):
            d = _dot_nt(kic, qi_pairs[j])
            acc = acc + jnp.maximum(d[:, :Q_BLOCK], 0.0) * wit[2 * j:2 * j + 1, :]
            acc = acc + jnp.maximum(d[:, Q_BLOCK:], 0.0) * wit[2 * j + 1:2 * j + 2, :]
        sc_ref[c] = jnp.where(c * kc + sub <= q_pos, acc, NEG_INF)
        return carry

    lax.fori_loop(0, nck, score_body, 0)

    def count(pred):
        def body(c, acc):
            part = jnp.where(pred(sc_ref[c], c * kc + sub), 1.0, 0.0)
            return acc + jnp.sum(part.reshape(kc // 64, 64, Q_BLOCK), axis=0)

        acc = lax.fori_loop(0, nck, body, jnp.zeros((64, Q_BLOCK), F32))
        return jnp.sum(acc, axis=0, keepdims=True)

    thr, jsel = _select_params(count, topk, pos_bits, j_ref)

    def bias_body(c, carry):
        s = sc_ref[c]
        pos = c * kc + sub
        sel = ((s > thr) | ((s == thr) & (pos <= jsel))) & (pos <= q_pos)
        bias_ref[c] = jnp.where(sel, 0.0, NEG_INF)
        return carry

    lax.fori_loop(0, nck, bias_body, 0)

    gsl = lambda g: slice(g * A_HEAD_DIM, (g + 1) * A_HEAD_DIM)
    row0 = lambda v: jnp.full((1, rep * Q_BLOCK), v, F32)
    for gs in [tuple(range(g0, g0 + ATT_GROUPS)) for g0 in range(0, A_KV_HEADS, ATT_GROUPS)]:
        qgs = [jnp.concatenate([q_ref[:, gsl(g * rep + r)] for r in range(rep)], axis=0) for g in gs]

        def qk_body(c, ms, gs=gs, qgs=qgs):
            rows = chunk_rows(c)
            b = bias_ref[c]
            bias = jnp.concatenate([b] * rep, axis=1)
            out = []
            for u, g in enumerate(gs):
                s = _dot_nt(k_ref[rows, gsl(g)], qgs[u]) + bias
                s_ref[c, u] = s
                out.append(jnp.maximum(ms[u], jnp.max(s, axis=0, keepdims=True)))
            return tuple(out)

        ms = lax.fori_loop(0, nck, qk_body, (row0(NEG_INF),) * len(gs))
        acc_ref[...] = jnp.zeros(acc_ref.shape, F32)

        def pv_body(c, ls, gs=gs, ms=ms):
            out = []
            for u, g in enumerate(gs):
                p = jnp.exp2(s_ref[c, u] - ms[u])
                out.append(ls[u] + jnp.sum(p, axis=0, keepdims=True))
                acc_ref[u] += _dot(vt_ref[c, gsl(g), :], p.astype(BF16))
            return tuple(out)

        ls = lax.fori_loop(0, nck, pv_body, (row0(0.0),) * len(gs))
        for u, g in enumerate(gs):
            o = acc_ref[u] / ls[u]
            for r in range(rep):
                o_ref[:, gsl(g * rep + r)] = o[:, r * Q_BLOCK:(r + 1) * Q_BLOCK].T.astype(BF16)


def _dsa_prompt(qi, wit, q, ki_b, k_b, v_b, batch, seq, topk):
    nqb = seq // Q_BLOCK
    rep = A_HEADS // A_KV_HEADS
    kc = min(512, seq)
    nchunks = seq // kc
    pos_bits = max(1, int(np.ceil(np.log2(seq))))
    qrow = lambda w: pl.BlockSpec((Q_BLOCK, w), lambda b, i: (b * nqb + i, 0))
    seqblk = lambda w: pl.BlockSpec((seq, w), lambda b, i: (b, 0), pipeline_mode=pl.Buffered(1))
    return pl.pallas_call(
        functools.partial(_dsa_prompt_kernel, topk=topk, kc=kc, pos_bits=pos_bits),
        grid=(batch, nqb),
        in_specs=[qrow(IDX_HEADS * IDX_DIM), pl.BlockSpec((IDX_HEADS, Q_BLOCK), lambda b, i: (0, b * nqb + i)),
                  qrow(A_WIDTH), seqblk(IDX_DIM), seqblk(A_KV_WIDTH), seqblk(A_KV_WIDTH)],
        out_specs=qrow(A_WIDTH),
        out_shape=jax.ShapeDtypeStruct((batch * seq, A_WIDTH), BF16),
        scratch_shapes=[pltpu.VMEM((nchunks, kc, Q_BLOCK), F32), pltpu.VMEM((nchunks, kc, Q_BLOCK), F32),
                        pltpu.VMEM((nchunks, A_KV_WIDTH, kc), BF16), pltpu.VMEM((1, Q_BLOCK), I32),
                        pltpu.VMEM((nchunks, ATT_GROUPS, kc, rep * Q_BLOCK), F32),
                        pltpu.VMEM((ATT_GROUPS, A_HEAD_DIM, rep * Q_BLOCK), F32)],
        compiler_params=_cparams(("arbitrary", "arbitrary")),
        name="dsa_prompt",
    )(qi, wit, q, ki_b, k_b, v_b)


def _fold_heads(e, nq):
    acc = e[0:8]
    for t in range(1, e.shape[0] // 8):
        acc = acc + e[8 * t:8 * (t + 1)]
    return acc + pltpu.roll(acc, nq, 0)


def _sample_scores_kernel(pt_ref, qp_ref, wc_ref, kin_ref, *refs, pages, nq):
    page_refs, (sc_ref, scn_ref) = refs[:pages], refs[pages:]
    qp = qp_ref[...]
    wc = wc_ref[...]
    kpt = jnp.concatenate([r[...] for r in page_refs], axis=1).astype(BF16)
    sc_ref[...] = _fold_heads(jnp.maximum(_dot(qp, kpt), 0.0) * wc, nq)

    @pl.when(pl.program_id(1) == 0)
    def _():
        kin = jnp.concatenate([kin_ref[...], jnp.zeros((PAGE - S_PAD, IDX_DIM), BF16)], axis=0)
        scn_ref[...] = _fold_heads(jnp.maximum(_dot_nt(qp, kin), 0.0) * wc, nq)


def _sample_scores(page_flat, qp, wcol, ki_new_b, cache_kidx, nseq, npages, pages, nq):
    steps = npages // pages
    rows = IDX_HEADS * nq
    page_specs = [pl.BlockSpec((None, IDX_DIM, PAGE),
                               lambda b, s, pt, p=p: (pt[b * npages + s * pages + p], 0, 0))
                  for p in range(pages)]
    grid_spec = pltpu.PrefetchScalarGridSpec(
        num_scalar_prefetch=1,
        grid=(nseq, steps),
        in_specs=[pl.BlockSpec((None, rows, IDX_DIM), lambda b, s, pt: (b, 0, 0)),
                  pl.BlockSpec((None, rows, 1), lambda b, s, pt: (b, 0, 0)),
                  pl.BlockSpec((S_PAD, IDX_DIM), lambda b, s, pt: (b, 0))] + page_specs,
        out_specs=[pl.BlockSpec((None, 8, pages * PAGE), lambda b, s, pt: (b, 0, s)),
                   pl.BlockSpec((None, 8, PAGE), lambda b, s, pt: (b, 0, 0))],
    )
    return pl.pallas_call(
        functools.partial(_sample_scores_kernel, pages=pages, nq=nq),
        grid_spec=grid_spec,
        out_shape=[jax.ShapeDtypeStruct((nseq, 8, npages * PAGE), F32),
                   jax.ShapeDtypeStruct((nseq, 8, PAGE), F32)],
        compiler_params=_cparams(("arbitrary", "arbitrary")),
        name="dsa_sample_scores",
    )(page_flat, qp, wcol, ki_new_b, *([cache_kidx] * pages))


def _sample_attn_kernel(pt_ref, sc_ref, scn_ref, qs_ref, kn_ref, vn_ref, *refs,
                        pages, nq, n_past, topk, pos_bits):
    k_pages, v_pages = refs[:pages], refs[pages:2 * pages]
    o_ref = refs[2 * pages]
    scm_ref, bias_ref, biasn_ref, j_ref, m_ref, l_ref, acc_ref, kbuf_ref, vbuf_ref = refs[2 * pages + 1:]
    s_idx = pl.program_id(1)
    kw = pages * PAGE
    rows_q = qs_ref.shape[1]

    @pl.when(s_idx == 0)
    def _():
        qrow = lax.broadcasted_iota(I32, (8, 1), 0) % nq
        lane_p = lax.broadcasted_iota(I32, (1, n_past), 1)
        lane_n = lax.broadcasted_iota(I32, (1, PAGE), 1)
        scm_ref[...] = jnp.where(lane_n <= qrow, scn_ref[...], NEG_INF)

        def count(pred):
            ones = jnp.where(pred(sc_ref[...], lane_p), 1.0, 0.0)
            parts = [ones[:, t * kw:(t + 1) * kw] for t in range(n_past // kw)]
            while len(parts) > 1:
                parts = [a + b for a, b in zip(parts[::2], parts[1::2])] + parts[len(parts) & ~1:]
            cp = jnp.sum(parts[0], axis=-1, keepdims=True)
            cn = jnp.sum(jnp.where(pred(scm_ref[...], n_past + lane_n), 1.0, 0.0), axis=-1, keepdims=True)
            return cp + cn

        thr, jsel = _select_params(count, topk, pos_bits, j_ref)
        kp = sc_ref[...]
        bias_p = jnp.where((kp > thr) | ((kp == thr) & (lane_p <= jsel)), 0.0, NEG_INF)
        for st in range(n_past // kw):
            bias_ref[st] = bias_p[:, st * kw:(st + 1) * kw]
        kn = scm_ref[...]
        seln = ((kn > thr) | ((kn == thr) & (n_past + lane_n <= jsel))) & (lane_n <= qrow)
        biasn_ref[...] = jnp.where(seln, 0.0, NEG_INF)
        m_ref[...] = jnp.full(m_ref.shape, NEG_INF, F32)
        l_ref[...] = jnp.zeros(l_ref.shape, F32)
        acc_ref[...] = jnp.zeros(acc_ref.shape, F32)

    for p in range(pages):
        for g in range(A_KV_HEADS):
            gsl = slice(g * A_HEAD_DIM, (g + 1) * A_HEAD_DIM)
            rows = pl.ds(g, PAGE, stride=A_KV_HEADS)
            kbuf_ref[p * PAGE:(p + 1) * PAGE, gsl] = k_pages[p][rows, :].astype(BF16)
            vbuf_ref[p * PAGE:(p + 1) * PAGE, gsl] = v_pages[p][rows, :].astype(BF16)

    def update(g, s, vv):
        m = m_ref[g]
        m_new = jnp.maximum(m, jnp.max(s, axis=-1, keepdims=True))
        m_safe = jnp.where(m_new == NEG_INF, 0.0, m_new)
        alpha = jnp.exp2(m - m_safe)
        p = jnp.exp2(s - m_safe)
        l_ref[g] = alpha * l_ref[g] + jnp.sum(p, axis=-1, keepdims=True)
        acc_ref[g] = alpha * acc_ref[g] + _dot(p.astype(BF16), vv)
        m_ref[g] = m_new

    bias = jnp.concatenate([bias_ref[s_idx]] * (rows_q // 8), axis=0)
    for g in range(A_KV_HEADS):
        gsl = slice(g * A_HEAD_DIM, (g + 1) * A_HEAD_DIM)
        update(g, _dot_nt(qs_ref[g], kbuf_ref[:, gsl]) + bias, vbuf_ref[:, gsl])

    @pl.when(s_idx == pl.num_programs(1) - 1)
    def _():
        bn = jnp.concatenate([biasn_ref[:, :S_PAD]] * (rows_q // 8), axis=0)
        for g in range(A_KV_HEADS):
            gsl = slice(g * A_HEAD_DIM, (g + 1) * A_HEAD_DIM)
            update(g, _dot_nt(qs_ref[g], kn_ref[:, gsl]) + bn, vn_ref[:, gsl])
            o_ref[g] = acc_ref[g] / l_ref[g]


def _sample_attn(page_flat, sc, scn, qs, k_new_b, v_new_b, cache_k, cache_v, nseq, npages, pages, nq, topk):
    steps = npages // pages
    n_past = npages * PAGE
    kw = pages * PAGE
    rows_q = qs.shape[2]
    pos_bits = int(np.ceil(np.log2(n_past + PAGE)))
    page_spec = lambda p: pl.BlockSpec((None, PAGE * A_KV_HEADS, A_HEAD_DIM),
                                       lambda b, s, pt, p=p: (pt[b * npages + s * pages + p], 0, 0))
    grid_spec = pltpu.PrefetchScalarGridSpec(
        num_scalar_prefetch=1,
        grid=(nseq, steps),
        in_specs=[pl.BlockSpec((None, 8, n_past), lambda b, s, pt: (b, 0, 0)),
                  pl.BlockSpec((None, 8, PAGE), lambda b, s, pt: (b, 0, 0)),
                  pl.BlockSpec((None, A_KV_HEADS, rows_q, A_HEAD_DIM), lambda b, s, pt: (b, 0, 0, 0)),
                  pl.BlockSpec((S_PAD, A_KV_WIDTH), lambda b, s, pt: (b, 0)),
                  pl.BlockSpec((S_PAD, A_KV_WIDTH), lambda b, s, pt: (b, 0))]
                 + [page_spec(p) for p in range(pages)] + [page_spec(p) for p in range(pages)],
        out_specs=pl.BlockSpec((None, A_KV_HEADS, rows_q, A_HEAD_DIM), lambda b, s, pt: (b, 0, 0, 0)),
        scratch_shapes=[pltpu.VMEM((8, PAGE), F32),
                        pltpu.VMEM((steps, 8, kw), F32), pltpu.VMEM((8, PAGE), F32),
                        pltpu.VMEM((8, 1), I32),
                        pltpu.VMEM((A_KV_HEADS, rows_q, 1), F32), pltpu.VMEM((A_KV_HEADS, rows_q, 1), F32),
                        pltpu.VMEM((A_KV_HEADS, rows_q, A_HEAD_DIM), F32),
                        pltpu.VMEM((kw, A_KV_WIDTH), BF16), pltpu.VMEM((kw, A_KV_WIDTH), BF16)],
    )
    return pl.pallas_call(
        functools.partial(_sample_attn_kernel, pages=pages, nq=nq, n_past=n_past, topk=topk, pos_bits=pos_bits),
        grid_spec=grid_spec,
        out_shape=jax.ShapeDtypeStruct((nseq, A_KV_HEADS, rows_q, A_HEAD_DIM), F32),
        compiler_params=_cparams(("arbitrary", "arbitrary")),
        name="dsa_sample_attn",
    )(page_flat, sc, scn, qs, k_new_b, v_new_b, *([cache_k] * pages), *([cache_v] * pages))


def _merge_kernel(x_ref, oa_ref, ag_ref, bu_ref, bv_ref, bg_ref, cq_ref, cg_ref, ra_ref, rb_ref, rc_ref,
                  mk_ref, mv_ref, ws_ref, bs_ref, gsgu_ref, gmq_ref, wpa_ref, wpb_ref, wpc_ref, wout_ref,
                  y_ref, *maybe_vn_ref, tm, chunk):
    f32 = lambda r: r[...].astype(F32)
    silu = lambda t: t * jax.nn.sigmoid(t)

    vn = _rms(f32(bv_ref), gsgu_ref[...])
    if maybe_vn_ref:
        maybe_vn_ref[0][...] = vn
    vnb = vn.astype(BF16)
    bu = f32(bu_ref)
    tril = (lax.broadcasted_iota(I32, (chunk, chunk), 1) <= lax.broadcasted_iota(I32, (chunk, chunk), 0))
    ob_cols = []
    for g in range(B_GROUPS):
        wg = jnp.where(tril, ws_ref[g], 0.0).astype(BF16)
        gsl = slice(g * B_GROUP_DIM, (g + 1) * B_GROUP_DIM)
        parts = [_dot(wg, vnb[c * chunk:(c + 1) * chunk, gsl]) + bs_ref[:, g:g + 1] for c in range(tm // chunk)]
        ob_cols.append(parts[0] if len(parts) == 1 else jnp.concatenate(parts, axis=0))
    ob = bu * jnp.concatenate(ob_cols, axis=1)
    pb = _dot((ob * silu(f32(bg_ref))).astype(BF16), wpb_ref[...])

    cq = f32(cq_ref)
    oc_cols = []
    for hh in range(M_HEADS):
        hsl = slice(hh * M_HEAD_DIM, (hh + 1) * M_HEAD_DIM)
        qn = (_rms(cq[:, hsl], gmq_ref[...]) * (M_HEAD_DIM ** -0.5)).astype(BF16)
        s = _dot_nt(qn, mk_ref[:, hsl])
        p = jnp.exp(s - jnp.max(s, axis=-1, keepdims=True))
        oc_cols.append(_dot(p.astype(BF16), mv_ref[:, hsl]) / jnp.sum(p, axis=-1, keepdims=True))
    oc = jnp.concatenate(oc_cols, axis=1)
    pc = _dot((oc * silu(f32(cg_ref))).astype(BF16), wpc_ref[...])

    pa = _dot((f32(oa_ref) * silu(f32(ag_ref))).astype(BF16), wpa_ref[...])
    sig = jax.nn.sigmoid
    m = sig(f32(ra_ref)) * pa + sig(f32(rb_ref)) * pb + sig(f32(rc_ref)) * pc
    y_ref[...] = x_ref[...] + _dot(m.astype(BF16), wout_ref[...])


def _merge(x, oa, zr, mk_b, mv_b, ws, bs_t, g_sgu, g_mq, w_pa, w_pb, w_pc, w_out, tm, chunk, mem_map, emit_vn):
    n = x.shape[0]
    col = lambda w, j: pl.BlockSpec((tm, w), lambda i, j=j: (i, j))
    mem = pl.BlockSpec((N_MEM, M_WIDTH), mem_map)
    in_specs = [col(D_MODEL, 0), col(A_WIDTH, 0),
                col(1024, 0), col(1024, 1), col(1024, 2), col(1024, 3), col(1024, 4), col(1024, 5),
                col(2048, 3), col(2048, 4), col(2048, 5),
                mem, mem,
                _const_spec((B_GROUPS, chunk, chunk)), _const_spec((chunk, B_GROUPS)),
                _const_spec((1, B_WIDTH)), _const_spec((1, M_HEAD_DIM)),
                _const_spec((A_WIDTH, D_MODEL)), _const_spec((B_WIDTH, D_MODEL)),
                _const_spec((M_WIDTH, D_MODEL)), _const_spec((D_MODEL, D_MODEL))]
    out_specs = [col(D_MODEL, 0)]
    out_shape = [jax.ShapeDtypeStruct((n, D_MODEL), F32)]
    if emit_vn:
        out_specs.append(col(B_WIDTH, 0))
        out_shape.append(jax.ShapeDtypeStruct((n, B_WIDTH), F32))
    return pl.pallas_call(
        functools.partial(_merge_kernel, tm=tm, chunk=chunk),
        grid=(n // tm,),
        in_specs=in_specs,
        out_specs=out_specs,
        out_shape=out_shape,
        compiler_params=_cparams(("arbitrary",)),
        name="merge",
    )(x, oa, *([zr] * 9), mk_b, mv_b, ws, bs_t, g_sgu, g_mq, w_pa, w_pb, w_pc, w_out)


def _rope_tables(pos):
    pos = pos.astype(F32)[:, None]

    def cs(half):
        freq = ROPE_THETA ** (-jnp.arange(half, dtype=F32) / half)
        ang = pos * freq[None, :]
        return jnp.cos(ang), jnp.sin(ang)

    c, s = cs(A_HEAD_DIM // 2)
    ci, si = cs(IDX_DIM // 2)
    z = jnp.zeros_like(si)
    return (jnp.concatenate([c, c], axis=1), jnp.concatenate([-s, s], axis=1),
            jnp.concatenate([ci] * 4, axis=1), jnp.concatenate([-si, z, -si, z], axis=1),
            jnp.concatenate([z, si, z, si], axis=1))


def kernel(x_prompt, x_sample, cache_k, cache_v, cache_kidx, cache_mem_k, cache_mem_v, page_table,
           mem_prompt, g_pre, w_in, g_q, g_k, g_mq, g_mk, g_mem, w_mem_kv, g_sgu, w_s, b_s,
           w_pa, w_pb, w_pc, w_out):
    batch, seq, _ = x_prompt.shape
    nseq, nq, _ = x_sample.shape
    assert nq == 4 and nq <= S_PAD
    npages = page_table.shape[1]
    n_past = npages * PAGE
    n_pool = cache_k.shape[0]
    row2 = lambda a: a.reshape(1, -1)

    w_in_t = w_in.T
    w_a = w_in_t[:A_COLS].astype(BF16)
    w_pa_b, w_pb_b, w_pc_b, w_out_b = (w.astype(BF16) for w in (w_pa, w_pb, w_pc, w_out))
    w_mem_b = w_mem_kv.astype(BF16)
    g_pre2, g_q2, g_k2, g_mq2, g_mk2, g_mem2, g_sgu2 = map(row2, (g_pre, g_q, g_k, g_mq, g_mk, g_mem, g_sgu))

    xp = x_prompt.reshape(batch * seq, D_MODEL)
    tm_a = min(256, seq)
    nblk = seq // tm_a
    tabs_p = _rope_tables(jnp.arange(seq))
    q, k_p, v_p, k_b, v_b, qi, ki_p, ki_b, wi = _proj_a(
        xp, g_pre2, w_a, g_q2, g_k2, tabs_p, tm_a, lambda i: (i % nblk, 0))
    zr = _proj_rest(xp, g_pre2, w_in_t, min(1024, seq), PROJ_TN)
    mk_p, mv_p, mk_b, mv_b = _mem_kv(mem_prompt.reshape(batch * N_MEM, D_MODEL), g_mem2, w_mem_b, g_mk2)
    oa = _dsa_prompt(qi, wi.T, q, ki_b, k_b, v_b, batch, seq, min(TOPK_MAX, seq // 4))
    tm_m = min(256, seq)
    nblk_m = seq // tm_m
    (y_p,) = _merge(xp, oa, zr, mk_b, mv_b, w_s, b_s.T, g_sgu2, g_mq2, w_pa_b, w_pb_b, w_pc_b, w_out_b,
                    tm_m, CHUNK, lambda i: (i // nblk_m, 0), False)

    xs = jnp.pad(x_sample, ((0, 0), (0, S_PAD - nq), (0, 0))).reshape(nseq * S_PAD, D_MODEL)
    tabs_s = tuple(jnp.tile(t, (nseq, 1)) for t in _rope_tables(n_past + jnp.arange(S_PAD)))
    rows_s = nseq * S_PAD
    q_s, k_s, v_s, k_sb, v_sb, qi_s, ki_s, ki_sb, wi_s = _proj_a(
        xs, g_pre2, w_a, g_q2, g_k2, tabs_s, rows_s, lambda i: (0, 0))
    zr_s = _proj_rest(xs, g_pre2, w_in_t, rows_s, PROJ_TN)

    qp = (qi_s.reshape(nseq, S_PAD, IDX_HEADS, IDX_DIM)[:, :nq]
          .transpose(0, 2, 1, 3).reshape(nseq, IDX_HEADS * nq, IDX_DIM))
    wcol = wi_s.reshape(nseq, S_PAD, IDX_HEADS)[:, :nq].transpose(0, 2, 1).reshape(nseq, IDX_HEADS * nq, 1)
    rep = A_HEADS // A_KV_HEADS
    qs = (q_s.reshape(nseq, S_PAD, A_KV_HEADS, rep, A_HEAD_DIM)[:, :nq]
          .transpose(0, 2, 3, 1, 4).reshape(nseq, A_KV_HEADS, rep * nq, A_HEAD_DIM))
    qs = jnp.concatenate([qs, qs], axis=2)
    page_flat = page_table.reshape(-1)
    pages = 8
    sc, scn = _sample_scores(page_flat, qp, wcol, ki_sb, jnp.swapaxes(cache_kidx, 1, 2), nseq, npages, pages, nq)
    o_s = _sample_attn(page_flat, sc, scn, qs, k_sb, v_sb,
                       cache_k.reshape(n_pool, PAGE * A_KV_HEADS, A_HEAD_DIM),
                       cache_v.reshape(n_pool, PAGE * A_KV_HEADS, A_HEAD_DIM),
                       nseq, npages, pages, nq, min(TOPK_MAX, (n_past + nq) // 4))
    oa_s = (o_s[:, :, :rep * nq].reshape(nseq, A_KV_HEADS, rep, nq, A_HEAD_DIM)
            .transpose(0, 3, 1, 2, 4).reshape(nseq, nq, A_WIDTH))
    oa_s = jnp.pad(oa_s, ((0, 0), (0, S_PAD - nq), (0, 0))).reshape(rows_s, A_WIDTH).astype(BF16)
    mk_s = cache_mem_k.reshape(nseq * N_MEM, M_WIDTH).astype(BF16)
    mv_s = cache_mem_v.reshape(nseq * N_MEM, M_WIDTH).astype(BF16)
    y_s, vn_s = _merge(xs, oa_s, zr_s, mk_s, mv_s, w_s[:, :S_PAD, :S_PAD], b_s[:, :S_PAD].T, g_sgu2, g_mq2,
                       w_pa_b, w_pb_b, w_pc_b, w_out_b, S_PAD, S_PAD, lambda i: (i, 0), True)

    take = lambda a, shape: a.reshape(nseq, S_PAD, -1)[:, :nq].reshape(shape)
    return (y_p.reshape(batch, seq, D_MODEL),
            take(y_s, (nseq, nq, D_MODEL)),
            k_p.reshape(batch, seq, A_KV_HEADS, A_HEAD_DIM),
            v_p.reshape(batch, seq, A_KV_HEADS, A_HEAD_DIM),
            ki_p.reshape(batch, seq, IDX_DIM),
            mk_p.reshape(batch, N_MEM, M_HEADS, M_HEAD_DIM),
            mv_p.reshape(batch, N_MEM, M_HEADS, M_HEAD_DIM),
            take(k_s, (nseq, nq, A_KV_HEADS, A_HEAD_DIM)),
            take(v_s, (nseq, nq, A_KV_HEADS, A_HEAD_DIM)),
            take(ki_s, (nseq, nq, IDX_DIM)),
            take(vn_s, (nseq, nq, B_GROUPS, B_GROUP_DIM)))
```

```python
import functools

import numpy as np
import jax
import jax.numpy as jnp
from jax import lax
from jax.experimental import pallas as pl
from jax.experimental.pallas import tpu as pltpu

F32 = jnp.float32
BF16 = jnp.bfloat16
I32 = jnp.int32

D_MODEL = 2048
PAGE = 128
A_HEADS = 8
A_KV_HEADS = 4
A_HEAD_DIM = 128
A_WIDTH = A_HEADS * A_HEAD_DIM
A_KV_WIDTH = A_KV_HEADS * A_HEAD_DIM
IDX_HEADS = 16
IDX_DIM = 64
TOPK_MAX = 256
Q_BLOCK = 128
ROPE_THETA = 10000.0
CHUNK = 128
B_GROUPS = 8
B_GROUP_DIM = 128
B_WIDTH = B_GROUPS * B_GROUP_DIM
N_MEM = 256
M_HEADS = 4
M_HEAD_DIM = 256
M_WIDTH = M_HEADS * M_HEAD_DIM
EPS = 1e-6

OFF_K = A_WIDTH
OFF_V = OFF_K + A_KV_WIDTH
OFF_QI = OFF_V + A_KV_WIDTH
OFF_KI = OFF_QI + IDX_HEADS * IDX_DIM
OFF_WI = OFF_KI + IDX_DIM
OFF_REST = OFF_WI + IDX_HEADS
A_COLS = 3200
REST_COLS = A_WIDTH + 3 * B_WIDTH + 2 * M_WIDTH + 3 * D_MODEL

Q_SCALE = float(np.log2(np.e)) * A_HEAD_DIM ** -0.5
S_PAD = 16
PROJ_TN = 1024
ATT_GROUPS = 4
SAMPLE_PAGES = 16
INT_MIN = np.int32(-2 ** 31)
INT_MAX = np.int32(2 ** 31 - 1)
NEG_INF = float("-inf")

V7X_VMEM_LIMIT = 56 * 1024 * 1024


def _cparams(sem):
    return pltpu.CompilerParams(dimension_semantics=sem, vmem_limit_bytes=V7X_VMEM_LIMIT)


def _dot(a, b):
    return jnp.dot(a, b, preferred_element_type=F32)


def _dot_nt(a, b):
    return lax.dot_general(a, b, (((1,), (1,)), ((), ())), preferred_element_type=F32)


def _rms(x, g):
    return x * lax.rsqrt(jnp.mean(x * x, axis=-1, keepdims=True) + EPS) * g


def _const_spec(shape):
    nd = len(shape)
    return pl.BlockSpec(shape, lambda *_: (0,) * nd, pipeline_mode=pl.Buffered(1))


def _proj_a_kernel(x_ref, g_ref, w_ref, gq_ref, gk_ref, cq_ref, sq_ref, ci_ref, sia_ref, sib_ref,
                   q_ref, k_ref, v_ref, kb_ref, vb_ref, qi_ref, ki_ref, kib_ref, wi_ref):
    h = _rms(x_ref[...], g_ref[...]).astype(BF16)
    z = _dot_nt(h, w_ref[...])
    tm = z.shape[0]
    cq, sq = cq_ref[...], sq_ref[...]
    ci, sia, sib = ci_ref[...], sia_ref[...], sib_ref[...]

    def norm_rope(zz, g):
        n = _rms(zz, g)
        return n * cq + pltpu.roll(n, A_HEAD_DIM // 2, 1) * sq

    def rope_idx(zz):
        return zz * ci + pltpu.roll(zz, 96, 1) * sia + pltpu.roll(zz, 32, 1) * sib

    for hh in range(A_HEADS):
        sl = slice(hh * A_HEAD_DIM, (hh + 1) * A_HEAD_DIM)
        q_ref[:, sl] = (norm_rope(z[:, sl], gq_ref[...]) * Q_SCALE).astype(BF16)
    for hh in range(A_KV_HEADS):
        sl = slice(hh * A_HEAD_DIM, (hh + 1) * A_HEAD_DIM)
        kh = norm_rope(z[:, OFF_K + hh * A_HEAD_DIM:OFF_K + (hh + 1) * A_HEAD_DIM], gk_ref[...])
        vh = z[:, OFF_V + hh * A_HEAD_DIM:OFF_V + (hh + 1) * A_HEAD_DIM]
        head_rows = pl.ds(hh, tm, stride=A_KV_HEADS)
        k_ref[head_rows, :] = kh
        v_ref[head_rows, :] = vh
        kb_ref[:, sl] = kh.astype(BF16)
        vb_ref[:, sl] = vh.astype(BF16)
    for t in range(IDX_HEADS * IDX_DIM // 128):
        sl = slice(t * 128, (t + 1) * 128)
        qi_ref[:, sl] = rope_idx(z[:, OFF_QI + t * 128:OFF_QI + (t + 1) * 128]).astype(BF16)
    last = z[:, OFF_KI:OFF_KI + 128]
    ki = rope_idx(last)[:, :IDX_DIM]
    ki_ref[...] = ki
    kib_ref[...] = ki.astype(BF16)
    wi_ref[...] = last[:, IDX_DIM:IDX_DIM + IDX_HEADS] * ((IDX_HEADS ** -0.5) * (IDX_DIM ** -0.5))


def _proj_a(x, g_pre, w_a, g_q, g_k, tabs, tm, tab_map):
    n = x.shape[0]
    row = lambda w: pl.BlockSpec((tm, w), lambda i: (i, 0))
    tab = pl.BlockSpec((tm, 128), tab_map)
    outs = [(1, A_WIDTH, BF16), (A_KV_HEADS, A_HEAD_DIM, F32), (A_KV_HEADS, A_HEAD_DIM, F32),
            (1, A_KV_WIDTH, BF16), (1, A_KV_WIDTH, BF16),
            (1, IDX_HEADS * IDX_DIM, BF16), (1, IDX_DIM, F32), (1, IDX_DIM, BF16), (1, IDX_HEADS, F32)]
    return pl.pallas_call(
        _proj_a_kernel,
        grid=(n // tm,),
        in_specs=[row(D_MODEL), _const_spec((1, D_MODEL)), _const_spec((A_COLS, D_MODEL)),
                  _const_spec((1, A_HEAD_DIM)), _const_spec((1, A_HEAD_DIM)), tab, tab, tab, tab, tab],
        out_specs=[pl.BlockSpec((tm * r, w), lambda i: (i, 0)) for r, w, _ in outs],
        out_shape=[jax.ShapeDtypeStruct((n * r, w), dt) for r, w, dt in outs],
        compiler_params=_cparams(("arbitrary",)),
        name="proj_a",
    )(x, g_pre, w_a, g_q, g_k, *tabs)


def _proj_rest_kernel(x_ref, g_ref, w_ref, o_ref, h_ref):
    @pl.when(pl.program_id(1) == 0)
    def _():
        h_ref[...] = _rms(x_ref[...], g_ref[...]).astype(BF16)

    o_ref[...] = _dot_nt(h_ref[...], w_ref[...].astype(BF16)).astype(BF16)


def _proj_rest(x, g_pre, w_in_t, tm, tn):
    n = x.shape[0]
    return pl.pallas_call(
        _proj_rest_kernel,
        grid=(n // tm, REST_COLS // tn),
        in_specs=[pl.BlockSpec((tm, D_MODEL), lambda i, j: (i, 0), pipeline_mode=pl.Buffered(1)),
                  pl.BlockSpec((1, D_MODEL), lambda i, j: (0, 0)),
                  pl.BlockSpec((pl.Element(tn), pl.Element(D_MODEL)), lambda i, j: (pl.multiple_of(OFF_REST + j * tn, 16), 0))],
        out_specs=pl.BlockSpec((tm, tn), lambda i, j: (i, j)),
        out_shape=jax.ShapeDtypeStruct((n, REST_COLS), BF16),
        scratch_shapes=[pltpu.VMEM((tm, D_MODEL), BF16)],
        compiler_params=_cparams(("arbitrary", "arbitrary")),
        name="proj_rest",
    )(x, g_pre, w_in_t)


def _mem_kv_kernel(x_ref, g_ref, w_ref, gk_ref, k_ref, v_ref, kb_ref, vb_ref):
    h = _rms(x_ref[...], g_ref[...]).astype(BF16)
    z = _dot(h, w_ref[...])
    for hh in range(M_HEADS):
        sl = slice(hh * M_HEAD_DIM, (hh + 1) * M_HEAD_DIM)
        kh = _rms(z[:, sl], gk_ref[...])
        k_ref[:, sl] = kh
        kb_ref[:, sl] = kh.astype(BF16)
    v = z[:, M_WIDTH:]
    v_ref[...] = v
    vb_ref[...] = v.astype(BF16)


def _mem_kv(mem, g_mem, w_mem, g_mk):
    n = mem.shape[0]
    blk = pl.BlockSpec((N_MEM, M_WIDTH), lambda i: (i, 0))
    return pl.pallas_call(
        _mem_kv_kernel,
        grid=(n // N_MEM,),
        in_specs=[pl.BlockSpec((N_MEM, D_MODEL), lambda i: (i, 0)), _const_spec((1, D_MODEL)),
                  _const_spec((D_MODEL, 2 * M_WIDTH)), _const_spec((1, M_HEAD_DIM))],
        out_specs=[blk, blk, blk, blk],
        out_shape=[jax.ShapeDtypeStruct((n, M_WIDTH), dt) for dt in (F32, F32, BF16, BF16)],
        compiler_params=_cparams(("arbitrary",)),
        name="mem_kv",
    )(mem, g_mem, w_mem, g_mk)


KEY_NEG_INF = np.int32(-0x7F800000)


def _key_to_f32(key):
    return pltpu.bitcast(jnp.where(key >= 0, key, INT_MIN - key), F32)


def _select_params(count, topk, pos_bits, j_ref):
    kf = float(topk)
    t0 = jnp.where(count(lambda s, p: s >= 0.0) >= kf, jnp.int32(0), INT_MIN)

    def bit_body(b, t):
        cand = t + lax.shift_left(jnp.int32(1), 30 - b)
        cand_f = _key_to_f32(cand)
        return jnp.where(count(lambda s, p: s >= cand_f) >= kf, cand, t)

    t = _key_to_f32(jnp.maximum(lax.fori_loop(0, 31, bit_body, t0), KEY_NEG_INF))
    tie = (count(lambda s, p: s >= t) > kf) & (t > NEG_INF)
    j_ref[...] = jnp.full(j_ref.shape, INT_MAX, I32)

    @pl.when(jnp.max(tie.astype(I32)) > 0)
    def _():
        n_gt = count(lambda s, p: s > t)

        def pos_body(b, p_lo):
            cand = p_lo + lax.shift_left(jnp.int32(1), pos_bits - 1 - b)
            n_eq = count(lambda s, p: (s == t) & (p < cand))
            return jnp.where(n_gt + n_eq < kf, cand, p_lo)

        p_sel = lax.fori_loop(0, pos_bits, pos_body, jnp.zeros(t.shape, I32))
        j_ref[...] = jnp.where(tie, p_sel, INT_MAX)

    return t, j_ref[...]


def _dsa_prompt_kernel(qi_ref, wit_ref, q_ref, ki_ref, k_ref, v_ref, o_ref,
                       sc_ref, bias_ref, vt_ref, j_ref, s_ref, acc_ref, *, topk, kc, pos_bits):
    i = pl.program_id(1)
    nck = (i * Q_BLOCK + Q_BLOCK + kc - 1) // kc
    nchunks = vt_ref.shape[0]
    rep = A_HEADS // A_KV_HEADS
    q_pos = i * Q_BLOCK + lax.broadcasted_iota(I32, (1, Q_BLOCK), 1)
    sub = lax.broadcasted_iota(I32, (kc, 1), 0)

    @pl.when(i == 0)
    def _():
        for c in range(nchunks):
            for g in range(A_KV_HEADS):
                gsl = slice(g * A_HEAD_DIM, (g + 1) * A_HEAD_DIM)
                vt_ref[c, gsl, :] = v_ref[c * kc:(c + 1) * kc, gsl].astype(F32).T.astype(BF16)

    def chunk_rows(c):
        return pl.ds(pl.multiple_of(c * kc, kc), kc)

    qi = qi_ref[...]
    wit = wit_ref[...]
    qi_pairs = [jnp.concatenate([qi[:, (2 * j) * IDX_DIM:(2 * j + 1) * IDX_DIM],
                                 qi[:, (2 * j + 1) * IDX_DIM:(2 * j + 2) * IDX_DIM]], axis=0)
                for j in range(IDX_HEADS // 2)]

    def score_body(c, carry):
        kic = ki_ref[chunk_rows(c), :]
        acc = jnp.zeros((kc, Q_BLOCK), F32)
        for j in range(IDX_HEADS // 2):
            d = _dot_nt(kic, qi_pairs[j])
            acc = acc + jnp.maximum(d[:, :Q_BLOCK], 0.0) * wit[2 * j:2 * j + 1, :]
            acc = acc + jnp.maximum(d[:, Q_BLOCK:], 0.0) * wit[2 * j + 1:2 * j + 2, :]
        sc_ref[c] = jnp.where(c * kc + sub <= q_pos, acc, NEG_INF)
        return carry

    lax.fori_loop(0, nck, score_body, 0)

    def count(pred):
        def body(c, acc):
            part = jnp.where(pred(sc_ref[c], c * kc + sub), 1.0, 0.0)
            return acc + jnp.sum(part.reshape(kc // 64, 64, Q_BLOCK), axis=0)

        acc = lax.fori_loop(0, nck, body, jnp.zeros((64, Q_BLOCK), F32))
        return jnp.sum(acc, axis=0, keepdims=True)

    thr, jsel = _select_params(count, topk, pos_bits, j_ref)

    def bias_body(c, carry):
        s = sc_ref[c]
        pos = c * kc + sub
        sel = ((s > thr) | ((s == thr) & (pos <= jsel))) & (pos <= q_pos)
        bias_ref[c] = jnp.where(sel, 0.0, NEG_INF)
        return carry

    lax.fori_loop(0, nck, bias_body, 0)

    gsl = lambda g: slice(g * A_HEAD_DIM, (g + 1) * A_HEAD_DIM)
    row0 = lambda v: jnp.full((1, rep * Q_BLOCK), v, F32)
    for gs in [tuple(range(g0, g0 + ATT_GROUPS)) for g0 in range(0, A_KV_HEADS, ATT_GROUPS)]:
        qgs = [jnp.concatenate([q_ref[:, gsl(g * rep + r)] for r in range(rep)], axis=0) for g in gs]

        def qk_body(c, ms, gs=gs, qgs=qgs):
            rows = chunk_rows(c)
            b = bias_ref[c]
            bias = jnp.concatenate([b] * rep, axis=1)
            out = []
            for u, g in enumerate(gs):
                s = _dot_nt(k_ref[rows, gsl(g)], qgs[u]) + bias
                s_ref[c, u] = s
                out.append(jnp.maximum(ms[u], jnp.max(s, axis=0, keepdims=True)))
            return tuple(out)

        ms = lax.fori_loop(0, nck, qk_body, (row0(NEG_INF),) * len(gs))
        acc_ref[...] = jnp.zeros(acc_ref.shape, F32)

        def pv_body(c, ls, gs=gs, ms=ms):
            out = []
            for u, g in enumerate(gs):
                p = jnp.exp2(s_ref[c, u] - ms[u])
                out.append(ls[u] + jnp.sum(p, axis=0, keepdims=True))
                acc_ref[u] += _dot(vt_ref[c, gsl(g), :], p.astype(BF16))
            return tuple(out)

        ls = lax.fori_loop(0, nck, pv_body, (row0(0.0),) * len(gs))
        for u, g in enumerate(gs):
            o = acc_ref[u] / ls[u]
            for r in range(rep):
                o_ref[:, gsl(g * rep + r)] = o[:, r * Q_BLOCK:(r + 1) * Q_BLOCK].T.astype(BF16)


def _dsa_prompt(qi, wit, q, ki_b, k_b, v_b, batch, seq, topk):
    nqb = seq // Q_BLOCK
    rep = A_HEADS // A_KV_HEADS
    kc = min(512, seq)
    nchunks = seq // kc
    pos_bits = max(1, int(np.ceil(np.log2(seq))))
    qrow = lambda w: pl.BlockSpec((Q_BLOCK, w), lambda b, i: (b * nqb + i, 0))
    seqblk = lambda w: pl.BlockSpec((seq, w), lambda b, i: (b, 0), pipeline_mode=pl.Buffered(1))
    return pl.pallas_call(
        functools.partial(_dsa_prompt_kernel, topk=topk, kc=kc, pos_bits=pos_bits),
        grid=(batch, nqb),
        in_specs=[qrow(IDX_HEADS * IDX_DIM), pl.BlockSpec((IDX_HEADS, Q_BLOCK), lambda b, i: (0, b * nqb + i)),
                  qrow(A_WIDTH), seqblk(IDX_DIM), seqblk(A_KV_WIDTH), seqblk(A_KV_WIDTH)],
        out_specs=qrow(A_WIDTH),
        out_shape=jax.ShapeDtypeStruct((batch * seq, A_WIDTH), BF16),
        scratch_shapes=[pltpu.VMEM((nchunks, kc, Q_BLOCK), F32), pltpu.VMEM((nchunks, kc, Q_BLOCK), F32),
                        pltpu.VMEM((nchunks, A_KV_WIDTH, kc), BF16), pltpu.VMEM((1, Q_BLOCK), I32),
                        pltpu.VMEM((nchunks, ATT_GROUPS, kc, rep * Q_BLOCK), F32),
                        pltpu.VMEM((ATT_GROUPS, A_HEAD_DIM, rep * Q_BLOCK), F32)],
        compiler_params=_cparams(("arbitrary", "arbitrary")),
        name="dsa_prompt",
    )(qi, wit, q, ki_b, k_b, v_b)


def _fold_heads(e, nq):
    acc = e[0:8]
    for t in range(1, e.shape[0] // 8):
        acc = acc + e[8 * t:8 * (t + 1)]
    return acc + pltpu.roll(acc, nq, 0)


def _sample_scores_kernel(pt_ref, qp_ref, wc_ref, kin_ref, *refs, pages, nq):
    page_refs, (sc_ref, scn_ref) = refs[:pages], refs[pages:]
    qp = qp_ref[...]
    wc = wc_ref[...]
    kpt = jnp.concatenate([r[...] for r in page_refs], axis=1).astype(BF16)
    sc_ref[...] = _fold_heads(jnp.maximum(_dot(qp, kpt), 0.0) * wc, nq)

    @pl.when(pl.program_id(1) == 0)
    def _():
        kin = jnp.concatenate([kin_ref[...], jnp.zeros((PAGE - S_PAD, IDX_DIM), BF16)], axis=0)
        scn_ref[...] = _fold_heads(jnp.maximum(_dot_nt(qp, kin), 0.0) * wc, nq)


def _sample_scores(page_flat, qp, wcol, ki_new_b, cache_kidx, nseq, npages, pages, nq):
    steps = npages // pages
    rows = IDX_HEADS * nq
    page_specs = [pl.BlockSpec((None, IDX_DIM, PAGE),
                               lambda b, s, pt, p=p: (pt[b * npages + s * pages + p], 0, 0))
                  for p in range(pages)]
    grid_spec = pltpu.PrefetchScalarGridSpec(
        num_scalar_prefetch=1,
        grid=(nseq, steps),
        in_specs=[pl.BlockSpec((None, rows, IDX_DIM), lambda b, s, pt: (b, 0, 0)),
                  pl.BlockSpec((None, rows, 1), lambda b, s, pt: (b, 0, 0)),
                  pl.BlockSpec((S_PAD, IDX_DIM), lambda b, s, pt: (b, 0))] + page_specs,
        out_specs=[pl.BlockSpec((None, 8, pages * PAGE), lambda b, s, pt: (b, 0, s)),
                   pl.BlockSpec((None, 8, PAGE), lambda b, s, pt: (b, 0, 0))],
    )
    return pl.pallas_call(
        functools.partial(_sample_scores_kernel, pages=pages, nq=nq),
        grid_spec=grid_spec,
        out_shape=[jax.ShapeDtypeStruct((nseq, 8, npages * PAGE), F32),
                   jax.ShapeDtypeStruct((nseq, 8, PAGE), F32)],
        compiler_params=_cparams(("arbitrary", "arbitrary")),
        name="dsa_sample_scores",
    )(page_flat, qp, wcol, ki_new_b, *([cache_kidx] * pages))


def _sample_attn_kernel(pt_ref, sc_ref, scn_ref, qs_ref, kn_ref, vn_ref, *refs,
                        pages, nq, n_past, topk, pos_bits):
    k_pages, v_pages = refs[:pages], refs[pages:2 * pages]
    o_ref = refs[2 * pages]
    scm_ref, bias_ref, biasn_ref, j_ref, m_ref, l_ref, acc_ref, kbuf_ref, vbuf_ref = refs[2 * pages + 1:]
    s_idx = pl.program_id(1)
    kw = pages * PAGE
    rows_q = qs_ref.shape[1]

    @pl.when(s_idx == 0)
    def _():
        qrow = lax.broadcasted_iota(I32, (8, 1), 0) % nq
        lane_p = lax.broadcasted_iota(I32, (1, n_past), 1)
        lane_n = lax.broadcasted_iota(I32, (1, PAGE), 1)
        scm_ref[...] = jnp.where(lane_n <= qrow, scn_ref[...], NEG_INF)

        def count(pred):
            ones = jnp.where(pred(sc_ref[...], lane_p), 1.0, 0.0)
            parts = [ones[:, t * kw:(t + 1) * kw] for t in range(n_past // kw)]
            while len(parts) > 1:
                parts = [a + b for a, b in zip(parts[::2], parts[1::2])] + parts[len(parts) & ~1:]
            cp = jnp.sum(parts[0], axis=-1, keepdims=True)
            cn = jnp.sum(jnp.where(pred(scm_ref[...], n_past + lane_n), 1.0, 0.0), axis=-1, keepdims=True)
            return cp + cn

        thr, jsel = _select_params(count, topk, pos_bits, j_ref)
        kp = sc_ref[...]
        bias_p = jnp.where((kp > thr) | ((kp == thr) & (lane_p <= jsel)), 0.0, NEG_INF)
        for st in range(n_past // kw):
            bias_ref[st] = bias_p[:, st * kw:(st + 1) * kw]
        kn = scm_ref[...]
        seln = ((kn > thr) | ((kn == thr) & (n_past + lane_n <= jsel))) & (lane_n <= qrow)
        biasn_ref[...] = jnp.where(seln, 0.0, NEG_INF)
        m_ref[...] = jnp.full(m_ref.shape, NEG_INF, F32)
        l_ref[...] = jnp.zeros(l_ref.shape, F32)
        acc_ref[...] = jnp.zeros(acc_ref.shape, F32)

    for p in range(pages):
        for g in range(A_KV_HEADS):
            gsl = slice(g * A_HEAD_DIM, (g + 1) * A_HEAD_DIM)
            rows = pl.ds(g, PAGE, stride=A_KV_HEADS)
            kbuf_ref[p * PAGE:(p + 1) * PAGE, gsl] = k_pages[p][rows, :].astype(BF16)
            vbuf_ref[p * PAGE:(p + 1) * PAGE, gsl] = v_pages[p][rows, :].astype(BF16)

    def update(g, s, vv):
        m = m_ref[g]
        m_new = jnp.maximum(m, jnp.max(s, axis=-1, keepdims=True))
        m_safe = jnp.where(m_new == NEG_INF, 0.0, m_new)
        alpha = jnp.exp2(m - m_safe)
        p = jnp.exp2(s - m_safe)
        l_ref[g] = alpha * l_ref[g] + jnp.sum(p, axis=-1, keepdims=True)
        acc_ref[g] = alpha * acc_ref[g] + _dot(p.astype(BF16), vv)
        m_ref[g] = m_new

    bias = jnp.concatenate([bias_ref[s_idx]] * (rows_q // 8), axis=0)
    for g in range(A_KV_HEADS):
        gsl = slice(g * A_HEAD_DIM, (g + 1) * A_HEAD_DIM)
        update(g, _dot_nt(qs_ref[g], kbuf_ref[:, gsl]) + bias, vbuf_ref[:, gsl])

    @pl.when(s_idx == pl.num_programs(1) - 1)
    def _():
        bn = jnp.concatenate([biasn_ref[:, :S_PAD]] * (rows_q // 8), axis=0)
        for g in range(A_KV_HEADS):
            gsl = slice(g * A_HEAD_DIM, (g + 1) * A_HEAD_DIM)
            update(g, _dot_nt(qs_ref[g], kn_ref[:, gsl]) + bn, vn_ref[:, gsl])
            o_ref[g] = acc_ref[g] / l_ref[g]


def _sample_attn(page_flat, sc, scn, qs, k_new_b, v_new_b, cache_k, cache_v, nseq, npages, pages, nq, topk):
    steps = npages // pages
    n_past = npages * PAGE
    kw = pages * PAGE
    rows_q = qs.shape[2]
    pos_bits = int(np.ceil(np.log2(n_past + PAGE)))
    page_spec = lambda p: pl.BlockSpec((None, PAGE * A_KV_HEADS, A_HEAD_DIM),
                                       lambda b, s, pt, p=p: (pt[b * npages + s * pages + p], 0, 0))
    grid_spec = pltpu.PrefetchScalarGridSpec(
        num_scalar_prefetch=1,
        grid=(nseq, steps),
        in_specs=[pl.BlockSpec((None, 8, n_past), lambda b, s, pt: (b, 0, 0)),
                  pl.BlockSpec((None, 8, PAGE), lambda b, s, pt: (b, 0, 0)),
                  pl.BlockSpec((None, A_KV_HEADS, rows_q, A_HEAD_DIM), lambda b, s, pt: (b, 0, 0, 0)),
                  pl.BlockSpec((S_PAD, A_KV_WIDTH), lambda b, s, pt: (b, 0)),
                  pl.BlockSpec((S_PAD, A_KV_WIDTH), lambda b, s, pt: (b, 0))]
                 + [page_spec(p) for p in range(pages)] + [page_spec(p) for p in range(pages)],
        out_specs=pl.BlockSpec((None, A_KV_HEADS, rows_q, A_HEAD_DIM), lambda b, s, pt: (b, 0, 0, 0)),
        scratch_shapes=[pltpu.VMEM((8, PAGE), F32),
                        pltpu.VMEM((steps, 8, kw), F32), pltpu.VMEM((8, PAGE), F32),
                        pltpu.VMEM((8, 1), I32),
                        pltpu.VMEM((A_KV_HEADS, rows_q, 1), F32), pltpu.VMEM((A_KV_HEADS, rows_q, 1), F32),
                        pltpu.VMEM((A_KV_HEADS, rows_q, A_HEAD_DIM), F32),
                        pltpu.VMEM((kw, A_KV_WIDTH), BF16), pltpu.VMEM((kw, A_KV_WIDTH), BF16)],
    )
    return pl.pallas_call(
        functools.partial(_sample_attn_kernel, pages=pages, nq=nq, n_past=n_past, topk=topk, pos_bits=pos_bits),
        grid_spec=grid_spec,
        out_shape=jax.ShapeDtypeStruct((nseq, A_KV_HEADS, rows_q, A_HEAD_DIM), F32),
        compiler_params=_cparams(("arbitrary", "arbitrary")),
        name="dsa_sample_attn",
    )(page_flat, sc, scn, qs, k_new_b, v_new_b, *([cache_k] * pages), *([cache_v] * pages))


def _tree(op, parts):
    parts = list(parts)
    while len(parts) > 1:
        parts = [op(a, b) for a, b in zip(parts[::2], parts[1::2])] + parts[len(parts) & ~1:]
    return parts[0]


def _lane_fold(op, x):
    return _tree(op, [x[:, t * 128:(t + 1) * 128] for t in range(x.shape[1] // 128)])


def _dsa_sample_kernel(pt_ref, qp_ref, wc_ref, kin_ref, qs_ref, kn_ref, vn_ref, *refs,
                       pages, nq, n_past, topk, pos_bits):
    ki_pages, k_pages, v_pages = refs[:pages], refs[pages:2 * pages], refs[2 * pages:3 * pages]
    o_ref = refs[3 * pages]
    (sc_ref, scn_ref, bias_ref, biasn_ref, j_ref, s_ref, sn_ref, m_ref, mfin_ref, l_ref, acc_ref,
     buf_ref) = refs[3 * pages + 1:]
    s_idx = pl.program_id(1)
    kw = pages * PAGE
    steps = n_past // kw
    rows_q = qs_ref.shape[1]
    dup = lambda x8: jnp.concatenate([x8] * (rows_q // 8), axis=0)
    gsl = lambda g: slice(g * A_HEAD_DIM, (g + 1) * A_HEAD_DIM)

    def load_pages(page_refs):
        for p in range(pages):
            for g in range(A_KV_HEADS):
                rows = pl.ds(g, PAGE, stride=A_KV_HEADS)
                buf_ref[p * PAGE:(p + 1) * PAGE, gsl(g)] = page_refs[p][rows, :].astype(BF16)

    @pl.when(s_idx < steps)
    def _():
        qp = qp_ref[...]
        wc = wc_ref[...]
        kpt = jnp.concatenate([r[...] for r in ki_pages], axis=1).astype(BF16)
        sc_ref[s_idx] = _fold_heads(jnp.maximum(_dot(qp, kpt), 0.0) * wc, nq)

        @pl.when(s_idx == steps - 1)
        def _():
            qrow = lax.broadcasted_iota(I32, (8, 1), 0) % nq
            lane = lax.broadcasted_iota(I32, (1, kw), 1)
            lane_n = lax.broadcasted_iota(I32, (1, PAGE), 1)
            kin = jnp.concatenate([kin_ref[...], jnp.zeros((PAGE - S_PAD, IDX_DIM), BF16)], axis=0)
            scn = _fold_heads(jnp.maximum(_dot_nt(qp, kin), 0.0) * wc, nq)
            scn_ref[...] = jnp.where(lane_n <= qrow, scn, NEG_INF)

            def count(pred):
                ones = _tree(jnp.add, [jnp.where(pred(sc_ref[t], t * kw + lane), 1.0, 0.0) for t in range(steps)])
                cn = jnp.where(pred(scn_ref[...], n_past + lane_n), 1.0, 0.0)
                return (jnp.sum(_lane_fold(jnp.add, ones), axis=-1, keepdims=True)
                        + jnp.sum(cn, axis=-1, keepdims=True))

            thr, jsel = _select_params(count, topk, pos_bits, j_ref)
            for t in range(steps):
                sp = sc_ref[t]
                bias_ref[t] = jnp.where((sp > thr) | ((sp == thr) & (t * kw + lane <= jsel)), 0.0, NEG_INF)
            sn = scn_ref[...]
            seln = ((sn > thr) | ((sn == thr) & (n_past + lane_n <= jsel))) & (lane_n <= qrow)
            biasn_ref[...] = jnp.where(seln, 0.0, NEG_INF)

    @pl.when((s_idx >= steps) & (s_idx < 2 * steps))
    def _():
        t = s_idx - steps
        load_pages(k_pages)
        bias = dup(bias_ref[t])

        @pl.when(t == 0)
        def _():
            m_ref[...] = jnp.full(m_ref.shape, NEG_INF, F32)

        for g in range(A_KV_HEADS):
            s = _dot_nt(qs_ref[g], buf_ref[:, gsl(g)]) + bias
            s_ref[g, t] = s
            m_ref[g] = jnp.maximum(m_ref[g], _lane_fold(jnp.maximum, s))

        @pl.when(t == steps - 1)
        def _():
            bn = dup(biasn_ref[:, :S_PAD])
            for g in range(A_KV_HEADS):
                sn = _dot_nt(qs_ref[g], kn_ref[:, gsl(g)]) + bn
                sn_ref[g] = sn
                mfin_ref[g] = jnp.maximum(jnp.max(m_ref[g], axis=-1, keepdims=True),
                                          jnp.max(sn, axis=-1, keepdims=True))

    @pl.when(s_idx >= 2 * steps)
    def _():
        t = s_idx - 2 * steps
        load_pages(v_pages)

        @pl.when(t == 0)
        def _():
            l_ref[...] = jnp.zeros(l_ref.shape, F32)
            acc_ref[...] = jnp.zeros(acc_ref.shape, F32)

        for g in range(A_KV_HEADS):
            p = jnp.exp2(s_ref[g, t] - mfin_ref[g])
            l_ref[g] += _lane_fold(jnp.add, p)
            acc_ref[g] += _dot(p.astype(BF16), buf_ref[:, gsl(g)])

        @pl.when(t == steps - 1)
        def _():
            for g in range(A_KV_HEADS):
                pn = jnp.exp2(sn_ref[g] - mfin_ref[g])
                l = jnp.sum(l_ref[g], axis=-1, keepdims=True) + jnp.sum(pn, axis=-1, keepdims=True)
                o_ref[g] = (acc_ref[g] + _dot(pn.astype(BF16), vn_ref[:, gsl(g)])) / l


def _dsa_sample(page_flat, qp, wcol, ki_new_b, qs, k_new_b, v_new_b, cache_kidx_t, cache_k, cache_v,
                nseq, npages, pages, nq, topk):
    steps = npages // pages
    n_past = npages * PAGE
    kw = pages * PAGE
    rows = IDX_HEADS * nq
    rows_q = qs.shape[2]
    pos_bits = int(np.ceil(np.log2(n_past + PAGE)))

    def page_spec(shape, phase, p):
        def index(b, s, pt):
            t = jnp.clip(s - phase * steps, 0, steps - 1)
            return (pt[b * npages + t * pages + p], 0, 0)
        return pl.BlockSpec((None,) + shape, index)

    seq3 = lambda shape: pl.BlockSpec((None,) + shape, lambda b, s, pt: (b,) + (0,) * len(shape))
    new_rows = lambda w: pl.BlockSpec((S_PAD, w), lambda b, s, pt: (b, 0))
    grid_spec = pltpu.PrefetchScalarGridSpec(
        num_scalar_prefetch=1,
        grid=(nseq, 3 * steps),
        in_specs=[seq3((rows, IDX_DIM)), seq3((rows, 1)), new_rows(IDX_DIM),
                  seq3((A_KV_HEADS, rows_q, A_HEAD_DIM)), new_rows(A_KV_WIDTH), new_rows(A_KV_WIDTH)]
                 + [page_spec((IDX_DIM, PAGE), 0, p) for p in range(pages)]
                 + [page_spec((PAGE * A_KV_HEADS, A_HEAD_DIM), 1, p) for p in range(pages)]
                 + [page_spec((PAGE * A_KV_HEADS, A_HEAD_DIM), 2, p) for p in range(pages)],
        out_specs=seq3((A_KV_HEADS, rows_q, A_HEAD_DIM)),
        scratch_shapes=[pltpu.VMEM((steps, 8, kw), F32), pltpu.VMEM((8, PAGE), F32),
                        pltpu.VMEM((steps, 8, kw), F32), pltpu.VMEM((8, PAGE), F32),
                        pltpu.VMEM((8, 1), I32),
                        pltpu.VMEM((A_KV_HEADS, steps, rows_q, kw), F32),
                        pltpu.VMEM((A_KV_HEADS, rows_q, S_PAD), F32),
                        pltpu.VMEM((A_KV_HEADS, rows_q, 128), F32),
                        pltpu.VMEM((A_KV_HEADS, rows_q, 1), F32),
                        pltpu.VMEM((A_KV_HEADS, rows_q, 128), F32),
                        pltpu.VMEM((A_KV_HEADS, rows_q, A_HEAD_DIM), F32),
                        pltpu.VMEM((kw, A_KV_WIDTH), BF16)],
    )
    return pl.pallas_call(
        functools.partial(_dsa_sample_kernel, pages=pages, nq=nq, n_past=n_past, topk=topk, pos_bits=pos_bits),
        grid_spec=grid_spec,
        out_shape=jax.ShapeDtypeStruct((nseq, A_KV_HEADS, rows_q, A_HEAD_DIM), F32),
        compiler_params=_cparams(("arbitrary", "arbitrary")),
        name="dsa_sample",
    )(page_flat, qp, wcol, ki_new_b, qs, k_new_b, v_new_b,
      *([cache_kidx_t] * pages), *([cache_k] * pages), *([cache_v] * pages))


def _merge_kernel(x_ref, oa_ref, ag_ref, bu_ref, bv_ref, bg_ref, cq_ref, cg_ref, ra_ref, rb_ref, rc_ref,
                  mk_ref, mv_ref, ws_ref, bs_ref, gsgu_ref, gmq_ref, wpa_ref, wpb_ref, wpc_ref, wout_ref,
                  y_ref, *maybe_vn_ref, tm, chunk):
    f32 = lambda r: r[...].astype(F32)
    silu = lambda t: t * jax.nn.sigmoid(t)

    vn = _rms(f32(bv_ref), gsgu_ref[...])
    if maybe_vn_ref:
        maybe_vn_ref[0][...] = vn
    vnb = vn.astype(BF16)
    bu = f32(bu_ref)
    tril = (lax.broadcasted_iota(I32, (chunk, chunk), 1) <= lax.broadcasted_iota(I32, (chunk, chunk), 0))
    ob_cols = []
    for g in range(B_GROUPS):
        wg = jnp.where(tril, ws_ref[g], 0.0).astype(BF16)
        gsl = slice(g * B_GROUP_DIM, (g + 1) * B_GROUP_DIM)
        parts = [_dot(wg, vnb[c * chunk:(c + 1) * chunk, gsl]) + bs_ref[:, g:g + 1] for c in range(tm // chunk)]
        ob_cols.append(parts[0] if len(parts) == 1 else jnp.concatenate(parts, axis=0))
    ob = bu * jnp.concatenate(ob_cols, axis=1)
    pb = _dot((ob * silu(f32(bg_ref))).astype(BF16), wpb_ref[...])

    cq = f32(cq_ref)
    oc_cols = []
    for hh in range(M_HEADS):
        hsl = slice(hh * M_HEAD_DIM, (hh + 1) * M_HEAD_DIM)
        qn = (_rms(cq[:, hsl], gmq_ref[...]) * (M_HEAD_DIM ** -0.5)).astype(BF16)
        s = _dot_nt(qn, mk_ref[:, hsl])
        p = jnp.exp(s - jnp.max(s, axis=-1, keepdims=True))
        oc_cols.append(_dot(p.astype(BF16), mv_ref[:, hsl]) / jnp.sum(p, axis=-1, keepdims=True))
    oc = jnp.concatenate(oc_cols, axis=1)
    pc = _dot((oc * silu(f32(cg_ref))).astype(BF16), wpc_ref[...])

    pa = _dot((f32(oa_ref) * silu(f32(ag_ref))).astype(BF16), wpa_ref[...])
    sig = jax.nn.sigmoid
    m = sig(f32(ra_ref)) * pa + sig(f32(rb_ref)) * pb + sig(f32(rc_ref)) * pc
    y_ref[...] = x_ref[...] + _dot(m.astype(BF16), wout_ref[...])


def _merge(x, oa, zr, mk_b, mv_b, ws, bs_t, g_sgu, g_mq, w_pa, w_pb, w_pc, w_out, tm, chunk, mem_map, emit_vn):
    n = x.shape[0]
    col = lambda w, j: pl.BlockSpec((tm, w), lambda i, j=j: (i, j))
    mem = pl.BlockSpec((N_MEM, M_WIDTH), mem_map)
    in_specs = [col(D_MODEL, 0), col(A_WIDTH, 0),
                col(1024, 0), col(1024, 1), col(1024, 2), col(1024, 3), col(1024, 4), col(1024, 5),
                col(2048, 3), col(2048, 4), col(2048, 5),
                mem, mem,
                _const_spec((B_GROUPS, chunk, chunk)), _const_spec((chunk, B_GROUPS)),
                _const_spec((1, B_WIDTH)), _const_spec((1, M_HEAD_DIM)),
                _const_spec((A_WIDTH, D_MODEL)), _const_spec((B_WIDTH, D_MODEL)),
                _const_spec((M_WIDTH, D_MODEL)), _const_spec((D_MODEL, D_MODEL))]
    out_specs = [col(D_MODEL, 0)]
    out_shape = [jax.ShapeDtypeStruct((n, D_MODEL), F32)]
    if emit_vn:
        out_specs.append(col(B_WIDTH, 0))
        out_shape.append(jax.ShapeDtypeStruct((n, B_WIDTH), F32))
    return pl.pallas_call(
        functools.partial(_merge_kernel, tm=tm, chunk=chunk),
        grid=(n // tm,),
        in_specs=in_specs,
        out_specs=out_specs,
        out_shape=out_shape,
        compiler_params=_cparams(("arbitrary",)),
        name="merge",
    )(x, oa, *([zr] * 9), mk_b, mv_b, ws, bs_t, g_sgu, g_mq, w_pa, w_pb, w_pc, w_out)


def _rope_tables(pos):
    pos = pos.astype(F32)[:, None]

    def cs(half):
        freq = ROPE_THETA ** (-jnp.arange(half, dtype=F32) / half)
        ang = pos * freq[None, :]
        return jnp.cos(ang), jnp.sin(ang)

    c, s = cs(A_HEAD_DIM // 2)
    ci, si = cs(IDX_DIM // 2)
    z = jnp.zeros_like(si)
    return (jnp.concatenate([c, c], axis=1), jnp.concatenate([-s, s], axis=1),
            jnp.concatenate([ci] * 4, axis=1), jnp.concatenate([-si, z, -si, z], axis=1),
            jnp.concatenate([z, si, z, si], axis=1))


def kernel(x_prompt, x_sample, cache_k, cache_v, cache_kidx, cache_mem_k, cache_mem_v, page_table,
           mem_prompt, g_pre, w_in, g_q, g_k, g_mq, g_mk, g_mem, w_mem_kv, g_sgu, w_s, b_s,
           w_pa, w_pb, w_pc, w_out):
    batch, seq, _ = x_prompt.shape
    nseq, nq, _ = x_sample.shape
    assert nq == 4 and nq <= S_PAD
    npages = page_table.shape[1]
    n_past = npages * PAGE
    n_pool = cache_k.shape[0]
    row2 = lambda a: a.reshape(1, -1)

    w_in_t = w_in.T
    w_a = w_in_t[:A_COLS].astype(BF16)
    w_pa_b, w_pb_b, w_pc_b, w_out_b = (w.astype(BF16) for w in (w_pa, w_pb, w_pc, w_out))
    w_mem_b = w_mem_kv.astype(BF16)
    g_pre2, g_q2, g_k2, g_mq2, g_mk2, g_mem2, g_sgu2 = map(row2, (g_pre, g_q, g_k, g_mq, g_mk, g_mem, g_sgu))

    xp = x_prompt.reshape(batch * seq, D_MODEL)
    tm_a = min(256, seq)
    nblk = seq // tm_a
    tabs_p = _rope_tables(jnp.arange(seq))
    q, k_p, v_p, k_b, v_b, qi, ki_p, ki_b, wi = _proj_a(
        xp, g_pre2, w_a, g_q2, g_k2, tabs_p, tm_a, lambda i: (i % nblk, 0))
    zr = _proj_rest(xp, g_pre2, w_in_t, min(1024, seq), PROJ_TN)
    mk_p, mv_p, mk_b, mv_b = _mem_kv(mem_prompt.reshape(batch * N_MEM, D_MODEL), g_mem2, w_mem_b, g_mk2)
    oa = _dsa_prompt(qi, wi.T, q, ki_b, k_b, v_b, batch, seq, min(TOPK_MAX, seq // 4))
    tm_m = min(256, seq)
    nblk_m = seq // tm_m
    (y_p,) = _merge(xp, oa, zr, mk_b, mv_b, w_s, b_s.T, g_sgu2, g_mq2, w_pa_b, w_pb_b, w_pc_b, w_out_b,
                    tm_m, CHUNK, lambda i: (i // nblk_m, 0), False)

    xs = jnp.pad(x_sample, ((0, 0), (0, S_PAD - nq), (0, 0))).reshape(nseq * S_PAD, D_MODEL)
    tabs_s = tuple(jnp.tile(t, (nseq, 1)) for t in _rope_tables(n_past + jnp.arange(S_PAD)))
    rows_s = nseq * S_PAD
    q_s, k_s, v_s, k_sb, v_sb, qi_s, ki_s, ki_sb, wi_s = _proj_a(
        xs, g_pre2, w_a, g_q2, g_k2, tabs_s, rows_s, lambda i: (0, 0))
    zr_s = _proj_rest(xs, g_pre2, w_in_t, rows_s, PROJ_TN)

    qp = (qi_s.reshape(nseq, S_PAD, IDX_HEADS, IDX_DIM)[:, :nq]
          .transpose(0, 2, 1, 3).reshape(nseq, IDX_HEADS * nq, IDX_DIM))
    wcol = wi_s.reshape(nseq, S_PAD, IDX_HEADS)[:, :nq].transpose(0, 2, 1).reshape(nseq, IDX_HEADS * nq, 1)
    rep = A_HEADS // A_KV_HEADS
    qs = (q_s.reshape(nseq, S_PAD, A_KV_HEADS, rep, A_HEAD_DIM)[:, :nq]
          .transpose(0, 2, 3, 1, 4).reshape(nseq, A_KV_HEADS, rep * nq, A_HEAD_DIM))
    qs = jnp.concatenate([qs, qs], axis=2)
    page_flat = page_table.reshape(-1)
    pages = min(SAMPLE_PAGES, npages)
    o_s = _dsa_sample(page_flat, qp, wcol, ki_sb, qs, k_sb, v_sb, jnp.swapaxes(cache_kidx, 1, 2),
                      cache_k.reshape(n_pool, PAGE * A_KV_HEADS, A_HEAD_DIM),
                      cache_v.reshape(n_pool, PAGE * A_KV_HEADS, A_HEAD_DIM),
                      nseq, npages, pages, nq, min(TOPK_MAX, (n_past + nq) // 4))
    oa_s = (o_s[:, :, :rep * nq].reshape(nseq, A_KV_HEADS, rep, nq, A_HEAD_DIM)
            .transpose(0, 3, 1, 2, 4).reshape(nseq, nq, A_WIDTH))
    oa_s = jnp.pad(oa_s, ((0, 0), (0, S_PAD - nq), (0, 0))).reshape(rows_s, A_WIDTH).astype(BF16)
    mk_s = cache_mem_k.reshape(nseq * N_MEM, M_WIDTH).astype(BF16)
    mv_s = cache_mem_v.reshape(nseq * N_MEM, M_WIDTH).astype(BF16)
    y_s, vn_s = _merge(xs, oa_s, zr_s, mk_s, mv_s, w_s[:, :S_PAD, :S_PAD], b_s[:, :S_PAD].T, g_sgu2, g_mq2,
                       w_pa_b, w_pb_b, w_pc_b, w_out_b, S_PAD, S_PAD, lambda i: (i, 0), True)

    take = lambda a, shape: a.reshape(nseq, S_PAD, -1)[:, :nq].reshape(shape)
    return (y_p.reshape(batch, seq, D_MODEL),
            take(y_s, (nseq, nq, D_MODEL)),
            k_p.reshape(batch, seq, A_KV_HEADS, A_HEAD_DIM),
            v_p.reshape(batch, seq, A_KV_HEADS, A_HEAD_DIM),
            ki_p.reshape(batch, seq, IDX_DIM),
            mk_p.reshape(batch, N_MEM, M_HEADS, M_HEAD_DIM),
            mv_p.reshape(batch, N_MEM, M_HEADS, M_HEAD_DIM),
            take(k_s, (nseq, nq, A_KV_HEADS, A_HEAD_DIM)),
            take(v_s, (nseq, nq, A_KV_HEADS, A_HEAD_DIM)),
            take(ki_s, (nseq, nq, IDX_DIM)),
            take(vn_s, (nseq, nq, B_GROUPS, B_GROUP_DIM)))
```

```python
import functools

import numpy as np
import jax
import jax.numpy as jnp
from jax import lax
from jax.experimental import pallas as pl
from jax.experimental.pallas import tpu as pltpu

F32 = jnp.float32
BF16 = jnp.bfloat16
I32 = jnp.int32

D_MODEL = 2048
PAGE = 128
A_HEADS = 8
A_KV_HEADS = 4
A_HEAD_DIM = 128
A_WIDTH = A_HEADS * A_HEAD_DIM
A_KV_WIDTH = A_KV_HEADS * A_HEAD_DIM
IDX_HEADS = 16
IDX_DIM = 64
TOPK_MAX = 256
Q_BLOCK = 128
ROPE_THETA = 10000.0
CHUNK = 128
B_GROUPS = 8
B_GROUP_DIM = 128
B_WIDTH = B_GROUPS * B_GROUP_DIM
N_MEM = 256
M_HEADS = 4
M_HEAD_DIM = 256
M_WIDTH = M_HEADS * M_HEAD_DIM
EPS = 1e-6

OFF_K = A_WIDTH
OFF_V = OFF_K + A_KV_WIDTH
OFF_QI = OFF_V + A_KV_WIDTH
OFF_KI = OFF_QI + IDX_HEADS * IDX_DIM
OFF_WI = OFF_KI + IDX_DIM
OFF_REST = OFF_WI + IDX_HEADS
A_COLS = 3200
REST_COLS = A_WIDTH + 3 * B_WIDTH + 2 * M_WIDTH + 3 * D_MODEL

Q_SCALE = float(np.log2(np.e)) * A_HEAD_DIM ** -0.5
S_PAD = 16
PROJ_TN = 1024
SAMPLE_PAGES = 16
INT_MIN = np.int32(-2 ** 31)
INT_MAX = np.int32(2 ** 31 - 1)
NEG_INF = float("-inf")

V7X_VMEM_LIMIT = 56 * 1024 * 1024


def _cparams(sem):
    return pltpu.CompilerParams(dimension_semantics=sem, vmem_limit_bytes=V7X_VMEM_LIMIT)


def _dot(a, b):
    return jnp.dot(a, b, preferred_element_type=F32)


def _dot_nt(a, b):
    return lax.dot_general(a, b, (((1,), (1,)), ((), ())), preferred_element_type=F32)


def _rms(x, g):
    return x * lax.rsqrt(jnp.mean(x * x, axis=-1, keepdims=True) + EPS) * g


def _const_spec(shape):
    nd = len(shape)
    return pl.BlockSpec(shape, lambda *_: (0,) * nd, pipeline_mode=pl.Buffered(1))


def _proj_a_kernel(x_ref, g_ref, w_ref, gq_ref, gk_ref, cq_ref, sq_ref, ci_ref, sia_ref, sib_ref,
                   q_ref, k_ref, v_ref, kb_ref, vb_ref, qi_ref, ki_ref, kib_ref, wi_ref):
    h = _rms(x_ref[...], g_ref[...]).astype(BF16)
    z = _dot_nt(h, w_ref[...])
    tm = z.shape[0]
    cq, sq = cq_ref[...], sq_ref[...]
    ci, sia, sib = ci_ref[...], sia_ref[...], sib_ref[...]

    def norm_rope(zz, g):
        n = _rms(zz, g)
        return n * cq + pltpu.roll(n, A_HEAD_DIM // 2, 1) * sq

    def rope_idx(zz):
        return zz * ci + pltpu.roll(zz, 96, 1) * sia + pltpu.roll(zz, 32, 1) * sib

    for hh in range(A_HEADS):
        sl = slice(hh * A_HEAD_DIM, (hh + 1) * A_HEAD_DIM)
        q_ref[:, sl] = (norm_rope(z[:, sl], gq_ref[...]) * Q_SCALE).astype(BF16)
    for hh in range(A_KV_HEADS):
        sl = slice(hh * A_HEAD_DIM, (hh + 1) * A_HEAD_DIM)
        kh = norm_rope(z[:, OFF_K + hh * A_HEAD_DIM:OFF_K + (hh + 1) * A_HEAD_DIM], gk_ref[...])
        vh = z[:, OFF_V + hh * A_HEAD_DIM:OFF_V + (hh + 1) * A_HEAD_DIM]
        head_rows = pl.ds(hh, tm, stride=A_KV_HEADS)
        k_ref[head_rows, :] = kh
        v_ref[head_rows, :] = vh
        kb_ref[:, sl] = kh.astype(BF16)
        vb_ref[:, sl] = vh.astype(BF16)
    for t in range(IDX_HEADS * IDX_DIM // 128):
        sl = slice(t * 128, (t + 1) * 128)
        qi_ref[:, sl] = rope_idx(z[:, OFF_QI + t * 128:OFF_QI + (t + 1) * 128]).astype(BF16)
    last = z[:, OFF_KI:OFF_KI + 128]
    ki = rope_idx(last)[:, :IDX_DIM]
    ki_ref[...] = ki
    kib_ref[...] = ki.astype(BF16)
    wi_ref[...] = last[:, IDX_DIM:IDX_DIM + IDX_HEADS] * ((IDX_HEADS ** -0.5) * (IDX_DIM ** -0.5))


def _proj_a(x, g_pre, w_a, g_q, g_k, tabs, tm, tab_map):
    n = x.shape[0]
    row = lambda w: pl.BlockSpec((tm, w), lambda i: (i, 0))
    tab = pl.BlockSpec((tm, 128), tab_map)
    outs = [(1, A_WIDTH, BF16), (A_KV_HEADS, A_HEAD_DIM, F32), (A_KV_HEADS, A_HEAD_DIM, F32),
            (1, A_KV_WIDTH, BF16), (1, A_KV_WIDTH, BF16),
            (1, IDX_HEADS * IDX_DIM, BF16), (1, IDX_DIM, F32), (1, IDX_DIM, BF16), (1, IDX_HEADS, F32)]
    return pl.pallas_call(
        _proj_a_kernel,
        grid=(n // tm,),
        in_specs=[row(D_MODEL), _const_spec((1, D_MODEL)), _const_spec((A_COLS, D_MODEL)),
                  _const_spec((1, A_HEAD_DIM)), _const_spec((1, A_HEAD_DIM)), tab, tab, tab, tab, tab],
        out_specs=[pl.BlockSpec((tm * r, w), lambda i: (i, 0)) for r, w, _ in outs],
        out_shape=[jax.ShapeDtypeStruct((n * r, w), dt) for r, w, dt in outs],
        compiler_params=_cparams(("arbitrary",)),
        name="proj_a",
    )(x, g_pre, w_a, g_q, g_k, *tabs)


def _proj_rest_kernel(x_ref, g_ref, w_ref, o_ref, h_ref):
    @pl.when(pl.program_id(1) == 0)
    def _():
        h_ref[...] = _rms(x_ref[...], g_ref[...]).astype(BF16)

    o_ref[...] = _dot_nt(h_ref[...], w_ref[...].astype(BF16)).astype(BF16)


def _proj_rest(x, g_pre, w_in_t, tm, tn):
    n = x.shape[0]
    return pl.pallas_call(
        _proj_rest_kernel,
        grid=(n // tm, REST_COLS // tn),
        in_specs=[pl.BlockSpec((tm, D_MODEL), lambda i, j: (i, 0), pipeline_mode=pl.Buffered(1)),
                  pl.BlockSpec((1, D_MODEL), lambda i, j: (0, 0)),
                  pl.BlockSpec((pl.Element(tn), pl.Element(D_MODEL)), lambda i, j: (pl.multiple_of(OFF_REST + j * tn, 16), 0))],
        out_specs=pl.BlockSpec((tm, tn), lambda i, j: (i, j)),
        out_shape=jax.ShapeDtypeStruct((n, REST_COLS), BF16),
        scratch_shapes=[pltpu.VMEM((tm, D_MODEL), BF16)],
        compiler_params=_cparams(("arbitrary", "arbitrary")),
        name="proj_rest",
    )(x, g_pre, w_in_t)


def _mem_kv_kernel(x_ref, g_ref, w_ref, gk_ref, k_ref, v_ref, kb_ref, vb_ref):
    h = _rms(x_ref[...], g_ref[...]).astype(BF16)
    z = _dot(h, w_ref[...])
    for hh in range(M_HEADS):
        sl = slice(hh * M_HEAD_DIM, (hh + 1) * M_HEAD_DIM)
        kh = _rms(z[:, sl], gk_ref[...])
        k_ref[:, sl] = kh
        kb_ref[:, sl] = kh.astype(BF16)
    v = z[:, M_WIDTH:]
    v_ref[...] = v
    vb_ref[...] = v.astype(BF16)


def _mem_kv(mem, g_mem, w_mem, g_mk):
    n = mem.shape[0]
    blk = pl.BlockSpec((N_MEM, M_WIDTH), lambda i: (i, 0))
    return pl.pallas_call(
        _mem_kv_kernel,
        grid=(n // N_MEM,),
        in_specs=[pl.BlockSpec((N_MEM, D_MODEL), lambda i: (i, 0)), _const_spec((1, D_MODEL)),
                  _const_spec((D_MODEL, 2 * M_WIDTH)), _const_spec((1, M_HEAD_DIM))],
        out_specs=[blk, blk, blk, blk],
        out_shape=[jax.ShapeDtypeStruct((n, M_WIDTH), dt) for dt in (F32, F32, BF16, BF16)],
        compiler_params=_cparams(("arbitrary",)),
        name="mem_kv",
    )(mem, g_mem, w_mem, g_mk)


KEY_NEG_INF = np.int32(-0x7F800000)


def _key_to_f32(key):
    return pltpu.bitcast(jnp.where(key >= 0, key, INT_MIN - key), F32)


def _select_params(count, topk, pos_bits, j_ref):
    kf = float(topk)
    t0 = jnp.where(count(lambda s, p: s >= 0.0) >= kf, jnp.int32(0), INT_MIN)

    def bit_body(b, t):
        cand = t + lax.shift_left(jnp.int32(1), 30 - b)
        cand_f = _key_to_f32(cand)
        return jnp.where(count(lambda s, p: s >= cand_f) >= kf, cand, t)

    t = _key_to_f32(jnp.maximum(lax.fori_loop(0, 31, bit_body, t0), KEY_NEG_INF))
    tie = (count(lambda s, p: s >= t) > kf) & (t > NEG_INF)
    j_ref[...] = jnp.full(j_ref.shape, INT_MAX, I32)

    @pl.when(jnp.max(tie.astype(I32)) > 0)
    def _():
        n_gt = count(lambda s, p: s > t)

        def pos_body(b, p_lo):
            cand = p_lo + lax.shift_left(jnp.int32(1), pos_bits - 1 - b)
            n_eq = count(lambda s, p: (s == t) & (p < cand))
            return jnp.where(n_gt + n_eq < kf, cand, p_lo)

        p_sel = lax.fori_loop(0, pos_bits, pos_body, jnp.zeros(t.shape, I32))
        j_ref[...] = jnp.where(tie, p_sel, INT_MAX)

    return t, j_ref[...]


def _fori_by_two(n, body, init):
    carry = lax.fori_loop(0, n // 2, lambda j, c: body(2 * j + 1, body(2 * j, c)), init)
    return lax.cond(n % 2 == 1, lambda c: body(n - 1, c), lambda c: c, carry)


def _dsa_prompt_kernel(qi_ref, wit_ref, q_ref, ki_ref, k_ref, v_ref, o_ref,
                       sc_ref, bias_ref, vt_ref, j_ref, s_ref, acc_ref, *, topk, kc, pos_bits):
    i = pl.program_id(1)
    nck = (i * Q_BLOCK + Q_BLOCK + kc - 1) // kc
    nchunks = vt_ref.shape[0]
    rep = A_HEADS // A_KV_HEADS
    q_pos = i * Q_BLOCK + lax.broadcasted_iota(I32, (1, Q_BLOCK), 1)
    sub = lax.broadcasted_iota(I32, (kc, 1), 0)

    @pl.when(i == 0)
    def _():
        for c in range(nchunks):
            for g in range(A_KV_HEADS):
                gsl = slice(g * A_HEAD_DIM, (g + 1) * A_HEAD_DIM)
                vt_ref[c, gsl, :] = v_ref[c * kc:(c + 1) * kc, gsl].astype(F32).T.astype(BF16)

    def chunk_rows(c):
        return pl.ds(pl.multiple_of(c * kc, kc), kc)

    qi = qi_ref[...]
    wit = wit_ref[...]
    qi_pairs = [jnp.concatenate([qi[:, (2 * j) * IDX_DIM:(2 * j + 1) * IDX_DIM],
                                 qi[:, (2 * j + 1) * IDX_DIM:(2 * j + 2) * IDX_DIM]], axis=0)
                for j in range(IDX_HEADS // 2)]

    def score_body(c, carry):
        kic = ki_ref[chunk_rows(c), :]
        acc = jnp.zeros((kc, Q_BLOCK), F32)
        for j in range(IDX_HEADS // 2):
            d = _dot_nt(kic, qi_pairs[j])
            acc = acc + jnp.maximum(d[:, :Q_BLOCK], 0.0) * wit[2 * j:2 * j + 1, :]
            acc = acc + jnp.maximum(d[:, Q_BLOCK:], 0.0) * wit[2 * j + 1:2 * j + 2, :]
        sc_ref[c] = jnp.where(c * kc + sub <= q_pos, acc, NEG_INF)
        return carry

    _fori_by_two(nck, score_body, 0)

    def count(pred):
        def body(c, acc):
            part = jnp.where(pred(sc_ref[c], c * kc + sub), 1.0, 0.0)
            return acc + jnp.sum(part.reshape(kc // 64, 64, Q_BLOCK), axis=0)

        acc = lax.fori_loop(0, nck, body, jnp.zeros((64, Q_BLOCK), F32))
        return jnp.sum(acc, axis=0, keepdims=True)

    thr, jsel = _select_params(count, topk, pos_bits, j_ref)

    def bias_body(c, carry):
        s = sc_ref[c]
        pos = c * kc + sub
        sel = ((s > thr) | ((s == thr) & (pos <= jsel))) & (pos <= q_pos)
        bias_ref[c] = jnp.where(sel, 0.0, NEG_INF)
        return carry

    lax.fori_loop(0, nck, bias_body, 0)

    gsl = lambda g: slice(g * A_HEAD_DIM, (g + 1) * A_HEAD_DIM)
    row0 = lambda v: jnp.full((1, rep * Q_BLOCK), v, F32)
    groups = tuple(range(A_KV_HEADS))
    qgs = [jnp.concatenate([q_ref[:, gsl(g * rep + r)] for r in range(rep)], axis=0) for g in groups]

    def write_out(ls):
        for g in groups:
            o = acc_ref[g] / ls[g]
            for r in range(rep):
                o_ref[:, gsl(g * rep + r)] = o[:, r * Q_BLOCK:(r + 1) * Q_BLOCK].T.astype(BF16)

    def qk_body(c, ms):
        b = bias_ref[c]
        bias = jnp.concatenate([b] * rep, axis=1)
        out = []
        for g in groups:
            s = _dot_nt(k_ref[chunk_rows(c), gsl(g)], qgs[g]) + bias
            s_ref[c, g] = s
            out.append(jnp.maximum(ms[g], jnp.max(s, axis=0, keepdims=True)))
        return tuple(out)

    ms = _fori_by_two(nck, qk_body, (row0(NEG_INF),) * len(groups))
    acc_ref[...] = jnp.zeros(acc_ref.shape, F32)

    def pv_body(c, ls):
        out = []
        for g in groups:
            p = jnp.exp2(s_ref[c, g] - ms[g])
            out.append(ls[g] + jnp.sum(p, axis=0, keepdims=True))
            acc_ref[g] += _dot(vt_ref[c, gsl(g), :], p.astype(BF16))
        return tuple(out)

    write_out(_fori_by_two(nck, pv_body, (row0(0.0),) * len(groups)))


def _dsa_prompt(qi, wit, q, ki_b, k_b, v_b, batch, seq, topk):
    nqb = seq // Q_BLOCK
    rep = A_HEADS // A_KV_HEADS
    kc = min(512, seq)
    nchunks = seq // kc
    pos_bits = max(1, int(np.ceil(np.log2(seq))))
    qrow = lambda w: pl.BlockSpec((Q_BLOCK, w), lambda b, i: (b * nqb + i, 0))
    seqblk = lambda w: pl.BlockSpec((seq, w), lambda b, i: (b, 0), pipeline_mode=pl.Buffered(1))
    return pl.pallas_call(
        functools.partial(_dsa_prompt_kernel, topk=topk, kc=kc, pos_bits=pos_bits),
        grid=(batch, nqb),
        in_specs=[qrow(IDX_HEADS * IDX_DIM), pl.BlockSpec((IDX_HEADS, Q_BLOCK), lambda b, i: (0, b * nqb + i)),
                  qrow(A_WIDTH), seqblk(IDX_DIM), seqblk(A_KV_WIDTH), seqblk(A_KV_WIDTH)],
        out_specs=qrow(A_WIDTH),
        out_shape=jax.ShapeDtypeStruct((batch * seq, A_WIDTH), BF16),
        scratch_shapes=[pltpu.VMEM((nchunks, kc, Q_BLOCK), F32), pltpu.VMEM((nchunks, kc, Q_BLOCK), F32),
                        pltpu.VMEM((nchunks, A_KV_WIDTH, kc), BF16), pltpu.VMEM((1, Q_BLOCK), I32),
                        pltpu.VMEM((nchunks, A_KV_HEADS, kc, rep * Q_BLOCK), F32),
                        pltpu.VMEM((A_KV_HEADS, A_HEAD_DIM, rep * Q_BLOCK), F32)],
        compiler_params=_cparams(("arbitrary", "arbitrary")),
        name="dsa_prompt",
    )(qi, wit, q, ki_b, k_b, v_b)


def _fold_heads(e, nq):
    acc = e[0:8]
    for t in range(1, e.shape[0] // 8):
        acc = acc + e[8 * t:8 * (t + 1)]
    return acc + pltpu.roll(acc, nq, 0)


def _sample_scores_kernel(pt_ref, qp_ref, wc_ref, kin_ref, *refs, pages, nq):
    page_refs, (sc_ref, scn_ref) = refs[:pages], refs[pages:]
    qp = qp_ref[...]
    wc = wc_ref[...]
    kpt = jnp.concatenate([r[...] for r in page_refs], axis=1).astype(BF16)
    sc_ref[...] = _fold_heads(jnp.maximum(_dot(qp, kpt), 0.0) * wc, nq)

    @pl.when(pl.program_id(1) == 0)
    def _():
        kin = jnp.concatenate([kin_ref[...], jnp.zeros((PAGE - S_PAD, IDX_DIM), BF16)], axis=0)
        scn_ref[...] = _fold_heads(jnp.maximum(_dot_nt(qp, kin), 0.0) * wc, nq)


def _sample_scores(page_flat, qp, wcol, ki_new_b, cache_kidx, nseq, npages, pages, nq):
    steps = npages // pages
    rows = IDX_HEADS * nq
    page_specs = [pl.BlockSpec((None, IDX_DIM, PAGE),
                               lambda b, s, pt, p=p: (pt[b * npages + s * pages + p], 0, 0))
                  for p in range(pages)]
    grid_spec = pltpu.PrefetchScalarGridSpec(
        num_scalar_prefetch=1,
        grid=(nseq, steps),
        in_specs=[pl.BlockSpec((None, rows, IDX_DIM), lambda b, s, pt: (b, 0, 0)),
                  pl.BlockSpec((None, rows, 1), lambda b, s, pt: (b, 0, 0)),
                  pl.BlockSpec((S_PAD, IDX_DIM), lambda b, s, pt: (b, 0))] + page_specs,
        out_specs=[pl.BlockSpec((None, 8, pages * PAGE), lambda b, s, pt: (b, 0, s)),
                   pl.BlockSpec((None, 8, PAGE), lambda b, s, pt: (b, 0, 0))],
    )
    return pl.pallas_call(
        functools.partial(_sample_scores_kernel, pages=pages, nq=nq),
        grid_spec=grid_spec,
        out_shape=[jax.ShapeDtypeStruct((nseq, 8, npages * PAGE), F32),
                   jax.ShapeDtypeStruct((nseq, 8, PAGE), F32)],
        compiler_params=_cparams(("arbitrary", "arbitrary")),
        name="dsa_sample_scores",
    )(page_flat, qp, wcol, ki_new_b, *([cache_kidx] * pages))


def _sample_attn_kernel(pt_ref, sc_ref, scn_ref, qs_ref, kn_ref, vn_ref, *refs,
                        pages, nq, n_past, topk, pos_bits):
    k_pages, v_pages = refs[:pages], refs[pages:2 * pages]
    o_ref = refs[2 * pages]
    scm_ref, bias_ref, biasn_ref, j_ref, m_ref, l_ref, acc_ref, kbuf_ref, vbuf_ref = refs[2 * pages + 1:]
    s_idx = pl.program_id(1)
    kw = pages * PAGE
    rows_q = qs_ref.shape[1]

    @pl.when(s_idx == 0)
    def _():
        qrow = lax.broadcasted_iota(I32, (8, 1), 0) % nq
        lane_p = lax.broadcasted_iota(I32, (1, n_past), 1)
        lane_n = lax.broadcasted_iota(I32, (1, PAGE), 1)
        scm_ref[...] = jnp.where(lane_n <= qrow, scn_ref[...], NEG_INF)

        def count(pred):
            ones = jnp.where(pred(sc_ref[...], lane_p), 1.0, 0.0)
            parts = [ones[:, t * kw:(t + 1) * kw] for t in range(n_past // kw)]
            while len(parts) > 1:
                parts = [a + b for a, b in zip(parts[::2], parts[1::2])] + parts[len(parts) & ~1:]
            cp = jnp.sum(parts[0], axis=-1, keepdims=True)
            cn = jnp.sum(jnp.where(pred(scm_ref[...], n_past + lane_n), 1.0, 0.0), axis=-1, keepdims=True)
            return cp + cn

        thr, jsel = _select_params(count, topk, pos_bits, j_ref)
        kp = sc_ref[...]
        bias_p = jnp.where((kp > thr) | ((kp == thr) & (lane_p <= jsel)), 0.0, NEG_INF)
        for st in range(n_past // kw):
            bias_ref[st] = bias_p[:, st * kw:(st + 1) * kw]
        kn = scm_ref[...]
        seln = ((kn > thr) | ((kn == thr) & (n_past + lane_n <= jsel))) & (lane_n <= qrow)
        biasn_ref[...] = jnp.where(seln, 0.0, NEG_INF)
        m_ref[...] = jnp.full(m_ref.shape, NEG_INF, F32)
        l_ref[...] = jnp.zeros(l_ref.shape, F32)
        acc_ref[...] = jnp.zeros(acc_ref.shape, F32)

    for p in range(pages):
        for g in range(A_KV_HEADS):
            gsl = slice(g * A_HEAD_DIM, (g + 1) * A_HEAD_DIM)
            rows = pl.ds(g, PAGE, stride=A_KV_HEADS)
            kbuf_ref[p * PAGE:(p + 1) * PAGE, gsl] = k_pages[p][rows, :].astype(BF16)
            vbuf_ref[p * PAGE:(p + 1) * PAGE, gsl] = v_pages[p][rows, :].astype(BF16)

    def update(g, s, vv):
        m = m_ref[g]
        m_new = jnp.maximum(m, jnp.max(s, axis=-1, keepdims=True))
        m_safe = jnp.where(m_new == NEG_INF, 0.0, m_new)
        alpha = jnp.exp2(m - m_safe)
        p = jnp.exp2(s - m_safe)
        l_ref[g] = alpha * l_ref[g] + jnp.sum(p, axis=-1, keepdims=True)
        acc_ref[g] = alpha * acc_ref[g] + _dot(p.astype(BF16), vv)
        m_ref[g] = m_new

    bias = jnp.concatenate([bias_ref[s_idx]] * (rows_q // 8), axis=0)
    for g in range(A_KV_HEADS):
        gsl = slice(g * A_HEAD_DIM, (g + 1) * A_HEAD_DIM)
        update(g, _dot_nt(qs_ref[g], kbuf_ref[:, gsl]) + bias, vbuf_ref[:, gsl])

    @pl.when(s_idx == pl.num_programs(1) - 1)
    def _():
        bn = jnp.concatenate([biasn_ref[:, :S_PAD]] * (rows_q // 8), axis=0)
        for g in range(A_KV_HEADS):
            gsl = slice(g * A_HEAD_DIM, (g + 1) * A_HEAD_DIM)
            update(g, _dot_nt(qs_ref[g], kn_ref[:, gsl]) + bn, vn_ref[:, gsl])
            o_ref[g] = acc_ref[g] / l_ref[g]


def _sample_attn(page_flat, sc, scn, qs, k_new_b, v_new_b, cache_k, cache_v, nseq, npages, pages, nq, topk):
    steps = npages // pages
    n_past = npages * PAGE
    kw = pages * PAGE
    rows_q = qs.shape[2]
    pos_bits = int(np.ceil(np.log2(n_past + PAGE)))
    page_spec = lambda p: pl.BlockSpec((None, PAGE * A_KV_HEADS, A_HEAD_DIM),
                                       lambda b, s, pt, p=p: (pt[b * npages + s * pages + p], 0, 0))
    grid_spec = pltpu.PrefetchScalarGridSpec(
        num_scalar_prefetch=1,
        grid=(nseq, steps),
        in_specs=[pl.BlockSpec((None, 8, n_past), lambda b, s, pt: (b, 0, 0)),
                  pl.BlockSpec((None, 8, PAGE), lambda b, s, pt: (b, 0, 0)),
                  pl.BlockSpec((None, A_KV_HEADS, rows_q, A_HEAD_DIM), lambda b, s, pt: (b, 0, 0, 0)),
                  pl.BlockSpec((S_PAD, A_KV_WIDTH), lambda b, s, pt: (b, 0)),
                  pl.BlockSpec((S_PAD, A_KV_WIDTH), lambda b, s, pt: (b, 0))]
                 + [page_spec(p) for p in range(pages)] + [page_spec(p) for p in range(pages)],
        out_specs=pl.BlockSpec((None, A_KV_HEADS, rows_q, A_HEAD_DIM), lambda b, s, pt: (b, 0, 0, 0)),
        scratch_shapes=[pltpu.VMEM((8, PAGE), F32),
                        pltpu.VMEM((steps, 8, kw), F32), pltpu.VMEM((8, PAGE), F32),
                        pltpu.VMEM((8, 1), I32),
                        pltpu.VMEM((A_KV_HEADS, rows_q, 1), F32), pltpu.VMEM((A_KV_HEADS, rows_q, 1), F32),
                        pltpu.VMEM((A_KV_HEADS, rows_q, A_HEAD_DIM), F32),
                        pltpu.VMEM((kw, A_KV_WIDTH), BF16), pltpu.VMEM((kw, A_KV_WIDTH), BF16)],
    )
    return pl.pallas_call(
        functools.partial(_sample_attn_kernel, pages=pages, nq=nq, n_past=n_past, topk=topk, pos_bits=pos_bits),
        grid_spec=grid_spec,
        out_shape=jax.ShapeDtypeStruct((nseq, A_KV_HEADS, rows_q, A_HEAD_DIM), F32),
        compiler_params=_cparams(("arbitrary", "arbitrary")),
        name="dsa_sample_attn",
    )(page_flat, sc, scn, qs, k_new_b, v_new_b, *([cache_k] * pages), *([cache_v] * pages))


def _tree(op, parts):
    parts = list(parts)
    while len(parts) > 1:
        parts = [op(a, b) for a, b in zip(parts[::2], parts[1::2])] + parts[len(parts) & ~1:]
    return parts[0]


def _lane_fold(op, x):
    return _tree(op, [x[:, t * 128:(t + 1) * 128] for t in range(x.shape[1] // 128)])


def _dsa_sample_kernel(pt_ref, qp_ref, wc_ref, kin_ref, qs_ref, kn_ref, vn_ref, *refs,
                       pages, nq, n_past, topk, pos_bits):
    ki_pages, k_pages, v_pages = refs[:pages], refs[pages:2 * pages], refs[2 * pages:3 * pages]
    o_ref = refs[3 * pages]
    (sc_ref, scn_ref, bias_ref, biasn_ref, j_ref, s_ref, sn_ref, m_ref, mfin_ref, l_ref, acc_ref,
     buf_ref) = refs[3 * pages + 1:]
    s_idx = pl.program_id(1)
    kw = pages * PAGE
    steps = n_past // kw
    rows_q = qs_ref.shape[1]
    dup = lambda x8: jnp.concatenate([x8] * (rows_q // 8), axis=0)
    gsl = lambda g: slice(g * A_HEAD_DIM, (g + 1) * A_HEAD_DIM)

    def load_pages(page_refs):
        for p in range(pages):
            for g in range(A_KV_HEADS):
                rows = pl.ds(g, PAGE, stride=A_KV_HEADS)
                buf_ref[p * PAGE:(p + 1) * PAGE, gsl(g)] = page_refs[p][rows, :].astype(BF16)

    @pl.when(s_idx < steps)
    def _():
        qp = qp_ref[...]
        wc = wc_ref[...]
        kpt = jnp.concatenate([r[...] for r in ki_pages], axis=1).astype(BF16)
        sc_ref[s_idx] = _fold_heads(jnp.maximum(_dot(qp, kpt), 0.0) * wc, nq)

        @pl.when(s_idx == steps - 1)
        def _():
            qrow = lax.broadcasted_iota(I32, (8, 1), 0) % nq
            lane = lax.broadcasted_iota(I32, (1, kw), 1)
            lane_n = lax.broadcasted_iota(I32, (1, PAGE), 1)
            kin = jnp.concatenate([kin_ref[...], jnp.zeros((PAGE - S_PAD, IDX_DIM), BF16)], axis=0)
            scn = _fold_heads(jnp.maximum(_dot_nt(qp, kin), 0.0) * wc, nq)
            scn_ref[...] = jnp.where(lane_n <= qrow, scn, NEG_INF)

            def count(pred):
                ones = _tree(jnp.add, [jnp.where(pred(sc_ref[t], t * kw + lane), 1.0, 0.0) for t in range(steps)])
                cn = jnp.where(pred(scn_ref[...], n_past + lane_n), 1.0, 0.0)
                return (jnp.sum(_lane_fold(jnp.add, ones), axis=-1, keepdims=True)
                        + jnp.sum(cn, axis=-1, keepdims=True))

            thr, jsel = _select_params(count, topk, pos_bits, j_ref)
            for t in range(steps):
                sp = sc_ref[t]
                bias_ref[t] = jnp.where((sp > thr) | ((sp == thr) & (t * kw + lane <= jsel)), 0.0, NEG_INF)
            sn = scn_ref[...]
            seln = ((sn > thr) | ((sn == thr) & (n_past + lane_n <= jsel))) & (lane_n <= qrow)
            biasn_ref[...] = jnp.where(seln, 0.0, NEG_INF)

    @pl.when((s_idx >= steps) & (s_idx < 2 * steps))
    def _():
        t = s_idx - steps
        load_pages(k_pages)
        bias = dup(bias_ref[t])

        @pl.when(t == 0)
        def _():
            m_ref[...] = jnp.full(m_ref.shape, NEG_INF, F32)

        for g in range(A_KV_HEADS):
            s = _dot_nt(qs_ref[g], buf_ref[:, gsl(g)]) + bias
            s_ref[g, t] = s
            m_ref[g] = jnp.maximum(m_ref[g], _lane_fold(jnp.maximum, s))

        @pl.when(t == steps - 1)
        def _():
            bn = dup(biasn_ref[:, :S_PAD])
            for g in range(A_KV_HEADS):
                sn = _dot_nt(qs_ref[g], kn_ref[:, gsl(g)]) + bn
                sn_ref[g] = sn
                mfin_ref[g] = jnp.maximum(jnp.max(m_ref[g], axis=-1, keepdims=True),
                                          jnp.max(sn, axis=-1, keepdims=True))

    @pl.when(s_idx >= 2 * steps)
    def _():
        t = s_idx - 2 * steps
        load_pages(v_pages)

        @pl.when(t == 0)
        def _():
            l_ref[...] = jnp.zeros(l_ref.shape, F32)
            acc_ref[...] = jnp.zeros(acc_ref.shape, F32)

        for g in range(A_KV_HEADS):
            p = jnp.exp2(s_ref[g, t] - mfin_ref[g])
            l_ref[g] += _lane_fold(jnp.add, p)
            acc_ref[g] += _dot(p.astype(BF16), buf_ref[:, gsl(g)])

        @pl.when(t == steps - 1)
        def _():
            for g in range(A_KV_HEADS):
                pn = jnp.exp2(sn_ref[g] - mfin_ref[g])
                l = jnp.sum(l_ref[g], axis=-1, keepdims=True) + jnp.sum(pn, axis=-1, keepdims=True)
                o_ref[g] = (acc_ref[g] + _dot(pn.astype(BF16), vn_ref[:, gsl(g)])) / l


def _dsa_sample(page_flat, qp, wcol, ki_new_b, qs, k_new_b, v_new_b, cache_kidx_t, cache_k, cache_v,
                nseq, npages, pages, nq, topk):
    steps = npages // pages
    n_past = npages * PAGE
    kw = pages * PAGE
    rows = IDX_HEADS * nq
    rows_q = qs.shape[2]
    pos_bits = int(np.ceil(np.log2(n_past + PAGE)))

    def page_spec(shape, phase, p):
        def index(b, s, pt):
            t = jnp.clip(s - phase * steps, 0, steps - 1)
            return (pt[b * npages + t * pages + p], 0, 0)
        return pl.BlockSpec((None,) + shape, index)

    seq3 = lambda shape: pl.BlockSpec((None,) + shape, lambda b, s, pt: (b,) + (0,) * len(shape))
    new_rows = lambda w: pl.BlockSpec((S_PAD, w), lambda b, s, pt: (b, 0))
    grid_spec = pltpu.PrefetchScalarGridSpec(
        num_scalar_prefetch=1,
        grid=(nseq, 3 * steps),
        in_specs=[seq3((rows, IDX_DIM)), seq3((rows, 1)), new_rows(IDX_DIM),
                  seq3((A_KV_HEADS, rows_q, A_HEAD_DIM)), new_rows(A_KV_WIDTH), new_rows(A_KV_WIDTH)]
                 + [page_spec((IDX_DIM, PAGE), 0, p) for p in range(pages)]
                 + [page_spec((PAGE * A_KV_HEADS, A_HEAD_DIM), 1, p) for p in range(pages)]
                 + [page_spec((PAGE * A_KV_HEADS, A_HEAD_DIM), 2, p) for p in range(pages)],
        out_specs=seq3((A_KV_HEADS, rows_q, A_HEAD_DIM)),
        scratch_shapes=[pltpu.VMEM((steps, 8, kw), F32), pltpu.VMEM((8, PAGE), F32),
                        pltpu.VMEM((steps, 8, kw), F32), pltpu.VMEM((8, PAGE), F32),
                        pltpu.VMEM((8, 1), I32),
                        pltpu.VMEM((A_KV_HEADS, steps, rows_q, kw), F32),
                        pltpu.VMEM((A_KV_HEADS, rows_q, S_PAD), F32),
                        pltpu.VMEM((A_KV_HEADS, rows_q, 128), F32),
                        pltpu.VMEM((A_KV_HEADS, rows_q, 1), F32),
                        pltpu.VMEM((A_KV_HEADS, rows_q, 128), F32),
                        pltpu.VMEM((A_KV_HEADS, rows_q, A_HEAD_DIM), F32),
                        pltpu.VMEM((kw, A_KV_WIDTH), BF16)],
    )
    return pl.pallas_call(
        functools.partial(_dsa_sample_kernel, pages=pages, nq=nq, n_past=n_past, topk=topk, pos_bits=pos_bits),
        grid_spec=grid_spec,
        out_shape=jax.ShapeDtypeStruct((nseq, A_KV_HEADS, rows_q, A_HEAD_DIM), F32),
        compiler_params=_cparams(("arbitrary", "arbitrary")),
        name="dsa_sample",
    )(page_flat, qp, wcol, ki_new_b, qs, k_new_b, v_new_b,
      *([cache_kidx_t] * pages), *([cache_k] * pages), *([cache_v] * pages))


def _merge_kernel(x_ref, oa_ref, ag_ref, bu_ref, bv_ref, bg_ref, cq_ref, cg_ref, ra_ref, rb_ref, rc_ref,
                  mk_ref, mv_ref, ws_ref, bs_ref, gsgu_ref, gmq_ref, wpa_ref, wpb_ref, wpc_ref, wout_ref,
                  y_ref, *maybe_vn_ref, tm, chunk):
    f32 = lambda r: r[...].astype(F32)
    silu = lambda t: t * jax.nn.sigmoid(t)

    vn = _rms(f32(bv_ref), gsgu_ref[...])
    if maybe_vn_ref:
        maybe_vn_ref[0][...] = vn
    vnb = vn.astype(BF16)
    bu = f32(bu_ref)
    tril = (lax.broadcasted_iota(I32, (chunk, chunk), 1) <= lax.broadcasted_iota(I32, (chunk, chunk), 0))
    ob_cols = []
    for g in range(B_GROUPS):
        wg = jnp.where(tril, ws_ref[g], 0.0).astype(BF16)
        gsl = slice(g * B_GROUP_DIM, (g + 1) * B_GROUP_DIM)
        parts = [_dot(wg, vnb[c * chunk:(c + 1) * chunk, gsl]) + bs_ref[:, g:g + 1] for c in range(tm // chunk)]
        ob_cols.append(parts[0] if len(parts) == 1 else jnp.concatenate(parts, axis=0))
    ob = bu * jnp.concatenate(ob_cols, axis=1)
    pb = _dot((ob * silu(f32(bg_ref))).astype(BF16), wpb_ref[...])

    cq = f32(cq_ref)
    oc_cols = []
    for hh in range(M_HEADS):
        hsl = slice(hh * M_HEAD_DIM, (hh + 1) * M_HEAD_DIM)
        qn = (_rms(cq[:, hsl], gmq_ref[...]) * (M_HEAD_DIM ** -0.5)).astype(BF16)
        s = _dot_nt(qn, mk_ref[:, hsl])
        p = jnp.exp(s - jnp.max(s, axis=-1, keepdims=True))
        oc_cols.append(_dot(p.astype(BF16), mv_ref[:, hsl]) / jnp.sum(p, axis=-1, keepdims=True))
    oc = jnp.concatenate(oc_cols, axis=1)
    pc = _dot((oc * silu(f32(cg_ref))).astype(BF16), wpc_ref[...])

    pa = _dot((f32(oa_ref) * silu(f32(ag_ref))).astype(BF16), wpa_ref[...])
    sig = jax.nn.sigmoid
    m = sig(f32(ra_ref)) * pa + sig(f32(rb_ref)) * pb + sig(f32(rc_ref)) * pc
    y_ref[...] = x_ref[...] + _dot(m.astype(BF16), wout_ref[...])


def _merge(x, oa, zr, mk_b, mv_b, ws, bs_t, g_sgu, g_mq, w_pa, w_pb, w_pc, w_out, tm, chunk, mem_map, emit_vn):
    n = x.shape[0]
    col = lambda w, j: pl.BlockSpec((tm, w), lambda i, j=j: (i, j))
    mem = pl.BlockSpec((N_MEM, M_WIDTH), mem_map)
    in_specs = [col(D_MODEL, 0), col(A_WIDTH, 0),
                col(1024, 0), col(1024, 1), col(1024, 2), col(1024, 3), col(1024, 4), col(1024, 5),
                col(2048, 3), col(2048, 4), col(2048, 5),
                mem, mem,
                _const_spec((B_GROUPS, chunk, chunk)), _const_spec((chunk, B_GROUPS)),
                _const_spec((1, B_WIDTH)), _const_spec((1, M_HEAD_DIM)),
                _const_spec((A_WIDTH, D_MODEL)), _const_spec((B_WIDTH, D_MODEL)),
                _const_spec((M_WIDTH, D_MODEL)), _const_spec((D_MODEL, D_MODEL))]
    out_specs = [col(D_MODEL, 0)]
    out_shape = [jax.ShapeDtypeStruct((n, D_MODEL), F32)]
    if emit_vn:
        out_specs.append(col(B_WIDTH, 0))
        out_shape.append(jax.ShapeDtypeStruct((n, B_WIDTH), F32))
    return pl.pallas_call(
        functools.partial(_merge_kernel, tm=tm, chunk=chunk),
        grid=(n // tm,),
        in_specs=in_specs,
        out_specs=out_specs,
        out_shape=out_shape,
        compiler_params=_cparams(("arbitrary",)),
        name="merge",
    )(x, oa, *([zr] * 9), mk_b, mv_b, ws, bs_t, g_sgu, g_mq, w_pa, w_pb, w_pc, w_out)


def _rope_tables(pos):
    pos = pos.astype(F32)[:, None]

    def cs(half):
        freq = ROPE_THETA ** (-jnp.arange(half, dtype=F32) / half)
        ang = pos * freq[None, :]
        return jnp.cos(ang), jnp.sin(ang)

    c, s = cs(A_HEAD_DIM // 2)
    ci, si = cs(IDX_DIM // 2)
    z = jnp.zeros_like(si)
    return (jnp.concatenate([c, c], axis=1), jnp.concatenate([-s, s], axis=1),
            jnp.concatenate([ci] * 4, axis=1), jnp.concatenate([-si, z, -si, z], axis=1),
            jnp.concatenate([z, si, z, si], axis=1))


def kernel(x_prompt, x_sample, cache_k, cache_v, cache_kidx, cache_mem_k, cache_mem_v, page_table,
           mem_prompt, g_pre, w_in, g_q, g_k, g_mq, g_mk, g_mem, w_mem_kv, g_sgu, w_s, b_s,
           w_pa, w_pb, w_pc, w_out):
    batch, seq, _ = x_prompt.shape
    nseq, nq, _ = x_sample.shape
    assert nq == 4 and nq <= S_PAD
    npages = page_table.shape[1]
    n_past = npages * PAGE
    n_pool = cache_k.shape[0]
    row2 = lambda a: a.reshape(1, -1)

    w_in_t = w_in.T
    w_a = w_in_t[:A_COLS].astype(BF16)
    w_pa_b, w_pb_b, w_pc_b, w_out_b = (w.astype(BF16) for w in (w_pa, w_pb, w_pc, w_out))
    w_mem_b = w_mem_kv.astype(BF16)
    g_pre2, g_q2, g_k2, g_mq2, g_mk2, g_mem2, g_sgu2 = map(row2, (g_pre, g_q, g_k, g_mq, g_mk, g_mem, g_sgu))

    xp = x_prompt.reshape(batch * seq, D_MODEL)
    tm_a = min(256, seq)
    nblk = seq // tm_a
    tabs_p = _rope_tables(jnp.arange(seq))
    q, k_p, v_p, k_b, v_b, qi, ki_p, ki_b, wi = _proj_a(
        xp, g_pre2, w_a, g_q2, g_k2, tabs_p, tm_a, lambda i: (i % nblk, 0))
    zr = _proj_rest(xp, g_pre2, w_in_t, min(1024, seq), PROJ_TN)
    mk_p, mv_p, mk_b, mv_b = _mem_kv(mem_prompt.reshape(batch * N_MEM, D_MODEL), g_mem2, w_mem_b, g_mk2)
    oa = _dsa_prompt(qi, wi.T, q, ki_b, k_b, v_b, batch, seq, min(TOPK_MAX, seq // 4))
    tm_m = min(256, seq)
    nblk_m = seq // tm_m
    (y_p,) = _merge(xp, oa, zr, mk_b, mv_b, w_s, b_s.T, g_sgu2, g_mq2, w_pa_b, w_pb_b, w_pc_b, w_out_b,
                    tm_m, CHUNK, lambda i: (i // nblk_m, 0), False)

    xs = jnp.pad(x_sample, ((0, 0), (0, S_PAD - nq), (0, 0))).reshape(nseq * S_PAD, D_MODEL)
    tabs_s = tuple(jnp.tile(t, (nseq, 1)) for t in _rope_tables(n_past + jnp.arange(S_PAD)))
    rows_s = nseq * S_PAD
    q_s, k_s, v_s, k_sb, v_sb, qi_s, ki_s, ki_sb, wi_s = _proj_a(
        xs, g_pre2, w_a, g_q2, g_k2, tabs_s, rows_s, lambda i: (0, 0))
    zr_s = _proj_rest(xs, g_pre2, w_in_t, rows_s, PROJ_TN)

    qp = (qi_s.reshape(nseq, S_PAD, IDX_HEADS, IDX_DIM)[:, :nq]
          .transpose(0, 2, 1, 3).reshape(nseq, IDX_HEADS * nq, IDX_DIM))
    wcol = wi_s.reshape(nseq, S_PAD, IDX_HEADS)[:, :nq].transpose(0, 2, 1).reshape(nseq, IDX_HEADS * nq, 1)
    rep = A_HEADS // A_KV_HEADS
    qs = (q_s.reshape(nseq, S_PAD, A_KV_HEADS, rep, A_HEAD_DIM)[:, :nq]
          .transpose(0, 2, 3, 1, 4).reshape(nseq, A_KV_HEADS, rep * nq, A_HEAD_DIM))
    qs = jnp.concatenate([qs, qs], axis=2)
    page_flat = page_table.reshape(-1)
    pages = min(SAMPLE_PAGES, npages)
    o_s = _dsa_sample(page_flat, qp, wcol, ki_sb, qs, k_sb, v_sb, jnp.swapaxes(cache_kidx, 1, 2),
                      cache_k.reshape(n_pool, PAGE * A_KV_HEADS, A_HEAD_DIM),
                      cache_v.reshape(n_pool, PAGE * A_KV_HEADS, A_HEAD_DIM),
                      nseq, npages, pages, nq, min(TOPK_MAX, (n_past + nq) // 4))
    oa_s = (o_s[:, :, :rep * nq].reshape(nseq, A_KV_HEADS, rep, nq, A_HEAD_DIM)
            .transpose(0, 3, 1, 2, 4).reshape(nseq, nq, A_WIDTH))
    oa_s = jnp.pad(oa_s, ((0, 0), (0, S_PAD - nq), (0, 0))).reshape(rows_s, A_WIDTH).astype(BF16)
    mk_s = cache_mem_k.reshape(nseq * N_MEM, M_WIDTH).astype(BF16)
    mv_s = cache_mem_v.reshape(nseq * N_MEM, M_WIDTH).astype(BF16)
    y_s, vn_s = _merge(xs, oa_s, zr_s, mk_s, mv_s, w_s[:, :S_PAD, :S_PAD], b_s[:, :S_PAD].T, g_sgu2, g_mq2,
                       w_pa_b, w_pb_b, w_pc_b, w_out_b, S_PAD, S_PAD, lambda i: (i, 0), True)

    take = lambda a, shape: a.reshape(nseq, S_PAD, -1)[:, :nq].reshape(shape)
    return (y_p.reshape(batch, seq, D_MODEL),
            take(y_s, (nseq, nq, D_MODEL)),
            k_p.reshape(batch, seq, A_KV_HEADS, A_HEAD_DIM),
            v_p.reshape(batch, seq, A_KV_HEADS, A_HEAD_DIM),
            ki_p.reshape(batch, seq, IDX_DIM),
            mk_p.reshape(batch, N_MEM, M_HEADS, M_HEAD_DIM),
            mv_p.reshape(batch, N_MEM, M_HEADS, M_HEAD_DIM),
            take(k_s, (nseq, nq, A_KV_HEADS, A_HEAD_DIM)),
            take(v_s, (nseq, nq, A_KV_HEADS, A_HEAD_DIM)),
            take(ki_s, (nseq, nq, IDX_DIM)),
            take(vn_s, (nseq, nq, B_GROUPS, B_GROUP_DIM)))
```

```python
import functools

import numpy as np
import jax
import jax.numpy as jnp
from jax import lax
from jax.experimental import pallas as pl
from jax.experimental.pallas import tpu as pltpu

F32 = jnp.float32
BF16 = jnp.bfloat16
I32 = jnp.int32

D_MODEL = 2048
PAGE = 128
A_HEADS = 8
A_KV_HEADS = 4
A_HEAD_DIM = 128
A_WIDTH = A_HEADS * A_HEAD_DIM
A_KV_WIDTH = A_KV_HEADS * A_HEAD_DIM
IDX_HEADS = 16
IDX_DIM = 64
TOPK_MAX = 256
Q_BLOCK = 128
ROPE_THETA = 10000.0
CHUNK = 128
B_GROUPS = 8
B_GROUP_DIM = 128
B_WIDTH = B_GROUPS * B_GROUP_DIM
N_MEM = 256
M_HEADS = 4
M_HEAD_DIM = 256
M_WIDTH = M_HEADS * M_HEAD_DIM
EPS = 1e-6

OFF_K = A_WIDTH
OFF_V = OFF_K + A_KV_WIDTH
OFF_QI = OFF_V + A_KV_WIDTH
OFF_KI = OFF_QI + IDX_HEADS * IDX_DIM
OFF_WI = OFF_KI + IDX_DIM
OFF_REST = OFF_WI + IDX_HEADS
A_COLS = 3200
REST_COLS = A_WIDTH + 3 * B_WIDTH + 2 * M_WIDTH + 3 * D_MODEL

Q_SCALE = float(np.log2(np.e)) * A_HEAD_DIM ** -0.5
S_PAD = 16
PROJ_TN = 1024
SAMPLE_PAGES = 16
INT_MIN = np.int32(-2 ** 31)
INT_MAX = np.int32(2 ** 31 - 1)
NEG_INF = float("-inf")

V7X_VMEM_LIMIT = 56 * 1024 * 1024


def _cparams(sem):
    return pltpu.CompilerParams(dimension_semantics=sem, vmem_limit_bytes=V7X_VMEM_LIMIT)


def _dot(a, b):
    return jnp.dot(a, b, preferred_element_type=F32)


def _dot_nt(a, b):
    return lax.dot_general(a, b, (((1,), (1,)), ((), ())), preferred_element_type=F32)


def _rms(x, g):
    return x * lax.rsqrt(jnp.mean(x * x, axis=-1, keepdims=True) + EPS) * g


def _const_spec(shape):
    nd = len(shape)
    return pl.BlockSpec(shape, lambda *_: (0,) * nd, pipeline_mode=pl.Buffered(1))


def _proj_a_kernel(x_ref, g_ref, w_ref, gq_ref, gk_ref, cq_ref, sq_ref, ci_ref, sia_ref, sib_ref,
                   q_ref, k_ref, v_ref, kb_ref, vb_ref, qi_ref, ki_ref, kib_ref, wi_ref):
    h = _rms(x_ref[...], g_ref[...]).astype(BF16)
    z = _dot_nt(h, w_ref[...])
    tm = z.shape[0]
    cq, sq = cq_ref[...], sq_ref[...]
    ci, sia, sib = ci_ref[...], sia_ref[...], sib_ref[...]

    def norm_rope(zz, g):
        n = _rms(zz, g)
        return n * cq + pltpu.roll(n, A_HEAD_DIM // 2, 1) * sq

    def rope_idx(zz):
        return zz * ci + pltpu.roll(zz, 96, 1) * sia + pltpu.roll(zz, 32, 1) * sib

    for hh in range(A_HEADS):
        sl = slice(hh * A_HEAD_DIM, (hh + 1) * A_HEAD_DIM)
        q_ref[:, sl] = (norm_rope(z[:, sl], gq_ref[...]) * Q_SCALE).astype(BF16)
    for hh in range(A_KV_HEADS):
        sl = slice(hh * A_HEAD_DIM, (hh + 1) * A_HEAD_DIM)
        kh = norm_rope(z[:, OFF_K + hh * A_HEAD_DIM:OFF_K + (hh + 1) * A_HEAD_DIM], gk_ref[...])
        vh = z[:, OFF_V + hh * A_HEAD_DIM:OFF_V + (hh + 1) * A_HEAD_DIM]
        head_rows = pl.ds(hh, tm, stride=A_KV_HEADS)
        k_ref[head_rows, :] = kh
        v_ref[head_rows, :] = vh
        kb_ref[:, sl] = kh.astype(BF16)
        vb_ref[:, sl] = vh.astype(BF16)
    for t in range(IDX_HEADS * IDX_DIM // 128):
        sl = slice(t * 128, (t + 1) * 128)
        qi_ref[:, sl] = rope_idx(z[:, OFF_QI + t * 128:OFF_QI + (t + 1) * 128]).astype(BF16)
    last = z[:, OFF_KI:OFF_KI + 128]
    ki = rope_idx(last)[:, :IDX_DIM]
    ki_ref[...] = ki
    kib_ref[...] = ki.astype(BF16)
    wi_ref[...] = last[:, IDX_DIM:IDX_DIM + IDX_HEADS] * ((IDX_HEADS ** -0.5) * (IDX_DIM ** -0.5))


def _proj_a(x, g_pre, w_a, g_q, g_k, tabs, tm, tab_map):
    n = x.shape[0]
    row = lambda w: pl.BlockSpec((tm, w), lambda i: (i, 0))
    tab = pl.BlockSpec((tm, 128), tab_map)
    outs = [(1, A_WIDTH, BF16), (A_KV_HEADS, A_HEAD_DIM, F32), (A_KV_HEADS, A_HEAD_DIM, F32),
            (1, A_KV_WIDTH, BF16), (1, A_KV_WIDTH, BF16),
            (1, IDX_HEADS * IDX_DIM, BF16), (1, IDX_DIM, F32), (1, IDX_DIM, BF16), (1, IDX_HEADS, F32)]
    return pl.pallas_call(
        _proj_a_kernel,
        grid=(n // tm,),
        in_specs=[row(D_MODEL), _const_spec((1, D_MODEL)), _const_spec((A_COLS, D_MODEL)),
                  _const_spec((1, A_HEAD_DIM)), _const_spec((1, A_HEAD_DIM)), tab, tab, tab, tab, tab],
        out_specs=[pl.BlockSpec((tm * r, w), lambda i: (i, 0)) for r, w, _ in outs],
        out_shape=[jax.ShapeDtypeStruct((n * r, w), dt) for r, w, dt in outs],
        compiler_params=_cparams(("arbitrary",)),
        name="proj_a",
    )(x, g_pre, w_a, g_q, g_k, *tabs)


def _proj_rest_kernel(x_ref, g_ref, w_ref, o_ref, h_ref):
    @pl.when(pl.program_id(1) == 0)
    def _():
        h_ref[...] = _rms(x_ref[...], g_ref[...]).astype(BF16)

    o_ref[...] = _dot_nt(h_ref[...], w_ref[...].astype(BF16)).astype(BF16)


def _proj_rest(x, g_pre, w_in_t, tm, tn):
    n = x.shape[0]
    return pl.pallas_call(
        _proj_rest_kernel,
        grid=(n // tm, REST_COLS // tn),
        in_specs=[pl.BlockSpec((tm, D_MODEL), lambda i, j: (i, 0), pipeline_mode=pl.Buffered(1)),
                  pl.BlockSpec((1, D_MODEL), lambda i, j: (0, 0)),
                  pl.BlockSpec((pl.Element(tn), pl.Element(D_MODEL)), lambda i, j: (pl.multiple_of(OFF_REST + j * tn, 16), 0))],
        out_specs=pl.BlockSpec((tm, tn), lambda i, j: (i, j)),
        out_shape=jax.ShapeDtypeStruct((n, REST_COLS), BF16),
        scratch_shapes=[pltpu.VMEM((tm, D_MODEL), BF16)],
        compiler_params=_cparams(("arbitrary", "arbitrary")),
        name="proj_rest",
    )(x, g_pre, w_in_t)


def _mem_kv_kernel(x_ref, g_ref, w_ref, gk_ref, k_ref, v_ref, kb_ref, vb_ref):
    h = _rms(x_ref[...], g_ref[...]).astype(BF16)
    z = _dot(h, w_ref[...])
    for hh in range(M_HEADS):
        sl = slice(hh * M_HEAD_DIM, (hh + 1) * M_HEAD_DIM)
        kh = _rms(z[:, sl], gk_ref[...])
        k_ref[:, sl] = kh
        kb_ref[:, sl] = kh.astype(BF16)
    v = z[:, M_WIDTH:]
    v_ref[...] = v
    vb_ref[...] = v.astype(BF16)


def _mem_kv(mem, g_mem, w_mem, g_mk):
    n = mem.shape[0]
    blk = pl.BlockSpec((N_MEM, M_WIDTH), lambda i: (i, 0))
    return pl.pallas_call(
        _mem_kv_kernel,
        grid=(n // N_MEM,),
        in_specs=[pl.BlockSpec((N_MEM, D_MODEL), lambda i: (i, 0)), _const_spec((1, D_MODEL)),
                  _const_spec((D_MODEL, 2 * M_WIDTH)), _const_spec((1, M_HEAD_DIM))],
        out_specs=[blk, blk, blk, blk],
        out_shape=[jax.ShapeDtypeStruct((n, M_WIDTH), dt) for dt in (F32, F32, BF16, BF16)],
        compiler_params=_cparams(("arbitrary",)),
        name="mem_kv",
    )(mem, g_mem, w_mem, g_mk)


KEY_NEG_INF = np.int32(-0x7F800000)


def _key_to_f32(key):
    return pltpu.bitcast(jnp.where(key >= 0, key, INT_MIN - key), F32)


def _select_params(count, topk, pos_bits, j_ref):
    kf = float(topk)
    t0 = jnp.where(count(lambda s, p: s >= 0.0) >= kf, jnp.int32(0), INT_MIN)

    def bit_body(b, t):
        cand = t + lax.shift_left(jnp.int32(1), 30 - b)
        cand_f = _key_to_f32(cand)
        return jnp.where(count(lambda s, p: s >= cand_f) >= kf, cand, t)

    t = _key_to_f32(jnp.maximum(lax.fori_loop(0, 31, bit_body, t0), KEY_NEG_INF))
    tie = (count(lambda s, p: s >= t) > kf) & (t > NEG_INF)
    j_ref[...] = jnp.full(j_ref.shape, INT_MAX, I32)

    @pl.when(jnp.max(tie.astype(I32)) > 0)
    def _():
        n_gt = count(lambda s, p: s > t)

        def pos_body(b, p_lo):
            cand = p_lo + lax.shift_left(jnp.int32(1), pos_bits - 1 - b)
            n_eq = count(lambda s, p: (s == t) & (p < cand))
            return jnp.where(n_gt + n_eq < kf, cand, p_lo)

        p_sel = lax.fori_loop(0, pos_bits, pos_body, jnp.zeros(t.shape, I32))
        j_ref[...] = jnp.where(tie, p_sel, INT_MAX)

    return t, j_ref[...]


def _fori_by_two(n, body, init):
    carry = lax.fori_loop(0, n // 2, lambda j, c: body(2 * j + 1, body(2 * j, c)), init)
    return lax.cond(n % 2 == 1, lambda c: body(n - 1, c), lambda c: c, carry)


def _dsa_prompt_kernel(qi_ref, wit_ref, q_ref, ki_ref, k_ref, v_ref, o_ref,
                       sc_ref, bias_ref, vt_ref, j_ref, s_ref, acc_ref, *, topk, kc, pos_bits):
    i = pl.program_id(1)
    nck = (i * Q_BLOCK + Q_BLOCK + kc - 1) // kc
    nchunks = vt_ref.shape[0]
    rep = A_HEADS // A_KV_HEADS
    q_pos = i * Q_BLOCK + lax.broadcasted_iota(I32, (1, Q_BLOCK), 1)
    sub = lax.broadcasted_iota(I32, (kc, 1), 0)

    @pl.when(i == 0)
    def _():
        for c in range(nchunks):
            for g in range(A_KV_HEADS):
                gsl = slice(g * A_HEAD_DIM, (g + 1) * A_HEAD_DIM)
                vt_ref[c, gsl, :] = v_ref[c * kc:(c + 1) * kc, gsl].astype(F32).T.astype(BF16)

    def chunk_rows(c):
        return pl.ds(pl.multiple_of(c * kc, kc), kc)

    qi = qi_ref[...]
    wit = wit_ref[...]
    qi_pairs = [jnp.concatenate([qi[:, (2 * j) * IDX_DIM:(2 * j + 1) * IDX_DIM],
                                 qi[:, (2 * j + 1) * IDX_DIM:(2 * j + 2) * IDX_DIM]], axis=0)
                for j in range(IDX_HEADS // 2)]

    def score_body(c, carry):
        kic = ki_ref[chunk_rows(c), :]
        acc = jnp.zeros((kc, Q_BLOCK), F32)
        for j in range(IDX_HEADS // 2):
            d = _dot_nt(kic, qi_pairs[j])
            acc = acc + jnp.maximum(d[:, :Q_BLOCK], 0.0) * wit[2 * j:2 * j + 1, :]
            acc = acc + jnp.maximum(d[:, Q_BLOCK:], 0.0) * wit[2 * j + 1:2 * j + 2, :]
        sc_ref[c] = jnp.where(c * kc + sub <= q_pos, acc, NEG_INF)
        return carry

    _fori_by_two(nck, score_body, 0)

    def count(pred):
        def body(c, acc):
            part = jnp.where(pred(sc_ref[c], c * kc + sub), 1.0, 0.0)
            return acc + jnp.sum(part.reshape(kc // 64, 64, Q_BLOCK), axis=0)

        acc = lax.fori_loop(0, nck, body, jnp.zeros((64, Q_BLOCK), F32))
        return jnp.sum(acc, axis=0, keepdims=True)

    thr, jsel = _select_params(count, topk, pos_bits, j_ref)

    def bias_body(c, carry):
        s = sc_ref[c]
        pos = c * kc + sub
        sel = ((s > thr) | ((s == thr) & (pos <= jsel))) & (pos <= q_pos)
        bias_ref[c] = jnp.where(sel, 0.0, NEG_INF)
        return carry

    lax.fori_loop(0, nck, bias_body, 0)

    gsl = lambda g: slice(g * A_HEAD_DIM, (g + 1) * A_HEAD_DIM)
    row0 = lambda v: jnp.full((1, rep * Q_BLOCK), v, F32)
    groups = tuple(range(A_KV_HEADS))
    qgs = [jnp.concatenate([q_ref[:, gsl(g * rep + r)] for r in range(rep)], axis=0) for g in groups]

    def write_out(ls):
        for g in groups:
            o = acc_ref[g] / ls[g]
            for r in range(rep):
                o_ref[:, gsl(g * rep + r)] = o[:, r * Q_BLOCK:(r + 1) * Q_BLOCK].T.astype(BF16)

    def qk_body(c, ms):
        b = bias_ref[c]
        bias = jnp.concatenate([b] * rep, axis=1)
        out = []
        for g in groups:
            s = _dot_nt(k_ref[chunk_rows(c), gsl(g)], qgs[g]) + bias
            s_ref[c, g] = s
            out.append(jnp.maximum(ms[g], jnp.max(s, axis=0, keepdims=True)))
        return tuple(out)

    ms = _fori_by_two(nck, qk_body, (row0(NEG_INF),) * len(groups))
    acc_ref[...] = jnp.zeros(acc_ref.shape, F32)

    def pv_body(c, ls):
        out = []
        for g in groups:
            p = jnp.exp2(s_ref[c, g] - ms[g])
            out.append(ls[g] + jnp.sum(p, axis=0, keepdims=True))
            acc_ref[g] += _dot(vt_ref[c, gsl(g), :], p.astype(BF16))
        return tuple(out)

    write_out(_fori_by_two(nck, pv_body, (row0(0.0),) * len(groups)))


def _dsa_prompt(qi, wit, q, ki_b, k_b, v_b, batch, seq, topk):
    nqb = seq // Q_BLOCK
    rep = A_HEADS // A_KV_HEADS
    kc = min(512, seq)
    nchunks = seq // kc
    pos_bits = max(1, int(np.ceil(np.log2(seq))))
    qrow = lambda w: pl.BlockSpec((Q_BLOCK, w), lambda b, i: (b * nqb + i, 0))
    seqblk = lambda w: pl.BlockSpec((seq, w), lambda b, i: (b, 0), pipeline_mode=pl.Buffered(1))
    return pl.pallas_call(
        functools.partial(_dsa_prompt_kernel, topk=topk, kc=kc, pos_bits=pos_bits),
        grid=(batch, nqb),
        in_specs=[qrow(IDX_HEADS * IDX_DIM), pl.BlockSpec((IDX_HEADS, Q_BLOCK), lambda b, i: (0, b * nqb + i)),
                  qrow(A_WIDTH), seqblk(IDX_DIM), seqblk(A_KV_WIDTH), seqblk(A_KV_WIDTH)],
        out_specs=qrow(A_WIDTH),
        out_shape=jax.ShapeDtypeStruct((batch * seq, A_WIDTH), BF16),
        scratch_shapes=[pltpu.VMEM((nchunks, kc, Q_BLOCK), F32), pltpu.VMEM((nchunks, kc, Q_BLOCK), F32),
                        pltpu.VMEM((nchunks, A_KV_WIDTH, kc), BF16), pltpu.VMEM((1, Q_BLOCK), I32),
                        pltpu.VMEM((nchunks, A_KV_HEADS, kc, rep * Q_BLOCK), F32),
                        pltpu.VMEM((A_KV_HEADS, A_HEAD_DIM, rep * Q_BLOCK), F32)],
        compiler_params=_cparams(("arbitrary", "arbitrary")),
        name="dsa_prompt",
    )(qi, wit, q, ki_b, k_b, v_b)


def _fold_heads(e, nq):
    acc = e[0:8]
    for t in range(1, e.shape[0] // 8):
        acc = acc + e[8 * t:8 * (t + 1)]
    return acc + pltpu.roll(acc, nq, 0)


def _sample_scores_kernel(pt_ref, qp_ref, wc_ref, kin_ref, *refs, pages, nq):
    page_refs, (sc_ref, scn_ref) = refs[:pages], refs[pages:]
    qp = qp_ref[...]
    wc = wc_ref[...]
    kpt = jnp.concatenate([r[...] for r in page_refs], axis=1).astype(BF16)
    sc_ref[...] = _fold_heads(jnp.maximum(_dot(qp, kpt), 0.0) * wc, nq)

    @pl.when(pl.program_id(1) == 0)
    def _():
        kin = jnp.concatenate([kin_ref[...], jnp.zeros((PAGE - S_PAD, IDX_DIM), BF16)], axis=0)
        scn_ref[...] = _fold_heads(jnp.maximum(_dot_nt(qp, kin), 0.0) * wc, nq)


def _sample_scores(page_flat, qp, wcol, ki_new_b, cache_kidx, nseq, npages, pages, nq):
    steps = npages // pages
    rows = IDX_HEADS * nq
    page_specs = [pl.BlockSpec((None, IDX_DIM, PAGE),
                               lambda b, s, pt, p=p: (pt[b * npages + s * pages + p], 0, 0))
                  for p in range(pages)]
    grid_spec = pltpu.PrefetchScalarGridSpec(
        num_scalar_prefetch=1,
        grid=(nseq, steps),
        in_specs=[pl.BlockSpec((None, rows, IDX_DIM), lambda b, s, pt: (b, 0, 0)),
                  pl.BlockSpec((None, rows, 1), lambda b, s, pt: (b, 0, 0)),
                  pl.BlockSpec((S_PAD, IDX_DIM), lambda b, s, pt: (b, 0))] + page_specs,
        out_specs=[pl.BlockSpec((None, 8, pages * PAGE), lambda b, s, pt: (b, 0, s)),
                   pl.BlockSpec((None, 8, PAGE), lambda b, s, pt: (b, 0, 0))],
    )
    return pl.pallas_call(
        functools.partial(_sample_scores_kernel, pages=pages, nq=nq),
        grid_spec=grid_spec,
        out_shape=[jax.ShapeDtypeStruct((nseq, 8, npages * PAGE), F32),
                   jax.ShapeDtypeStruct((nseq, 8, PAGE), F32)],
        compiler_params=_cparams(("arbitrary", "arbitrary")),
        name="dsa_sample_scores",
    )(page_flat, qp, wcol, ki_new_b, *([cache_kidx] * pages))


def _sample_attn_kernel(pt_ref, sc_ref, scn_ref, qs_ref, kn_ref, vn_ref, *refs,
                        pages, nq, n_past, topk, pos_bits):
    k_pages, v_pages = refs[:pages], refs[pages:2 * pages]
    o_ref = refs[2 * pages]
    scm_ref, bias_ref, biasn_ref, j_ref, m_ref, l_ref, acc_ref, kbuf_ref, vbuf_ref = refs[2 * pages + 1:]
    s_idx = pl.program_id(1)
    kw = pages * PAGE
    rows_q = qs_ref.shape[1]

    @pl.when(s_idx == 0)
    def _():
        qrow = lax.broadcasted_iota(I32, (8, 1), 0) % nq
        lane_p = lax.broadcasted_iota(I32, (1, n_past), 1)
        lane_n = lax.broadcasted_iota(I32, (1, PAGE), 1)
        scm_ref[...] = jnp.where(lane_n <= qrow, scn_ref[...], NEG_INF)

        def count(pred):
            ones = jnp.where(pred(sc_ref[...], lane_p), 1.0, 0.0)
            parts = [ones[:, t * kw:(t + 1) * kw] for t in range(n_past // kw)]
            while len(parts) > 1:
                parts = [a + b for a, b in zip(parts[::2], parts[1::2])] + parts[len(parts) & ~1:]
            cp = jnp.sum(parts[0], axis=-1, keepdims=True)
            cn = jnp.sum(jnp.where(pred(scm_ref[...], n_past + lane_n), 1.0, 0.0), axis=-1, keepdims=True)
            return cp + cn

        thr, jsel = _select_params(count, topk, pos_bits, j_ref)
        kp = sc_ref[...]
        bias_p = jnp.where((kp > thr) | ((kp == thr) & (lane_p <= jsel)), 0.0, NEG_INF)
        for st in range(n_past // kw):
            bias_ref[st] = bias_p[:, st * kw:(st + 1) * kw]
        kn = scm_ref[...]
        seln = ((kn > thr) | ((kn == thr) & (n_past + lane_n <= jsel))) & (lane_n <= qrow)
        biasn_ref[...] = jnp.where(seln, 0.0, NEG_INF)
        m_ref[...] = jnp.full(m_ref.shape, NEG_INF, F32)
        l_ref[...] = jnp.zeros(l_ref.shape, F32)
        acc_ref[...] = jnp.zeros(acc_ref.shape, F32)

    for p in range(pages):
        for g in range(A_KV_HEADS):
            gsl = slice(g * A_HEAD_DIM, (g + 1) * A_HEAD_DIM)
            rows = pl.ds(g, PAGE, stride=A_KV_HEADS)
            kbuf_ref[p * PAGE:(p + 1) * PAGE, gsl] = k_pages[p][rows, :].astype(BF16)
            vbuf_ref[p * PAGE:(p + 1) * PAGE, gsl] = v_pages[p][rows, :].astype(BF16)

    def update(g, s, vv):
        m = m_ref[g]
        m_new = jnp.maximum(m, jnp.max(s, axis=-1, keepdims=True))
        m_safe = jnp.where(m_new == NEG_INF, 0.0, m_new)
        alpha = jnp.exp2(m - m_safe)
        p = jnp.exp2(s - m_safe)
        l_ref[g] = alpha * l_ref[g] + jnp.sum(p, axis=-1, keepdims=True)
        acc_ref[g] = alpha * acc_ref[g] + _dot(p.astype(BF16), vv)
        m_ref[g] = m_new

    bias = jnp.concatenate([bias_ref[s_idx]] * (rows_q // 8), axis=0)
    for g in range(A_KV_HEADS):
        gsl = slice(g * A_HEAD_DIM, (g + 1) * A_HEAD_DIM)
        update(g, _dot_nt(qs_ref[g], kbuf_ref[:, gsl]) + bias, vbuf_ref[:, gsl])

    @pl.when(s_idx == pl.num_programs(1) - 1)
    def _():
        bn = jnp.concatenate([biasn_ref[:, :S_PAD]] * (rows_q // 8), axis=0)
        for g in range(A_KV_HEADS):
            gsl = slice(g * A_HEAD_DIM, (g + 1) * A_HEAD_DIM)
            update(g, _dot_nt(qs_ref[g], kn_ref[:, gsl]) + bn, vn_ref[:, gsl])
            o_ref[g] = acc_ref[g] / l_ref[g]


def _sample_attn(page_flat, sc, scn, qs, k_new_b, v_new_b, cache_k, cache_v, nseq, npages, pages, nq, topk):
    steps = npages // pages
    n_past = npages * PAGE
    kw = pages * PAGE
    rows_q = qs.shape[2]
    pos_bits = int(np.ceil(np.log2(n_past + PAGE)))
    page_spec = lambda p: pl.BlockSpec((None, PAGE * A_KV_HEADS, A_HEAD_DIM),
                                       lambda b, s, pt, p=p: (pt[b * npages + s * pages + p], 0, 0))
    grid_spec = pltpu.PrefetchScalarGridSpec(
        num_scalar_prefetch=1,
        grid=(nseq, steps),
        in_specs=[pl.BlockSpec((None, 8, n_past), lambda b, s, pt: (b, 0, 0)),
                  pl.BlockSpec((None, 8, PAGE), lambda b, s, pt: (b, 0, 0)),
                  pl.BlockSpec((None, A_KV_HEADS, rows_q, A_HEAD_DIM), lambda b, s, pt: (b, 0, 0, 0)),
                  pl.BlockSpec((S_PAD, A_KV_WIDTH), lambda b, s, pt: (b, 0)),
                  pl.BlockSpec((S_PAD, A_KV_WIDTH), lambda b, s, pt: (b, 0))]
                 + [page_spec(p) for p in range(pages)] + [page_spec(p) for p in range(pages)],
        out_specs=pl.BlockSpec((None, A_KV_HEADS, rows_q, A_HEAD_DIM), lambda b, s, pt: (b, 0, 0, 0)),
        scratch_shapes=[pltpu.VMEM((8, PAGE), F32),
                        pltpu.VMEM((steps, 8, kw), F32), pltpu.VMEM((8, PAGE), F32),
                        pltpu.VMEM((8, 1), I32),
                        pltpu.VMEM((A_KV_HEADS, rows_q, 1), F32), pltpu.VMEM((A_KV_HEADS, rows_q, 1), F32),
                        pltpu.VMEM((A_KV_HEADS, rows_q, A_HEAD_DIM), F32),
                        pltpu.VMEM((kw, A_KV_WIDTH), BF16), pltpu.VMEM((kw, A_KV_WIDTH), BF16)],
    )
    return pl.pallas_call(
        functools.partial(_sample_attn_kernel, pages=pages, nq=nq, n_past=n_past, topk=topk, pos_bits=pos_bits),
        grid_spec=grid_spec,
        out_shape=jax.ShapeDtypeStruct((nseq, A_KV_HEADS, rows_q, A_HEAD_DIM), F32),
        compiler_params=_cparams(("arbitrary", "arbitrary")),
        name="dsa_sample_attn",
    )(page_flat, sc, scn, qs, k_new_b, v_new_b, *([cache_k] * pages), *([cache_v] * pages))


def _tree(op, parts):
    parts = list(parts)
    while len(parts) > 1:
        parts = [op(a, b) for a, b in zip(parts[::2], parts[1::2])] + parts[len(parts) & ~1:]
    return parts[0]


def _lane_fold(op, x):
    return _tree(op, [x[:, t * 128:(t + 1) * 128] for t in range(x.shape[1] // 128)])


def _dsa_sample_kernel(pt_ref, qp_ref, wc_ref, kin_ref, qs_ref, kn_ref, vn_ref, kidx_hbm, k_hbm, v_hbm, o_ref,
                       sc_ref, scn_ref, bias_ref, biasn_ref, j_ref, s_ref, sn_ref, m_ref, mfin_ref, l_ref, acc_ref,
                       buf_ref, idx_raw, kv_raw, idx_sem, kv_sem, *, pages, npages, nq, topk, pos_bits):
    b_idx, s_idx = pl.program_id(0), pl.program_id(1)
    kw = pages * PAGE
    steps = npages // pages
    n_past = npages * PAGE
    rows_q = qs_ref.shape[1]
    dup = lambda x8: jnp.concatenate([x8] * (rows_q // 8), axis=0)
    gsl = lambda g: slice(g * A_HEAD_DIM, (g + 1) * A_HEAD_DIM)

    n = b_idx * (3 * steps) + s_idx
    slot = n % 2

    def block_copies(seq, step, sl, start):
        for phase, (src, dst, sem) in enumerate(((kidx_hbm, idx_raw, idx_sem), (k_hbm, kv_raw, kv_sem),
                                                 (v_hbm, kv_raw, kv_sem))):
            @pl.when((step >= phase * steps) & (step < (phase + 1) * steps))
            def _(phase=phase, src=src, dst=dst, sem=sem):
                for p in range(pages):
                    page = pt_ref[seq * npages + (step - phase * steps) * pages + p] if start else 0
                    copy = pltpu.make_async_copy(src.at[page], dst.at[sl, p], sem.at[sl])
                    copy.start() if start else copy.wait()

    @pl.when(n == 0)
    def _():
        block_copies(b_idx, s_idx, slot, True)

    @pl.when(n + 1 < pl.num_programs(0) * 3 * steps)
    def _():
        wrap = s_idx + 1 == 3 * steps
        block_copies(jnp.where(wrap, b_idx + 1, b_idx), jnp.where(wrap, 0, s_idx + 1), 1 - slot, True)

    block_copies(b_idx, s_idx, slot, False)

    def load_pages():
        for p in range(pages):
            for g in range(A_KV_HEADS):
                rows = pl.ds(g, PAGE, stride=A_KV_HEADS)
                buf_ref[p * PAGE:(p + 1) * PAGE, gsl(g)] = kv_raw[slot, p, rows, :].astype(BF16)

    @pl.when(s_idx < steps)
    def _():
        qp = qp_ref[...]
        wc = wc_ref[...]
        kpt = jnp.concatenate([idx_raw[slot, p] for p in range(pages)], axis=1).astype(BF16)
        sc_ref[s_idx] = _fold_heads(jnp.maximum(_dot(qp, kpt), 0.0) * wc, nq)

        @pl.when(s_idx == steps - 1)
        def _():
            qrow = lax.broadcasted_iota(I32, (8, 1), 0) % nq
            lane = lax.broadcasted_iota(I32, (1, kw), 1)
            lane_n = lax.broadcasted_iota(I32, (1, PAGE), 1)
            kin = jnp.concatenate([kin_ref[...], jnp.zeros((PAGE - S_PAD, IDX_DIM), BF16)], axis=0)
            scn = _fold_heads(jnp.maximum(_dot_nt(qp, kin), 0.0) * wc, nq)
            scn_ref[...] = jnp.where(lane_n <= qrow, scn, NEG_INF)

            def count(pred):
                ones = _tree(jnp.add, [jnp.where(pred(sc_ref[t], t * kw + lane), 1.0, 0.0) for t in range(steps)])
                cn = jnp.where(pred(scn_ref[...], n_past + lane_n), 1.0, 0.0)
                return (jnp.sum(_lane_fold(jnp.add, ones), axis=-1, keepdims=True)
                        + jnp.sum(cn, axis=-1, keepdims=True))

            thr, jsel = _select_params(count, topk, pos_bits, j_ref)
            for t in range(steps):
                sp = sc_ref[t]
                bias_ref[t] = jnp.where((sp > thr) | ((sp == thr) & (t * kw + lane <= jsel)), 0.0, NEG_INF)
            sn = scn_ref[...]
            seln = ((sn > thr) | ((sn == thr) & (n_past + lane_n <= jsel))) & (lane_n <= qrow)
            biasn_ref[...] = jnp.where(seln, 0.0, NEG_INF)

    @pl.when((s_idx >= steps) & (s_idx < 2 * steps))
    def _():
        t = s_idx - steps
        load_pages()
        bias = dup(bias_ref[t])

        @pl.when(t == 0)
        def _():
            m_ref[...] = jnp.full(m_ref.shape, NEG_INF, F32)

        for g in range(A_KV_HEADS):
            s = _dot_nt(qs_ref[g], buf_ref[:, gsl(g)]) + bias
            s_ref[g, t] = s
            m_ref[g] = jnp.maximum(m_ref[g], _lane_fold(jnp.maximum, s))

        @pl.when(t == steps - 1)
        def _():
            bn = dup(biasn_ref[:, :S_PAD])
            for g in range(A_KV_HEADS):
                sn = _dot_nt(qs_ref[g], kn_ref[:, gsl(g)]) + bn
                sn_ref[g] = sn
                mfin_ref[g] = jnp.maximum(jnp.max(m_ref[g], axis=-1, keepdims=True),
                                          jnp.max(sn, axis=-1, keepdims=True))

    @pl.when(s_idx >= 2 * steps)
    def _():
        t = s_idx - 2 * steps
        load_pages()

        @pl.when(t == 0)
        def _():
            l_ref[...] = jnp.zeros(l_ref.shape, F32)
            acc_ref[...] = jnp.zeros(acc_ref.shape, F32)

        for g in range(A_KV_HEADS):
            p = jnp.exp2(s_ref[g, t] - mfin_ref[g])
            l_ref[g] += _lane_fold(jnp.add, p)
            acc_ref[g] += _dot(p.astype(BF16), buf_ref[:, gsl(g)])

        @pl.when(t == steps - 1)
        def _():
            for g in range(A_KV_HEADS):
                pn = jnp.exp2(sn_ref[g] - mfin_ref[g])
                l = jnp.sum(l_ref[g], axis=-1, keepdims=True) + jnp.sum(pn, axis=-1, keepdims=True)
                o_ref[g] = (acc_ref[g] + _dot(pn.astype(BF16), vn_ref[:, gsl(g)])) / l


def _dsa_sample(page_flat, qp, wcol, ki_new_b, qs, k_new_b, v_new_b, cache_kidx_t, cache_k, cache_v,
                nseq, npages, pages, nq, topk):
    steps = npages // pages
    n_past = npages * PAGE
    kw = pages * PAGE
    rows = IDX_HEADS * nq
    rows_q = qs.shape[2]
    pos_bits = int(np.ceil(np.log2(n_past + PAGE)))

    seq3 = lambda shape: pl.BlockSpec((None,) + shape, lambda b, s, pt: (b,) + (0,) * len(shape))
    new_rows = lambda w: pl.BlockSpec((S_PAD, w), lambda b, s, pt: (b, 0))
    in_hbm = pl.BlockSpec(memory_space=pl.ANY)
    grid_spec = pltpu.PrefetchScalarGridSpec(
        num_scalar_prefetch=1,
        grid=(nseq, 3 * steps),
        in_specs=[seq3((rows, IDX_DIM)), seq3((rows, 1)), new_rows(IDX_DIM),
                  seq3((A_KV_HEADS, rows_q, A_HEAD_DIM)), new_rows(A_KV_WIDTH), new_rows(A_KV_WIDTH),
                  in_hbm, in_hbm, in_hbm],
        out_specs=seq3((A_KV_HEADS, rows_q, A_HEAD_DIM)),
        scratch_shapes=[pltpu.VMEM((steps, 8, kw), F32), pltpu.VMEM((8, PAGE), F32),
                        pltpu.VMEM((steps, 8, kw), F32), pltpu.VMEM((8, PAGE), F32),
                        pltpu.VMEM((8, 1), I32),
                        pltpu.VMEM((A_KV_HEADS, steps, rows_q, kw), F32),
                        pltpu.VMEM((A_KV_HEADS, rows_q, S_PAD), F32),
                        pltpu.VMEM((A_KV_HEADS, rows_q, 128), F32),
                        pltpu.VMEM((A_KV_HEADS, rows_q, 1), F32),
                        pltpu.VMEM((A_KV_HEADS, rows_q, 128), F32),
                        pltpu.VMEM((A_KV_HEADS, rows_q, A_HEAD_DIM), F32),
                        pltpu.VMEM((kw, A_KV_WIDTH), BF16),
                        pltpu.VMEM((2, pages, IDX_DIM, PAGE), F32),
                        pltpu.VMEM((2, pages, PAGE * A_KV_HEADS, A_HEAD_DIM), F32),
                        pltpu.SemaphoreType.DMA((2,)), pltpu.SemaphoreType.DMA((2,))],
    )
    return pl.pallas_call(
        functools.partial(_dsa_sample_kernel, pages=pages, npages=npages, nq=nq, topk=topk, pos_bits=pos_bits),
        grid_spec=grid_spec,
        out_shape=jax.ShapeDtypeStruct((nseq, A_KV_HEADS, rows_q, A_HEAD_DIM), F32),
        compiler_params=_cparams(("arbitrary", "arbitrary")),
        name="dsa_sample",
    )(page_flat, qp, wcol, ki_new_b, qs, k_new_b, v_new_b, cache_kidx_t, cache_k, cache_v)


def _merge_kernel(x_ref, oa_ref, ag_ref, bu_ref, bv_ref, bg_ref, cq_ref, cg_ref, ra_ref, rb_ref, rc_ref,
                  mk_ref, mv_ref, ws_ref, bs_ref, gsgu_ref, gmq_ref, wpa_ref, wpb_ref, wpc_ref, wout_ref,
                  y_ref, *maybe_vn_ref, tm, chunk):
    f32 = lambda r: r[...].astype(F32)
    silu = lambda t: t * jax.nn.sigmoid(t)

    vn = _rms(f32(bv_ref), gsgu_ref[...])
    if maybe_vn_ref:
        maybe_vn_ref[0][...] = vn
    vnb = vn.astype(BF16)
    bu = f32(bu_ref)
    tril = (lax.broadcasted_iota(I32, (chunk, chunk), 1) <= lax.broadcasted_iota(I32, (chunk, chunk), 0))
    ob_cols = []
    for g in range(B_GROUPS):
        wg = jnp.where(tril, ws_ref[g], 0.0).astype(BF16)
        gsl = slice(g * B_GROUP_DIM, (g + 1) * B_GROUP_DIM)
        parts = [_dot(wg, vnb[c * chunk:(c + 1) * chunk, gsl]) + bs_ref[:, g:g + 1] for c in range(tm // chunk)]
        ob_cols.append(parts[0] if len(parts) == 1 else jnp.concatenate(parts, axis=0))
    ob = bu * jnp.concatenate(ob_cols, axis=1)
    pb = _dot((ob * silu(f32(bg_ref))).astype(BF16), wpb_ref[...])

    cq = f32(cq_ref)
    oc_cols = []
    for hh in range(M_HEADS):
        hsl = slice(hh * M_HEAD_DIM, (hh + 1) * M_HEAD_DIM)
        qn = (_rms(cq[:, hsl], gmq_ref[...]) * (M_HEAD_DIM ** -0.5)).astype(BF16)
        s = _dot_nt(qn, mk_ref[:, hsl])
        p = jnp.exp(s - jnp.max(s, axis=-1, keepdims=True))
        oc_cols.append(_dot(p.astype(BF16), mv_ref[:, hsl]) / jnp.sum(p, axis=-1, keepdims=True))
    oc = jnp.concatenate(oc_cols, axis=1)
    pc = _dot((oc * silu(f32(cg_ref))).astype(BF16), wpc_ref[...])

    pa = _dot((f32(oa_ref) * silu(f32(ag_ref))).astype(BF16), wpa_ref[...])
    sig = jax.nn.sigmoid
    m = sig(f32(ra_ref)) * pa + sig(f32(rb_ref)) * pb + sig(f32(rc_ref)) * pc
    y_ref[...] = x_ref[...] + _dot(m.astype(BF16), wout_ref[...])


def _merge(x, oa, zr, mk_b, mv_b, ws, bs_t, g_sgu, g_mq, w_pa, w_pb, w_pc, w_out, tm, chunk, mem_map, emit_vn):
    n = x.shape[0]
    col = lambda w, j: pl.BlockSpec((tm, w), lambda i, j=j: (i, j))
    mem = pl.BlockSpec((N_MEM, M_WIDTH), mem_map)
    in_specs = [col(D_MODEL, 0), col(A_WIDTH, 0),
                col(1024, 0), col(1024, 1), col(1024, 2), col(1024, 3), col(1024, 4), col(1024, 5),
                col(2048, 3), col(2048, 4), col(2048, 5),
                mem, mem,
                _const_spec((B_GROUPS, chunk, chunk)), _const_spec((chunk, B_GROUPS)),
                _const_spec((1, B_WIDTH)), _const_spec((1, M_HEAD_DIM)),
                _const_spec((A_WIDTH, D_MODEL)), _const_spec((B_WIDTH, D_MODEL)),
                _const_spec((M_WIDTH, D_MODEL)), _const_spec((D_MODEL, D_MODEL))]
    out_specs = [col(D_MODEL, 0)]
    out_shape = [jax.ShapeDtypeStruct((n, D_MODEL), F32)]
    if emit_vn:
        out_specs.append(col(B_WIDTH, 0))
        out_shape.append(jax.ShapeDtypeStruct((n, B_WIDTH), F32))
    return pl.pallas_call(
        functools.partial(_merge_kernel, tm=tm, chunk=chunk),
        grid=(n // tm,),
        in_specs=in_specs,
        out_specs=out_specs,
        out_shape=out_shape,
        compiler_params=_cparams(("arbitrary",)),
        name="merge",
    )(x, oa, *([zr] * 9), mk_b, mv_b, ws, bs_t, g_sgu, g_mq, w_pa, w_pb, w_pc, w_out)


def _rope_tables(pos):
    pos = pos.astype(F32)[:, None]

    def cs(half):
        freq = ROPE_THETA ** (-jnp.arange(half, dtype=F32) / half)
        ang = pos * freq[None, :]
        return jnp.cos(ang), jnp.sin(ang)

    c, s = cs(A_HEAD_DIM // 2)
    ci, si = cs(IDX_DIM // 2)
    z = jnp.zeros_like(si)
    return (jnp.concatenate([c, c], axis=1), jnp.concatenate([-s, s], axis=1),
            jnp.concatenate([ci] * 4, axis=1), jnp.concatenate([-si, z, -si, z], axis=1),
            jnp.concatenate([z, si, z, si], axis=1))


def kernel(x_prompt, x_sample, cache_k, cache_v, cache_kidx, cache_mem_k, cache_mem_v, page_table,
           mem_prompt, g_pre, w_in, g_q, g_k, g_mq, g_mk, g_mem, w_mem_kv, g_sgu, w_s, b_s,
           w_pa, w_pb, w_pc, w_out):
    batch, seq, _ = x_prompt.shape
    nseq, nq, _ = x_sample.shape
    assert nq == 4 and nq <= S_PAD
    npages = page_table.shape[1]
    n_past = npages * PAGE
    n_pool = cache_k.shape[0]
    row2 = lambda a: a.reshape(1, -1)

    w_in_t = w_in.T
    w_a = w_in_t[:A_COLS].astype(BF16)
    w_pa_b, w_pb_b, w_pc_b, w_out_b = (w.astype(BF16) for w in (w_pa, w_pb, w_pc, w_out))
    w_mem_b = w_mem_kv.astype(BF16)
    g_pre2, g_q2, g_k2, g_mq2, g_mk2, g_mem2, g_sgu2 = map(row2, (g_pre, g_q, g_k, g_mq, g_mk, g_mem, g_sgu))

    xp = x_prompt.reshape(batch * seq, D_MODEL)
    tm_a = min(256, seq)
    nblk = seq // tm_a
    tabs_p = _rope_tables(jnp.arange(seq))
    q, k_p, v_p, k_b, v_b, qi, ki_p, ki_b, wi = _proj_a(
        xp, g_pre2, w_a, g_q2, g_k2, tabs_p, tm_a, lambda i: (i % nblk, 0))
    zr = _proj_rest(xp, g_pre2, w_in_t, min(1024, seq), PROJ_TN)
    mk_p, mv_p, mk_b, mv_b = _mem_kv(mem_prompt.reshape(batch * N_MEM, D_MODEL), g_mem2, w_mem_b, g_mk2)
    oa = _dsa_prompt(qi, wi.T, q, ki_b, k_b, v_b, batch, seq, min(TOPK_MAX, seq // 4))
    tm_m = min(256, seq)
    nblk_m = seq // tm_m
    (y_p,) = _merge(xp, oa, zr, mk_b, mv_b, w_s, b_s.T, g_sgu2, g_mq2, w_pa_b, w_pb_b, w_pc_b, w_out_b,
                    tm_m, CHUNK, lambda i: (i // nblk_m, 0), False)

    xs = jnp.pad(x_sample, ((0, 0), (0, S_PAD - nq), (0, 0))).reshape(nseq * S_PAD, D_MODEL)
    tabs_s = tuple(jnp.tile(t, (nseq, 1)) for t in _rope_tables(n_past + jnp.arange(S_PAD)))
    rows_s = nseq * S_PAD
    q_s, k_s, v_s, k_sb, v_sb, qi_s, ki_s, ki_sb, wi_s = _proj_a(
        xs, g_pre2, w_a, g_q2, g_k2, tabs_s, rows_s, lambda i: (0, 0))
    zr_s = _proj_rest(xs, g_pre2, w_in_t, rows_s, PROJ_TN)

    qp = (qi_s.reshape(nseq, S_PAD, IDX_HEADS, IDX_DIM)[:, :nq]
          .transpose(0, 2, 1, 3).reshape(nseq, IDX_HEADS * nq, IDX_DIM))
    wcol = wi_s.reshape(nseq, S_PAD, IDX_HEADS)[:, :nq].transpose(0, 2, 1).reshape(nseq, IDX_HEADS * nq, 1)
    rep = A_HEADS // A_KV_HEADS
    qs = (q_s.reshape(nseq, S_PAD, A_KV_HEADS, rep, A_HEAD_DIM)[:, :nq]
          .transpose(0, 2, 3, 1, 4).reshape(nseq, A_KV_HEADS, rep * nq, A_HEAD_DIM))
    qs = jnp.concatenate([qs, qs], axis=2)
    page_flat = page_table.reshape(-1)
    pages = min(SAMPLE_PAGES, npages)
    o_s = _dsa_sample(page_flat, qp, wcol, ki_sb, qs, k_sb, v_sb, jnp.swapaxes(cache_kidx, 1, 2),
                      cache_k.reshape(n_pool, PAGE * A_KV_HEADS, A_HEAD_DIM),
                      cache_v.reshape(n_pool, PAGE * A_KV_HEADS, A_HEAD_DIM),
                      nseq, npages, pages, nq, min(TOPK_MAX, (n_past + nq) // 4))
    oa_s = (o_s[:, :, :rep * nq].reshape(nseq, A_KV_HEADS, rep, nq, A_HEAD_DIM)
            .transpose(0, 3, 1, 2, 4).reshape(nseq, nq, A_WIDTH))
    oa_s = jnp.pad(oa_s, ((0, 0), (0, S_PAD - nq), (0, 0))).reshape(rows_s, A_WIDTH).astype(BF16)
    mk_s = cache_mem_k.reshape(nseq * N_MEM, M_WIDTH).astype(BF16)
    mv_s = cache_mem_v.reshape(nseq * N_MEM, M_WIDTH).astype(BF16)
    y_s, vn_s = _merge(xs, oa_s, zr_s, mk_s, mv_s, w_s[:, :S_PAD, :S_PAD], b_s[:, :S_PAD].T, g_sgu2, g_mq2,
                       w_pa_b, w_pb_b, w_pc_b, w_out_b, S_PAD, S_PAD, lambda i: (i, 0), True)

    take = lambda a, shape: a.reshape(nseq, S_PAD, -1)[:, :nq].reshape(shape)
    return (y_p.reshape(batch, seq, D_MODEL),
            take(y_s, (nseq, nq, D_MODEL)),
            k_p.reshape(batch, seq, A_KV_HEADS, A_HEAD_DIM),
            v_p.reshape(batch, seq, A_KV_HEADS, A_HEAD_DIM),
            ki_p.reshape(batch, seq, IDX_DIM),
            mk_p.reshape(batch, N_MEM, M_HEADS, M_HEAD_DIM),
            mv_p.reshape(batch, N_MEM, M_HEADS, M_HEAD_DIM),
            take(k_s, (nseq, nq, A_KV_HEADS, A_HEAD_DIM)),
            take(v_s, (nseq, nq, A_KV_HEADS, A_HEAD_DIM)),
            take(ki_s, (nseq, nq, IDX_DIM)),
            take(vn_s, (nseq, nq, B_GROUPS, B_GROUP_DIM)))
```

```python
import functools

import numpy as np
import jax
import jax.numpy as jnp
from jax import lax
from jax.experimental import pallas as pl
from jax.experimental.pallas import tpu as pltpu

F32 = jnp.float32
BF16 = jnp.bfloat16
I32 = jnp.int32

D_MODEL = 2048
PAGE = 128
A_HEADS = 8
A_KV_HEADS = 4
A_HEAD_DIM = 128
A_WIDTH = A_HEADS * A_HEAD_DIM
A_KV_WIDTH = A_KV_HEADS * A_HEAD_DIM
IDX_HEADS = 16
IDX_DIM = 64
TOPK_MAX = 256
Q_BLOCK = 128
ROPE_THETA = 10000.0
CHUNK = 128
B_GROUPS = 8
B_GROUP_DIM = 128
B_WIDTH = B_GROUPS * B_GROUP_DIM
N_MEM = 256
M_HEADS = 4
M_HEAD_DIM = 256
M_WIDTH = M_HEADS * M_HEAD_DIM
EPS = 1e-6

OFF_K = A_WIDTH
OFF_V = OFF_K + A_KV_WIDTH
OFF_QI = OFF_V + A_KV_WIDTH
OFF_KI = OFF_QI + IDX_HEADS * IDX_DIM
OFF_WI = OFF_KI + IDX_DIM
OFF_REST = OFF_WI + IDX_HEADS
A_COLS = 3200
REST_COLS = A_WIDTH + 3 * B_WIDTH + 2 * M_WIDTH + 3 * D_MODEL

Q_SCALE = float(np.log2(np.e)) * A_HEAD_DIM ** -0.5
S_PAD = 16
PROJ_TN = 1024
SAMPLE_PAGES = 16
INT_MIN = np.int32(-2 ** 31)
INT_MAX = np.int32(2 ** 31 - 1)
NEG_INF = float("-inf")

V7X_VMEM_LIMIT = 56 * 1024 * 1024


def _cparams(sem):
    return pltpu.CompilerParams(dimension_semantics=sem, vmem_limit_bytes=V7X_VMEM_LIMIT)


def _dot(a, b):
    return jnp.dot(a, b, preferred_element_type=F32)


def _dot_nt(a, b):
    return lax.dot_general(a, b, (((1,), (1,)), ((), ())), preferred_element_type=F32)


def _rms(x, g):
    return x * lax.rsqrt(jnp.mean(x * x, axis=-1, keepdims=True) + EPS) * g


def _const_spec(shape):
    nd = len(shape)
    return pl.BlockSpec(shape, lambda *_: (0,) * nd, pipeline_mode=pl.Buffered(1))


def _proj_a_kernel(x_ref, g_ref, w_ref, gq_ref, gk_ref, cq_ref, sq_ref, ci_ref, sia_ref, sib_ref,
                   q_ref, k_ref, v_ref, kb_ref, vb_ref, qi_ref, ki_ref, kib_ref, wi_ref):
    h = _rms(x_ref[...], g_ref[...]).astype(BF16)
    z = _dot_nt(h, w_ref[...])
    tm = z.shape[0]
    cq, sq = cq_ref[...], sq_ref[...]
    ci, sia, sib = ci_ref[...], sia_ref[...], sib_ref[...]

    def norm_rope(zz, g):
        n = _rms(zz, g)
        return n * cq + pltpu.roll(n, A_HEAD_DIM // 2, 1) * sq

    def rope_idx(zz):
        return zz * ci + pltpu.roll(zz, 96, 1) * sia + pltpu.roll(zz, 32, 1) * sib

    for hh in range(A_HEADS):
        sl = slice(hh * A_HEAD_DIM, (hh + 1) * A_HEAD_DIM)
        q_ref[:, sl] = (norm_rope(z[:, sl], gq_ref[...]) * Q_SCALE).astype(BF16)
    for hh in range(A_KV_HEADS):
        sl = slice(hh * A_HEAD_DIM, (hh + 1) * A_HEAD_DIM)
        kh = norm_rope(z[:, OFF_K + hh * A_HEAD_DIM:OFF_K + (hh + 1) * A_HEAD_DIM], gk_ref[...])
        vh = z[:, OFF_V + hh * A_HEAD_DIM:OFF_V + (hh + 1) * A_HEAD_DIM]
        head_rows = pl.ds(hh, tm, stride=A_KV_HEADS)
        k_ref[head_rows, :] = kh
        v_ref[head_rows, :] = vh
        kb_ref[:, sl] = kh.astype(BF16)
        vb_ref[:, sl] = vh.astype(BF16)
    for t in range(IDX_HEADS * IDX_DIM // 128):
        sl = slice(t * 128, (t + 1) * 128)
        qi_ref[:, sl] = rope_idx(z[:, OFF_QI + t * 128:OFF_QI + (t + 1) * 128]).astype(BF16)
    last = z[:, OFF_KI:OFF_KI + 128]
    ki = rope_idx(last)[:, :IDX_DIM]
    ki_ref[...] = ki
    kib_ref[...] = ki.astype(BF16)
    wi_ref[...] = last[:, IDX_DIM:IDX_DIM + IDX_HEADS] * ((IDX_HEADS ** -0.5) * (IDX_DIM ** -0.5))


def _proj_a(x, g_pre, w_a, g_q, g_k, tabs, tm, tab_map):
    n = x.shape[0]
    row = lambda w: pl.BlockSpec((tm, w), lambda i: (i, 0))
    tab = pl.BlockSpec((tm, 128), tab_map)
    outs = [(1, A_WIDTH, BF16), (A_KV_HEADS, A_HEAD_DIM, F32), (A_KV_HEADS, A_HEAD_DIM, F32),
            (1, A_KV_WIDTH, BF16), (1, A_KV_WIDTH, BF16),
            (1, IDX_HEADS * IDX_DIM, BF16), (1, IDX_DIM, F32), (1, IDX_DIM, BF16), (1, IDX_HEADS, F32)]
    return pl.pallas_call(
        _proj_a_kernel,
        grid=(n // tm,),
        in_specs=[row(D_MODEL), _const_spec((1, D_MODEL)), _const_spec((A_COLS, D_MODEL)),
                  _const_spec((1, A_HEAD_DIM)), _const_spec((1, A_HEAD_DIM)), tab, tab, tab, tab, tab],
        out_specs=[pl.BlockSpec((tm * r, w), lambda i: (i, 0)) for r, w, _ in outs],
        out_shape=[jax.ShapeDtypeStruct((n * r, w), dt) for r, w, dt in outs],
        compiler_params=_cparams(("arbitrary",)),
        name="proj_a",
    )(x, g_pre, w_a, g_q, g_k, *tabs)


def _proj_rest_kernel(x_ref, xs_ref, g_ref, w_ref, o_ref, os_ref, h_ref, hs_ref):
    i, j = pl.program_id(0), pl.program_id(1)

    @pl.when(j == 0)
    def _():
        h_ref[...] = _rms(x_ref[...], g_ref[...]).astype(BF16)

    @pl.when((i == 0) & (j == 0))
    def _():
        hs_ref[...] = _rms(xs_ref[...], g_ref[...]).astype(BF16)

    w = w_ref[...].astype(BF16)
    o_ref[...] = _dot_nt(h_ref[...], w).astype(BF16)

    @pl.when(i == 0)
    def _():
        os_ref[...] = _dot_nt(hs_ref[...], w).astype(BF16)


def _proj_rest(x, xs, g_pre, w_in_t, tm, tn):
    n, ns = x.shape[0], xs.shape[0]
    ncols = REST_COLS // tn
    xs_cols = lambda i, j: (0, jnp.where(i == 0, j, ncols - 1))
    return pl.pallas_call(
        _proj_rest_kernel,
        grid=(n // tm, ncols),
        in_specs=[pl.BlockSpec((tm, D_MODEL), lambda i, j: (i, 0), pipeline_mode=pl.Buffered(1)),
                  pl.BlockSpec((ns, D_MODEL), lambda i, j: (0, 0), pipeline_mode=pl.Buffered(1)),
                  pl.BlockSpec((1, D_MODEL), lambda i, j: (0, 0)),
                  pl.BlockSpec((pl.Element(tn), pl.Element(D_MODEL)),
                               lambda i, j: (pl.multiple_of(OFF_REST + j * tn, 16), 0))],
        out_specs=[pl.BlockSpec((tm, tn), lambda i, j: (i, j)), pl.BlockSpec((ns, tn), xs_cols)],
        out_shape=[jax.ShapeDtypeStruct((n, REST_COLS), BF16), jax.ShapeDtypeStruct((ns, REST_COLS), BF16)],
        scratch_shapes=[pltpu.VMEM((tm, D_MODEL), BF16), pltpu.VMEM((ns, D_MODEL), BF16)],
        compiler_params=_cparams(("arbitrary", "arbitrary")),
        name="proj_rest",
    )(x, xs, g_pre, w_in_t)


def _mem_kv_kernel(x_ref, g_ref, w_ref, gk_ref, k_ref, v_ref, kb_ref, vb_ref):
    h = _rms(x_ref[...], g_ref[...]).astype(BF16)
    z = _dot(h, w_ref[...])
    for hh in range(M_HEADS):
        sl = slice(hh * M_HEAD_DIM, (hh + 1) * M_HEAD_DIM)
        kh = _rms(z[:, sl], gk_ref[...])
        k_ref[:, sl] = kh
        kb_ref[:, sl] = kh.astype(BF16)
    v = z[:, M_WIDTH:]
    v_ref[...] = v
    vb_ref[...] = v.astype(BF16)


def _mem_kv(mem, g_mem, w_mem, g_mk):
    n = mem.shape[0]
    blk = pl.BlockSpec((N_MEM, M_WIDTH), lambda i: (i, 0))
    return pl.pallas_call(
        _mem_kv_kernel,
        grid=(n // N_MEM,),
        in_specs=[pl.BlockSpec((N_MEM, D_MODEL), lambda i: (i, 0)), _const_spec((1, D_MODEL)),
                  _const_spec((D_MODEL, 2 * M_WIDTH)), _const_spec((1, M_HEAD_DIM))],
        out_specs=[blk, blk, blk, blk],
        out_shape=[jax.ShapeDtypeStruct((n, M_WIDTH), dt) for dt in (F32, F32, BF16, BF16)],
        compiler_params=_cparams(("arbitrary",)),
        name="mem_kv",
    )(mem, g_mem, w_mem, g_mk)


KEY_NEG_INF = np.int32(-0x7F800000)


def _key_to_f32(key):
    return pltpu.bitcast(jnp.where(key >= 0, key, INT_MIN - key), F32)


def _select_params(count, topk, pos_bits, j_ref):
    kf = float(topk)
    t0 = jnp.where(count(lambda s, p: s >= 0.0) >= kf, jnp.int32(0), INT_MIN)

    def bit_body(b, t):
        cand = t + lax.shift_left(jnp.int32(1), 30 - b)
        cand_f = _key_to_f32(cand)
        return jnp.where(count(lambda s, p: s >= cand_f) >= kf, cand, t)

    t = _key_to_f32(jnp.maximum(lax.fori_loop(0, 31, bit_body, t0), KEY_NEG_INF))
    tie = (count(lambda s, p: s >= t) > kf) & (t > NEG_INF)
    j_ref[...] = jnp.full(j_ref.shape, INT_MAX, I32)

    @pl.when(jnp.max(tie.astype(I32)) > 0)
    def _():
        n_gt = count(lambda s, p: s > t)

        def pos_body(b, p_lo):
            cand = p_lo + lax.shift_left(jnp.int32(1), pos_bits - 1 - b)
            n_eq = count(lambda s, p: (s == t) & (p < cand))
            return jnp.where(n_gt + n_eq < kf, cand, p_lo)

        p_sel = lax.fori_loop(0, pos_bits, pos_body, jnp.zeros(t.shape, I32))
        j_ref[...] = jnp.where(tie, p_sel, INT_MAX)

    return t, j_ref[...]


def _fori_by_two(n, body, init):
    carry = lax.fori_loop(0, n // 2, lambda j, c: body(2 * j + 1, body(2 * j, c)), init)
    return lax.cond(n % 2 == 1, lambda c: body(n - 1, c), lambda c: c, carry)


def _dsa_prompt_kernel(qi_ref, wit_ref, q_ref, ki_ref, k_ref, v_ref, o_ref,
                       sc_ref, bias_ref, vt_ref, j_ref, s_ref, acc_ref, *, topk, kc, pos_bits):
    i = pl.program_id(1)
    nck = (i * Q_BLOCK + Q_BLOCK + kc - 1) // kc
    nchunks = vt_ref.shape[0]
    rep = A_HEADS // A_KV_HEADS
    q_pos = i * Q_BLOCK + lax.broadcasted_iota(I32, (1, Q_BLOCK), 1)
    sub = lax.broadcasted_iota(I32, (kc, 1), 0)

    @pl.when(i == 0)
    def _():
        for c in range(nchunks):
            for g in range(A_KV_HEADS):
                gsl = slice(g * A_HEAD_DIM, (g + 1) * A_HEAD_DIM)
                vt_ref[c, gsl, :] = v_ref[c * kc:(c + 1) * kc, gsl].astype(F32).T.astype(BF16)

    def chunk_rows(c):
        return pl.ds(pl.multiple_of(c * kc, kc), kc)

    qi = qi_ref[...]
    wit = wit_ref[...]
    qi_pairs = [jnp.concatenate([qi[:, (2 * j) * IDX_DIM:(2 * j + 1) * IDX_DIM],
                                 qi[:, (2 * j + 1) * IDX_DIM:(2 * j + 2) * IDX_DIM]], axis=0)
                for j in range(IDX_HEADS // 2)]

    def score_body(c, carry):
        kic = ki_ref[chunk_rows(c), :]
        acc = jnp.zeros((kc, Q_BLOCK), F32)
        for j in range(IDX_HEADS // 2):
            d = _dot_nt(kic, qi_pairs[j])
            acc = acc + jnp.maximum(d[:, :Q_BLOCK], 0.0) * wit[2 * j:2 * j + 1, :]
            acc = acc + jnp.maximum(d[:, Q_BLOCK:], 0.0) * wit[2 * j + 1:2 * j + 2, :]
        sc_ref[c] = jnp.where(c * kc + sub <= q_pos, acc, NEG_INF)
        return carry

    _fori_by_two(nck, score_body, 0)

    def count(pred):
        def body(c, acc):
            part = jnp.where(pred(sc_ref[c], c * kc + sub), 1.0, 0.0)
            return acc + jnp.sum(part.reshape(kc // 64, 64, Q_BLOCK), axis=0)

        acc = lax.fori_loop(0, nck, body, jnp.zeros((64, Q_BLOCK), F32))
        return jnp.sum(acc, axis=0, keepdims=True)

    thr, jsel = _select_params(count, topk, pos_bits, j_ref)

    def bias_body(c, carry):
        s = sc_ref[c]
        pos = c * kc + sub
        sel = ((s > thr) | ((s == thr) & (pos <= jsel))) & (pos <= q_pos)
        bias_ref[c] = jnp.where(sel, 0.0, NEG_INF)
        return carry

    lax.fori_loop(0, nck, bias_body, 0)

    gsl = lambda g: slice(g * A_HEAD_DIM, (g + 1) * A_HEAD_DIM)
    row0 = lambda v: jnp.full((1, rep * Q_BLOCK), v, F32)
    groups = tuple(range(A_KV_HEADS))
    qgs = [jnp.concatenate([q_ref[:, gsl(g * rep + r)] for r in range(rep)], axis=0) for g in groups]

    def write_out(ls):
        for g in groups:
            o = acc_ref[g] / ls[g]
            for r in range(rep):
                o_ref[:, gsl(g * rep + r)] = o[:, r * Q_BLOCK:(r + 1) * Q_BLOCK].T.astype(BF16)

    def qk_body(c, ms):
        b = bias_ref[c]
        bias = jnp.concatenate([b] * rep, axis=1)
        out = []
        for g in groups:
            s = _dot_nt(k_ref[chunk_rows(c), gsl(g)], qgs[g]) + bias
            s_ref[c, g] = s
            out.append(jnp.maximum(ms[g], jnp.max(s, axis=0, keepdims=True)))
        return tuple(out)

    ms = _fori_by_two(nck, qk_body, (row0(NEG_INF),) * len(groups))
    acc_ref[...] = jnp.zeros(acc_ref.shape, F32)

    def pv_body(c, ls):
        out = []
        for g in groups:
            p = jnp.exp2(s_ref[c, g] - ms[g])
            out.append(ls[g] + jnp.sum(p, axis=0, keepdims=True))
            acc_ref[g] += _dot(vt_ref[c, gsl(g), :], p.astype(BF16))
        return tuple(out)

    write_out(_fori_by_two(nck, pv_body, (row0(0.0),) * len(groups)))


def _dsa_prompt(qi, wit, q, ki_b, k_b, v_b, batch, seq, topk):
    nqb = seq // Q_BLOCK
    rep = A_HEADS // A_KV_HEADS
    kc = min(512, seq)
    nchunks = seq // kc
    pos_bits = max(1, int(np.ceil(np.log2(seq))))
    qrow = lambda w: pl.BlockSpec((Q_BLOCK, w), lambda b, i: (b * nqb + i, 0))
    seqblk = lambda w: pl.BlockSpec((seq, w), lambda b, i: (b, 0), pipeline_mode=pl.Buffered(1))
    return pl.pallas_call(
        functools.partial(_dsa_prompt_kernel, topk=topk, kc=kc, pos_bits=pos_bits),
        grid=(batch, nqb),
        in_specs=[qrow(IDX_HEADS * IDX_DIM), pl.BlockSpec((IDX_HEADS, Q_BLOCK), lambda b, i: (0, b * nqb + i)),
                  qrow(A_WIDTH), seqblk(IDX_DIM), seqblk(A_KV_WIDTH), seqblk(A_KV_WIDTH)],
        out_specs=qrow(A_WIDTH),
        out_shape=jax.ShapeDtypeStruct((batch * seq, A_WIDTH), BF16),
        scratch_shapes=[pltpu.VMEM((nchunks, kc, Q_BLOCK), F32), pltpu.VMEM((nchunks, kc, Q_BLOCK), F32),
                        pltpu.VMEM((nchunks, A_KV_WIDTH, kc), BF16), pltpu.VMEM((1, Q_BLOCK), I32),
                        pltpu.VMEM((nchunks, A_KV_HEADS, kc, rep * Q_BLOCK), F32),
                        pltpu.VMEM((A_KV_HEADS, A_HEAD_DIM, rep * Q_BLOCK), F32)],
        compiler_params=_cparams(("arbitrary", "arbitrary")),
        name="dsa_prompt",
    )(qi, wit, q, ki_b, k_b, v_b)


def _fold_heads(e, nq):
    acc = e[0:8]
    for t in range(1, e.shape[0] // 8):
        acc = acc + e[8 * t:8 * (t + 1)]
    return acc + pltpu.roll(acc, nq, 0)


def _tree(op, parts):
    parts = list(parts)
    while len(parts) > 1:
        parts = [op(a, b) for a, b in zip(parts[::2], parts[1::2])] + parts[len(parts) & ~1:]
    return parts[0]


def _lane_fold(op, x):
    return _tree(op, [x[:, t * 128:(t + 1) * 128] for t in range(x.shape[1] // 128)])


def _dsa_sample_kernel(pt_ref, qp_ref, wc_ref, kin_ref, qs_ref, kn_ref, vn_ref, kidx_hbm, k_hbm, v_hbm, o_ref,
                       sc_ref, scn_ref, bias_ref, biasn_ref, j_ref, s_ref, sn_ref, m_ref, mfin_ref, l_ref, acc_ref,
                       buf_ref, idx_raw, kv_raw, idx_sem, kv_sem, *, pages, npages, nq, topk, pos_bits):
    b_idx, s_idx = pl.program_id(0), pl.program_id(1)
    kw = pages * PAGE
    steps = npages // pages
    n_past = npages * PAGE
    rows_q = qs_ref.shape[1]
    dup = lambda x8: jnp.concatenate([x8] * (rows_q // 8), axis=0)
    gsl = lambda g: slice(g * A_HEAD_DIM, (g + 1) * A_HEAD_DIM)

    n = b_idx * (3 * steps) + s_idx
    slot = n % 2

    def block_copies(seq, step, sl, start):
        for phase, (src, dst, sem) in enumerate(((kidx_hbm, idx_raw, idx_sem), (k_hbm, kv_raw, kv_sem),
                                                 (v_hbm, kv_raw, kv_sem))):
            @pl.when((step >= phase * steps) & (step < (phase + 1) * steps))
            def _(phase=phase, src=src, dst=dst, sem=sem):
                for p in range(pages):
                    page = pt_ref[seq * npages + (step - phase * steps) * pages + p] if start else 0
                    copy = pltpu.make_async_copy(src.at[page], dst.at[sl, p], sem.at[sl])
                    copy.start() if start else copy.wait()

    @pl.when(n == 0)
    def _():
        block_copies(b_idx, s_idx, slot, True)

    @pl.when(n + 1 < pl.num_programs(0) * 3 * steps)
    def _():
        wrap = s_idx + 1 == 3 * steps
        block_copies(jnp.where(wrap, b_idx + 1, b_idx), jnp.where(wrap, 0, s_idx + 1), 1 - slot, True)

    block_copies(b_idx, s_idx, slot, False)

    def load_pages():
        for p in range(pages):
            for g in range(A_KV_HEADS):
                rows = pl.ds(g, PAGE, stride=A_KV_HEADS)
                buf_ref[p * PAGE:(p + 1) * PAGE, gsl(g)] = kv_raw[slot, p, rows, :].astype(BF16)

    @pl.when(s_idx < steps)
    def _():
        qp = qp_ref[...]
        wc = wc_ref[...]
        kpt = jnp.concatenate([idx_raw[slot, p] for p in range(pages)], axis=1).astype(BF16)
        sc_ref[s_idx] = _fold_heads(jnp.maximum(_dot(qp, kpt), 0.0) * wc, nq)

        @pl.when(s_idx == steps - 1)
        def _():
            qrow = lax.broadcasted_iota(I32, (8, 1), 0) % nq
            lane = lax.broadcasted_iota(I32, (1, kw), 1)
            lane_n = lax.broadcasted_iota(I32, (1, PAGE), 1)
            kin = jnp.concatenate([kin_ref[...], jnp.zeros((PAGE - S_PAD, IDX_DIM), BF16)], axis=0)
            scn = _fold_heads(jnp.maximum(_dot_nt(qp, kin), 0.0) * wc, nq)
            scn_ref[...] = jnp.where(lane_n <= qrow, scn, NEG_INF)

            def count(pred):
                ones = _tree(jnp.add, [jnp.where(pred(sc_ref[t], t * kw + lane), 1.0, 0.0) for t in range(steps)])
                cn = jnp.where(pred(scn_ref[...], n_past + lane_n), 1.0, 0.0)
                return (jnp.sum(_lane_fold(jnp.add, ones), axis=-1, keepdims=True)
                        + jnp.sum(cn, axis=-1, keepdims=True))

            thr, jsel = _select_params(count, topk, pos_bits, j_ref)
            for t in range(steps):
                sp = sc_ref[t]
                bias_ref[t] = jnp.where((sp > thr) | ((sp == thr) & (t * kw + lane <= jsel)), 0.0, NEG_INF)
            sn = scn_ref[...]
            seln = ((sn > thr) | ((sn == thr) & (n_past + lane_n <= jsel))) & (lane_n <= qrow)
            biasn_ref[...] = jnp.where(seln, 0.0, NEG_INF)

    @pl.when((s_idx >= steps) & (s_idx < 2 * steps))
    def _():
        t = s_idx - steps
        load_pages()
        bias = dup(bias_ref[t])

        @pl.when(t == 0)
        def _():
            m_ref[...] = jnp.full(m_ref.shape, NEG_INF, F32)

        for g in range(A_KV_HEADS):
            s = _dot_nt(qs_ref[g], buf_ref[:, gsl(g)]) + bias
            s_ref[g, t] = s
            m_ref[g] = jnp.maximum(m_ref[g], _lane_fold(jnp.maximum, s))

        @pl.when(t == steps - 1)
        def _():
            bn = dup(biasn_ref[:, :S_PAD])
            for g in range(A_KV_HEADS):
                sn = _dot_nt(qs_ref[g], kn_ref[:, gsl(g)]) + bn
                sn_ref[g] = sn
                mfin_ref[g] = jnp.maximum(jnp.max(m_ref[g], axis=-1, keepdims=True),
                                          jnp.max(sn, axis=-1, keepdims=True))

    @pl.when(s_idx >= 2 * steps)
    def _():
        t = s_idx - 2 * steps
        load_pages()

        @pl.when(t == 0)
        def _():
            l_ref[...] = jnp.zeros(l_ref.shape, F32)
            acc_ref[...] = jnp.zeros(acc_ref.shape, F32)

        for g in range(A_KV_HEADS):
            p = jnp.exp2(s_ref[g, t] - mfin_ref[g])
            l_ref[g] += _lane_fold(jnp.add, p)
            acc_ref[g] += _dot(p.astype(BF16), buf_ref[:, gsl(g)])

        @pl.when(t == steps - 1)
        def _():
            for g in range(A_KV_HEADS):
                pn = jnp.exp2(sn_ref[g] - mfin_ref[g])
                l = jnp.sum(l_ref[g], axis=-1, keepdims=True) + jnp.sum(pn, axis=-1, keepdims=True)
                o_ref[g] = (acc_ref[g] + _dot(pn.astype(BF16), vn_ref[:, gsl(g)])) / l


def _dsa_sample(page_flat, qp, wcol, ki_new_b, qs, k_new_b, v_new_b, cache_kidx_t, cache_k, cache_v,
                nseq, npages, pages, nq, topk):
    steps = npages // pages
    n_past = npages * PAGE
    kw = pages * PAGE
    rows = IDX_HEADS * nq
    rows_q = qs.shape[2]
    pos_bits = int(np.ceil(np.log2(n_past + PAGE)))

    seq3 = lambda shape: pl.BlockSpec((None,) + shape, lambda b, s, pt: (b,) + (0,) * len(shape))
    new_rows = lambda w: pl.BlockSpec((S_PAD, w), lambda b, s, pt: (b, 0))
    in_hbm = pl.BlockSpec(memory_space=pl.ANY)
    grid_spec = pltpu.PrefetchScalarGridSpec(
        num_scalar_prefetch=1,
        grid=(nseq, 3 * steps),
        in_specs=[seq3((rows, IDX_DIM)), seq3((rows, 1)), new_rows(IDX_DIM),
                  seq3((A_KV_HEADS, rows_q, A_HEAD_DIM)), new_rows(A_KV_WIDTH), new_rows(A_KV_WIDTH),
                  in_hbm, in_hbm, in_hbm],
        out_specs=seq3((A_KV_HEADS, rows_q, A_HEAD_DIM)),
        scratch_shapes=[pltpu.VMEM((steps, 8, kw), F32), pltpu.VMEM((8, PAGE), F32),
                        pltpu.VMEM((steps, 8, kw), F32), pltpu.VMEM((8, PAGE), F32),
                        pltpu.VMEM((8, 1), I32),
                        pltpu.VMEM((A_KV_HEADS, steps, rows_q, kw), F32),
                        pltpu.VMEM((A_KV_HEADS, rows_q, S_PAD), F32),
                        pltpu.VMEM((A_KV_HEADS, rows_q, 128), F32),
                        pltpu.VMEM((A_KV_HEADS, rows_q, 1), F32),
                        pltpu.VMEM((A_KV_HEADS, rows_q, 128), F32),
                        pltpu.VMEM((A_KV_HEADS, rows_q, A_HEAD_DIM), F32),
                        pltpu.VMEM((kw, A_KV_WIDTH), BF16),
                        pltpu.VMEM((2, pages, IDX_DIM, PAGE), F32),
                        pltpu.VMEM((2, pages, PAGE * A_KV_HEADS, A_HEAD_DIM), F32),
                        pltpu.SemaphoreType.DMA((2,)), pltpu.SemaphoreType.DMA((2,))],
    )
    return pl.pallas_call(
        functools.partial(_dsa_sample_kernel, pages=pages, npages=npages, nq=nq, topk=topk, pos_bits=pos_bits),
        grid_spec=grid_spec,
        out_shape=jax.ShapeDtypeStruct((nseq, A_KV_HEADS, rows_q, A_HEAD_DIM), F32),
        compiler_params=_cparams(("arbitrary", "arbitrary")),
        name="dsa_sample",
    )(page_flat, qp, wcol, ki_new_b, qs, k_new_b, v_new_b, cache_kidx_t, cache_k, cache_v)


def _merge_kernel(x_ref, oa_ref, ag_ref, bu_ref, bv_ref, bg_ref, cq_ref, cg_ref, ra_ref, rb_ref, rc_ref,
                  mk_ref, mv_ref, ws_ref, bs_ref, gsgu_ref, gmq_ref, wpa_ref, wpb_ref, wpc_ref, wout_ref,
                  y_ref, *maybe_vn_ref, tm, chunk, mem_groups):
    f32 = lambda r: r[...].astype(F32)
    silu = lambda t: t * jax.nn.sigmoid(t)

    vn = _rms(f32(bv_ref), gsgu_ref[...])
    if maybe_vn_ref:
        maybe_vn_ref[0][...] = vn
    vnb = vn.astype(BF16)
    bu = f32(bu_ref)
    tril = (lax.broadcasted_iota(I32, (chunk, chunk), 1) <= lax.broadcasted_iota(I32, (chunk, chunk), 0))
    ob_cols = []
    for g in range(B_GROUPS):
        wg = jnp.where(tril, ws_ref[g], 0.0).astype(BF16)
        gsl = slice(g * B_GROUP_DIM, (g + 1) * B_GROUP_DIM)
        parts = [_dot(wg, vnb[c * chunk:(c + 1) * chunk, gsl]) + bs_ref[:, g:g + 1] for c in range(tm // chunk)]
        ob_cols.append(parts[0] if len(parts) == 1 else jnp.concatenate(parts, axis=0))
    ob = bu * jnp.concatenate(ob_cols, axis=1)
    pb = _dot((ob * silu(f32(bg_ref))).astype(BF16), wpb_ref[...])

    cq = f32(cq_ref)
    rows_g = tm // mem_groups
    oc_cols = []
    for hh in range(M_HEADS):
        hsl = slice(hh * M_HEAD_DIM, (hh + 1) * M_HEAD_DIM)
        qn = (_rms(cq[:, hsl], gmq_ref[...]) * (M_HEAD_DIM ** -0.5)).astype(BF16)
        oc_rows = []
        for u in range(mem_groups):
            msl = slice(u * N_MEM, (u + 1) * N_MEM)
            s = _dot_nt(qn[u * rows_g:(u + 1) * rows_g], mk_ref[msl, hsl])
            p = jnp.exp(s - jnp.max(s, axis=-1, keepdims=True))
            oc_rows.append(_dot(p.astype(BF16), mv_ref[msl, hsl]) / jnp.sum(p, axis=-1, keepdims=True))
        oc_cols.append(oc_rows[0] if mem_groups == 1 else jnp.concatenate(oc_rows, axis=0))
    oc = jnp.concatenate(oc_cols, axis=1)
    pc = _dot((oc * silu(f32(cg_ref))).astype(BF16), wpc_ref[...])

    pa = _dot((f32(oa_ref) * silu(f32(ag_ref))).astype(BF16), wpa_ref[...])
    sig = jax.nn.sigmoid
    m = sig(f32(ra_ref)) * pa + sig(f32(rb_ref)) * pb + sig(f32(rc_ref)) * pc
    y_ref[...] = x_ref[...] + _dot(m.astype(BF16), wout_ref[...])


def _merge(x, oa, zr, mk_b, mv_b, ws, bs_t, g_sgu, g_mq, w_pa, w_pb, w_pc, w_out, tm, chunk, mem_groups, mem_map,
           emit_vn):
    n = x.shape[0]
    col = lambda w, j: pl.BlockSpec((tm, w), lambda i, j=j: (i, j))
    mem = pl.BlockSpec((mem_groups * N_MEM, M_WIDTH), mem_map)
    in_specs = [col(D_MODEL, 0), col(A_WIDTH, 0),
                col(1024, 0), col(1024, 1), col(1024, 2), col(1024, 3), col(1024, 4), col(1024, 5),
                col(2048, 3), col(2048, 4), col(2048, 5),
                mem, mem,
                _const_spec((B_GROUPS, chunk, chunk)), _const_spec((chunk, B_GROUPS)),
                _const_spec((1, B_WIDTH)), _const_spec((1, M_HEAD_DIM)),
                _const_spec((A_WIDTH, D_MODEL)), _const_spec((B_WIDTH, D_MODEL)),
                _const_spec((M_WIDTH, D_MODEL)), _const_spec((D_MODEL, D_MODEL))]
    out_specs = [col(D_MODEL, 0)]
    out_shape = [jax.ShapeDtypeStruct((n, D_MODEL), F32)]
    if emit_vn:
        out_specs.append(col(B_WIDTH, 0))
        out_shape.append(jax.ShapeDtypeStruct((n, B_WIDTH), F32))
    return pl.pallas_call(
        functools.partial(_merge_kernel, tm=tm, chunk=chunk, mem_groups=mem_groups),
        grid=(n // tm,),
        in_specs=in_specs,
        out_specs=out_specs,
        out_shape=out_shape,
        compiler_params=_cparams(("arbitrary",)),
        name="merge",
    )(x, oa, *([zr] * 9), mk_b, mv_b, ws, bs_t, g_sgu, g_mq, w_pa, w_pb, w_pc, w_out)


def _rope_tables(pos):
    pos = pos.astype(F32)[:, None]

    def cs(half):
        freq = ROPE_THETA ** (-jnp.arange(half, dtype=F32) / half)
        ang = pos * freq[None, :]
        return jnp.cos(ang), jnp.sin(ang)

    c, s = cs(A_HEAD_DIM // 2)
    ci, si = cs(IDX_DIM // 2)
    z = jnp.zeros_like(si)
    return (jnp.concatenate([c, c], axis=1), jnp.concatenate([-s, s], axis=1),
            jnp.concatenate([ci] * 4, axis=1), jnp.concatenate([-si, z, -si, z], axis=1),
            jnp.concatenate([z, si, z, si], axis=1))


def kernel(x_prompt, x_sample, cache_k, cache_v, cache_kidx, cache_mem_k, cache_mem_v, page_table,
           mem_prompt, g_pre, w_in, g_q, g_k, g_mq, g_mk, g_mem, w_mem_kv, g_sgu, w_s, b_s,
           w_pa, w_pb, w_pc, w_out):
    batch, seq, _ = x_prompt.shape
    nseq, nq, _ = x_sample.shape
    assert nq == 4 and nq <= S_PAD
    npages = page_table.shape[1]
    n_past = npages * PAGE
    n_pool = cache_k.shape[0]
    row2 = lambda a: a.reshape(1, -1)

    w_in_t = w_in.T
    w_a = w_in_t[:A_COLS].astype(BF16)
    w_pa_b, w_pb_b, w_pc_b, w_out_b = (w.astype(BF16) for w in (w_pa, w_pb, w_pc, w_out))
    w_mem_b = w_mem_kv.astype(BF16)
    g_pre2, g_q2, g_k2, g_mq2, g_mk2, g_mem2, g_sgu2 = map(row2, (g_pre, g_q, g_k, g_mq, g_mk, g_mem, g_sgu))

    xp = x_prompt.reshape(batch * seq, D_MODEL)
    tm_a = min(256, seq)
    nblk = seq // tm_a
    tabs_p = _rope_tables(jnp.arange(seq))
    q, k_p, v_p, k_b, v_b, qi, ki_p, ki_b, wi = _proj_a(
        xp, g_pre2, w_a, g_q2, g_k2, tabs_p, tm_a, lambda i: (i % nblk, 0))
    xs = jnp.pad(x_sample, ((0, 0), (0, S_PAD - nq), (0, 0))).reshape(nseq * S_PAD, D_MODEL)
    rows_s = nseq * S_PAD
    zr, zr_s = _proj_rest(xp, xs, g_pre2, w_in_t, min(1024, seq), PROJ_TN)
    mk_p, mv_p, mk_b, mv_b = _mem_kv(mem_prompt.reshape(batch * N_MEM, D_MODEL), g_mem2, w_mem_b, g_mk2)
    oa = _dsa_prompt(qi, wi.T, q, ki_b, k_b, v_b, batch, seq, min(TOPK_MAX, seq // 4))
    tm_m = min(256, seq)
    nblk_m = seq // tm_m
    (y_p,) = _merge(xp, oa, zr, mk_b, mv_b, w_s, b_s.T, g_sgu2, g_mq2, w_pa_b, w_pb_b, w_pc_b, w_out_b,
                    tm_m, CHUNK, 1, lambda i: (i // nblk_m, 0), False)

    tabs_s = tuple(jnp.tile(t, (nseq, 1)) for t in _rope_tables(n_past + jnp.arange(S_PAD)))
    q_s, k_s, v_s, k_sb, v_sb, qi_s, ki_s, ki_sb, wi_s = _proj_a(
        xs, g_pre2, w_a, g_q2, g_k2, tabs_s, rows_s, lambda i: (0, 0))

    qp = (qi_s.reshape(nseq, S_PAD, IDX_HEADS, IDX_DIM)[:, :nq]
          .transpose(0, 2, 1, 3).reshape(nseq, IDX_HEADS * nq, IDX_DIM))
    wcol = wi_s.reshape(nseq, S_PAD, IDX_HEADS)[:, :nq].transpose(0, 2, 1).reshape(nseq, IDX_HEADS * nq, 1)
    rep = A_HEADS // A_KV_HEADS
    qs = (q_s.reshape(nseq, S_PAD, A_KV_HEADS, rep, A_HEAD_DIM)[:, :nq]
          .transpose(0, 2, 3, 1, 4).reshape(nseq, A_KV_HEADS, rep * nq, A_HEAD_DIM))
    qs = jnp.concatenate([qs, qs], axis=2)
    page_flat = page_table.reshape(-1)
    pages = min(SAMPLE_PAGES, npages)
    o_s = _dsa_sample(page_flat, qp, wcol, ki_sb, qs, k_sb, v_sb, jnp.swapaxes(cache_kidx, 1, 2),
                      cache_k.reshape(n_pool, PAGE * A_KV_HEADS, A_HEAD_DIM),
                      cache_v.reshape(n_pool, PAGE * A_KV_HEADS, A_HEAD_DIM),
                      nseq, npages, pages, nq, min(TOPK_MAX, (n_past + nq) // 4))
    oa_s = (o_s[:, :, :rep * nq].reshape(nseq, A_KV_HEADS, rep, nq, A_HEAD_DIM)
            .transpose(0, 3, 1, 2, 4).reshape(nseq, nq, A_WIDTH))
    oa_s = jnp.pad(oa_s, ((0, 0), (0, S_PAD - nq), (0, 0))).reshape(rows_s, A_WIDTH).astype(BF16)
    mk_s = cache_mem_k.reshape(nseq * N_MEM, M_WIDTH).astype(BF16)
    mv_s = cache_mem_v.reshape(nseq * N_MEM, M_WIDTH).astype(BF16)
    y_s, vn_s = _merge(xs, oa_s, zr_s, mk_s, mv_s, w_s[:, :S_PAD, :S_PAD], b_s[:, :S_PAD].T, g_sgu2, g_mq2,
                       w_pa_b, w_pb_b, w_pc_b, w_out_b, rows_s, S_PAD, nseq, lambda i: (0, 0), True)

    take = lambda a, shape: a.reshape(nseq, S_PAD, -1)[:, :nq].reshape(shape)
    return (y_p.reshape(batch, seq, D_MODEL),
            take(y_s, (nseq, nq, D_MODEL)),
            k_p.reshape(batch, seq, A_KV_HEADS, A_HEAD_DIM),
            v_p.reshape(batch, seq, A_KV_HEADS, A_HEAD_DIM),
            ki_p.reshape(batch, seq, IDX_DIM),
            mk_p.reshape(batch, N_MEM, M_HEADS, M_HEAD_DIM),
            mv_p.reshape(batch, N_MEM, M_HEADS, M_HEAD_DIM),
            take(k_s, (nseq, nq, A_KV_HEADS, A_HEAD_DIM)),
            take(v_s, (nseq, nq, A_KV_HEADS, A_HEAD_DIM)),
            take(ki_s, (nseq, nq, IDX_DIM)),
            take(vn_s, (nseq, nq, B_GROUPS, B_GROUP_DIM)))
```

```python
import functools

import numpy as np
import jax
import jax.numpy as jnp
from jax import lax
from jax.experimental import pallas as pl
from jax.experimental.pallas import tpu as pltpu

F32 = jnp.float32
BF16 = jnp.bfloat16
I32 = jnp.int32

D_MODEL = 2048
PAGE = 128
A_HEADS = 8
A_KV_HEADS = 4
A_HEAD_DIM = 128
A_WIDTH = A_HEADS * A_HEAD_DIM
A_KV_WIDTH = A_KV_HEADS * A_HEAD_DIM
IDX_HEADS = 16
IDX_DIM = 64
TOPK_MAX = 256
Q_BLOCK = 128
ROPE_THETA = 10000.0
CHUNK = 128
B_GROUPS = 8
B_GROUP_DIM = 128
B_WIDTH = B_GROUPS * B_GROUP_DIM
N_MEM = 256
M_HEADS = 4
M_HEAD_DIM = 256
M_WIDTH = M_HEADS * M_HEAD_DIM
EPS = 1e-6

OFF_K = A_WIDTH
OFF_V = OFF_K + A_KV_WIDTH
OFF_QI = OFF_V + A_KV_WIDTH
OFF_KI = OFF_QI + IDX_HEADS * IDX_DIM
OFF_WI = OFF_KI + IDX_DIM
OFF_REST = OFF_WI + IDX_HEADS
A_COLS = 3200
REST_COLS = A_WIDTH + 3 * B_WIDTH + 2 * M_WIDTH + 3 * D_MODEL

Q_SCALE = float(np.log2(np.e)) * A_HEAD_DIM ** -0.5
S_PAD = 16
PROJ_TN = 1024
BISECT_STEPS = 24
SAMPLE_PAGES = 16
INT_MIN = np.int32(-2 ** 31)
INT_MAX = np.int32(2 ** 31 - 1)
NEG_INF = float("-inf")

V7X_VMEM_LIMIT = 56 * 1024 * 1024


def _cparams(sem):
    return pltpu.CompilerParams(dimension_semantics=sem, vmem_limit_bytes=V7X_VMEM_LIMIT)


def _dot(a, b):
    return jnp.dot(a, b, preferred_element_type=F32)


def _dot_nt(a, b):
    return lax.dot_general(a, b, (((1,), (1,)), ((), ())), preferred_element_type=F32)


def _rms(x, g):
    return x * lax.rsqrt(jnp.mean(x * x, axis=-1, keepdims=True) + EPS) * g


def _const_spec(shape):
    nd = len(shape)
    return pl.BlockSpec(shape, lambda *_: (0,) * nd, pipeline_mode=pl.Buffered(1))


def _proj_a_kernel(x_ref, g_ref, w_ref, gq_ref, gk_ref, cq_ref, sq_ref, ci_ref, sia_ref, sib_ref,
                   q_ref, k_ref, v_ref, kb_ref, vb_ref, qi_ref, ki_ref, kib_ref, wi_ref):
    h = _rms(x_ref[...], g_ref[...]).astype(BF16)
    z = _dot_nt(h, w_ref[...])
    tm = z.shape[0]
    cq, sq = cq_ref[...], sq_ref[...]
    ci, sia, sib = ci_ref[...], sia_ref[...], sib_ref[...]

    def norm_rope(zz, g):
        n = _rms(zz, g)
        return n * cq + pltpu.roll(n, A_HEAD_DIM // 2, 1) * sq

    def rope_idx(zz):
        return zz * ci + pltpu.roll(zz, 96, 1) * sia + pltpu.roll(zz, 32, 1) * sib

    for hh in range(A_HEADS):
        sl = slice(hh * A_HEAD_DIM, (hh + 1) * A_HEAD_DIM)
        q_ref[:, sl] = (norm_rope(z[:, sl], gq_ref[...]) * Q_SCALE).astype(BF16)
    for hh in range(A_KV_HEADS):
        sl = slice(hh * A_HEAD_DIM, (hh + 1) * A_HEAD_DIM)
        kh = norm_rope(z[:, OFF_K + hh * A_HEAD_DIM:OFF_K + (hh + 1) * A_HEAD_DIM], gk_ref[...])
        vh = z[:, OFF_V + hh * A_HEAD_DIM:OFF_V + (hh + 1) * A_HEAD_DIM]
        head_rows = pl.ds(hh, tm, stride=A_KV_HEADS)
        k_ref[head_rows, :] = kh
        v_ref[head_rows, :] = vh
        kb_ref[:, sl] = kh.astype(BF16)
        vb_ref[:, sl] = vh.astype(BF16)
    for t in range(IDX_HEADS * IDX_DIM // 128):
        sl = slice(t * 128, (t + 1) * 128)
        qi_ref[:, sl] = rope_idx(z[:, OFF_QI + t * 128:OFF_QI + (t + 1) * 128]).astype(BF16)
    last = z[:, OFF_KI:OFF_KI + 128]
    ki = rope_idx(last)[:, :IDX_DIM]
    ki_ref[...] = ki
    kib_ref[...] = ki.astype(BF16)
    wi_ref[...] = last[:, IDX_DIM:IDX_DIM + IDX_HEADS] * ((IDX_HEADS ** -0.5) * (IDX_DIM ** -0.5))


def _proj_a(x, g_pre, w_a, g_q, g_k, tabs, tm, tab_map):
    n = x.shape[0]
    row = lambda w: pl.BlockSpec((tm, w), lambda i: (i, 0))
    tab = pl.BlockSpec((tm, 128), tab_map)
    outs = [(1, A_WIDTH, BF16), (A_KV_HEADS, A_HEAD_DIM, F32), (A_KV_HEADS, A_HEAD_DIM, F32),
            (1, A_KV_WIDTH, BF16), (1, A_KV_WIDTH, BF16),
            (1, IDX_HEADS * IDX_DIM, BF16), (1, IDX_DIM, F32), (1, IDX_DIM, BF16), (1, IDX_HEADS, F32)]
    return pl.pallas_call(
        _proj_a_kernel,
        grid=(n // tm,),
        in_specs=[row(D_MODEL), _const_spec((1, D_MODEL)), _const_spec((A_COLS, D_MODEL)),
                  _const_spec((1, A_HEAD_DIM)), _const_spec((1, A_HEAD_DIM)), tab, tab, tab, tab, tab],
        out_specs=[pl.BlockSpec((tm * r, w), lambda i: (i, 0)) for r, w, _ in outs],
        out_shape=[jax.ShapeDtypeStruct((n * r, w), dt) for r, w, dt in outs],
        compiler_params=_cparams(("arbitrary",)),
        name="proj_a",
    )(x, g_pre, w_a, g_q, g_k, *tabs)


def _proj_rest_kernel(x_ref, xs_ref, g_ref, w_ref, o_ref, os_ref, h_ref, hs_ref):
    i, j = pl.program_id(0), pl.program_id(1)

    @pl.when(j == 0)
    def _():
        h_ref[...] = _rms(x_ref[...], g_ref[...]).astype(BF16)

    @pl.when((i == 0) & (j == 0))
    def _():
        hs_ref[...] = _rms(xs_ref[...], g_ref[...]).astype(BF16)

    w = w_ref[...].astype(BF16)
    o_ref[...] = _dot_nt(h_ref[...], w).astype(BF16)

    @pl.when(i == 0)
    def _():
        os_ref[...] = _dot_nt(hs_ref[...], w).astype(BF16)


def _proj_rest(x, xs, g_pre, w_in_t, tm, tn):
    n, ns = x.shape[0], xs.shape[0]
    ncols = REST_COLS // tn
    xs_cols = lambda i, j: (0, jnp.where(i == 0, j, ncols - 1))
    return pl.pallas_call(
        _proj_rest_kernel,
        grid=(n // tm, ncols),
        in_specs=[pl.BlockSpec((tm, D_MODEL), lambda i, j: (i, 0)),
                  pl.BlockSpec((ns, D_MODEL), lambda i, j: (0, 0), pipeline_mode=pl.Buffered(1)),
                  pl.BlockSpec((1, D_MODEL), lambda i, j: (0, 0)),
                  pl.BlockSpec((pl.Element(tn), pl.Element(D_MODEL)),
                               lambda i, j: (pl.multiple_of(OFF_REST + j * tn, 16), 0))],
        out_specs=[pl.BlockSpec((tm, tn), lambda i, j: (i, j)), pl.BlockSpec((ns, tn), xs_cols)],
        out_shape=[jax.ShapeDtypeStruct((n, REST_COLS), BF16), jax.ShapeDtypeStruct((ns, REST_COLS), BF16)],
        scratch_shapes=[pltpu.VMEM((tm, D_MODEL), BF16), pltpu.VMEM((ns, D_MODEL), BF16)],
        compiler_params=_cparams(("arbitrary", "arbitrary")),
        name="proj_rest",
    )(x, xs, g_pre, w_in_t)


def _mem_kv_kernel(x_ref, g_ref, w_ref, gk_ref, k_ref, v_ref, kb_ref, vb_ref):
    h = _rms(x_ref[...], g_ref[...]).astype(BF16)
    z = _dot(h, w_ref[...])
    for hh in range(M_HEADS):
        sl = slice(hh * M_HEAD_DIM, (hh + 1) * M_HEAD_DIM)
        kh = _rms(z[:, sl], gk_ref[...])
        k_ref[:, sl] = kh
        kb_ref[:, sl] = kh.astype(BF16)
    v = z[:, M_WIDTH:]
    v_ref[...] = v
    vb_ref[...] = v.astype(BF16)


def _mem_kv(mem, g_mem, w_mem, g_mk):
    n = mem.shape[0]
    blk = pl.BlockSpec((N_MEM, M_WIDTH), lambda i: (i, 0))
    return pl.pallas_call(
        _mem_kv_kernel,
        grid=(n // N_MEM,),
        in_specs=[pl.BlockSpec((N_MEM, D_MODEL), lambda i: (i, 0)), _const_spec((1, D_MODEL)),
                  _const_spec((D_MODEL, 2 * M_WIDTH)), _const_spec((1, M_HEAD_DIM))],
        out_specs=[blk, blk, blk, blk],
        out_shape=[jax.ShapeDtypeStruct((n, M_WIDTH), dt) for dt in (F32, F32, BF16, BF16)],
        compiler_params=_cparams(("arbitrary",)),
        name="mem_kv",
    )(mem, g_mem, w_mem, g_mk)


KEY_NEG_INF = np.int32(-0x7F800000)


def _key_to_f32(key):
    return pltpu.bitcast(jnp.where(key >= 0, key, INT_MIN - key), F32)


def _select_params(count, topk, pos_bits, amax, n_valid, t_ref, j_ref):
    kf = float(topk)
    hi0 = amax * 1.000001 + 1e-30
    all_selected = (n_valid <= kf).astype(I32)

    def bisect_body(_, state):
        lo, hi, t, done = state
        mid = 0.5 * lo + 0.5 * hi
        n_ge = count(lambda s, p: s >= mid)
        hit = (n_ge == kf) & (done == 0)
        return (jnp.where(n_ge >= kf, mid, lo), jnp.where(n_ge >= kf, hi, mid),
                jnp.where(hit, mid, t), jnp.where(hit, 1, done))

    _, _, t_bis, done = lax.fori_loop(
        0, BISECT_STEPS, bisect_body, (-hi0, hi0, jnp.full(hi0.shape, NEG_INF, F32), all_selected))
    t_ref[...] = t_bis
    j_ref[...] = jnp.full(j_ref.shape, INT_MAX, I32)

    @pl.when(jnp.min(done) == 0)
    def _():
        t0 = jnp.where(count(lambda s, p: s >= 0.0) >= kf, jnp.int32(0), INT_MIN)

        def bit_body(b, t):
            cand = t + lax.shift_left(jnp.int32(1), 30 - b)
            cand_f = _key_to_f32(cand)
            return jnp.where(count(lambda s, p: s >= cand_f) >= kf, cand, t)

        t = _key_to_f32(jnp.maximum(lax.fori_loop(0, 31, bit_body, t0), KEY_NEG_INF))
        t_ref[...] = t
        tie = (count(lambda s, p: s >= t) > kf) & (t > NEG_INF)

        @pl.when(jnp.max(tie.astype(I32)) > 0)
        def _():
            n_gt = count(lambda s, p: s > t)

            def pos_body(b, p_lo):
                cand = p_lo + lax.shift_left(jnp.int32(1), pos_bits - 1 - b)
                n_eq = count(lambda s, p: (s == t) & (p < cand))
                return jnp.where(n_gt + n_eq < kf, cand, p_lo)

            p_sel = lax.fori_loop(0, pos_bits, pos_body, jnp.zeros(t.shape, I32))
            j_ref[...] = jnp.where(tie, p_sel, INT_MAX)

    return t_ref[...], j_ref[...]


def _fori_by_two(n, body, init):
    carry = lax.fori_loop(0, n // 2, lambda j, c: body(2 * j + 1, body(2 * j, c)), init)
    return lax.cond(n % 2 == 1, lambda c: body(n - 1, c), lambda c: c, carry)


def _dsa_prompt_kernel(qi_ref, wit_ref, q_ref, ki_ref, k_ref, v_ref, o_ref,
                       sc_ref, bias_ref, vt_ref, t_ref, j_ref, s_ref, acc_ref, *, topk, kc, pos_bits):
    i = pl.program_id(1)
    nck = (i * Q_BLOCK + Q_BLOCK + kc - 1) // kc
    nchunks = vt_ref.shape[0]
    rep = A_HEADS // A_KV_HEADS
    q_pos = i * Q_BLOCK + lax.broadcasted_iota(I32, (1, Q_BLOCK), 1)
    sub = lax.broadcasted_iota(I32, (kc, 1), 0)

    @pl.when(i == 0)
    def _():
        for c in range(nchunks):
            for g in range(A_KV_HEADS):
                gsl = slice(g * A_HEAD_DIM, (g + 1) * A_HEAD_DIM)
                vt_ref[c, gsl, :] = v_ref[c * kc:(c + 1) * kc, gsl].astype(F32).T.astype(BF16)

    def chunk_rows(c):
        return pl.ds(pl.multiple_of(c * kc, kc), kc)

    qi = qi_ref[...]
    wit = wit_ref[...]
    qi_pairs = [jnp.concatenate([qi[:, (2 * j) * IDX_DIM:(2 * j + 1) * IDX_DIM],
                                 qi[:, (2 * j + 1) * IDX_DIM:(2 * j + 2) * IDX_DIM]], axis=0)
                for j in range(IDX_HEADS // 2)]

    def score_body(c, amax):
        kic = ki_ref[chunk_rows(c), :]
        acc = jnp.zeros((kc, Q_BLOCK), F32)
        for j in range(IDX_HEADS // 2):
            d = _dot_nt(kic, qi_pairs[j])
            acc = acc + jnp.maximum(d[:, :Q_BLOCK], 0.0) * wit[2 * j:2 * j + 1, :]
            acc = acc + jnp.maximum(d[:, Q_BLOCK:], 0.0) * wit[2 * j + 1:2 * j + 2, :]
        sc_ref[c] = jnp.where(c * kc + sub <= q_pos, acc, NEG_INF)
        return jnp.maximum(amax, jnp.max(jnp.abs(acc).reshape(kc // 64, 64, Q_BLOCK), axis=0))

    amax = jnp.max(_fori_by_two(nck, score_body, jnp.zeros((64, Q_BLOCK), F32)), axis=0, keepdims=True)

    def count(pred):
        def body(c, acc):
            part = jnp.where(pred(sc_ref[c], c * kc + sub), 1.0, 0.0)
            return acc + jnp.sum(part.reshape(kc // 64, 64, Q_BLOCK), axis=0)

        acc = lax.fori_loop(0, nck, body, jnp.zeros((64, Q_BLOCK), F32))
        return jnp.sum(acc, axis=0, keepdims=True)

    thr, jsel = _select_params(count, topk, pos_bits, amax, (q_pos + 1).astype(F32), t_ref, j_ref)

    def bias_body(c, carry):
        s = sc_ref[c]
        pos = c * kc + sub
        sel = ((s > thr) | ((s == thr) & (pos <= jsel))) & (pos <= q_pos)
        bias_ref[c] = jnp.where(sel, 0.0, NEG_INF)
        return carry

    lax.fori_loop(0, nck, bias_body, 0)

    gsl = lambda g: slice(g * A_HEAD_DIM, (g + 1) * A_HEAD_DIM)
    row0 = lambda v: jnp.full((1, rep * Q_BLOCK), v, F32)
    groups = tuple(range(A_KV_HEADS))
    qgs = [jnp.concatenate([q_ref[:, gsl(g * rep + r)] for r in range(rep)], axis=0) for g in groups]

    def write_out(ls):
        for g in groups:
            o = acc_ref[g] / ls[g]
            for r in range(rep):
                o_ref[:, gsl(g * rep + r)] = o[:, r * Q_BLOCK:(r + 1) * Q_BLOCK].T.astype(BF16)

    def qk_body(c, ms):
        b = bias_ref[c]
        bias = jnp.concatenate([b] * rep, axis=1)
        out = []
        for g in groups:
            s = _dot_nt(k_ref[chunk_rows(c), gsl(g)], qgs[g]) + bias
            s_ref[c, g] = s
            out.append(jnp.maximum(ms[g], jnp.max(s, axis=0, keepdims=True)))
        return tuple(out)

    ms = _fori_by_two(nck, qk_body, (row0(NEG_INF),) * len(groups))
    acc_ref[...] = jnp.zeros(acc_ref.shape, F32)

    def pv_body(c, ls):
        out = []
        for g in groups:
            p = jnp.exp2(s_ref[c, g] - ms[g])
            out.append(ls[g] + jnp.sum(p, axis=0, keepdims=True))
            acc_ref[g] += _dot(vt_ref[c, gsl(g), :], p.astype(BF16))
        return tuple(out)

    write_out(_fori_by_two(nck, pv_body, (row0(0.0),) * len(groups)))


def _dsa_prompt(qi, wit, q, ki_b, k_b, v_b, batch, seq, topk):
    nqb = seq // Q_BLOCK
    rep = A_HEADS // A_KV_HEADS
    kc = min(512, seq)
    nchunks = seq // kc
    pos_bits = max(1, int(np.ceil(np.log2(seq))))
    qrow = lambda w: pl.BlockSpec((Q_BLOCK, w), lambda b, i: (b * nqb + i, 0))
    seqblk = lambda w: pl.BlockSpec((seq, w), lambda b, i: (b, 0), pipeline_mode=pl.Buffered(1))
    return pl.pallas_call(
        functools.partial(_dsa_prompt_kernel, topk=topk, kc=kc, pos_bits=pos_bits),
        grid=(batch, nqb),
        in_specs=[qrow(IDX_HEADS * IDX_DIM), pl.BlockSpec((IDX_HEADS, Q_BLOCK), lambda b, i: (0, b * nqb + i)),
                  qrow(A_WIDTH), seqblk(IDX_DIM), seqblk(A_KV_WIDTH), seqblk(A_KV_WIDTH)],
        out_specs=qrow(A_WIDTH),
        out_shape=jax.ShapeDtypeStruct((batch * seq, A_WIDTH), BF16),
        scratch_shapes=[pltpu.VMEM((nchunks, kc, Q_BLOCK), F32), pltpu.VMEM((nchunks, kc, Q_BLOCK), F32),
                        pltpu.VMEM((nchunks, A_KV_WIDTH, kc), BF16),
                        pltpu.VMEM((1, Q_BLOCK), F32), pltpu.VMEM((1, Q_BLOCK), I32),
                        pltpu.VMEM((nchunks, A_KV_HEADS, kc, rep * Q_BLOCK), F32),
                        pltpu.VMEM((A_KV_HEADS, A_HEAD_DIM, rep * Q_BLOCK), F32)],
        compiler_params=_cparams(("arbitrary", "arbitrary")),
        name="dsa_prompt",
    )(qi, wit, q, ki_b, k_b, v_b)


def _fold_heads(e, nq):
    acc = e[0:8]
    for t in range(1, e.shape[0] // 8):
        acc = acc + e[8 * t:8 * (t + 1)]
    return acc + pltpu.roll(acc, nq, 0)


def _tree(op, parts):
    parts = list(parts)
    while len(parts) > 1:
        parts = [op(a, b) for a, b in zip(parts[::2], parts[1::2])] + parts[len(parts) & ~1:]
    return parts[0]


def _lane_fold(op, x):
    return _tree(op, [x[:, t * 128:(t + 1) * 128] for t in range(x.shape[1] // 128)])


def _dsa_sample_kernel(pt_ref, qp_ref, wc_ref, kin_ref, qs_ref, kn_ref, vn_ref, kidx_hbm, k_hbm, v_hbm, o_ref,
                       sc_ref, scn_ref, bias_ref, biasn_ref, t_ref, j_ref, s_ref, sn_ref, m_ref, mfin_ref, l_ref, acc_ref,
                       buf_ref, idx_raw, kv_raw, idx_sem, kv_sem, *, pages, npages, nq, topk, pos_bits):
    b_idx, s_idx = pl.program_id(0), pl.program_id(1)
    kw = pages * PAGE
    steps = npages // pages
    n_past = npages * PAGE
    rows_q = qs_ref.shape[1]
    dup = lambda x8: jnp.concatenate([x8] * (rows_q // 8), axis=0)
    gsl = lambda g: slice(g * A_HEAD_DIM, (g + 1) * A_HEAD_DIM)

    n = b_idx * (3 * steps) + s_idx
    slot = n % 2

    def block_copies(seq, step, sl, start):
        for phase, (src, dst, sem) in enumerate(((kidx_hbm, idx_raw, idx_sem), (k_hbm, kv_raw, kv_sem),
                                                 (v_hbm, kv_raw, kv_sem))):
            @pl.when((step >= phase * steps) & (step < (phase + 1) * steps))
            def _(phase=phase, src=src, dst=dst, sem=sem):
                for p in range(pages):
                    page = pt_ref[seq * npages + (step - phase * steps) * pages + p] if start else 0
                    copy = pltpu.make_async_copy(src.at[page], dst.at[sl, p], sem.at[sl])
                    copy.start() if start else copy.wait()

    @pl.when(n == 0)
    def _():
        block_copies(b_idx, s_idx, slot, True)

    @pl.when(n + 1 < pl.num_programs(0) * 3 * steps)
    def _():
        wrap = s_idx + 1 == 3 * steps
        block_copies(jnp.where(wrap, b_idx + 1, b_idx), jnp.where(wrap, 0, s_idx + 1), 1 - slot, True)

    block_copies(b_idx, s_idx, slot, False)

    def load_pages():
        for p in range(pages):
            for g in range(A_KV_HEADS):
                rows = pl.ds(g, PAGE, stride=A_KV_HEADS)
                buf_ref[p * PAGE:(p + 1) * PAGE, gsl(g)] = kv_raw[slot, p, rows, :].astype(BF16)

    @pl.when(s_idx < steps)
    def _():
        qp = qp_ref[...]
        wc = wc_ref[...]
        kpt = jnp.concatenate([idx_raw[slot, p] for p in range(pages)], axis=1).astype(BF16)
        sc_ref[s_idx] = _fold_heads(jnp.maximum(_dot(qp, kpt), 0.0) * wc, nq)

        @pl.when(s_idx == steps - 1)
        def _():
            qrow = lax.broadcasted_iota(I32, (8, 1), 0) % nq
            lane = lax.broadcasted_iota(I32, (1, kw), 1)
            lane_n = lax.broadcasted_iota(I32, (1, PAGE), 1)
            kin = jnp.concatenate([kin_ref[...], jnp.zeros((PAGE - S_PAD, IDX_DIM), BF16)], axis=0)
            scn = _fold_heads(jnp.maximum(_dot_nt(qp, kin), 0.0) * wc, nq)
            scn_ref[...] = jnp.where(lane_n <= qrow, scn, NEG_INF)

            def count(pred):
                ones = _tree(jnp.add, [jnp.where(pred(sc_ref[t], t * kw + lane), 1.0, 0.0) for t in range(steps)])
                cn = jnp.where(pred(scn_ref[...], n_past + lane_n), 1.0, 0.0)
                return (jnp.sum(_lane_fold(jnp.add, ones), axis=-1, keepdims=True)
                        + jnp.sum(cn, axis=-1, keepdims=True))

            amax = jnp.maximum(
                jnp.max(_lane_fold(jnp.maximum, _tree(jnp.maximum, [jnp.abs(sc_ref[t]) for t in range(steps)])),
                        axis=-1, keepdims=True),
                jnp.max(jnp.abs(scn), axis=-1, keepdims=True))
            n_valid = (n_past + 1 + qrow).astype(F32)
            thr, jsel = _select_params(count, topk, pos_bits, amax, n_valid, t_ref, j_ref)
            for t in range(steps):
                sp = sc_ref[t]
                bias_ref[t] = jnp.where((sp > thr) | ((sp == thr) & (t * kw + lane <= jsel)), 0.0, NEG_INF)
            sn = scn_ref[...]
            seln = ((sn > thr) | ((sn == thr) & (n_past + lane_n <= jsel))) & (lane_n <= qrow)
            biasn_ref[...] = jnp.where(seln, 0.0, NEG_INF)

    @pl.when((s_idx >= steps) & (s_idx < 2 * steps))
    def _():
        t = s_idx - steps
        load_pages()
        bias = dup(bias_ref[t])

        @pl.when(t == 0)
        def _():
            m_ref[...] = jnp.full(m_ref.shape, NEG_INF, F32)

        for g in range(A_KV_HEADS):
            s = _dot_nt(qs_ref[g], buf_ref[:, gsl(g)]) + bias
            s_ref[g, t] = s
            m_ref[g] = jnp.maximum(m_ref[g], _lane_fold(jnp.maximum, s))

        @pl.when(t == steps - 1)
        def _():
            bn = dup(biasn_ref[:, :S_PAD])
            for g in range(A_KV_HEADS):
                sn = _dot_nt(qs_ref[g], kn_ref[:, gsl(g)]) + bn
                sn_ref[g] = sn
                mfin_ref[g] = jnp.maximum(jnp.max(m_ref[g], axis=-1, keepdims=True),
                                          jnp.max(sn, axis=-1, keepdims=True))

    @pl.when(s_idx >= 2 * steps)
    def _():
        t = s_idx - 2 * steps
        load_pages()

        @pl.when(t == 0)
        def _():
            l_ref[...] = jnp.zeros(l_ref.shape, F32)
            acc_ref[...] = jnp.zeros(acc_ref.shape, F32)

        for g in range(A_KV_HEADS):
            p = jnp.exp2(s_ref[g, t] - mfin_ref[g])
            l_ref[g] += _lane_fold(jnp.add, p)
            acc_ref[g] += _dot(p.astype(BF16), buf_ref[:, gsl(g)])

        @pl.when(t == steps - 1)
        def _():
            for g in range(A_KV_HEADS):
                pn = jnp.exp2(sn_ref[g] - mfin_ref[g])
                l = jnp.sum(l_ref[g], axis=-1, keepdims=True) + jnp.sum(pn, axis=-1, keepdims=True)
                o_ref[g] = (acc_ref[g] + _dot(pn.astype(BF16), vn_ref[:, gsl(g)])) / l


def _dsa_sample(page_flat, qp, wcol, ki_new_b, qs, k_new_b, v_new_b, cache_kidx_t, cache_k, cache_v,
                nseq, npages, pages, nq, topk):
    steps = npages // pages
    n_past = npages * PAGE
    kw = pages * PAGE
    rows = IDX_HEADS * nq
    rows_q = qs.shape[2]
    pos_bits = int(np.ceil(np.log2(n_past + PAGE)))

    seq3 = lambda shape: pl.BlockSpec((None,) + shape, lambda b, s, pt: (b,) + (0,) * len(shape))
    new_rows = lambda w: pl.BlockSpec((S_PAD, w), lambda b, s, pt: (b, 0))
    in_hbm = pl.BlockSpec(memory_space=pl.ANY)
    grid_spec = pltpu.PrefetchScalarGridSpec(
        num_scalar_prefetch=1,
        grid=(nseq, 3 * steps),
        in_specs=[seq3((rows, IDX_DIM)), seq3((rows, 1)), new_rows(IDX_DIM),
                  seq3((A_KV_HEADS, rows_q, A_HEAD_DIM)), new_rows(A_KV_WIDTH), new_rows(A_KV_WIDTH),
                  in_hbm, in_hbm, in_hbm],
        out_specs=seq3((A_KV_HEADS, rows_q, A_HEAD_DIM)),
        scratch_shapes=[pltpu.VMEM((steps, 8, kw), F32), pltpu.VMEM((8, PAGE), F32),
                        pltpu.VMEM((steps, 8, kw), F32), pltpu.VMEM((8, PAGE), F32),
                        pltpu.VMEM((8, 1), F32), pltpu.VMEM((8, 1), I32),
                        pltpu.VMEM((A_KV_HEADS, steps, rows_q, kw), F32),
                        pltpu.VMEM((A_KV_HEADS, rows_q, S_PAD), F32),
                        pltpu.VMEM((A_KV_HEADS, rows_q, 128), F32),
                        pltpu.VMEM((A_KV_HEADS, rows_q, 1), F32),
                        pltpu.VMEM((A_KV_HEADS, rows_q, 128), F32),
                        pltpu.VMEM((A_KV_HEADS, rows_q, A_HEAD_DIM), F32),
                        pltpu.VMEM((kw, A_KV_WIDTH), BF16),
                        pltpu.VMEM((2, pages, IDX_DIM, PAGE), F32),
                        pltpu.VMEM((2, pages, PAGE * A_KV_HEADS, A_HEAD_DIM), F32),
                        pltpu.SemaphoreType.DMA((2,)), pltpu.SemaphoreType.DMA((2,))],
    )
    return pl.pallas_call(
        functools.partial(_dsa_sample_kernel, pages=pages, npages=npages, nq=nq, topk=topk, pos_bits=pos_bits),
        grid_spec=grid_spec,
        out_shape=jax.ShapeDtypeStruct((nseq, A_KV_HEADS, rows_q, A_HEAD_DIM), F32),
        compiler_params=_cparams(("arbitrary", "arbitrary")),
        name="dsa_sample",
    )(page_flat, qp, wcol, ki_new_b, qs, k_new_b, v_new_b, cache_kidx_t, cache_k, cache_v)


def _merge_kernel(x_ref, oa_ref, ag_ref, bu_ref, bv_ref, bg_ref, cq_ref, cg_ref, ra_ref, rb_ref, rc_ref,
                  mk_ref, mv_ref, ws_ref, bs_ref, gsgu_ref, gmq_ref, wpa_ref, wpb_ref, wpc_ref, wout_ref,
                  y_ref, *maybe_vn_ref, tm, chunk, mem_groups):
    f32 = lambda r: r[...].astype(F32)
    silu = lambda t: t * jax.nn.sigmoid(t)

    vn = _rms(f32(bv_ref), gsgu_ref[...])
    if maybe_vn_ref:
        maybe_vn_ref[0][...] = vn
    vnb = vn.astype(BF16)
    bu = f32(bu_ref)
    tril = (lax.broadcasted_iota(I32, (chunk, chunk), 1) <= lax.broadcasted_iota(I32, (chunk, chunk), 0))
    ob_cols = []
    for g in range(B_GROUPS):
        wg = jnp.where(tril, ws_ref[g], 0.0).astype(BF16)
        gsl = slice(g * B_GROUP_DIM, (g + 1) * B_GROUP_DIM)
        parts = [_dot(wg, vnb[c * chunk:(c + 1) * chunk, gsl]) + bs_ref[:, g:g + 1] for c in range(tm // chunk)]
        ob_cols.append(parts[0] if len(parts) == 1 else jnp.concatenate(parts, axis=0))
    ob = bu * jnp.concatenate(ob_cols, axis=1)
    pb = _dot((ob * silu(f32(bg_ref))).astype(BF16), wpb_ref[...])

    cq = f32(cq_ref)
    rows_g = tm // mem_groups
    oc_cols = []
    for hh in range(M_HEADS):
        hsl = slice(hh * M_HEAD_DIM, (hh + 1) * M_HEAD_DIM)
        qn = (_rms(cq[:, hsl], gmq_ref[...]) * (M_HEAD_DIM ** -0.5)).astype(BF16)
        oc_rows = []
        for u in range(mem_groups):
            msl = slice(u * N_MEM, (u + 1) * N_MEM)
            s = _dot_nt(qn[u * rows_g:(u + 1) * rows_g], mk_ref[msl, hsl])
            p = jnp.exp(s - jnp.max(s, axis=-1, keepdims=True))
            oc_rows.append(_dot(p.astype(BF16), mv_ref[msl, hsl]) / jnp.sum(p, axis=-1, keepdims=True))
        oc_cols.append(oc_rows[0] if mem_groups == 1 else jnp.concatenate(oc_rows, axis=0))
    oc = jnp.concatenate(oc_cols, axis=1)
    pc = _dot((oc * silu(f32(cg_ref))).astype(BF16), wpc_ref[...])

    pa = _dot((f32(oa_ref) * silu(f32(ag_ref))).astype(BF16), wpa_ref[...])
    sig = jax.nn.sigmoid
    m = sig(f32(ra_ref)) * pa + sig(f32(rb_ref)) * pb + sig(f32(rc_ref)) * pc
    y_ref[...] = x_ref[...] + _dot(m.astype(BF16), wout_ref[...])


def _merge(x, oa, zr, mk_b, mv_b, ws, bs_t, g_sgu, g_mq, w_pa, w_pb, w_pc, w_out, tm, chunk, mem_groups, mem_map,
           emit_vn):
    n = x.shape[0]
    col = lambda w, j: pl.BlockSpec((tm, w), lambda i, j=j: (i, j))
    mem = pl.BlockSpec((mem_groups * N_MEM, M_WIDTH), mem_map)
    in_specs = [col(D_MODEL, 0), col(A_WIDTH, 0),
                col(1024, 0), col(1024, 1), col(1024, 2), col(1024, 3), col(1024, 4), col(1024, 5),
                col(2048, 3), col(2048, 4), col(2048, 5),
                mem, mem,
                _const_spec((B_GROUPS, chunk, chunk)), _const_spec((chunk, B_GROUPS)),
                _const_spec((1, B_WIDTH)), _const_spec((1, M_HEAD_DIM)),
                _const_spec((A_WIDTH, D_MODEL)), _const_spec((B_WIDTH, D_MODEL)),
                _const_spec((M_WIDTH, D_MODEL)), _const_spec((D_MODEL, D_MODEL))]
    out_specs = [col(D_MODEL, 0)]
    out_shape = [jax.ShapeDtypeStruct((n, D_MODEL), F32)]
    if emit_vn:
        out_specs.append(col(B_WIDTH, 0))
        out_shape.append(jax.ShapeDtypeStruct((n, B_WIDTH), F32))
    return pl.pallas_call(
        functools.partial(_merge_kernel, tm=tm, chunk=chunk, mem_groups=mem_groups),
        grid=(n // tm,),
        in_specs=in_specs,
        out_specs=out_specs,
        out_shape=out_shape,
        compiler_params=_cparams(("arbitrary",)),
        name="merge",
    )(x, oa, *([zr] * 9), mk_b, mv_b, ws, bs_t, g_sgu, g_mq, w_pa, w_pb, w_pc, w_out)


def _rope_tables(pos):
    pos = pos.astype(F32)[:, None]

    def cs(half):
        freq = ROPE_THETA ** (-jnp.arange(half, dtype=F32) / half)
        ang = pos * freq[None, :]
        return jnp.cos(ang), jnp.sin(ang)

    c, s = cs(A_HEAD_DIM // 2)
    ci, si = cs(IDX_DIM // 2)
    z = jnp.zeros_like(si)
    return (jnp.concatenate([c, c], axis=1), jnp.concatenate([-s, s], axis=1),
            jnp.concatenate([ci] * 4, axis=1), jnp.concatenate([-si, z, -si, z], axis=1),
            jnp.concatenate([z, si, z, si], axis=1))


def kernel(x_prompt, x_sample, cache_k, cache_v, cache_kidx, cache_mem_k, cache_mem_v, page_table,
           mem_prompt, g_pre, w_in, g_q, g_k, g_mq, g_mk, g_mem, w_mem_kv, g_sgu, w_s, b_s,
           w_pa, w_pb, w_pc, w_out):
    batch, seq, _ = x_prompt.shape
    nseq, nq, _ = x_sample.shape
    assert nq == 4 and nq <= S_PAD
    npages = page_table.shape[1]
    n_past = npages * PAGE
    n_pool = cache_k.shape[0]
    row2 = lambda a: a.reshape(1, -1)

    w_in_t = w_in.T
    w_a = w_in_t[:A_COLS].astype(BF16)
    w_pa_b, w_pb_b, w_pc_b, w_out_b = (w.astype(BF16) for w in (w_pa, w_pb, w_pc, w_out))
    w_mem_b = w_mem_kv.astype(BF16)
    g_pre2, g_q2, g_k2, g_mq2, g_mk2, g_mem2, g_sgu2 = map(row2, (g_pre, g_q, g_k, g_mq, g_mk, g_mem, g_sgu))

    xp = x_prompt.reshape(batch * seq, D_MODEL)
    tm_a = min(256, seq)
    nblk = seq // tm_a
    tabs_p = _rope_tables(jnp.arange(seq))
    q, k_p, v_p, k_b, v_b, qi, ki_p, ki_b, wi = _proj_a(
        xp, g_pre2, w_a, g_q2, g_k2, tabs_p, tm_a, lambda i: (i % nblk, 0))
    xs = jnp.pad(x_sample, ((0, 0), (0, S_PAD - nq), (0, 0))).reshape(nseq * S_PAD, D_MODEL)
    rows_s = nseq * S_PAD
    zr, zr_s = _proj_rest(xp, xs, g_pre2, w_in_t, min(1024, seq), PROJ_TN)
    mk_p, mv_p, mk_b, mv_b = _mem_kv(mem_prompt.reshape(batch * N_MEM, D_MODEL), g_mem2, w_mem_b, g_mk2)
    oa = _dsa_prompt(qi, wi.T, q, ki_b, k_b, v_b, batch, seq, min(TOPK_MAX, seq // 4))
    tm_m = min(256, seq)
    nblk_m = seq // tm_m
    (y_p,) = _merge(xp, oa, zr, mk_b, mv_b, w_s, b_s.T, g_sgu2, g_mq2, w_pa_b, w_pb_b, w_pc_b, w_out_b,
                    tm_m, CHUNK, 1, lambda i: (i // nblk_m, 0), False)

    tabs_s = tuple(jnp.tile(t, (nseq, 1)) for t in _rope_tables(n_past + jnp.arange(S_PAD)))
    q_s, k_s, v_s, k_sb, v_sb, qi_s, ki_s, ki_sb, wi_s = _proj_a(
        xs, g_pre2, w_a, g_q2, g_k2, tabs_s, rows_s, lambda i: (0, 0))

    qp = (qi_s.reshape(nseq, S_PAD, IDX_HEADS, IDX_DIM)[:, :nq]
          .transpose(0, 2, 1, 3).reshape(nseq, IDX_HEADS * nq, IDX_DIM))
    wcol = wi_s.reshape(nseq, S_PAD, IDX_HEADS)[:, :nq].transpose(0, 2, 1).reshape(nseq, IDX_HEADS * nq, 1)
    rep = A_HEADS // A_KV_HEADS
    qs = (q_s.reshape(nseq, S_PAD, A_KV_HEADS, rep, A_HEAD_DIM)[:, :nq]
          .transpose(0, 2, 3, 1, 4).reshape(nseq, A_KV_HEADS, rep * nq, A_HEAD_DIM))
    qs = jnp.concatenate([qs, qs], axis=2)
    page_flat = page_table.reshape(-1)
    pages = min(SAMPLE_PAGES, npages)
    o_s = _dsa_sample(page_flat, qp, wcol, ki_sb, qs, k_sb, v_sb, jnp.swapaxes(cache_kidx, 1, 2),
                      cache_k.reshape(n_pool, PAGE * A_KV_HEADS, A_HEAD_DIM),
                      cache_v.reshape(n_pool, PAGE * A_KV_HEADS, A_HEAD_DIM),
                      nseq, npages, pages, nq, min(TOPK_MAX, (n_past + nq) // 4))
    oa_s = (o_s[:, :, :rep * nq].reshape(nseq, A_KV_HEADS, rep, nq, A_HEAD_DIM)
            .transpose(0, 3, 1, 2, 4).reshape(nseq, nq, A_WIDTH))
    oa_s = jnp.pad(oa_s, ((0, 0), (0, S_PAD - nq), (0, 0))).reshape(rows_s, A_WIDTH).astype(BF16)
    mk_s = cache_mem_k.reshape(nseq * N_MEM, M_WIDTH).astype(BF16)
    mv_s = cache_mem_v.reshape(nseq * N_MEM, M_WIDTH).astype(BF16)
    y_s, vn_s = _merge(xs, oa_s, zr_s, mk_s, mv_s, w_s[:, :S_PAD, :S_PAD], b_s[:, :S_PAD].T, g_sgu2, g_mq2,
                       w_pa_b, w_pb_b, w_pc_b, w_out_b, rows_s, S_PAD, nseq, lambda i: (0, 0), True)

    take = lambda a, shape: a.reshape(nseq, S_PAD, -1)[:, :nq].reshape(shape)
    return (y_p.reshape(batch, seq, D_MODEL),
            take(y_s, (nseq, nq, D_MODEL)),
            k_p.reshape(batch, seq, A_KV_HEADS, A_HEAD_DIM),
            v_p.reshape(batch, seq, A_KV_HEADS, A_HEAD_DIM),
            ki_p.reshape(batch, seq, IDX_DIM),
            mk_p.reshape(batch, N_MEM, M_HEADS, M_HEAD_DIM),
            mv_p.reshape(batch, N_MEM, M_HEADS, M_HEAD_DIM),
            take(k_s, (nseq, nq, A_KV_HEADS, A_HEAD_DIM)),
            take(v_s, (nseq, nq, A_KV_HEADS, A_HEAD_DIM)),
            take(ki_s, (nseq, nq, IDX_DIM)),
            take(vn_s, (nseq, nq, B_GROUPS, B_GROUP_DIM)))
```

```python
import functools

import numpy as np
import jax
import jax.numpy as jnp
from jax import lax
from jax.experimental import pallas as pl
from jax.experimental.pallas import tpu as pltpu

F32 = jnp.float32
BF16 = jnp.bfloat16
I32 = jnp.int32

D_MODEL = 2048
PAGE = 128
A_HEADS = 8
A_KV_HEADS = 4
A_HEAD_DIM = 128
A_WIDTH = A_HEADS * A_HEAD_DIM
A_KV_WIDTH = A_KV_HEADS * A_HEAD_DIM
IDX_HEADS = 16
IDX_DIM = 64
TOPK_MAX = 256
Q_BLOCK = 128
ROPE_THETA = 10000.0
CHUNK = 128
B_GROUPS = 8
B_GROUP_DIM = 128
B_WIDTH = B_GROUPS * B_GROUP_DIM
N_MEM = 256
M_HEADS = 4
M_HEAD_DIM = 256
M_WIDTH = M_HEADS * M_HEAD_DIM
EPS = 1e-6

OFF_K = A_WIDTH
OFF_V = OFF_K + A_KV_WIDTH
OFF_QI = OFF_V + A_KV_WIDTH
OFF_KI = OFF_QI + IDX_HEADS * IDX_DIM
OFF_WI = OFF_KI + IDX_DIM
OFF_REST = OFF_WI + IDX_HEADS
A_COLS = 3200
REST_COLS = A_WIDTH + 3 * B_WIDTH + 2 * M_WIDTH + 3 * D_MODEL

Q_SCALE = float(np.log2(np.e)) * A_HEAD_DIM ** -0.5
S_PAD = 16
PROJ_TN = 1024
BISECT_STEPS = 24
SAMPLE_PAGES = 16
INT_MIN = np.int32(-2 ** 31)
INT_MAX = np.int32(2 ** 31 - 1)
NEG_INF = float("-inf")

V7X_VMEM_LIMIT = 56 * 1024 * 1024


def _cparams(sem):
    return pltpu.CompilerParams(dimension_semantics=sem, vmem_limit_bytes=V7X_VMEM_LIMIT)


def _dot(a, b):
    return jnp.dot(a, b, preferred_element_type=F32)


def _dot_nt(a, b):
    return lax.dot_general(a, b, (((1,), (1,)), ((), ())), preferred_element_type=F32)


def _rms(x, g):
    return x * lax.rsqrt(jnp.mean(x * x, axis=-1, keepdims=True) + EPS) * g


def _const_spec(shape):
    nd = len(shape)
    return pl.BlockSpec(shape, lambda *_: (0,) * nd, pipeline_mode=pl.Buffered(1))


def _proj_a_kernel(x_ref, g_ref, w_ref, gq_ref, gk_ref, cq_ref, sq_ref, ci_ref, sia_ref, sib_ref,
                   q_ref, k_ref, v_ref, kb_ref, vb_ref, qi_ref, ki_ref, kib_ref, wi_ref):
    h = _rms(x_ref[...], g_ref[...]).astype(BF16)
    z = _dot_nt(h, w_ref[...])
    tm = z.shape[0]
    cq, sq = cq_ref[...], sq_ref[...]
    ci, sia, sib = ci_ref[...], sia_ref[...], sib_ref[...]

    def norm_rope(zz, g):
        n = _rms(zz, g)
        return n * cq + pltpu.roll(n, A_HEAD_DIM // 2, 1) * sq

    def rope_idx(zz):
        return zz * ci + pltpu.roll(zz, 96, 1) * sia + pltpu.roll(zz, 32, 1) * sib

    for hh in range(A_HEADS):
        sl = slice(hh * A_HEAD_DIM, (hh + 1) * A_HEAD_DIM)
        q_ref[:, sl] = (norm_rope(z[:, sl], gq_ref[...]) * Q_SCALE).astype(BF16)
    for hh in range(A_KV_HEADS):
        sl = slice(hh * A_HEAD_DIM, (hh + 1) * A_HEAD_DIM)
        kh = norm_rope(z[:, OFF_K + hh * A_HEAD_DIM:OFF_K + (hh + 1) * A_HEAD_DIM], gk_ref[...])
        vh = z[:, OFF_V + hh * A_HEAD_DIM:OFF_V + (hh + 1) * A_HEAD_DIM]
        head_rows = pl.ds(hh, tm, stride=A_KV_HEADS)
        k_ref[head_rows, :] = kh
        v_ref[head_rows, :] = vh
        kb_ref[:, sl] = kh.astype(BF16)
        vb_ref[:, sl] = vh.astype(BF16)
    for t in range(IDX_HEADS * IDX_DIM // 128):
        sl = slice(t * 128, (t + 1) * 128)
        qi_ref[:, sl] = rope_idx(z[:, OFF_QI + t * 128:OFF_QI + (t + 1) * 128]).astype(BF16)
    last = z[:, OFF_KI:OFF_KI + 128]
    ki = rope_idx(last)[:, :IDX_DIM]
    ki_ref[...] = ki
    kib_ref[...] = ki.astype(BF16)
    wi_ref[...] = last[:, IDX_DIM:IDX_DIM + IDX_HEADS] * ((IDX_HEADS ** -0.5) * (IDX_DIM ** -0.5))


def _proj_a(x, g_pre, w_a, g_q, g_k, tabs, tm, tab_map):
    n = x.shape[0]
    row = lambda w: pl.BlockSpec((tm, w), lambda i: (i, 0))
    tab = pl.BlockSpec((tm, 128), tab_map)
    outs = [(1, A_WIDTH, BF16), (A_KV_HEADS, A_HEAD_DIM, F32), (A_KV_HEADS, A_HEAD_DIM, F32),
            (1, A_KV_WIDTH, BF16), (1, A_KV_WIDTH, BF16),
            (1, IDX_HEADS * IDX_DIM, BF16), (1, IDX_DIM, F32), (1, IDX_DIM, BF16), (1, IDX_HEADS, F32)]
    return pl.pallas_call(
        _proj_a_kernel,
        grid=(n // tm,),
        in_specs=[row(D_MODEL), _const_spec((1, D_MODEL)), _const_spec((A_COLS, D_MODEL)),
                  _const_spec((1, A_HEAD_DIM)), _const_spec((1, A_HEAD_DIM)), tab, tab, tab, tab, tab],
        out_specs=[pl.BlockSpec((tm * r, w), lambda i: (i, 0)) for r, w, _ in outs],
        out_shape=[jax.ShapeDtypeStruct((n * r, w), dt) for r, w, dt in outs],
        compiler_params=_cparams(("arbitrary",)),
        name="proj_a",
    )(x, g_pre, w_a, g_q, g_k, *tabs)


def _proj_rest_kernel(x_ref, xs_ref, g_ref, w_ref, o_ref, os_ref, h_ref, hs_ref):
    i, j = pl.program_id(0), pl.program_id(1)

    @pl.when(j == 0)
    def _():
        h_ref[...] = _rms(x_ref[...], g_ref[...]).astype(BF16)

    @pl.when((i == 0) & (j == 0))
    def _():
        hs_ref[...] = _rms(xs_ref[...], g_ref[...]).astype(BF16)

    w = w_ref[...].astype(BF16)
    o_ref[...] = _dot_nt(h_ref[...], w).astype(BF16)

    @pl.when(i == 0)
    def _():
        os_ref[...] = _dot_nt(hs_ref[...], w).astype(BF16)


def _proj_rest(x, xs, g_pre, w_in_t, tm, tn):
    n, ns = x.shape[0], xs.shape[0]
    ncols = REST_COLS // tn
    xs_cols = lambda i, j: (0, jnp.where(i == 0, j, ncols - 1))
    return pl.pallas_call(
        _proj_rest_kernel,
        grid=(n // tm, ncols),
        in_specs=[pl.BlockSpec((tm, D_MODEL), lambda i, j: (i, 0)),
                  pl.BlockSpec((ns, D_MODEL), lambda i, j: (0, 0), pipeline_mode=pl.Buffered(1)),
                  pl.BlockSpec((1, D_MODEL), lambda i, j: (0, 0)),
                  pl.BlockSpec((pl.Element(tn), pl.Element(D_MODEL)),
                               lambda i, j: (pl.multiple_of(OFF_REST + j * tn, 16), 0))],
        out_specs=[pl.BlockSpec((tm, tn), lambda i, j: (i, j)), pl.BlockSpec((ns, tn), xs_cols)],
        out_shape=[jax.ShapeDtypeStruct((n, REST_COLS), BF16), jax.ShapeDtypeStruct((ns, REST_COLS), BF16)],
        scratch_shapes=[pltpu.VMEM((tm, D_MODEL), BF16), pltpu.VMEM((ns, D_MODEL), BF16)],
        compiler_params=_cparams(("arbitrary", "arbitrary")),
        name="proj_rest",
    )(x, xs, g_pre, w_in_t)


def _mem_kv_kernel(x_ref, g_ref, w_ref, gk_ref, k_ref, v_ref, kb_ref, vb_ref):
    h = _rms(x_ref[...], g_ref[...]).astype(BF16)
    z = _dot(h, w_ref[...])
    for hh in range(M_HEADS):
        sl = slice(hh * M_HEAD_DIM, (hh + 1) * M_HEAD_DIM)
        kh = _rms(z[:, sl], gk_ref[...])
        k_ref[:, sl] = kh
        kb_ref[:, sl] = kh.astype(BF16)
    v = z[:, M_WIDTH:]
    v_ref[...] = v
    vb_ref[...] = v.astype(BF16)


def _mem_kv(mem, g_mem, w_mem, g_mk):
    n = mem.shape[0]
    blk = pl.BlockSpec((N_MEM, M_WIDTH), lambda i: (i, 0))
    return pl.pallas_call(
        _mem_kv_kernel,
        grid=(n // N_MEM,),
        in_specs=[pl.BlockSpec((N_MEM, D_MODEL), lambda i: (i, 0)), _const_spec((1, D_MODEL)),
                  _const_spec((D_MODEL, 2 * M_WIDTH)), _const_spec((1, M_HEAD_DIM))],
        out_specs=[blk, blk, blk, blk],
        out_shape=[jax.ShapeDtypeStruct((n, M_WIDTH), dt) for dt in (F32, F32, BF16, BF16)],
        compiler_params=_cparams(("arbitrary",)),
        name="mem_kv",
    )(mem, g_mem, w_mem, g_mk)


KEY_NEG_INF = np.int32(-0x7F800000)


def _key_to_f32(key):
    return pltpu.bitcast(jnp.where(key >= 0, key, INT_MIN - key), F32)


def _select_params(count, topk, pos_bits, amax, n_valid, t_ref, j_ref):
    kf = float(topk)
    hi0 = amax * 1.000001 + 1e-30
    all_selected = (n_valid <= kf).astype(I32)

    def bisect_body(_, state):
        lo, hi, t, done = state
        mid = 0.5 * lo + 0.5 * hi
        n_ge = count(lambda s, p: s >= mid)
        hit = (n_ge == kf) & (done == 0)
        return (jnp.where(n_ge >= kf, mid, lo), jnp.where(n_ge >= kf, hi, mid),
                jnp.where(hit, mid, t), jnp.where(hit, 1, done))

    _, _, t_bis, done = lax.fori_loop(
        0, BISECT_STEPS, bisect_body, (-hi0, hi0, jnp.full(hi0.shape, NEG_INF, F32), all_selected))
    t_ref[...] = t_bis
    j_ref[...] = jnp.full(j_ref.shape, INT_MAX, I32)

    @pl.when(jnp.min(done) == 0)
    def _():
        t0 = jnp.where(count(lambda s, p: s >= 0.0) >= kf, jnp.int32(0), INT_MIN)

        def bit_body(b, t):
            cand = t + lax.shift_left(jnp.int32(1), 30 - b)
            cand_f = _key_to_f32(cand)
            return jnp.where(count(lambda s, p: s >= cand_f) >= kf, cand, t)

        t = _key_to_f32(jnp.maximum(lax.fori_loop(0, 31, bit_body, t0), KEY_NEG_INF))
        t_ref[...] = t
        tie = (count(lambda s, p: s >= t) > kf) & (t > NEG_INF)

        @pl.when(jnp.max(tie.astype(I32)) > 0)
        def _():
            n_gt = count(lambda s, p: s > t)

            def pos_body(b, p_lo):
                cand = p_lo + lax.shift_left(jnp.int32(1), pos_bits - 1 - b)
                n_eq = count(lambda s, p: (s == t) & (p < cand))
                return jnp.where(n_gt + n_eq < kf, cand, p_lo)

            p_sel = lax.fori_loop(0, pos_bits, pos_body, jnp.zeros(t.shape, I32))
            j_ref[...] = jnp.where(tie, p_sel, INT_MAX)

    return t_ref[...], j_ref[...]


def _fori_by_two(n, body, init):
    carry = lax.fori_loop(0, n // 2, lambda j, c: body(2 * j + 1, body(2 * j, c)), init)
    return lax.cond(n % 2 == 1, lambda c: body(n - 1, c), lambda c: c, carry)


def _dsa_prompt_kernel(qi_ref, wit_ref, q_ref, ki_ref, k_ref, v_ref, o_ref,
                       sc_ref, bias_ref, vt_ref, t_ref, j_ref, s_ref, acc_ref, *, topk, kc, pos_bits):
    i = pl.program_id(1)
    nck = (i * Q_BLOCK + Q_BLOCK + kc - 1) // kc
    nchunks = vt_ref.shape[0]
    rep = A_HEADS // A_KV_HEADS
    q_pos = i * Q_BLOCK + lax.broadcasted_iota(I32, (1, Q_BLOCK), 1)
    sub = lax.broadcasted_iota(I32, (kc, 1), 0)

    @pl.when(i == 0)
    def _():
        for c in range(nchunks):
            for g in range(A_KV_HEADS):
                gsl = slice(g * A_HEAD_DIM, (g + 1) * A_HEAD_DIM)
                vt_ref[c, gsl, :] = v_ref[c * kc:(c + 1) * kc, gsl].astype(F32).T.astype(BF16)

    def chunk_rows(c):
        return pl.ds(pl.multiple_of(c * kc, kc), kc)

    qi = qi_ref[...]
    wit = wit_ref[...]
    qi_pairs = [jnp.concatenate([qi[:, (2 * j) * IDX_DIM:(2 * j + 1) * IDX_DIM],
                                 qi[:, (2 * j + 1) * IDX_DIM:(2 * j + 2) * IDX_DIM]], axis=0)
                for j in range(IDX_HEADS // 2)]

    def score_body(c, amax):
        kic = ki_ref[chunk_rows(c), :]
        acc = jnp.zeros((kc, Q_BLOCK), F32)
        for j in range(IDX_HEADS // 2):
            d = _dot_nt(kic, qi_pairs[j])
            acc = acc + jnp.maximum(d[:, :Q_BLOCK], 0.0) * wit[2 * j:2 * j + 1, :]
            acc = acc + jnp.maximum(d[:, Q_BLOCK:], 0.0) * wit[2 * j + 1:2 * j + 2, :]
        sc_ref[c] = jnp.where(c * kc + sub <= q_pos, acc, NEG_INF)
        return jnp.maximum(amax, jnp.max(jnp.abs(acc).reshape(kc // 64, 64, Q_BLOCK), axis=0))

    amax = jnp.max(_fori_by_two(nck, score_body, jnp.zeros((64, Q_BLOCK), F32)), axis=0, keepdims=True)

    def count(pred):
        def body(c, acc):
            part = jnp.where(pred(sc_ref[c], c * kc + sub), 1.0, 0.0)
            return acc + jnp.sum(part.reshape(kc // 64, 64, Q_BLOCK), axis=0)

        acc = lax.fori_loop(0, nck, body, jnp.zeros((64, Q_BLOCK), F32))
        return jnp.sum(acc, axis=0, keepdims=True)

    thr, jsel = _select_params(count, topk, pos_bits, amax, (q_pos + 1).astype(F32), t_ref, j_ref)

    def bias_body(c, carry):
        s = sc_ref[c]
        pos = c * kc + sub
        sel = ((s > thr) | ((s == thr) & (pos <= jsel))) & (pos <= q_pos)
        bias_ref[c] = jnp.where(sel, 0.0, NEG_INF)
        return carry

    lax.fori_loop(0, nck, bias_body, 0)

    gsl = lambda g: slice(g * A_HEAD_DIM, (g + 1) * A_HEAD_DIM)
    row0 = lambda v: jnp.full((1, rep * Q_BLOCK), v, F32)
    groups = tuple(range(A_KV_HEADS))
    qgs = [jnp.concatenate([q_ref[:, gsl(g * rep + r)] for r in range(rep)], axis=0) for g in groups]

    def write_out(ls):
        for g in groups:
            o = acc_ref[g] / ls[g]
            for r in range(rep):
                o_ref[:, gsl(g * rep + r)] = o[:, r * Q_BLOCK:(r + 1) * Q_BLOCK].T.astype(BF16)

    def qk_body(c, ms):
        b = bias_ref[c]
        bias = jnp.concatenate([b] * rep, axis=1)
        out = []
        for g in groups:
            s = _dot_nt(k_ref[chunk_rows(c), gsl(g)], qgs[g]) + bias
            s_ref[c, g] = s
            out.append(jnp.maximum(ms[g], jnp.max(s, axis=0, keepdims=True)))
        return tuple(out)

    ms = _fori_by_two(nck, qk_body, (row0(NEG_INF),) * len(groups))
    acc_ref[...] = jnp.zeros(acc_ref.shape, F32)

    def pv_body(c, ls):
        out = []
        for g in groups:
            p = jnp.exp2(s_ref[c, g] - ms[g])
            out.append(ls[g] + jnp.sum(p, axis=0, keepdims=True))
            acc_ref[g] += _dot(vt_ref[c, gsl(g), :], p.astype(BF16))
        return tuple(out)

    write_out(_fori_by_two(nck, pv_body, (row0(0.0),) * len(groups)))


def _dsa_prompt(qi, wit, q, ki_b, k_b, v_b, batch, seq, topk):
    nqb = seq // Q_BLOCK
    rep = A_HEADS // A_KV_HEADS
    kc = min(512, seq)
    nchunks = seq // kc
    pos_bits = max(1, int(np.ceil(np.log2(seq))))
    qrow = lambda w: pl.BlockSpec((Q_BLOCK, w), lambda b, i: (b * nqb + i, 0))
    seqblk = lambda w: pl.BlockSpec((seq, w), lambda b, i: (b, 0), pipeline_mode=pl.Buffered(1))
    return pl.pallas_call(
        functools.partial(_dsa_prompt_kernel, topk=topk, kc=kc, pos_bits=pos_bits),
        grid=(batch, nqb),
        in_specs=[qrow(IDX_HEADS * IDX_DIM), pl.BlockSpec((IDX_HEADS, Q_BLOCK), lambda b, i: (0, b * nqb + i)),
                  qrow(A_WIDTH), seqblk(IDX_DIM), seqblk(A_KV_WIDTH), seqblk(A_KV_WIDTH)],
        out_specs=qrow(A_WIDTH),
        out_shape=jax.ShapeDtypeStruct((batch * seq, A_WIDTH), BF16),
        scratch_shapes=[pltpu.VMEM((nchunks, kc, Q_BLOCK), F32), pltpu.VMEM((nchunks, kc, Q_BLOCK), F32),
                        pltpu.VMEM((nchunks, A_KV_WIDTH, kc), BF16),
                        pltpu.VMEM((1, Q_BLOCK), F32), pltpu.VMEM((1, Q_BLOCK), I32),
                        pltpu.VMEM((nchunks, A_KV_HEADS, kc, rep * Q_BLOCK), F32),
                        pltpu.VMEM((A_KV_HEADS, A_HEAD_DIM, rep * Q_BLOCK), F32)],
        compiler_params=_cparams(("arbitrary", "arbitrary")),
        name="dsa_prompt",
    )(qi, wit, q, ki_b, k_b, v_b)


def _fold_heads(e, nq):
    acc = e[0:8]
    for t in range(1, e.shape[0] // 8):
        acc = acc + e[8 * t:8 * (t + 1)]
    return acc + pltpu.roll(acc, nq, 0)


def _tree(op, parts):
    parts = list(parts)
    while len(parts) > 1:
        parts = [op(a, b) for a, b in zip(parts[::2], parts[1::2])] + parts[len(parts) & ~1:]
    return parts[0]


def _lane_fold(op, x):
    return _tree(op, [x[:, t * 128:(t + 1) * 128] for t in range(x.shape[1] // 128)])


def _dsa_sample_kernel(pt_ref, qp_ref, wc_ref, kin_ref, qs_ref, kn_ref, vn_ref, kidx_hbm, k_hbm, v_hbm, o_ref,
                       sc_ref, scn_ref, bias_ref, biasn_ref, t_ref, j_ref, s_ref, sn_ref, m_ref, mfin_ref, l_ref, acc_ref,
                       idx_raw, kv_raw, idx_sem, kv_sem, *, pages, npages, nq, topk, pos_bits):
    b_idx, s_idx = pl.program_id(0), pl.program_id(1)
    kw = pages * PAGE
    steps = npages // pages
    n_past = npages * PAGE
    rows_q = qs_ref.shape[1]
    dup = lambda x8: jnp.concatenate([x8] * (rows_q // 8), axis=0)
    gsl = lambda g: slice(g * A_HEAD_DIM, (g + 1) * A_HEAD_DIM)

    n = b_idx * (3 * steps) + s_idx
    slot = n % 2

    def block_copies(seq, step, sl, start):
        for phase, (src, dst, sem) in enumerate(((kidx_hbm, idx_raw, idx_sem), (k_hbm, kv_raw, kv_sem),
                                                 (v_hbm, kv_raw, kv_sem))):
            @pl.when((step >= phase * steps) & (step < (phase + 1) * steps))
            def _(phase=phase, src=src, dst=dst, sem=sem):
                for p in range(pages):
                    page = pt_ref[seq * npages + (step - phase * steps) * pages + p] if start else 0
                    copy = pltpu.make_async_copy(src.at[page], dst.at[sl, p], sem.at[sl])
                    copy.start() if start else copy.wait()

    @pl.when(n == 0)
    def _():
        block_copies(b_idx, s_idx, slot, True)

    @pl.when(n + 1 < pl.num_programs(0) * 3 * steps)
    def _():
        wrap = s_idx + 1 == 3 * steps
        block_copies(jnp.where(wrap, b_idx + 1, b_idx), jnp.where(wrap, 0, s_idx + 1), 1 - slot, True)

    block_copies(b_idx, s_idx, slot, False)

    def head_rows(g):
        rows = pl.ds(g, PAGE, stride=A_KV_HEADS)
        return jnp.concatenate([kv_raw[slot, p, rows, :] for p in range(pages)], axis=0).astype(BF16)

    @pl.when(s_idx < steps)
    def _():
        qp = qp_ref[...]
        wc = wc_ref[...]
        kpt = jnp.concatenate([idx_raw[slot, p] for p in range(pages)], axis=1).astype(BF16)
        sc_ref[s_idx] = _fold_heads(jnp.maximum(_dot(qp, kpt), 0.0) * wc, nq)

        @pl.when(s_idx == steps - 1)
        def _():
            qrow = lax.broadcasted_iota(I32, (8, 1), 0) % nq
            lane = lax.broadcasted_iota(I32, (1, kw), 1)
            lane_n = lax.broadcasted_iota(I32, (1, PAGE), 1)
            kin = jnp.concatenate([kin_ref[...], jnp.zeros((PAGE - S_PAD, IDX_DIM), BF16)], axis=0)
            scn = _fold_heads(jnp.maximum(_dot_nt(qp, kin), 0.0) * wc, nq)
            scn_ref[...] = jnp.where(lane_n <= qrow, scn, NEG_INF)

            def count(pred):
                ones = _tree(jnp.add, [jnp.where(pred(sc_ref[t], t * kw + lane), 1.0, 0.0) for t in range(steps)])
                cn = jnp.where(pred(scn_ref[...], n_past + lane_n), 1.0, 0.0)
                return (jnp.sum(_lane_fold(jnp.add, ones), axis=-1, keepdims=True)
                        + jnp.sum(cn, axis=-1, keepdims=True))

            amax = jnp.maximum(
                jnp.max(_lane_fold(jnp.maximum, _tree(jnp.maximum, [jnp.abs(sc_ref[t]) for t in range(steps)])),
                        axis=-1, keepdims=True),
                jnp.max(jnp.abs(scn), axis=-1, keepdims=True))
            n_valid = (n_past + 1 + qrow).astype(F32)
            thr, jsel = _select_params(count, topk, pos_bits, amax, n_valid, t_ref, j_ref)
            for t in range(steps):
                sp = sc_ref[t]
                bias_ref[t] = jnp.where((sp > thr) | ((sp == thr) & (t * kw + lane <= jsel)), 0.0, NEG_INF)
            sn = scn_ref[...]
            seln = ((sn > thr) | ((sn == thr) & (n_past + lane_n <= jsel))) & (lane_n <= qrow)
            biasn_ref[...] = jnp.where(seln, 0.0, NEG_INF)

    @pl.when((s_idx >= steps) & (s_idx < 2 * steps))
    def _():
        t = s_idx - steps

        @pl.when(t == 0)
        def _():
            m_ref[...] = jnp.full(m_ref.shape, NEG_INF, F32)

        bias = dup(bias_ref[t])
        for g in range(A_KV_HEADS):
            s = _dot_nt(qs_ref[g], head_rows(g)) + bias
            s_ref[g, t] = s
            m_ref[g] = jnp.maximum(m_ref[g], _lane_fold(jnp.maximum, s))

        @pl.when(t == steps - 1)
        def _():
            bn = dup(biasn_ref[:, :S_PAD])
            for g in range(A_KV_HEADS):
                sn = _dot_nt(qs_ref[g], kn_ref[:, gsl(g)]) + bn
                sn_ref[g] = sn
                mfin_ref[g] = jnp.maximum(jnp.max(m_ref[g], axis=-1, keepdims=True),
                                          jnp.max(sn, axis=-1, keepdims=True))

    @pl.when(s_idx >= 2 * steps)
    def _():
        t = s_idx - 2 * steps

        @pl.when(t == 0)
        def _():
            l_ref[...] = jnp.zeros(l_ref.shape, F32)
            acc_ref[...] = jnp.zeros(acc_ref.shape, F32)

        for g in range(A_KV_HEADS):
            p = jnp.exp2(s_ref[g, t] - mfin_ref[g])
            l_ref[g] += _lane_fold(jnp.add, p)
            acc_ref[g] += _dot(p.astype(BF16), head_rows(g))

        @pl.when(t == steps - 1)
        def _():
            for g in range(A_KV_HEADS):
                pn = jnp.exp2(sn_ref[g] - mfin_ref[g])
                l = jnp.sum(l_ref[g], axis=-1, keepdims=True) + jnp.sum(pn, axis=-1, keepdims=True)
                o_ref[g] = (acc_ref[g] + _dot(pn.astype(BF16), vn_ref[:, gsl(g)])) / l


def _dsa_sample(page_flat, qp, wcol, ki_new_b, qs, k_new_b, v_new_b, cache_kidx_t, cache_k, cache_v,
                nseq, npages, pages, nq, topk):
    steps = npages // pages
    n_past = npages * PAGE
    kw = pages * PAGE
    rows = IDX_HEADS * nq
    rows_q = qs.shape[2]
    pos_bits = int(np.ceil(np.log2(n_past + PAGE)))

    seq3 = lambda shape: pl.BlockSpec((None,) + shape, lambda b, s, pt: (b,) + (0,) * len(shape))
    new_rows = lambda w: pl.BlockSpec((S_PAD, w), lambda b, s, pt: (b, 0))
    in_hbm = pl.BlockSpec(memory_space=pl.ANY)
    grid_spec = pltpu.PrefetchScalarGridSpec(
        num_scalar_prefetch=1,
        grid=(nseq, 3 * steps),
        in_specs=[seq3((rows, IDX_DIM)), seq3((rows, 1)), new_rows(IDX_DIM),
                  seq3((A_KV_HEADS, rows_q, A_HEAD_DIM)), new_rows(A_KV_WIDTH), new_rows(A_KV_WIDTH),
                  in_hbm, in_hbm, in_hbm],
        out_specs=seq3((A_KV_HEADS, rows_q, A_HEAD_DIM)),
        scratch_shapes=[pltpu.VMEM((steps, 8, kw), F32), pltpu.VMEM((8, PAGE), F32),
                        pltpu.VMEM((steps, 8, kw), F32), pltpu.VMEM((8, PAGE), F32),
                        pltpu.VMEM((8, 1), F32), pltpu.VMEM((8, 1), I32),
                        pltpu.VMEM((A_KV_HEADS, steps, rows_q, kw), F32),
                        pltpu.VMEM((A_KV_HEADS, rows_q, S_PAD), F32),
                        pltpu.VMEM((A_KV_HEADS, rows_q, 128), F32),
                        pltpu.VMEM((A_KV_HEADS, rows_q, 1), F32),
                        pltpu.VMEM((A_KV_HEADS, rows_q, 128), F32),
                        pltpu.VMEM((A_KV_HEADS, rows_q, A_HEAD_DIM), F32),
                        pltpu.VMEM((2, pages, IDX_DIM, PAGE), F32),
                        pltpu.VMEM((2, pages, PAGE * A_KV_HEADS, A_HEAD_DIM), F32),
                        pltpu.SemaphoreType.DMA((2,)), pltpu.SemaphoreType.DMA((2,))],
    )
    return pl.pallas_call(
        functools.partial(_dsa_sample_kernel, pages=pages, npages=npages, nq=nq, topk=topk, pos_bits=pos_bits),
        grid_spec=grid_spec,
        out_shape=jax.ShapeDtypeStruct((nseq, A_KV_HEADS, rows_q, A_HEAD_DIM), F32),
        compiler_params=_cparams(("arbitrary", "arbitrary")),
        name="dsa_sample",
    )(page_flat, qp, wcol, ki_new_b, qs, k_new_b, v_new_b, cache_kidx_t, cache_k, cache_v)


def _merge_kernel(x_ref, oa_ref, ag_ref, bu_ref, bv_ref, bg_ref, cq_ref, cg_ref, ra_ref, rb_ref, rc_ref,
                  mk_ref, mv_ref, ws_ref, bs_ref, gsgu_ref, gmq_ref, wpa_ref, wpb_ref, wpc_ref, wout_ref,
                  y_ref, *maybe_vn_ref, tm, chunk, mem_groups):
    f32 = lambda r: r[...].astype(F32)
    silu = lambda t: t * jax.nn.sigmoid(t)

    vn = _rms(f32(bv_ref), gsgu_ref[...])
    if maybe_vn_ref:
        maybe_vn_ref[0][...] = vn
    vnb = vn.astype(BF16)
    bu = f32(bu_ref)
    tril = (lax.broadcasted_iota(I32, (chunk, chunk), 1) <= lax.broadcasted_iota(I32, (chunk, chunk), 0))
    ob_cols = []
    for g in range(B_GROUPS):
        wg = jnp.where(tril, ws_ref[g], 0.0).astype(BF16)
        gsl = slice(g * B_GROUP_DIM, (g + 1) * B_GROUP_DIM)
        parts = [_dot(wg, vnb[c * chunk:(c + 1) * chunk, gsl]) + bs_ref[:, g:g + 1] for c in range(tm // chunk)]
        ob_cols.append(parts[0] if len(parts) == 1 else jnp.concatenate(parts, axis=0))
    ob = bu * jnp.concatenate(ob_cols, axis=1)
    pb = _dot((ob * silu(f32(bg_ref))).astype(BF16), wpb_ref[...])

    cq = f32(cq_ref)
    rows_g = tm // mem_groups
    oc_cols = []
    for hh in range(M_HEADS):
        hsl = slice(hh * M_HEAD_DIM, (hh + 1) * M_HEAD_DIM)
        qn = (_rms(cq[:, hsl], gmq_ref[...]) * (M_HEAD_DIM ** -0.5)).astype(BF16)
        oc_rows = []
        for u in range(mem_groups):
            msl = slice(u * N_MEM, (u + 1) * N_MEM)
            s = _dot_nt(qn[u * rows_g:(u + 1) * rows_g], mk_ref[msl, hsl])
            p = jnp.exp(s - jnp.max(s, axis=-1, keepdims=True))
            oc_rows.append(_dot(p.astype(BF16), mv_ref[msl, hsl]) / jnp.sum(p, axis=-1, keepdims=True))
        oc_cols.append(oc_rows[0] if mem_groups == 1 else jnp.concatenate(oc_rows, axis=0))
    oc = jnp.concatenate(oc_cols, axis=1)
    pc = _dot((oc * silu(f32(cg_ref))).astype(BF16), wpc_ref[...])

    pa = _dot((f32(oa_ref) * silu(f32(ag_ref))).astype(BF16), wpa_ref[...])
    sig = jax.nn.sigmoid
    m = sig(f32(ra_ref)) * pa + sig(f32(rb_ref)) * pb + sig(f32(rc_ref)) * pc
    y_ref[...] = x_ref[...] + _dot(m.astype(BF16), wout_ref[...])


def _merge(x, oa, zr, mk_b, mv_b, ws, bs_t, g_sgu, g_mq, w_pa, w_pb, w_pc, w_out, tm, chunk, mem_groups, mem_map,
           emit_vn):
    n = x.shape[0]
    col = lambda w, j: pl.BlockSpec((tm, w), lambda i, j=j: (i, j))
    mem = pl.BlockSpec((mem_groups * N_MEM, M_WIDTH), mem_map)
    in_specs = [col(D_MODEL, 0), col(A_WIDTH, 0),
                col(1024, 0), col(1024, 1), col(1024, 2), col(1024, 3), col(1024, 4), col(1024, 5),
                col(2048, 3), col(2048, 4), col(2048, 5),
                mem, mem,
                _const_spec((B_GROUPS, chunk, chunk)), _const_spec((chunk, B_GROUPS)),
                _const_spec((1, B_WIDTH)), _const_spec((1, M_HEAD_DIM)),
                _const_spec((A_WIDTH, D_MODEL)), _const_spec((B_WIDTH, D_MODEL)),
                _const_spec((M_WIDTH, D_MODEL)), _const_spec((D_MODEL, D_MODEL))]
    out_specs = [col(D_MODEL, 0)]
    out_shape = [jax.ShapeDtypeStruct((n, D_MODEL), F32)]
    if emit_vn:
        out_specs.append(col(B_WIDTH, 0))
        out_shape.append(jax.ShapeDtypeStruct((n, B_WIDTH), F32))
    return pl.pallas_call(
        functools.partial(_merge_kernel, tm=tm, chunk=chunk, mem_groups=mem_groups),
        grid=(n // tm,),
        in_specs=in_specs,
        out_specs=out_specs,
        out_shape=out_shape,
        compiler_params=_cparams(("arbitrary",)),
        name="merge",
    )(x, oa, *([zr] * 9), mk_b, mv_b, ws, bs_t, g_sgu, g_mq, w_pa, w_pb, w_pc, w_out)


def _rope_tables(pos):
    pos = pos.astype(F32)[:, None]

    def cs(half):
        freq = ROPE_THETA ** (-jnp.arange(half, dtype=F32) / half)
        ang = pos * freq[None, :]
        return jnp.cos(ang), jnp.sin(ang)

    c, s = cs(A_HEAD_DIM // 2)
    ci, si = cs(IDX_DIM // 2)
    z = jnp.zeros_like(si)
    return (jnp.concatenate([c, c], axis=1), jnp.concatenate([-s, s], axis=1),
            jnp.concatenate([ci] * 4, axis=1), jnp.concatenate([-si, z, -si, z], axis=1),
            jnp.concatenate([z, si, z, si], axis=1))


def kernel(x_prompt, x_sample, cache_k, cache_v, cache_kidx, cache_mem_k, cache_mem_v, page_table,
           mem_prompt, g_pre, w_in, g_q, g_k, g_mq, g_mk, g_mem, w_mem_kv, g_sgu, w_s, b_s,
           w_pa, w_pb, w_pc, w_out):
    batch, seq, _ = x_prompt.shape
    nseq, nq, _ = x_sample.shape
    assert nq == 4 and nq <= S_PAD
    npages = page_table.shape[1]
    n_past = npages * PAGE
    n_pool = cache_k.shape[0]
    row2 = lambda a: a.reshape(1, -1)

    w_in_t = w_in.T
    w_a = w_in_t[:A_COLS].astype(BF16)
    w_pa_b, w_pb_b, w_pc_b, w_out_b = (w.astype(BF16) for w in (w_pa, w_pb, w_pc, w_out))
    w_mem_b = w_mem_kv.astype(BF16)
    g_pre2, g_q2, g_k2, g_mq2, g_mk2, g_mem2, g_sgu2 = map(row2, (g_pre, g_q, g_k, g_mq, g_mk, g_mem, g_sgu))

    xp = x_prompt.reshape(batch * seq, D_MODEL)
    tm_a = min(256, seq)
    nblk = seq // tm_a
    tabs_p = _rope_tables(jnp.arange(seq))
    q, k_p, v_p, k_b, v_b, qi, ki_p, ki_b, wi = _proj_a(
        xp, g_pre2, w_a, g_q2, g_k2, tabs_p, tm_a, lambda i: (i % nblk, 0))
    xs = jnp.pad(x_sample, ((0, 0), (0, S_PAD - nq), (0, 0))).reshape(nseq * S_PAD, D_MODEL)
    rows_s = nseq * S_PAD
    zr, zr_s = _proj_rest(xp, xs, g_pre2, w_in_t, min(1024, seq), PROJ_TN)
    mk_p, mv_p, mk_b, mv_b = _mem_kv(mem_prompt.reshape(batch * N_MEM, D_MODEL), g_mem2, w_mem_b, g_mk2)
    oa = _dsa_prompt(qi, wi.T, q, ki_b, k_b, v_b, batch, seq, min(TOPK_MAX, seq // 4))
    tm_m = min(256, seq)
    nblk_m = seq // tm_m
    (y_p,) = _merge(xp, oa, zr, mk_b, mv_b, w_s, b_s.T, g_sgu2, g_mq2, w_pa_b, w_pb_b, w_pc_b, w_out_b,
                    tm_m, CHUNK, 1, lambda i: (i // nblk_m, 0), False)

    tabs_s = tuple(jnp.tile(t, (nseq, 1)) for t in _rope_tables(n_past + jnp.arange(S_PAD)))
    q_s, k_s, v_s, k_sb, v_sb, qi_s, ki_s, ki_sb, wi_s = _proj_a(
        xs, g_pre2, w_a, g_q2, g_k2, tabs_s, rows_s, lambda i: (0, 0))

    qp = (qi_s.reshape(nseq, S_PAD, IDX_HEADS, IDX_DIM)[:, :nq]
          .transpose(0, 2, 1, 3).reshape(nseq, IDX_HEADS * nq, IDX_DIM))
    wcol = wi_s.reshape(nseq, S_PAD, IDX_HEADS)[:, :nq].transpose(0, 2, 1).reshape(nseq, IDX_HEADS * nq, 1)
    rep = A_HEADS // A_KV_HEADS
    qs = (q_s.reshape(nseq, S_PAD, A_KV_HEADS, rep, A_HEAD_DIM)[:, :nq]
          .transpose(0, 2, 3, 1, 4).reshape(nseq, A_KV_HEADS, rep * nq, A_HEAD_DIM))
    qs = jnp.concatenate([qs, qs], axis=2)
    page_flat = page_table.reshape(-1)
    pages = min(SAMPLE_PAGES, npages)
    o_s = _dsa_sample(page_flat, qp, wcol, ki_sb, qs, k_sb, v_sb, jnp.swapaxes(cache_kidx, 1, 2),
                      cache_k.reshape(n_pool, PAGE * A_KV_HEADS, A_HEAD_DIM),
                      cache_v.reshape(n_pool, PAGE * A_KV_HEADS, A_HEAD_DIM),
                      nseq, npages, pages, nq, min(TOPK_MAX, (n_past + nq) // 4))
    oa_s = (o_s[:, :, :rep * nq].reshape(nseq, A_KV_HEADS, rep, nq, A_HEAD_DIM)
            .transpose(0, 3, 1, 2, 4).reshape(nseq, nq, A_WIDTH))
    oa_s = jnp.pad(oa_s, ((0, 0), (0, S_PAD - nq), (0, 0))).reshape(rows_s, A_WIDTH).astype(BF16)
    mk_s = cache_mem_k.reshape(nseq * N_MEM, M_WIDTH).astype(BF16)
    mv_s = cache_mem_v.reshape(nseq * N_MEM, M_WIDTH).astype(BF16)
    y_s, vn_s = _merge(xs, oa_s, zr_s, mk_s, mv_s, w_s[:, :S_PAD, :S_PAD], b_s[:, :S_PAD].T, g_sgu2, g_mq2,
                       w_pa_b, w_pb_b, w_pc_b, w_out_b, rows_s, S_PAD, nseq, lambda i: (0, 0), True)

    take = lambda a, shape: a.reshape(nseq, S_PAD, -1)[:, :nq].reshape(shape)
    return (y_p.reshape(batch, seq, D_MODEL),
            take(y_s, (nseq, nq, D_MODEL)),
            k_p.reshape(batch, seq, A_KV_HEADS, A_HEAD_DIM),
            v_p.reshape(batch, seq, A_KV_HEADS, A_HEAD_DIM),
            ki_p.reshape(batch, seq, IDX_DIM),
            mk_p.reshape(batch, N_MEM, M_HEADS, M_HEAD_DIM),
            mv_p.reshape(batch, N_MEM, M_HEADS, M_HEAD_DIM),
            take(k_s, (nseq, nq, A_KV_HEADS, A_HEAD_DIM)),
            take(v_s, (nseq, nq, A_KV_HEADS, A_HEAD_DIM)),
            take(ki_s, (nseq, nq, IDX_DIM)),
            take(vn_s, (nseq, nq, B_GROUPS, B_GROUP_DIM)))
```

```python
import functools

import numpy as np
import jax
import jax.numpy as jnp
from jax import lax
from jax.experimental import pallas as pl
from jax.experimental.pallas import tpu as pltpu

F32 = jnp.float32
BF16 = jnp.bfloat16
I32 = jnp.int32

D_MODEL = 2048
PAGE = 128
A_HEADS = 8
A_KV_HEADS = 4
A_HEAD_DIM = 128
A_WIDTH = A_HEADS * A_HEAD_DIM
A_KV_WIDTH = A_KV_HEADS * A_HEAD_DIM
IDX_HEADS = 16
IDX_DIM = 64
TOPK_MAX = 256
Q_BLOCK = 128
ROPE_THETA = 10000.0
CHUNK = 128
B_GROUPS = 8
B_GROUP_DIM = 128
B_WIDTH = B_GROUPS * B_GROUP_DIM
N_MEM = 256
M_HEADS = 4
M_HEAD_DIM = 256
M_WIDTH = M_HEADS * M_HEAD_DIM
EPS = 1e-6

OFF_K = A_WIDTH
OFF_V = OFF_K + A_KV_WIDTH
OFF_QI = OFF_V + A_KV_WIDTH
OFF_KI = OFF_QI + IDX_HEADS * IDX_DIM
OFF_WI = OFF_KI + IDX_DIM
OFF_REST = OFF_WI + IDX_HEADS
A_COLS = 3200
REST_COLS = A_WIDTH + 3 * B_WIDTH + 2 * M_WIDTH + 3 * D_MODEL

Q_SCALE = float(np.log2(np.e)) * A_HEAD_DIM ** -0.5
S_PAD = 16
PROJ_TN = 1024
BISECT_STEPS = 24
SAMPLE_PAGES = 16
INT_MIN = np.int32(-2 ** 31)
INT_MAX = np.int32(2 ** 31 - 1)
NEG_INF = float("-inf")

V7X_VMEM_LIMIT = 56 * 1024 * 1024


def _cparams(sem):
    return pltpu.CompilerParams(dimension_semantics=sem, vmem_limit_bytes=V7X_VMEM_LIMIT)


def _dot(a, b):
    return jnp.dot(a, b, preferred_element_type=F32)


def _dot_nt(a, b):
    return lax.dot_general(a, b, (((1,), (1,)), ((), ())), preferred_element_type=F32)


def _rms(x, g):
    return x * lax.rsqrt(jnp.mean(x * x, axis=-1, keepdims=True) + EPS) * g


def _const_spec(shape):
    nd = len(shape)
    return pl.BlockSpec(shape, lambda *_: (0,) * nd, pipeline_mode=pl.Buffered(1))


def _proj_a_kernel(x_ref, g_ref, w_ref, gq_ref, gk_ref, cos_ref, sin_ref,
                   q_ref, k_ref, v_ref, kb_ref, vb_ref, qi_ref, ki_ref, kib_ref, wi_ref):
    h = _rms(x_ref[...], g_ref[...]).astype(BF16)
    z = _dot_nt(h, w_ref[...])
    tm = z.shape[0]
    c, s = cos_ref[...], sin_ref[...]
    c_sw, s_sw = pltpu.roll(c, 64, 1), pltpu.roll(s, 64, 1)
    lane = lax.broadcasted_iota(I32, (1, 128), 1)
    low = lane < 64
    cq, sq = jnp.where(low, c, c_sw), jnp.where(low, -s, s_sw)
    ci, si = jnp.where(low, c_sw, c), jnp.where(low, s_sw, s)
    first_half = lane % IDX_DIM < IDX_DIM // 2
    sia, sib = jnp.where(first_half, -si, 0.0), jnp.where(first_half, 0.0, si)

    def norm_rope(zz, g):
        n = _rms(zz, g)
        return n * cq + pltpu.roll(n, A_HEAD_DIM // 2, 1) * sq

    def rope_idx(zz):
        return zz * ci + pltpu.roll(zz, 96, 1) * sia + pltpu.roll(zz, 32, 1) * sib

    for hh in range(A_HEADS):
        sl = slice(hh * A_HEAD_DIM, (hh + 1) * A_HEAD_DIM)
        q_ref[:, sl] = (norm_rope(z[:, sl], gq_ref[...]) * Q_SCALE).astype(BF16)
    for hh in range(A_KV_HEADS):
        sl = slice(hh * A_HEAD_DIM, (hh + 1) * A_HEAD_DIM)
        kh = norm_rope(z[:, OFF_K + hh * A_HEAD_DIM:OFF_K + (hh + 1) * A_HEAD_DIM], gk_ref[...])
        vh = z[:, OFF_V + hh * A_HEAD_DIM:OFF_V + (hh + 1) * A_HEAD_DIM]
        head_rows = pl.ds(hh, tm, stride=A_KV_HEADS)
        k_ref[head_rows, :] = kh
        v_ref[head_rows, :] = vh
        kb_ref[:, sl] = kh.astype(BF16)
        vb_ref[:, sl] = vh.astype(BF16)
    for t in range(IDX_HEADS * IDX_DIM // 128):
        sl = slice(t * 128, (t + 1) * 128)
        qi_ref[:, sl] = rope_idx(z[:, OFF_QI + t * 128:OFF_QI + (t + 1) * 128]).astype(BF16)
    last = z[:, OFF_KI:OFF_KI + 128]
    ki = rope_idx(last)[:, :IDX_DIM]
    ki_ref[...] = ki
    kib_ref[...] = ki.astype(BF16)
    wi_ref[...] = last[:, IDX_DIM:IDX_DIM + IDX_HEADS] * ((IDX_HEADS ** -0.5) * (IDX_DIM ** -0.5))


def _proj_a(x, g_pre, w_a, g_q, g_k, tabs, tm, tab_map):
    n = x.shape[0]
    row = lambda w: pl.BlockSpec((tm, w), lambda i: (i, 0))
    tab = pl.BlockSpec((tm, 128), tab_map)
    outs = [(1, A_WIDTH, BF16), (A_KV_HEADS, A_HEAD_DIM, F32), (A_KV_HEADS, A_HEAD_DIM, F32),
            (1, A_KV_WIDTH, BF16), (1, A_KV_WIDTH, BF16),
            (1, IDX_HEADS * IDX_DIM, BF16), (1, IDX_DIM, F32), (1, IDX_DIM, BF16), (1, IDX_HEADS, F32)]
    return pl.pallas_call(
        _proj_a_kernel,
        grid=(n // tm,),
        in_specs=[row(D_MODEL), _const_spec((1, D_MODEL)), _const_spec((A_COLS, D_MODEL)),
                  _const_spec((1, A_HEAD_DIM)), _const_spec((1, A_HEAD_DIM)), tab, tab],
        out_specs=[pl.BlockSpec((tm * r, w), lambda i: (i, 0)) for r, w, _ in outs],
        out_shape=[jax.ShapeDtypeStruct((n * r, w), dt) for r, w, dt in outs],
        compiler_params=_cparams(("arbitrary",)),
        name="proj_a",
    )(x, g_pre, w_a, g_q, g_k, *tabs)


def _proj_rest_kernel(x_ref, xs_ref, g_ref, w_ref, o_ref, os_ref, h_ref, hs_ref):
    i, j = pl.program_id(0), pl.program_id(1)

    @pl.when(j == 0)
    def _():
        h_ref[...] = _rms(x_ref[...], g_ref[...]).astype(BF16)

    @pl.when((i == 0) & (j == 0))
    def _():
        hs_ref[...] = _rms(xs_ref[...], g_ref[...]).astype(BF16)

    w = w_ref[...].astype(BF16)
    o_ref[...] = _dot_nt(h_ref[...], w).astype(BF16)

    @pl.when(i == 0)
    def _():
        os_ref[...] = _dot_nt(hs_ref[...], w).astype(BF16)


def _proj_rest(x, xs, g_pre, w_in_t, tm, tn):
    n, ns = x.shape[0], xs.shape[0]
    ncols = REST_COLS // tn
    xs_cols = lambda i, j: (0, jnp.where(i == 0, j, ncols - 1))
    return pl.pallas_call(
        _proj_rest_kernel,
        grid=(n // tm, ncols),
        in_specs=[pl.BlockSpec((tm, D_MODEL), lambda i, j: (i, 0)),
                  pl.BlockSpec((ns, D_MODEL), lambda i, j: (0, 0), pipeline_mode=pl.Buffered(1)),
                  pl.BlockSpec((1, D_MODEL), lambda i, j: (0, 0)),
                  pl.BlockSpec((pl.Element(tn), pl.Element(D_MODEL)),
                               lambda i, j: (pl.multiple_of(OFF_REST + j * tn, 16), 0))],
        out_specs=[pl.BlockSpec((tm, tn), lambda i, j: (i, j)), pl.BlockSpec((ns, tn), xs_cols)],
        out_shape=[jax.ShapeDtypeStruct((n, REST_COLS), BF16), jax.ShapeDtypeStruct((ns, REST_COLS), BF16)],
        scratch_shapes=[pltpu.VMEM((tm, D_MODEL), BF16), pltpu.VMEM((ns, D_MODEL), BF16)],
        compiler_params=_cparams(("arbitrary", "arbitrary")),
        name="proj_rest",
    )(x, xs, g_pre, w_in_t)


def _mem_kv_kernel(x_ref, g_ref, w_ref, gk_ref, k_ref, v_ref, kb_ref, vb_ref):
    h = _rms(x_ref[...], g_ref[...]).astype(BF16)
    z = _dot(h, w_ref[...])
    for hh in range(M_HEADS):
        sl = slice(hh * M_HEAD_DIM, (hh + 1) * M_HEAD_DIM)
        kh = _rms(z[:, sl], gk_ref[...])
        k_ref[:, sl] = kh
        kb_ref[:, sl] = kh.astype(BF16)
    v = z[:, M_WIDTH:]
    v_ref[...] = v
    vb_ref[...] = v.astype(BF16)


def _mem_kv(mem, g_mem, w_mem, g_mk):
    n = mem.shape[0]
    blk = pl.BlockSpec((N_MEM, M_WIDTH), lambda i: (i, 0))
    return pl.pallas_call(
        _mem_kv_kernel,
        grid=(n // N_MEM,),
        in_specs=[pl.BlockSpec((N_MEM, D_MODEL), lambda i: (i, 0)), _const_spec((1, D_MODEL)),
                  _const_spec((D_MODEL, 2 * M_WIDTH)), _const_spec((1, M_HEAD_DIM))],
        out_specs=[blk, blk, blk, blk],
        out_shape=[jax.ShapeDtypeStruct((n, M_WIDTH), dt) for dt in (F32, F32, BF16, BF16)],
        compiler_params=_cparams(("arbitrary",)),
        name="mem_kv",
    )(mem, g_mem, w_mem, g_mk)


KEY_NEG_INF = np.int32(-0x7F800000)


def _key_to_f32(key):
    return pltpu.bitcast(jnp.where(key >= 0, key, INT_MIN - key), F32)


def _select_params(count, topk, pos_bits, amax, n_valid, t_ref, j_ref):
    kf = float(topk)
    hi0 = amax * 1.000001 + 1e-30
    all_selected = (n_valid <= kf).astype(I32)

    def bisect_body(_, state):
        lo, hi, t, done = state
        mid = 0.5 * lo + 0.5 * hi
        n_ge = count(lambda s, p: s >= mid)
        hit = (n_ge == kf) & (done == 0)
        return (jnp.where(n_ge >= kf, mid, lo), jnp.where(n_ge >= kf, hi, mid),
                jnp.where(hit, mid, t), jnp.where(hit, 1, done))

    _, _, t_bis, done = lax.fori_loop(
        0, BISECT_STEPS, bisect_body, (-hi0, hi0, jnp.full(hi0.shape, NEG_INF, F32), all_selected))
    t_ref[...] = t_bis
    j_ref[...] = jnp.full(j_ref.shape, INT_MAX, I32)

    @pl.when(jnp.min(done) == 0)
    def _():
        t0 = jnp.where(count(lambda s, p: s >= 0.0) >= kf, jnp.int32(0), INT_MIN)

        def bit_body(b, t):
            cand = t + lax.shift_left(jnp.int32(1), 30 - b)
            cand_f = _key_to_f32(cand)
            return jnp.where(count(lambda s, p: s >= cand_f) >= kf, cand, t)

        t = _key_to_f32(jnp.maximum(lax.fori_loop(0, 31, bit_body, t0), KEY_NEG_INF))
        t_ref[...] = t
        tie = (count(lambda s, p: s >= t) > kf) & (t > NEG_INF)

        @pl.when(jnp.max(tie.astype(I32)) > 0)
        def _():
            n_gt = count(lambda s, p: s > t)

            def pos_body(b, p_lo):
                cand = p_lo + lax.shift_left(jnp.int32(1), pos_bits - 1 - b)
                n_eq = count(lambda s, p: (s == t) & (p < cand))
                return jnp.where(n_gt + n_eq < kf, cand, p_lo)

            p_sel = lax.fori_loop(0, pos_bits, pos_body, jnp.zeros(t.shape, I32))
            j_ref[...] = jnp.where(tie, p_sel, INT_MAX)

    return t_ref[...], j_ref[...]


def _fori_by_two(n, body, init):
    carry = lax.fori_loop(0, n // 2, lambda j, c: body(2 * j + 1, body(2 * j, c)), init)
    return lax.cond(n % 2 == 1, lambda c: body(n - 1, c), lambda c: c, carry)


def _dsa_prompt_kernel(qi_ref, wit_ref, q_ref, ki_ref, k_ref, v_ref, o_ref,
                       sc_ref, bias_ref, vt_ref, t_ref, j_ref, s_ref, acc_ref, *, topk, kc, pos_bits):
    i = pl.program_id(1)
    nck = (i * Q_BLOCK + Q_BLOCK + kc - 1) // kc
    nchunks = vt_ref.shape[0]
    rep = A_HEADS // A_KV_HEADS
    q_pos = i * Q_BLOCK + lax.broadcasted_iota(I32, (1, Q_BLOCK), 1)
    sub = lax.broadcasted_iota(I32, (kc, 1), 0)

    @pl.when(i == 0)
    def _():
        for c in range(nchunks):
            for g in range(A_KV_HEADS):
                gsl = slice(g * A_HEAD_DIM, (g + 1) * A_HEAD_DIM)
                vt_ref[c, gsl, :] = v_ref[c * kc:(c + 1) * kc, gsl].astype(F32).T.astype(BF16)

    def chunk_rows(c):
        return pl.ds(pl.multiple_of(c * kc, kc), kc)

    qi = qi_ref[...]
    wit = wit_ref[...]
    qi_pairs = [jnp.concatenate([qi[:, (2 * j) * IDX_DIM:(2 * j + 1) * IDX_DIM],
                                 qi[:, (2 * j + 1) * IDX_DIM:(2 * j + 2) * IDX_DIM]], axis=0)
                for j in range(IDX_HEADS // 2)]

    def score_body(c, amax):
        kic = ki_ref[chunk_rows(c), :]
        acc = jnp.zeros((kc, Q_BLOCK), F32)
        for j in range(IDX_HEADS // 2):
            d = _dot_nt(kic, qi_pairs[j])
            acc = acc + jnp.maximum(d[:, :Q_BLOCK], 0.0) * wit[2 * j:2 * j + 1, :]
            acc = acc + jnp.maximum(d[:, Q_BLOCK:], 0.0) * wit[2 * j + 1:2 * j + 2, :]
        sc_ref[c] = jnp.where(c * kc + sub <= q_pos, acc, NEG_INF)
        return jnp.maximum(amax, jnp.max(jnp.abs(acc).reshape(kc // 64, 64, Q_BLOCK), axis=0))

    amax = jnp.max(_fori_by_two(nck, score_body, jnp.zeros((64, Q_BLOCK), F32)), axis=0, keepdims=True)

    def count(pred):
        def body(c, acc):
            part = jnp.where(pred(sc_ref[c], c * kc + sub), 1.0, 0.0)
            return acc + jnp.sum(part.reshape(kc // 64, 64, Q_BLOCK), axis=0)

        acc = lax.fori_loop(0, nck, body, jnp.zeros((64, Q_BLOCK), F32))
        return jnp.sum(acc, axis=0, keepdims=True)

    thr, jsel = _select_params(count, topk, pos_bits, amax, (q_pos + 1).astype(F32), t_ref, j_ref)

    def bias_body(c, carry):
        s = sc_ref[c]
        pos = c * kc + sub
        sel = ((s > thr) | ((s == thr) & (pos <= jsel))) & (pos <= q_pos)
        bias_ref[c] = jnp.where(sel, 0.0, NEG_INF)
        return carry

    lax.fori_loop(0, nck, bias_body, 0)

    gsl = lambda g: slice(g * A_HEAD_DIM, (g + 1) * A_HEAD_DIM)
    row0 = lambda v: jnp.full((1, rep * Q_BLOCK), v, F32)
    groups = tuple(range(A_KV_HEADS))
    qgs = [jnp.concatenate([q_ref[:, gsl(g * rep + r)] for r in range(rep)], axis=0) for g in groups]


    def qk_body(c, ms):
        b = bias_ref[c]
        bias = jnp.concatenate([b] * rep, axis=1)
        out = []
        for g in groups:
            s = _dot_nt(k_ref[chunk_rows(c), gsl(g)], qgs[g]) + bias
            s_ref[c, g] = s
            out.append(jnp.maximum(ms[g], jnp.max(s, axis=0, keepdims=True)))
        return tuple(out)

    ms = _fori_by_two(nck, qk_body, (row0(NEG_INF),) * len(groups))
    acc_ref[...] = jnp.zeros(acc_ref.shape, F32)

    def pv_body(c, ls):
        out = []
        for g in groups:
            p = jnp.exp2(s_ref[c, g] - ms[g])
            out.append(ls[g] + jnp.sum(p, axis=0, keepdims=True))
            acc_ref[g] += _dot(vt_ref[c, gsl(g), :], p.astype(BF16))
        return tuple(out)

    ls = _fori_by_two(nck, pv_body, (row0(0.0),) * len(groups))
    for g in groups:
        o = acc_ref[g] / ls[g]
        for r in range(rep):
            o_ref[:, gsl(g * rep + r)] = o[:, r * Q_BLOCK:(r + 1) * Q_BLOCK].T.astype(BF16)


def _dsa_prompt(qi, wit, q, ki_b, k_b, v_b, batch, seq, topk):
    nqb = seq // Q_BLOCK
    rep = A_HEADS // A_KV_HEADS
    kc = min(512, seq)
    nchunks = seq // kc
    pos_bits = max(1, int(np.ceil(np.log2(seq))))
    qrow = lambda w: pl.BlockSpec((Q_BLOCK, w), lambda b, i: (b * nqb + i, 0))
    seqblk = lambda w: pl.BlockSpec((seq, w), lambda b, i: (b, 0), pipeline_mode=pl.Buffered(1))
    return pl.pallas_call(
        functools.partial(_dsa_prompt_kernel, topk=topk, kc=kc, pos_bits=pos_bits),
        grid=(batch, nqb),
        in_specs=[qrow(IDX_HEADS * IDX_DIM), pl.BlockSpec((IDX_HEADS, Q_BLOCK), lambda b, i: (0, b * nqb + i)),
                  qrow(A_WIDTH), seqblk(IDX_DIM), seqblk(A_KV_WIDTH), seqblk(A_KV_WIDTH)],
        out_specs=qrow(A_WIDTH),
        out_shape=jax.ShapeDtypeStruct((batch * seq, A_WIDTH), BF16),
        scratch_shapes=[pltpu.VMEM((nchunks, kc, Q_BLOCK), F32), pltpu.VMEM((nchunks, kc, Q_BLOCK), F32),
                        pltpu.VMEM((nchunks, A_KV_WIDTH, kc), BF16),
                        pltpu.VMEM((1, Q_BLOCK), F32), pltpu.VMEM((1, Q_BLOCK), I32),
                        pltpu.VMEM((nchunks, A_KV_HEADS, kc, rep * Q_BLOCK), F32),
                        pltpu.VMEM((A_KV_HEADS, A_HEAD_DIM, rep * Q_BLOCK), F32)],
        compiler_params=_cparams(("arbitrary", "arbitrary")),
        name="dsa_prompt",
    )(qi, wit, q, ki_b, k_b, v_b)


def _fold_heads(e, nq):
    acc = e[0:8]
    for t in range(1, e.shape[0] // 8):
        acc = acc + e[8 * t:8 * (t + 1)]
    return acc + pltpu.roll(acc, nq, 0)


def _tree(op, parts):
    parts = list(parts)
    while len(parts) > 1:
        parts = [op(a, b) for a, b in zip(parts[::2], parts[1::2])] + parts[len(parts) & ~1:]
    return parts[0]


def _lane_fold(op, x):
    return _tree(op, [x[:, t * 128:(t + 1) * 128] for t in range(x.shape[1] // 128)])


def _dsa_sample_kernel(pt_ref, qp_ref, wc_ref, kin_ref, qs_ref, kn_ref, vn_ref, kidx_hbm, k_hbm, v_hbm, o_ref,
                       sc_ref, scn_ref, bias_ref, biasn_ref, t_ref, j_ref, s_ref, sn_ref, m_ref, mfin_ref, l_ref, acc_ref,
                       idx_raw, kv_raw, idx_sem, kv_sem, *, pages, npages, nq, topk, pos_bits):
    b_idx, s_idx = pl.program_id(0), pl.program_id(1)
    kw = pages * PAGE
    steps = npages // pages
    n_past = npages * PAGE
    rows_q = qs_ref.shape[1]
    dup = lambda x8: jnp.concatenate([x8] * (rows_q // 8), axis=0)
    gsl = lambda g: slice(g * A_HEAD_DIM, (g + 1) * A_HEAD_DIM)

    n = b_idx * (3 * steps) + s_idx
    slot = n % 2

    def block_copies(seq, step, sl, start):
        for phase, (src, dst, sem) in enumerate(((kidx_hbm, idx_raw, idx_sem), (k_hbm, kv_raw, kv_sem),
                                                 (v_hbm, kv_raw, kv_sem))):
            @pl.when((step >= phase * steps) & (step < (phase + 1) * steps))
            def _(phase=phase, src=src, dst=dst, sem=sem):
                for p in range(pages):
                    page = pt_ref[seq * npages + (step - phase * steps) * pages + p] if start else 0
                    copy = pltpu.make_async_copy(src.at[page], dst.at[sl, p], sem.at[sl])
                    copy.start() if start else copy.wait()

    @pl.when(n == 0)
    def _():
        block_copies(b_idx, s_idx, slot, True)

    @pl.when(n + 1 < pl.num_programs(0) * 3 * steps)
    def _():
        wrap = s_idx + 1 == 3 * steps
        block_copies(jnp.where(wrap, b_idx + 1, b_idx), jnp.where(wrap, 0, s_idx + 1), 1 - slot, True)

    block_copies(b_idx, s_idx, slot, False)

    def head_rows(g):
        rows = pl.ds(g, PAGE, stride=A_KV_HEADS)
        return jnp.concatenate([kv_raw[slot, p, rows, :] for p in range(pages)], axis=0).astype(BF16)

    @pl.when(s_idx < steps)
    def _():
        qp = qp_ref[...]
        wc = wc_ref[...]
        kpt = jnp.concatenate([idx_raw[slot, p] for p in range(pages)], axis=1).astype(BF16)
        sc_ref[s_idx] = _fold_heads(jnp.maximum(_dot(qp, kpt), 0.0) * wc, nq)

        @pl.when(s_idx == steps - 1)
        def _():
            qrow = lax.broadcasted_iota(I32, (8, 1), 0) % nq
            lane = lax.broadcasted_iota(I32, (1, kw), 1)
            lane_n = lax.broadcasted_iota(I32, (1, PAGE), 1)
            kin = jnp.concatenate([kin_ref[...], jnp.zeros((PAGE - S_PAD, IDX_DIM), BF16)], axis=0)
            scn = _fold_heads(jnp.maximum(_dot_nt(qp, kin), 0.0) * wc, nq)
            scn_ref[...] = jnp.where(lane_n <= qrow, scn, NEG_INF)

            def count(pred):
                ones = _tree(jnp.add, [jnp.where(pred(sc_ref[t], t * kw + lane), 1.0, 0.0) for t in range(steps)])
                cn = jnp.where(pred(scn_ref[...], n_past + lane_n), 1.0, 0.0)
                return (jnp.sum(_lane_fold(jnp.add, ones), axis=-1, keepdims=True)
                        + jnp.sum(cn, axis=-1, keepdims=True))

            amax = jnp.maximum(
                jnp.max(_lane_fold(jnp.maximum, _tree(jnp.maximum, [jnp.abs(sc_ref[t]) for t in range(steps)])),
                        axis=-1, keepdims=True),
                jnp.max(jnp.abs(scn), axis=-1, keepdims=True))
            n_valid = (n_past + 1 + qrow).astype(F32)
            thr, jsel = _select_params(count, topk, pos_bits, amax, n_valid, t_ref, j_ref)
            for t in range(steps):
                sp = sc_ref[t]
                bias_ref[t] = jnp.where((sp > thr) | ((sp == thr) & (t * kw + lane <= jsel)), 0.0, NEG_INF)
            sn = scn_ref[...]
            seln = ((sn > thr) | ((sn == thr) & (n_past + lane_n <= jsel))) & (lane_n <= qrow)
            biasn_ref[...] = jnp.where(seln, 0.0, NEG_INF)

    @pl.when((s_idx >= steps) & (s_idx < 2 * steps))
    def _():
        t = s_idx - steps

        @pl.when(t == 0)
        def _():
            m_ref[...] = jnp.full(m_ref.shape, NEG_INF, F32)

        bias = dup(bias_ref[t])
        for g in range(A_KV_HEADS):
            s = _dot_nt(qs_ref[g], head_rows(g)) + bias
            s_ref[g, t] = s
            m_ref[g] = jnp.maximum(m_ref[g], _lane_fold(jnp.maximum, s))

        @pl.when(t == steps - 1)
        def _():
            bn = dup(biasn_ref[:, :S_PAD])
            for g in range(A_KV_HEADS):
                sn = _dot_nt(qs_ref[g], kn_ref[:, gsl(g)]) + bn
                sn_ref[g] = sn
                mfin_ref[g] = jnp.maximum(jnp.max(m_ref[g], axis=-1, keepdims=True),
                                          jnp.max(sn, axis=-1, keepdims=True))

    @pl.when(s_idx >= 2 * steps)
    def _():
        t = s_idx - 2 * steps

        @pl.when(t == 0)
        def _():
            l_ref[...] = jnp.zeros(l_ref.shape, F32)
            acc_ref[...] = jnp.zeros(acc_ref.shape, F32)

        for g in range(A_KV_HEADS):
            p = jnp.exp2(s_ref[g, t] - mfin_ref[g])
            l_ref[g] += _lane_fold(jnp.add, p)
            acc_ref[g] += _dot(p.astype(BF16), head_rows(g))

        @pl.when(t == steps - 1)
        def _():
            for g in range(A_KV_HEADS):
                pn = jnp.exp2(sn_ref[g] - mfin_ref[g])
                l = jnp.sum(l_ref[g], axis=-1, keepdims=True) + jnp.sum(pn, axis=-1, keepdims=True)
                o_ref[g] = (acc_ref[g] + _dot(pn.astype(BF16), vn_ref[:, gsl(g)])) / l


def _dsa_sample(page_flat, qp, wcol, ki_new_b, qs, k_new_b, v_new_b, cache_kidx_t, cache_k, cache_v,
                nseq, npages, pages, nq, topk):
    steps = npages // pages
    n_past = npages * PAGE
    kw = pages * PAGE
    rows = IDX_HEADS * nq
    rows_q = qs.shape[2]
    pos_bits = int(np.ceil(np.log2(n_past + PAGE)))

    seq3 = lambda shape: pl.BlockSpec((None,) + shape, lambda b, s, pt: (b,) + (0,) * len(shape))
    new_rows = lambda w: pl.BlockSpec((S_PAD, w), lambda b, s, pt: (b, 0))
    in_hbm = pl.BlockSpec(memory_space=pl.ANY)
    grid_spec = pltpu.PrefetchScalarGridSpec(
        num_scalar_prefetch=1,
        grid=(nseq, 3 * steps),
        in_specs=[seq3((rows, IDX_DIM)), seq3((rows, 1)), new_rows(IDX_DIM),
                  seq3((A_KV_HEADS, rows_q, A_HEAD_DIM)), new_rows(A_KV_WIDTH), new_rows(A_KV_WIDTH),
                  in_hbm, in_hbm, in_hbm],
        out_specs=seq3((A_KV_HEADS, rows_q, A_HEAD_DIM)),
        scratch_shapes=[pltpu.VMEM((steps, 8, kw), F32), pltpu.VMEM((8, PAGE), F32),
                        pltpu.VMEM((steps, 8, kw), F32), pltpu.VMEM((8, PAGE), F32),
                        pltpu.VMEM((8, 1), F32), pltpu.VMEM((8, 1), I32),
                        pltpu.VMEM((A_KV_HEADS, steps, rows_q, kw), F32),
                        pltpu.VMEM((A_KV_HEADS, rows_q, S_PAD), F32),
                        pltpu.VMEM((A_KV_HEADS, rows_q, 128), F32),
                        pltpu.VMEM((A_KV_HEADS, rows_q, 1), F32),
                        pltpu.VMEM((A_KV_HEADS, rows_q, 128), F32),
                        pltpu.VMEM((A_KV_HEADS, rows_q, A_HEAD_DIM), F32),
                        pltpu.VMEM((2, pages, IDX_DIM, PAGE), F32),
                        pltpu.VMEM((2, pages, PAGE * A_KV_HEADS, A_HEAD_DIM), F32),
                        pltpu.SemaphoreType.DMA((2,)), pltpu.SemaphoreType.DMA((2,))],
    )
    return pl.pallas_call(
        functools.partial(_dsa_sample_kernel, pages=pages, npages=npages, nq=nq, topk=topk, pos_bits=pos_bits),
        grid_spec=grid_spec,
        out_shape=jax.ShapeDtypeStruct((nseq, A_KV_HEADS, rows_q, A_HEAD_DIM), F32),
        compiler_params=_cparams(("arbitrary", "arbitrary")),
        name="dsa_sample",
    )(page_flat, qp, wcol, ki_new_b, qs, k_new_b, v_new_b, cache_kidx_t, cache_k, cache_v)


def _merge_kernel(x_ref, oa_ref, ag_ref, bu_ref, bv_ref, bg_ref, cq_ref, cg_ref, ra_ref, rb_ref, rc_ref,
                  mk_ref, mv_ref, ws_ref, bs_ref, gsgu_ref, gmq_ref, wpa_ref, wpb_ref, wpc_ref, wout_ref,
                  y_ref, *maybe_vn_ref, tm, chunk, mem_groups):
    f32 = lambda r: r[...].astype(F32)
    silu = lambda t: t * jax.nn.sigmoid(t)

    vn = _rms(f32(bv_ref), gsgu_ref[...])
    if maybe_vn_ref:
        maybe_vn_ref[0][...] = vn
    vnb = vn.astype(BF16)
    bu = f32(bu_ref)
    tril = (lax.broadcasted_iota(I32, (chunk, chunk), 1) <= lax.broadcasted_iota(I32, (chunk, chunk), 0))
    ob_cols = []
    for g in range(B_GROUPS):
        wg = jnp.where(tril, ws_ref[g], 0.0).astype(BF16)
        gsl = slice(g * B_GROUP_DIM, (g + 1) * B_GROUP_DIM)
        parts = [_dot(wg, vnb[c * chunk:(c + 1) * chunk, gsl]) + bs_ref[:, g:g + 1] for c in range(tm // chunk)]
        ob_cols.append(parts[0] if len(parts) == 1 else jnp.concatenate(parts, axis=0))
    ob = bu * jnp.concatenate(ob_cols, axis=1)
    pb = _dot((ob * silu(f32(bg_ref))).astype(BF16), wpb_ref[...])

    cq = f32(cq_ref)
    rows_g = tm // mem_groups
    oc_cols = []
    for hh in range(M_HEADS):
        hsl = slice(hh * M_HEAD_DIM, (hh + 1) * M_HEAD_DIM)
        qn = (_rms(cq[:, hsl], gmq_ref[...]) * (M_HEAD_DIM ** -0.5)).astype(BF16)
        oc_rows = []
        for u in range(mem_groups):
            msl = slice(u * N_MEM, (u + 1) * N_MEM)
            s = _dot_nt(qn[u * rows_g:(u + 1) * rows_g], mk_ref[msl, hsl])
            p = jnp.exp(s - jnp.max(s, axis=-1, keepdims=True))
            oc_rows.append(_dot(p.astype(BF16), mv_ref[msl, hsl]) / jnp.sum(p, axis=-1, keepdims=True))
        oc_cols.append(oc_rows[0] if mem_groups == 1 else jnp.concatenate(oc_rows, axis=0))
    oc = jnp.concatenate(oc_cols, axis=1)
    pc = _dot((oc * silu(f32(cg_ref))).astype(BF16), wpc_ref[...])

    pa = _dot((f32(oa_ref) * silu(f32(ag_ref))).astype(BF16), wpa_ref[...])
    sig = jax.nn.sigmoid
    m = sig(f32(ra_ref)) * pa + sig(f32(rb_ref)) * pb + sig(f32(rc_ref)) * pc
    y_ref[...] = x_ref[...] + _dot(m.astype(BF16), wout_ref[...])


def _merge(x, oa, zr, mk_b, mv_b, ws, bs_t, g_sgu, g_mq, w_pa, w_pb, w_pc, w_out, tm, chunk, mem_groups, mem_map,
           emit_vn):
    n = x.shape[0]
    col = lambda w, j: pl.BlockSpec((tm, w), lambda i, j=j: (i, j))
    mem = pl.BlockSpec((mem_groups * N_MEM, M_WIDTH), mem_map)
    in_specs = [col(D_MODEL, 0), col(A_WIDTH, 0),
                col(1024, 0), col(1024, 1), col(1024, 2), col(1024, 3), col(1024, 4), col(1024, 5),
                col(2048, 3), col(2048, 4), col(2048, 5),
                mem, mem,
                _const_spec((B_GROUPS, chunk, chunk)), _const_spec((chunk, B_GROUPS)),
                _const_spec((1, B_WIDTH)), _const_spec((1, M_HEAD_DIM)),
                _const_spec((A_WIDTH, D_MODEL)), _const_spec((B_WIDTH, D_MODEL)),
                _const_spec((M_WIDTH, D_MODEL)), _const_spec((D_MODEL, D_MODEL))]
    out_specs = [col(D_MODEL, 0)]
    out_shape = [jax.ShapeDtypeStruct((n, D_MODEL), F32)]
    if emit_vn:
        out_specs.append(col(B_WIDTH, 0))
        out_shape.append(jax.ShapeDtypeStruct((n, B_WIDTH), F32))
    return pl.pallas_call(
        functools.partial(_merge_kernel, tm=tm, chunk=chunk, mem_groups=mem_groups),
        grid=(n // tm,),
        in_specs=in_specs,
        out_specs=out_specs,
        out_shape=out_shape,
        compiler_params=_cparams(("arbitrary",)),
        name="merge",
    )(x, oa, *([zr] * 9), mk_b, mv_b, ws, bs_t, g_sgu, g_mq, w_pa, w_pb, w_pc, w_out)


def _rope_tables(pos):
    freq = lambda half: ROPE_THETA ** (-jnp.arange(half, dtype=F32) / half)
    f_idx = freq(IDX_DIM // 2)
    ang = pos.astype(F32)[:, None] * jnp.concatenate([freq(A_HEAD_DIM // 2), f_idx, f_idx])[None, :]
    return jnp.cos(ang), jnp.sin(ang)


def kernel(x_prompt, x_sample, cache_k, cache_v, cache_kidx, cache_mem_k, cache_mem_v, page_table,
           mem_prompt, g_pre, w_in, g_q, g_k, g_mq, g_mk, g_mem, w_mem_kv, g_sgu, w_s, b_s,
           w_pa, w_pb, w_pc, w_out):
    batch, seq, _ = x_prompt.shape
    nseq, nq, _ = x_sample.shape
    assert nq == 4 and nq <= S_PAD
    npages = page_table.shape[1]
    n_past = npages * PAGE
    n_pool = cache_k.shape[0]
    row2 = lambda a: a.reshape(1, -1)

    w_in_t = w_in.T
    w_a = w_in_t[:A_COLS].astype(BF16)
    w_pa_b, w_pb_b, w_pc_b, w_out_b = (w.astype(BF16) for w in (w_pa, w_pb, w_pc, w_out))
    w_mem_b = w_mem_kv.astype(BF16)
    g_pre2, g_q2, g_k2, g_mq2, g_mk2, g_mem2, g_sgu2 = map(row2, (g_pre, g_q, g_k, g_mq, g_mk, g_mem, g_sgu))

    xp = x_prompt.reshape(batch * seq, D_MODEL)
    tm_a = min(256, seq)
    nblk = seq // tm_a
    tabs_p = _rope_tables(jnp.arange(seq))
    q, k_p, v_p, k_b, v_b, qi, ki_p, ki_b, wi = _proj_a(
        xp, g_pre2, w_a, g_q2, g_k2, tabs_p, tm_a, lambda i: (i % nblk, 0))
    xs = jnp.pad(x_sample, ((0, 0), (0, S_PAD - nq), (0, 0))).reshape(nseq * S_PAD, D_MODEL)
    rows_s = nseq * S_PAD
    zr, zr_s = _proj_rest(xp, xs, g_pre2, w_in_t, min(1024, seq), PROJ_TN)
    mk_p, mv_p, mk_b, mv_b = _mem_kv(mem_prompt.reshape(batch * N_MEM, D_MODEL), g_mem2, w_mem_b, g_mk2)
    oa = _dsa_prompt(qi, wi.T, q, ki_b, k_b, v_b, batch, seq, min(TOPK_MAX, seq // 4))
    tm_m = min(256, seq)
    nblk_m = seq // tm_m
    (y_p,) = _merge(xp, oa, zr, mk_b, mv_b, w_s, b_s.T, g_sgu2, g_mq2, w_pa_b, w_pb_b, w_pc_b, w_out_b,
                    tm_m, CHUNK, 1, lambda i: (i // nblk_m, 0), False)

    tabs_s = tuple(jnp.tile(t, (nseq, 1)) for t in _rope_tables(n_past + jnp.arange(S_PAD)))
    q_s, k_s, v_s, k_sb, v_sb, qi_s, ki_s, ki_sb, wi_s = _proj_a(
        xs, g_pre2, w_a, g_q2, g_k2, tabs_s, rows_s, lambda i: (0, 0))

    qp = (qi_s.reshape(nseq, S_PAD, IDX_HEADS, IDX_DIM)[:, :nq]
          .transpose(0, 2, 1, 3).reshape(nseq, IDX_HEADS * nq, IDX_DIM))
    wcol = wi_s.reshape(nseq, S_PAD, IDX_HEADS)[:, :nq].transpose(0, 2, 1).reshape(nseq, IDX_HEADS * nq, 1)
    rep = A_HEADS // A_KV_HEADS
    qs = (q_s.reshape(nseq, S_PAD, A_KV_HEADS, rep, A_HEAD_DIM)[:, :nq]
          .transpose(0, 2, 3, 1, 4).reshape(nseq, A_KV_HEADS, rep * nq, A_HEAD_DIM))
    qs = jnp.concatenate([qs, qs], axis=2)
    page_flat = page_table.reshape(-1)
    pages = min(SAMPLE_PAGES, npages)
    o_s = _dsa_sample(page_flat, qp, wcol, ki_sb, qs, k_sb, v_sb, jnp.swapaxes(cache_kidx, 1, 2),
                      cache_k.reshape(n_pool, PAGE * A_KV_HEADS, A_HEAD_DIM),
                      cache_v.reshape(n_pool, PAGE * A_KV_HEADS, A_HEAD_DIM),
                      nseq, npages, pages, nq, min(TOPK_MAX, (n_past + nq) // 4))
    oa_s = (o_s[:, :, :rep * nq].reshape(nseq, A_KV_HEADS, rep, nq, A_HEAD_DIM)
            .transpose(0, 3, 1, 2, 4).reshape(nseq, nq, A_WIDTH))
    oa_s = jnp.pad(oa_s, ((0, 0), (0, S_PAD - nq), (0, 0))).reshape(rows_s, A_WIDTH).astype(BF16)
    mk_s = cache_mem_k.reshape(nseq * N_MEM, M_WIDTH).astype(BF16)
    mv_s = cache_mem_v.reshape(nseq * N_MEM, M_WIDTH).astype(BF16)
    y_s, vn_s = _merge(xs, oa_s, zr_s, mk_s, mv_s, w_s[:, :S_PAD, :S_PAD], b_s[:, :S_PAD].T, g_sgu2, g_mq2,
                       w_pa_b, w_pb_b, w_pc_b, w_out_b, rows_s, S_PAD, nseq, lambda i: (0, 0), True)

    take = lambda a, shape: a.reshape(nseq, S_PAD, -1)[:, :nq].reshape(shape)
    return (y_p.reshape(batch, seq, D_MODEL),
            take(y_s, (nseq, nq, D_MODEL)),
            k_p.reshape(batch, seq, A_KV_HEADS, A_HEAD_DIM),
            v_p.reshape(batch, seq, A_KV_HEADS, A_HEAD_DIM),
            ki_p.reshape(batch, seq, IDX_DIM),
            mk_p.reshape(batch, N_MEM, M_HEADS, M_HEAD_DIM),
            mv_p.reshape(batch, N_MEM, M_HEADS, M_HEAD_DIM),
            take(k_s, (nseq, nq, A_KV_HEADS, A_HEAD_DIM)),
            take(v_s, (nseq, nq, A_KV_HEADS, A_HEAD_DIM)),
            take(ki_s, (nseq, nq, IDX_DIM)),
            take(vn_s, (nseq, nq, B_GROUPS, B_GROUP_DIM)))
```

```python
import functools

import numpy as np
import jax
import jax.numpy as jnp
from jax import lax
from jax.experimental import pallas as pl
from jax.experimental.pallas import tpu as pltpu

F32 = jnp.float32
BF16 = jnp.bfloat16
I32 = jnp.int32

D_MODEL = 2048
PAGE = 128
A_HEADS = 8
A_KV_HEADS = 4
A_HEAD_DIM = 128
A_WIDTH = A_HEADS * A_HEAD_DIM
A_KV_WIDTH = A_KV_HEADS * A_HEAD_DIM
IDX_HEADS = 16
IDX_DIM = 64
TOPK_MAX = 256
Q_BLOCK = 128
ROPE_THETA = 10000.0
CHUNK = 128
B_GROUPS = 8
B_GROUP_DIM = 128
B_WIDTH = B_GROUPS * B_GROUP_DIM
N_MEM = 256
M_HEADS = 4
M_HEAD_DIM = 256
M_WIDTH = M_HEADS * M_HEAD_DIM
EPS = 1e-6

OFF_K = A_WIDTH
OFF_V = OFF_K + A_KV_WIDTH
OFF_QI = OFF_V + A_KV_WIDTH
OFF_KI = OFF_QI + IDX_HEADS * IDX_DIM
OFF_WI = OFF_KI + IDX_DIM
OFF_REST = OFF_WI + IDX_HEADS
A_COLS = 3200
REST_COLS = A_WIDTH + 3 * B_WIDTH + 2 * M_WIDTH + 3 * D_MODEL

Q_SCALE = float(np.log2(np.e)) * A_HEAD_DIM ** -0.5
S_PAD = 16
PROJ_TN = 1024
BISECT_STEPS = 24
SAMPLE_PAGES = 16
SAMPLE_IDX_PAGES = 64
INT_MIN = np.int32(-2 ** 31)
INT_MAX = np.int32(2 ** 31 - 1)
NEG_INF = float("-inf")

V7X_VMEM_LIMIT = 56 * 1024 * 1024


def _cparams(sem):
    return pltpu.CompilerParams(dimension_semantics=sem, vmem_limit_bytes=V7X_VMEM_LIMIT)


def _dot(a, b):
    return jnp.dot(a, b, preferred_element_type=F32)


def _dot_nt(a, b):
    return lax.dot_general(a, b, (((1,), (1,)), ((), ())), preferred_element_type=F32)


def _rms(x, g):
    return x * lax.rsqrt(jnp.mean(x * x, axis=-1, keepdims=True) + EPS) * g


def _const_spec(shape):
    nd = len(shape)
    return pl.BlockSpec(shape, lambda *_: (0,) * nd, pipeline_mode=pl.Buffered(1))


def _proj_a_kernel(x_ref, g_ref, w_ref, gq_ref, gk_ref, cos_ref, sin_ref,
                   q_ref, k_ref, v_ref, kb_ref, vb_ref, qi_ref, ki_ref, kib_ref, wi_ref):
    h = _rms(x_ref[...], g_ref[...]).astype(BF16)
    z = _dot_nt(h, w_ref[...])
    tm = z.shape[0]
    c, s = cos_ref[...], sin_ref[...]
    c_sw, s_sw = pltpu.roll(c, 64, 1), pltpu.roll(s, 64, 1)
    lane = lax.broadcasted_iota(I32, (1, 128), 1)
    low = lane < 64
    cq, sq = jnp.where(low, c, c_sw), jnp.where(low, -s, s_sw)
    ci, si = jnp.where(low, c_sw, c), jnp.where(low, s_sw, s)
    first_half = lane % IDX_DIM < IDX_DIM // 2
    sia, sib = jnp.where(first_half, -si, 0.0), jnp.where(first_half, 0.0, si)

    def norm_rope(zz, g):
        n = _rms(zz, g)
        return n * cq + pltpu.roll(n, A_HEAD_DIM // 2, 1) * sq

    def rope_idx(zz):
        return zz * ci + pltpu.roll(zz, 96, 1) * sia + pltpu.roll(zz, 32, 1) * sib

    for hh in range(A_HEADS):
        sl = slice(hh * A_HEAD_DIM, (hh + 1) * A_HEAD_DIM)
        q_ref[:, sl] = (norm_rope(z[:, sl], gq_ref[...]) * Q_SCALE).astype(BF16)
    for hh in range(A_KV_HEADS):
        sl = slice(hh * A_HEAD_DIM, (hh + 1) * A_HEAD_DIM)
        kh = norm_rope(z[:, OFF_K + hh * A_HEAD_DIM:OFF_K + (hh + 1) * A_HEAD_DIM], gk_ref[...])
        vh = z[:, OFF_V + hh * A_HEAD_DIM:OFF_V + (hh + 1) * A_HEAD_DIM]
        head_rows = pl.ds(hh, tm, stride=A_KV_HEADS)
        k_ref[head_rows, :] = kh
        v_ref[head_rows, :] = vh
        kb_ref[:, sl] = kh.astype(BF16)
        vb_ref[:, sl] = vh.astype(BF16)
    for t in range(IDX_HEADS * IDX_DIM // 128):
        sl = slice(t * 128, (t + 1) * 128)
        qi_ref[:, sl] = rope_idx(z[:, OFF_QI + t * 128:OFF_QI + (t + 1) * 128]).astype(BF16)
    last = z[:, OFF_KI:OFF_KI + 128]
    ki = rope_idx(last)[:, :IDX_DIM]
    ki_ref[...] = ki
    kib_ref[...] = ki.astype(BF16)
    wi_ref[...] = last[:, IDX_DIM:IDX_DIM + IDX_HEADS] * ((IDX_HEADS ** -0.5) * (IDX_DIM ** -0.5))


def _proj_a(x, g_pre, w_a, g_q, g_k, tabs, tm, tab_map):
    n = x.shape[0]
    row = lambda w: pl.BlockSpec((tm, w), lambda i: (i, 0))
    tab = pl.BlockSpec((tm, 128), tab_map)
    outs = [(1, A_WIDTH, BF16), (A_KV_HEADS, A_HEAD_DIM, F32), (A_KV_HEADS, A_HEAD_DIM, F32),
            (1, A_KV_WIDTH, BF16), (1, A_KV_WIDTH, BF16),
            (1, IDX_HEADS * IDX_DIM, BF16), (1, IDX_DIM, F32), (1, IDX_DIM, BF16), (1, IDX_HEADS, F32)]
    return pl.pallas_call(
        _proj_a_kernel,
        grid=(n // tm,),
        in_specs=[row(D_MODEL), _const_spec((1, D_MODEL)), _const_spec((A_COLS, D_MODEL)),
                  _const_spec((1, A_HEAD_DIM)), _const_spec((1, A_HEAD_DIM)), tab, tab],
        out_specs=[pl.BlockSpec((tm * r, w), lambda i: (i, 0)) for r, w, _ in outs],
        out_shape=[jax.ShapeDtypeStruct((n * r, w), dt) for r, w, dt in outs],
        compiler_params=_cparams(("arbitrary",)),
        name="proj_a",
    )(x, g_pre, w_a, g_q, g_k, *tabs)


def _proj_rest_kernel(x_ref, xs_ref, g_ref, w_ref, o_ref, os_ref, h_ref, hs_ref):
    i, j = pl.program_id(0), pl.program_id(1)

    @pl.when(j == 0)
    def _():
        h_ref[...] = _rms(x_ref[...], g_ref[...]).astype(BF16)

    @pl.when((i == 0) & (j == 0))
    def _():
        hs_ref[...] = _rms(xs_ref[...], g_ref[...]).astype(BF16)

    w = w_ref[...].astype(BF16)
    o_ref[...] = _dot_nt(h_ref[...], w).astype(BF16)

    @pl.when(i == 0)
    def _():
        os_ref[...] = _dot_nt(hs_ref[...], w).astype(BF16)


def _proj_rest(x, xs, g_pre, w_in_t, tm, tn):
    n, ns = x.shape[0], xs.shape[0]
    ncols = REST_COLS // tn
    xs_cols = lambda i, j: (0, jnp.where(i == 0, j, ncols - 1))
    return pl.pallas_call(
        _proj_rest_kernel,
        grid=(n // tm, ncols),
        in_specs=[pl.BlockSpec((tm, D_MODEL), lambda i, j: (i, 0)),
                  pl.BlockSpec((ns, D_MODEL), lambda i, j: (0, 0), pipeline_mode=pl.Buffered(1)),
                  pl.BlockSpec((1, D_MODEL), lambda i, j: (0, 0)),
                  pl.BlockSpec((pl.Element(tn), pl.Element(D_MODEL)),
                               lambda i, j: (pl.multiple_of(OFF_REST + j * tn, 16), 0))],
        out_specs=[pl.BlockSpec((tm, tn), lambda i, j: (i, j)), pl.BlockSpec((ns, tn), xs_cols)],
        out_shape=[jax.ShapeDtypeStruct((n, REST_COLS), BF16), jax.ShapeDtypeStruct((ns, REST_COLS), BF16)],
        scratch_shapes=[pltpu.VMEM((tm, D_MODEL), BF16), pltpu.VMEM((ns, D_MODEL), BF16)],
        compiler_params=_cparams(("arbitrary", "arbitrary")),
        name="proj_rest",
    )(x, xs, g_pre, w_in_t)


def _mem_kv_kernel(x_ref, g_ref, w_ref, gk_ref, k_ref, v_ref, kb_ref, vb_ref):
    h = _rms(x_ref[...], g_ref[...]).astype(BF16)
    z = _dot(h, w_ref[...])
    for hh in range(M_HEADS):
        sl = slice(hh * M_HEAD_DIM, (hh + 1) * M_HEAD_DIM)
        kh = _rms(z[:, sl], gk_ref[...])
        k_ref[:, sl] = kh
        kb_ref[:, sl] = kh.astype(BF16)
    v = z[:, M_WIDTH:]
    v_ref[...] = v
    vb_ref[...] = v.astype(BF16)


def _mem_kv(mem, g_mem, w_mem, g_mk):
    n = mem.shape[0]
    blk = pl.BlockSpec((N_MEM, M_WIDTH), lambda i: (i, 0))
    return pl.pallas_call(
        _mem_kv_kernel,
        grid=(n // N_MEM,),
        in_specs=[pl.BlockSpec((N_MEM, D_MODEL), lambda i: (i, 0)), _const_spec((1, D_MODEL)),
                  _const_spec((D_MODEL, 2 * M_WIDTH)), _const_spec((1, M_HEAD_DIM))],
        out_specs=[blk, blk, blk, blk],
        out_shape=[jax.ShapeDtypeStruct((n, M_WIDTH), dt) for dt in (F32, F32, BF16, BF16)],
        compiler_params=_cparams(("arbitrary",)),
        name="mem_kv",
    )(mem, g_mem, w_mem, g_mk)


KEY_NEG_INF = np.int32(-0x7F800000)


def _key_to_f32(key):
    return pltpu.bitcast(jnp.where(key >= 0, key, INT_MIN - key), F32)


def _select_params(count, topk, pos_bits, amax, n_valid, t_ref, j_ref):
    kf = float(topk)
    hi0 = amax * 1.000001 + 1e-30
    all_selected = (n_valid <= kf).astype(I32)

    def bisect_body(_, state):
        lo, hi, t, done = state
        mid = 0.5 * lo + 0.5 * hi
        n_ge = count(lambda s, p: s >= mid)
        hit = (n_ge == kf) & (done == 0)
        return (jnp.where(n_ge >= kf, mid, lo), jnp.where(n_ge >= kf, hi, mid),
                jnp.where(hit, mid, t), jnp.where(hit, 1, done))

    _, _, t_bis, done = lax.fori_loop(
        0, BISECT_STEPS, bisect_body, (-hi0, hi0, jnp.full(hi0.shape, NEG_INF, F32), all_selected))
    t_ref[...] = t_bis
    j_ref[...] = jnp.full(j_ref.shape, INT_MAX, I32)

    @pl.when(jnp.min(done) == 0)
    def _():
        t0 = jnp.where(count(lambda s, p: s >= 0.0) >= kf, jnp.int32(0), INT_MIN)

        def bit_body(b, t):
            cand = t + lax.shift_left(jnp.int32(1), 30 - b)
            cand_f = _key_to_f32(cand)
            return jnp.where(count(lambda s, p: s >= cand_f) >= kf, cand, t)

        t = _key_to_f32(jnp.maximum(lax.fori_loop(0, 31, bit_body, t0), KEY_NEG_INF))
        t_ref[...] = t
        tie = (count(lambda s, p: s >= t) > kf) & (t > NEG_INF)

        @pl.when(jnp.max(tie.astype(I32)) > 0)
        def _():
            n_gt = count(lambda s, p: s > t)

            def pos_body(b, p_lo):
                cand = p_lo + lax.shift_left(jnp.int32(1), pos_bits - 1 - b)
                n_eq = count(lambda s, p: (s == t) & (p < cand))
                return jnp.where(n_gt + n_eq < kf, cand, p_lo)

            p_sel = lax.fori_loop(0, pos_bits, pos_body, jnp.zeros(t.shape, I32))
            j_ref[...] = jnp.where(tie, p_sel, INT_MAX)

    return t_ref[...], j_ref[...]


def _fori_by_two(n, body, init):
    carry = lax.fori_loop(0, n // 2, lambda j, c: body(2 * j + 1, body(2 * j, c)), init)
    return lax.cond(n % 2 == 1, lambda c: body(n - 1, c), lambda c: c, carry)


def _dsa_prompt_kernel(qi_ref, wit_ref, q_ref, ki_ref, k_ref, v_ref, o_ref,
                       sc_ref, bias_ref, vt_ref, t_ref, j_ref, s_ref, acc_ref, *, topk, kc, pos_bits):
    i = pl.program_id(1)
    nck = (i * Q_BLOCK + Q_BLOCK + kc - 1) // kc
    nchunks = vt_ref.shape[0]
    rep = A_HEADS // A_KV_HEADS
    q_pos = i * Q_BLOCK + lax.broadcasted_iota(I32, (1, Q_BLOCK), 1)
    sub = lax.broadcasted_iota(I32, (kc, 1), 0)

    @pl.when(i == 0)
    def _():
        for c in range(nchunks):
            for g in range(A_KV_HEADS):
                gsl = slice(g * A_HEAD_DIM, (g + 1) * A_HEAD_DIM)
                vt_ref[c, gsl, :] = v_ref[c * kc:(c + 1) * kc, gsl].astype(F32).T.astype(BF16)

    def chunk_rows(c):
        return pl.ds(pl.multiple_of(c * kc, kc), kc)

    qi = qi_ref[...]
    wit = wit_ref[...]
    qi_pairs = [jnp.concatenate([qi[:, (2 * j) * IDX_DIM:(2 * j + 1) * IDX_DIM],
                                 qi[:, (2 * j + 1) * IDX_DIM:(2 * j + 2) * IDX_DIM]], axis=0)
                for j in range(IDX_HEADS // 2)]

    def score_body(c, amax):
        kic = ki_ref[chunk_rows(c), :]
        acc = jnp.zeros((kc, Q_BLOCK), F32)
        for j in range(IDX_HEADS // 2):
            d = _dot_nt(kic, qi_pairs[j])
            acc = acc + jnp.maximum(d[:, :Q_BLOCK], 0.0) * wit[2 * j:2 * j + 1, :]
            acc = acc + jnp.maximum(d[:, Q_BLOCK:], 0.0) * wit[2 * j + 1:2 * j + 2, :]
        sc_ref[c] = jnp.where(c * kc + sub <= q_pos, acc, NEG_INF)
        return jnp.maximum(amax, jnp.max(jnp.abs(acc).reshape(kc // 64, 64, Q_BLOCK), axis=0))

    amax = jnp.max(_fori_by_two(nck, score_body, jnp.zeros((64, Q_BLOCK), F32)), axis=0, keepdims=True)

    def count(pred):
        def body(c, acc):
            part = jnp.where(pred(sc_ref[c], c * kc + sub), 1.0, 0.0)
            return acc + jnp.sum(part.reshape(kc // 64, 64, Q_BLOCK), axis=0)

        acc = lax.fori_loop(0, nck, body, jnp.zeros((64, Q_BLOCK), F32))
        return jnp.sum(acc, axis=0, keepdims=True)

    thr, jsel = _select_params(count, topk, pos_bits, amax, (q_pos + 1).astype(F32), t_ref, j_ref)

    def bias_body(c, carry):
        s = sc_ref[c]
        pos = c * kc + sub
        sel = ((s > thr) | ((s == thr) & (pos <= jsel))) & (pos <= q_pos)
        bias_ref[c] = jnp.where(sel, 0.0, NEG_INF)
        return carry

    lax.fori_loop(0, nck, bias_body, 0)

    gsl = lambda g: slice(g * A_HEAD_DIM, (g + 1) * A_HEAD_DIM)
    row0 = lambda v: jnp.full((1, rep * Q_BLOCK), v, F32)
    groups = tuple(range(A_KV_HEADS))
    qgs = [jnp.concatenate([q_ref[:, gsl(g * rep + r)] for r in range(rep)], axis=0) for g in groups]


    def qk_body(c, ms):
        b = bias_ref[c]
        bias = jnp.concatenate([b] * rep, axis=1)
        out = []
        for g in groups:
            s = _dot_nt(k_ref[chunk_rows(c), gsl(g)], qgs[g]) + bias
            s_ref[c, g] = s
            out.append(jnp.maximum(ms[g], jnp.max(s, axis=0, keepdims=True)))
        return tuple(out)

    ms = _fori_by_two(nck, qk_body, (row0(NEG_INF),) * len(groups))
    acc_ref[...] = jnp.zeros(acc_ref.shape, F32)

    def pv_body(c, ls):
        out = []
        for g in groups:
            p = jnp.exp2(s_ref[c, g] - ms[g])
            out.append(ls[g] + jnp.sum(p, axis=0, keepdims=True))
            acc_ref[g] += _dot(vt_ref[c, gsl(g), :], p.astype(BF16))
        return tuple(out)

    ls = _fori_by_two(nck, pv_body, (row0(0.0),) * len(groups))
    for g in groups:
        o = acc_ref[g] / ls[g]
        for r in range(rep):
            o_ref[:, gsl(g * rep + r)] = o[:, r * Q_BLOCK:(r + 1) * Q_BLOCK].T.astype(BF16)


def _dsa_prompt(qi, wit, q, ki_b, k_b, v_b, batch, seq, topk):
    nqb = seq // Q_BLOCK
    rep = A_HEADS // A_KV_HEADS
    kc = min(512, seq)
    nchunks = seq // kc
    pos_bits = max(1, int(np.ceil(np.log2(seq))))
    qrow = lambda w: pl.BlockSpec((Q_BLOCK, w), lambda b, i: (b * nqb + i, 0))
    seqblk = lambda w: pl.BlockSpec((seq, w), lambda b, i: (b, 0), pipeline_mode=pl.Buffered(1))
    return pl.pallas_call(
        functools.partial(_dsa_prompt_kernel, topk=topk, kc=kc, pos_bits=pos_bits),
        grid=(batch, nqb),
        in_specs=[qrow(IDX_HEADS * IDX_DIM), pl.BlockSpec((IDX_HEADS, Q_BLOCK), lambda b, i: (0, b * nqb + i)),
                  qrow(A_WIDTH), seqblk(IDX_DIM), seqblk(A_KV_WIDTH), seqblk(A_KV_WIDTH)],
        out_specs=qrow(A_WIDTH),
        out_shape=jax.ShapeDtypeStruct((batch * seq, A_WIDTH), BF16),
        scratch_shapes=[pltpu.VMEM((nchunks, kc, Q_BLOCK), F32), pltpu.VMEM((nchunks, kc, Q_BLOCK), F32),
                        pltpu.VMEM((nchunks, A_KV_WIDTH, kc), BF16),
                        pltpu.VMEM((1, Q_BLOCK), F32), pltpu.VMEM((1, Q_BLOCK), I32),
                        pltpu.VMEM((nchunks, A_KV_HEADS, kc, rep * Q_BLOCK), F32),
                        pltpu.VMEM((A_KV_HEADS, A_HEAD_DIM, rep * Q_BLOCK), F32)],
        compiler_params=_cparams(("arbitrary", "arbitrary")),
        name="dsa_prompt",
    )(qi, wit, q, ki_b, k_b, v_b)


def _fold_heads(e, nq):
    acc = e[0:8]
    for t in range(1, e.shape[0] // 8):
        acc = acc + e[8 * t:8 * (t + 1)]
    return acc + pltpu.roll(acc, nq, 0)


def _tree(op, parts):
    parts = list(parts)
    while len(parts) > 1:
        parts = [op(a, b) for a, b in zip(parts[::2], parts[1::2])] + parts[len(parts) & ~1:]
    return parts[0]


def _lane_fold(op, x):
    return _tree(op, [x[:, t * 128:(t + 1) * 128] for t in range(x.shape[1] // 128)])


def _dsa_sample_kernel(pt_ref, qp_ref, wc_ref, kin_ref, qs_ref, kn_ref, vn_ref, kidx_hbm, k_hbm, v_hbm, o_ref,
                       sc_ref, scn_ref, bias_ref, biasn_ref, t_ref, j_ref, s_ref, sn_ref, m_ref, mfin_ref, l_ref, acc_ref,
                       idx_raw, kv_raw, idx_sem, kv_sem, *, ipages, pages, npages, nq, topk, pos_bits):
    b_idx, s_idx = pl.program_id(0), pl.program_id(1)
    ikw, kw = ipages * PAGE, pages * PAGE
    isteps, steps = npages // ipages, npages // pages
    nsteps = isteps + 2 * steps
    n_past = npages * PAGE
    rows_q = qs_ref.shape[1]
    dup = lambda x8: jnp.concatenate([x8] * (rows_q // 8), axis=0)
    gsl = lambda g: slice(g * A_HEAD_DIM, (g + 1) * A_HEAD_DIM)

    n = b_idx * nsteps + s_idx
    slot = n % 2
    phases = ((kidx_hbm, idx_raw, idx_sem, 0, isteps, ipages),
              (k_hbm, kv_raw, kv_sem, isteps, steps, pages),
              (v_hbm, kv_raw, kv_sem, isteps + steps, steps, pages))

    def block_copies(seq, step, sl, start):
        for src, dst, sem, first, count, per_step in phases:
            @pl.when((step >= first) & (step < first + count))
            def _(src=src, dst=dst, sem=sem, first=first, per_step=per_step):
                for p in range(per_step):
                    page = pt_ref[seq * npages + (step - first) * per_step + p] if start else 0
                    copy = pltpu.make_async_copy(src.at[page], dst.at[sl, p], sem.at[sl])
                    copy.start() if start else copy.wait()

    @pl.when(n == 0)
    def _():
        block_copies(b_idx, s_idx, slot, True)

    @pl.when(n + 1 < pl.num_programs(0) * nsteps)
    def _():
        wrap = s_idx + 1 == nsteps
        block_copies(jnp.where(wrap, b_idx + 1, b_idx), jnp.where(wrap, 0, s_idx + 1), 1 - slot, True)

    block_copies(b_idx, s_idx, slot, False)

    def head_rows(g):
        rows = pl.ds(g, PAGE, stride=A_KV_HEADS)
        return jnp.concatenate([kv_raw[slot, p, rows, :] for p in range(pages)], axis=0).astype(BF16)

    @pl.when(s_idx < isteps)
    def _():
        qp = qp_ref[...]
        wc = wc_ref[...]
        kpt = jnp.concatenate([idx_raw[slot, p] for p in range(ipages)], axis=1).astype(BF16)
        sc_ref[s_idx] = _fold_heads(jnp.maximum(_dot(qp, kpt), 0.0) * wc, nq)

        @pl.when(s_idx == isteps - 1)
        def _():
            qrow = lax.broadcasted_iota(I32, (8, 1), 0) % nq
            lane = lax.broadcasted_iota(I32, (1, ikw), 1)
            lane_n = lax.broadcasted_iota(I32, (1, PAGE), 1)
            kin = jnp.concatenate([kin_ref[...], jnp.zeros((PAGE - S_PAD, IDX_DIM), BF16)], axis=0)
            scn = _fold_heads(jnp.maximum(_dot_nt(qp, kin), 0.0) * wc, nq)
            scn_ref[...] = jnp.where(lane_n <= qrow, scn, NEG_INF)

            def count(pred):
                ones = _tree(jnp.add, [jnp.where(pred(sc_ref[t], t * ikw + lane), 1.0, 0.0) for t in range(isteps)])
                cn = jnp.where(pred(scn_ref[...], n_past + lane_n), 1.0, 0.0)
                return (jnp.sum(_lane_fold(jnp.add, ones), axis=-1, keepdims=True)
                        + jnp.sum(cn, axis=-1, keepdims=True))

            amax = jnp.maximum(
                jnp.max(_lane_fold(jnp.maximum, _tree(jnp.maximum, [jnp.abs(sc_ref[t]) for t in range(isteps)])),
                        axis=-1, keepdims=True),
                jnp.max(jnp.abs(scn), axis=-1, keepdims=True))
            n_valid = (n_past + 1 + qrow).astype(F32)
            thr, jsel = _select_params(count, topk, pos_bits, amax, n_valid, t_ref, j_ref)
            for t in range(steps):
                first = t * kw
                sp = sc_ref[first // ikw][:, first % ikw:first % ikw + kw]
                pos = first + lane[:, :kw]
                bias_ref[t] = jnp.where((sp > thr) | ((sp == thr) & (pos <= jsel)), 0.0, NEG_INF)
            sn = scn_ref[...]
            seln = ((sn > thr) | ((sn == thr) & (n_past + lane_n <= jsel))) & (lane_n <= qrow)
            biasn_ref[...] = jnp.where(seln, 0.0, NEG_INF)

    @pl.when((s_idx >= isteps) & (s_idx < isteps + steps))
    def _():
        t = s_idx - isteps

        @pl.when(t == 0)
        def _():
            m_ref[...] = jnp.full(m_ref.shape, NEG_INF, F32)

        bias = dup(bias_ref[t])
        for g in range(A_KV_HEADS):
            s = _dot_nt(qs_ref[g], head_rows(g)) + bias
            s_ref[g, t] = s
            m_ref[g] = jnp.maximum(m_ref[g], _lane_fold(jnp.maximum, s))

        @pl.when(t == steps - 1)
        def _():
            bn = dup(biasn_ref[:, :S_PAD])
            for g in range(A_KV_HEADS):
                sn = _dot_nt(qs_ref[g], kn_ref[:, gsl(g)]) + bn
                sn_ref[g] = sn
                mfin_ref[g] = jnp.maximum(jnp.max(m_ref[g], axis=-1, keepdims=True),
                                          jnp.max(sn, axis=-1, keepdims=True))

    @pl.when(s_idx >= isteps + steps)
    def _():
        t = s_idx - isteps - steps

        @pl.when(t == 0)
        def _():
            l_ref[...] = jnp.zeros(l_ref.shape, F32)
            acc_ref[...] = jnp.zeros(acc_ref.shape, F32)

        for g in range(A_KV_HEADS):
            p = jnp.exp2(s_ref[g, t] - mfin_ref[g])
            l_ref[g] += _lane_fold(jnp.add, p)
            acc_ref[g] += _dot(p.astype(BF16), head_rows(g))

        @pl.when(t == steps - 1)
        def _():
            for g in range(A_KV_HEADS):
                pn = jnp.exp2(sn_ref[g] - mfin_ref[g])
                l = jnp.sum(l_ref[g], axis=-1, keepdims=True) + jnp.sum(pn, axis=-1, keepdims=True)
                o_ref[g] = (acc_ref[g] + _dot(pn.astype(BF16), vn_ref[:, gsl(g)])) / l


def _dsa_sample(page_flat, qp, wcol, ki_new_b, qs, k_new_b, v_new_b, cache_kidx_t, cache_k, cache_v,
                nseq, npages, ipages, pages, nq, topk):
    isteps, steps = npages // ipages, npages // pages
    n_past = npages * PAGE
    ikw, kw = ipages * PAGE, pages * PAGE
    rows = IDX_HEADS * nq
    rows_q = qs.shape[2]
    pos_bits = int(np.ceil(np.log2(n_past + PAGE)))

    seq3 = lambda shape: pl.BlockSpec((None,) + shape, lambda b, s, pt: (b,) + (0,) * len(shape))
    new_rows = lambda w: pl.BlockSpec((S_PAD, w), lambda b, s, pt: (b, 0))
    in_hbm = pl.BlockSpec(memory_space=pl.ANY)
    grid_spec = pltpu.PrefetchScalarGridSpec(
        num_scalar_prefetch=1,
        grid=(nseq, isteps + 2 * steps),
        in_specs=[seq3((rows, IDX_DIM)), seq3((rows, 1)), new_rows(IDX_DIM),
                  seq3((A_KV_HEADS, rows_q, A_HEAD_DIM)), new_rows(A_KV_WIDTH), new_rows(A_KV_WIDTH),
                  in_hbm, in_hbm, in_hbm],
        out_specs=seq3((A_KV_HEADS, rows_q, A_HEAD_DIM)),
        scratch_shapes=[pltpu.VMEM((isteps, 8, ikw), F32), pltpu.VMEM((8, PAGE), F32),
                        pltpu.VMEM((steps, 8, kw), F32), pltpu.VMEM((8, PAGE), F32),
                        pltpu.VMEM((8, 1), F32), pltpu.VMEM((8, 1), I32),
                        pltpu.VMEM((A_KV_HEADS, steps, rows_q, kw), F32),
                        pltpu.VMEM((A_KV_HEADS, rows_q, S_PAD), F32),
                        pltpu.VMEM((A_KV_HEADS, rows_q, 128), F32),
                        pltpu.VMEM((A_KV_HEADS, rows_q, 1), F32),
                        pltpu.VMEM((A_KV_HEADS, rows_q, 128), F32),
                        pltpu.VMEM((A_KV_HEADS, rows_q, A_HEAD_DIM), F32),
                        pltpu.VMEM((2, ipages, IDX_DIM, PAGE), F32),
                        pltpu.VMEM((2, pages, PAGE * A_KV_HEADS, A_HEAD_DIM), F32),
                        pltpu.SemaphoreType.DMA((2,)), pltpu.SemaphoreType.DMA((2,))],
    )
    return pl.pallas_call(
        functools.partial(_dsa_sample_kernel, ipages=ipages, pages=pages, npages=npages, nq=nq, topk=topk,
                          pos_bits=pos_bits),
        grid_spec=grid_spec,
        out_shape=jax.ShapeDtypeStruct((nseq, A_KV_HEADS, rows_q, A_HEAD_DIM), F32),
        compiler_params=_cparams(("arbitrary", "arbitrary")),
        name="dsa_sample",
    )(page_flat, qp, wcol, ki_new_b, qs, k_new_b, v_new_b, cache_kidx_t, cache_k, cache_v)


def _merge_kernel(x_ref, oa_ref, ag_ref, bu_ref, bv_ref, bg_ref, cq_ref, cg_ref, ra_ref, rb_ref, rc_ref,
                  mk_ref, mv_ref, ws_ref, bs_ref, gsgu_ref, gmq_ref, wpa_ref, wpb_ref, wpc_ref, wout_ref,
                  y_ref, *maybe_vn_ref, tm, chunk, mem_groups):
    f32 = lambda r: r[...].astype(F32)
    silu = lambda t: t * jax.nn.sigmoid(t)

    vn = _rms(f32(bv_ref), gsgu_ref[...])
    if maybe_vn_ref:
        maybe_vn_ref[0][...] = vn
    vnb = vn.astype(BF16)
    bu = f32(bu_ref)
    tril = (lax.broadcasted_iota(I32, (chunk, chunk), 1) <= lax.broadcasted_iota(I32, (chunk, chunk), 0))
    ob_cols = []
    for g in range(B_GROUPS):
        wg = jnp.where(tril, ws_ref[g], 0.0).astype(BF16)
        gsl = slice(g * B_GROUP_DIM, (g + 1) * B_GROUP_DIM)
        parts = [_dot(wg, vnb[c * chunk:(c + 1) * chunk, gsl]) + bs_ref[:, g:g + 1] for c in range(tm // chunk)]
        ob_cols.append(parts[0] if len(parts) == 1 else jnp.concatenate(parts, axis=0))
    ob = bu * jnp.concatenate(ob_cols, axis=1)
    pb = _dot((ob * silu(f32(bg_ref))).astype(BF16), wpb_ref[...])

    cq = f32(cq_ref)
    rows_g = tm // mem_groups
    oc_cols = []
    for hh in range(M_HEADS):
        hsl = slice(hh * M_HEAD_DIM, (hh + 1) * M_HEAD_DIM)
        qn = (_rms(cq[:, hsl], gmq_ref[...]) * (M_HEAD_DIM ** -0.5)).astype(BF16)
        oc_rows = []
        for u in range(mem_groups):
            msl = slice(u * N_MEM, (u + 1) * N_MEM)
            s = _dot_nt(qn[u * rows_g:(u + 1) * rows_g], mk_ref[msl, hsl])
            p = jnp.exp(s - jnp.max(s, axis=-1, keepdims=True))
            oc_rows.append(_dot(p.astype(BF16), mv_ref[msl, hsl]) / jnp.sum(p, axis=-1, keepdims=True))
        oc_cols.append(oc_rows[0] if mem_groups == 1 else jnp.concatenate(oc_rows, axis=0))
    oc = jnp.concatenate(oc_cols, axis=1)
    pc = _dot((oc * silu(f32(cg_ref))).astype(BF16), wpc_ref[...])

    pa = _dot((f32(oa_ref) * silu(f32(ag_ref))).astype(BF16), wpa_ref[...])
    sig = jax.nn.sigmoid
    m = sig(f32(ra_ref)) * pa + sig(f32(rb_ref)) * pb + sig(f32(rc_ref)) * pc
    y_ref[...] = x_ref[...] + _dot(m.astype(BF16), wout_ref[...])


def _merge(x, oa, zr, mk_b, mv_b, ws, bs_t, g_sgu, g_mq, w_pa, w_pb, w_pc, w_out, tm, chunk, mem_groups, mem_map,
           emit_vn):
    n = x.shape[0]
    col = lambda w, j: pl.BlockSpec((tm, w), lambda i, j=j: (i, j))
    mem = pl.BlockSpec((mem_groups * N_MEM, M_WIDTH), mem_map)
    in_specs = [col(D_MODEL, 0), col(A_WIDTH, 0),
                col(1024, 0), col(1024, 1), col(1024, 2), col(1024, 3), col(1024, 4), col(1024, 5),
                col(2048, 3), col(2048, 4), col(2048, 5),
                mem, mem,
                _const_spec((B_GROUPS, chunk, chunk)), _const_spec((chunk, B_GROUPS)),
                _const_spec((1, B_WIDTH)), _const_spec((1, M_HEAD_DIM)),
                _const_spec((A_WIDTH, D_MODEL)), _const_spec((B_WIDTH, D_MODEL)),
                _const_spec((M_WIDTH, D_MODEL)), _const_spec((D_MODEL, D_MODEL))]
    out_specs = [col(D_MODEL, 0)]
    out_shape = [jax.ShapeDtypeStruct((n, D_MODEL), F32)]
    if emit_vn:
        out_specs.append(col(B_WIDTH, 0))
        out_shape.append(jax.ShapeDtypeStruct((n, B_WIDTH), F32))
    return pl.pallas_call(
        functools.partial(_merge_kernel, tm=tm, chunk=chunk, mem_groups=mem_groups),
        grid=(n // tm,),
        in_specs=in_specs,
        out_specs=out_specs,
        out_shape=out_shape,
        compiler_params=_cparams(("arbitrary",)),
        name="merge",
    )(x, oa, *([zr] * 9), mk_b, mv_b, ws, bs_t, g_sgu, g_mq, w_pa, w_pb, w_pc, w_out)


def _rope_tables(pos):
    freq = lambda half: ROPE_THETA ** (-jnp.arange(half, dtype=F32) / half)
    f_idx = freq(IDX_DIM // 2)
    ang = pos.astype(F32)[:, None] * jnp.concatenate([freq(A_HEAD_DIM // 2), f_idx, f_idx])[None, :]
    return jnp.cos(ang), jnp.sin(ang)


def kernel(x_prompt, x_sample, cache_k, cache_v, cache_kidx, cache_mem_k, cache_mem_v, page_table,
           mem_prompt, g_pre, w_in, g_q, g_k, g_mq, g_mk, g_mem, w_mem_kv, g_sgu, w_s, b_s,
           w_pa, w_pb, w_pc, w_out):
    batch, seq, _ = x_prompt.shape
    nseq, nq, _ = x_sample.shape
    assert nq == 4 and nq <= S_PAD
    npages = page_table.shape[1]
    n_past = npages * PAGE
    n_pool = cache_k.shape[0]
    row2 = lambda a: a.reshape(1, -1)

    w_in_t = w_in.T
    w_a = w_in_t[:A_COLS].astype(BF16)
    w_pa_b, w_pb_b, w_pc_b, w_out_b = (w.astype(BF16) for w in (w_pa, w_pb, w_pc, w_out))
    w_mem_b = w_mem_kv.astype(BF16)
    g_pre2, g_q2, g_k2, g_mq2, g_mk2, g_mem2, g_sgu2 = map(row2, (g_pre, g_q, g_k, g_mq, g_mk, g_mem, g_sgu))

    xp = x_prompt.reshape(batch * seq, D_MODEL)
    tm_a = min(256, seq)
    nblk = seq // tm_a
    tabs_p = _rope_tables(jnp.arange(seq))
    q, k_p, v_p, k_b, v_b, qi, ki_p, ki_b, wi = _proj_a(
        xp, g_pre2, w_a, g_q2, g_k2, tabs_p, tm_a, lambda i: (i % nblk, 0))
    xs = jnp.pad(x_sample, ((0, 0), (0, S_PAD - nq), (0, 0))).reshape(nseq * S_PAD, D_MODEL)
    rows_s = nseq * S_PAD
    zr, zr_s = _proj_rest(xp, xs, g_pre2, w_in_t, min(1024, seq), PROJ_TN)
    mk_p, mv_p, mk_b, mv_b = _mem_kv(mem_prompt.reshape(batch * N_MEM, D_MODEL), g_mem2, w_mem_b, g_mk2)
    oa = _dsa_prompt(qi, wi.T, q, ki_b, k_b, v_b, batch, seq, min(TOPK_MAX, seq // 4))
    tm_m = min(256, seq)
    nblk_m = seq // tm_m
    (y_p,) = _merge(xp, oa, zr, mk_b, mv_b, w_s, b_s.T, g_sgu2, g_mq2, w_pa_b, w_pb_b, w_pc_b, w_out_b,
                    tm_m, CHUNK, 1, lambda i: (i // nblk_m, 0), False)

    tabs_s = tuple(jnp.tile(t, (nseq, 1)) for t in _rope_tables(n_past + jnp.arange(S_PAD)))
    q_s, k_s, v_s, k_sb, v_sb, qi_s, ki_s, ki_sb, wi_s = _proj_a(
        xs, g_pre2, w_a, g_q2, g_k2, tabs_s, rows_s, lambda i: (0, 0))

    qp = (qi_s.reshape(nseq, S_PAD, IDX_HEADS, IDX_DIM)[:, :nq]
          .transpose(0, 2, 1, 3).reshape(nseq, IDX_HEADS * nq, IDX_DIM))
    wcol = wi_s.reshape(nseq, S_PAD, IDX_HEADS)[:, :nq].transpose(0, 2, 1).reshape(nseq, IDX_HEADS * nq, 1)
    rep = A_HEADS // A_KV_HEADS
    qs = (q_s.reshape(nseq, S_PAD, A_KV_HEADS, rep, A_HEAD_DIM)[:, :nq]
          .transpose(0, 2, 3, 1, 4).reshape(nseq, A_KV_HEADS, rep * nq, A_HEAD_DIM))
    qs = jnp.concatenate([qs, qs], axis=2)
    page_flat = page_table.reshape(-1)
    pages, ipages = min(SAMPLE_PAGES, npages), min(SAMPLE_IDX_PAGES, npages)
    o_s = _dsa_sample(page_flat, qp, wcol, ki_sb, qs, k_sb, v_sb, jnp.swapaxes(cache_kidx, 1, 2),
                      cache_k.reshape(n_pool, PAGE * A_KV_HEADS, A_HEAD_DIM),
                      cache_v.reshape(n_pool, PAGE * A_KV_HEADS, A_HEAD_DIM),
                      nseq, npages, ipages, pages, nq, min(TOPK_MAX, (n_past + nq) // 4))
    oa_s = (o_s[:, :, :rep * nq].reshape(nseq, A_KV_HEADS, rep, nq, A_HEAD_DIM)
            .transpose(0, 3, 1, 2, 4).reshape(nseq, nq, A_WIDTH))
    oa_s = jnp.pad(oa_s, ((0, 0), (0, S_PAD - nq), (0, 0))).reshape(rows_s, A_WIDTH).astype(BF16)
    mk_s = cache_mem_k.reshape(nseq * N_MEM, M_WIDTH).astype(BF16)
    mv_s = cache_mem_v.reshape(nseq * N_MEM, M_WIDTH).astype(BF16)
    y_s, vn_s = _merge(xs, oa_s, zr_s, mk_s, mv_s, w_s[:, :S_PAD, :S_PAD], b_s[:, :S_PAD].T, g_sgu2, g_mq2,
                       w_pa_b, w_pb_b, w_pc_b, w_out_b, rows_s, S_PAD, nseq, lambda i: (0, 0), True)

    take = lambda a, shape: a.reshape(nseq, S_PAD, -1)[:, :nq].reshape(shape)
    return (y_p.reshape(batch, seq, D_MODEL),
            take(y_s, (nseq, nq, D_MODEL)),
            k_p.reshape(batch, seq, A_KV_HEADS, A_HEAD_DIM),
            v_p.reshape(batch, seq, A_KV_HEADS, A_HEAD_DIM),
            ki_p.reshape(batch, seq, IDX_DIM),
            mk_p.reshape(batch, N_MEM, M_HEADS, M_HEAD_DIM),
            mv_p.reshape(batch, N_MEM, M_HEADS, M_HEAD_DIM),
            take(k_s, (nseq, nq, A_KV_HEADS, A_HEAD_DIM)),
            take(v_s, (nseq, nq, A_KV_HEADS, A_HEAD_DIM)),
            take(ki_s, (nseq, nq, IDX_DIM)),
            take(vn_s, (nseq, nq, B_GROUPS, B_GROUP_DIM)))
```

```python
import functools

import numpy as np
import jax
import jax.numpy as jnp
from jax import lax
from jax.experimental import pallas as pl
from jax.experimental.pallas import tpu as pltpu

F32 = jnp.float32
BF16 = jnp.bfloat16
I32 = jnp.int32

D_MODEL = 2048
PAGE = 128
A_HEADS = 8
A_KV_HEADS = 4
A_HEAD_DIM = 128
A_WIDTH = A_HEADS * A_HEAD_DIM
A_KV_WIDTH = A_KV_HEADS * A_HEAD_DIM
IDX_HEADS = 16
IDX_DIM = 64
TOPK_MAX = 256
Q_BLOCK = 128
ROPE_THETA = 10000.0
CHUNK = 128
B_GROUPS = 8
B_GROUP_DIM = 128
B_WIDTH = B_GROUPS * B_GROUP_DIM
N_MEM = 256
M_HEADS = 4
M_HEAD_DIM = 256
M_WIDTH = M_HEADS * M_HEAD_DIM
EPS = 1e-6

OFF_K = A_WIDTH
OFF_V = OFF_K + A_KV_WIDTH
OFF_QI = OFF_V + A_KV_WIDTH
OFF_KI = OFF_QI + IDX_HEADS * IDX_DIM
OFF_WI = OFF_KI + IDX_DIM
OFF_REST = OFF_WI + IDX_HEADS
A_COLS = 3200
REST_COLS = A_WIDTH + 3 * B_WIDTH + 2 * M_WIDTH + 3 * D_MODEL

Q_SCALE = float(np.log2(np.e)) * A_HEAD_DIM ** -0.5
S_PAD = 16
PROJ_TN = 1024
BISECT_STEPS = 24
SAMPLE_PAGES = 16
SAMPLE_IDX_PAGES = 64
INT_MIN = np.int32(-2 ** 31)
INT_MAX = np.int32(2 ** 31 - 1)
NEG_INF = float("-inf")

V7X_VMEM_LIMIT = 56 * 1024 * 1024


def _cparams(sem):
    return pltpu.CompilerParams(dimension_semantics=sem, vmem_limit_bytes=V7X_VMEM_LIMIT)


def _dot(a, b):
    return jnp.dot(a, b, preferred_element_type=F32)


def _dot_nt(a, b):
    return lax.dot_general(a, b, (((1,), (1,)), ((), ())), preferred_element_type=F32)


def _rms(x, g):
    return x * lax.rsqrt(jnp.mean(x * x, axis=-1, keepdims=True) + EPS) * g


def _const_spec(shape):
    nd = len(shape)
    return pl.BlockSpec(shape, lambda *_: (0,) * nd, pipeline_mode=pl.Buffered(1))


def _proj_a_kernel(x_ref, g_ref, w_ref, gq_ref, gk_ref, cos_ref, sin_ref,
                   q_ref, k_ref, v_ref, kb_ref, vb_ref, qi_ref, ki_ref, kib_ref, wi_ref):
    h = _rms(x_ref[...], g_ref[...]).astype(BF16)
    z = _dot_nt(h, w_ref[...])
    tm = z.shape[0]
    c, s = cos_ref[...], sin_ref[...]
    c_sw, s_sw = pltpu.roll(c, 64, 1), pltpu.roll(s, 64, 1)
    lane = lax.broadcasted_iota(I32, (1, 128), 1)
    low = lane < 64
    cq, sq = jnp.where(low, c, c_sw), jnp.where(low, -s, s_sw)
    ci, si = jnp.where(low, c_sw, c), jnp.where(low, s_sw, s)
    first_half = lane % IDX_DIM < IDX_DIM // 2
    sia, sib = jnp.where(first_half, -si, 0.0), jnp.where(first_half, 0.0, si)

    def norm_rope(zz, g):
        n = _rms(zz, g)
        return n * cq + pltpu.roll(n, A_HEAD_DIM // 2, 1) * sq

    def rope_idx(zz):
        return zz * ci + pltpu.roll(zz, 96, 1) * sia + pltpu.roll(zz, 32, 1) * sib

    for hh in range(A_HEADS):
        sl = slice(hh * A_HEAD_DIM, (hh + 1) * A_HEAD_DIM)
        q_ref[:, sl] = (norm_rope(z[:, sl], gq_ref[...]) * Q_SCALE).astype(BF16)
    for hh in range(A_KV_HEADS):
        sl = slice(hh * A_HEAD_DIM, (hh + 1) * A_HEAD_DIM)
        kh = norm_rope(z[:, OFF_K + hh * A_HEAD_DIM:OFF_K + (hh + 1) * A_HEAD_DIM], gk_ref[...])
        vh = z[:, OFF_V + hh * A_HEAD_DIM:OFF_V + (hh + 1) * A_HEAD_DIM]
        head_rows = pl.ds(hh, tm, stride=A_KV_HEADS)
        k_ref[head_rows, :] = kh
        v_ref[head_rows, :] = vh
        kb_ref[:, sl] = kh.astype(BF16)
        vb_ref[:, sl] = vh.astype(BF16)
    for t in range(IDX_HEADS * IDX_DIM // 128):
        sl = slice(t * 128, (t + 1) * 128)
        qi_ref[:, sl] = rope_idx(z[:, OFF_QI + t * 128:OFF_QI + (t + 1) * 128]).astype(BF16)
    last = z[:, OFF_KI:OFF_KI + 128]
    ki = rope_idx(last)[:, :IDX_DIM]
    ki_ref[...] = ki
    kib_ref[...] = ki.astype(BF16)
    wi_ref[...] = last[:, IDX_DIM:IDX_DIM + IDX_HEADS] * ((IDX_HEADS ** -0.5) * (IDX_DIM ** -0.5))


def _proj_a(x, g_pre, w_a, g_q, g_k, tabs, tm, tab_map):
    n = x.shape[0]
    row = lambda w: pl.BlockSpec((tm, w), lambda i: (i, 0))
    tab = pl.BlockSpec((tm, 128), tab_map)
    outs = [(1, A_WIDTH, BF16), (A_KV_HEADS, A_HEAD_DIM, F32), (A_KV_HEADS, A_HEAD_DIM, F32),
            (1, A_KV_WIDTH, BF16), (1, A_KV_WIDTH, BF16),
            (1, IDX_HEADS * IDX_DIM, BF16), (1, IDX_DIM, F32), (1, IDX_DIM, BF16), (1, IDX_HEADS, F32)]
    return pl.pallas_call(
        _proj_a_kernel,
        grid=(n // tm,),
        in_specs=[row(D_MODEL), _const_spec((1, D_MODEL)), _const_spec((A_COLS, D_MODEL)),
                  _const_spec((1, A_HEAD_DIM)), _const_spec((1, A_HEAD_DIM)), tab, tab],
        out_specs=[pl.BlockSpec((tm * r, w), lambda i: (i, 0)) for r, w, _ in outs],
        out_shape=[jax.ShapeDtypeStruct((n * r, w), dt) for r, w, dt in outs],
        compiler_params=_cparams(("arbitrary",)),
        name="proj_a",
    )(x, g_pre, w_a, g_q, g_k, *tabs)


def _proj_rest_kernel(x_ref, xs_ref, g_ref, w_ref, o_ref, os_ref, h_ref, hs_ref):
    i, j = pl.program_id(0), pl.program_id(1)

    @pl.when(j == 0)
    def _():
        h_ref[...] = _rms(x_ref[...], g_ref[...]).astype(BF16)

    @pl.when((i == 0) & (j == 0))
    def _():
        hs_ref[...] = _rms(xs_ref[...], g_ref[...]).astype(BF16)

    w = w_ref[...].astype(BF16)
    o_ref[...] = _dot_nt(h_ref[...], w).astype(BF16)

    @pl.when(i == 0)
    def _():
        os_ref[...] = _dot_nt(hs_ref[...], w).astype(BF16)


def _proj_rest(x, xs, g_pre, w_in_t, tm, tn):
    n, ns = x.shape[0], xs.shape[0]
    ncols = REST_COLS // tn
    xs_cols = lambda i, j: (0, jnp.where(i == 0, j, ncols - 1))
    return pl.pallas_call(
        _proj_rest_kernel,
        grid=(n // tm, ncols),
        in_specs=[pl.BlockSpec((tm, D_MODEL), lambda i, j: (i, 0)),
                  pl.BlockSpec((ns, D_MODEL), lambda i, j: (0, 0), pipeline_mode=pl.Buffered(1)),
                  pl.BlockSpec((1, D_MODEL), lambda i, j: (0, 0)),
                  pl.BlockSpec((pl.Element(tn), pl.Element(D_MODEL)),
                               lambda i, j: (pl.multiple_of(OFF_REST + j * tn, 16), 0))],
        out_specs=[pl.BlockSpec((tm, tn), lambda i, j: (i, j)), pl.BlockSpec((ns, tn), xs_cols)],
        out_shape=[jax.ShapeDtypeStruct((n, REST_COLS), BF16), jax.ShapeDtypeStruct((ns, REST_COLS), BF16)],
        scratch_shapes=[pltpu.VMEM((tm, D_MODEL), BF16), pltpu.VMEM((ns, D_MODEL), BF16)],
        compiler_params=_cparams(("arbitrary", "arbitrary")),
        name="proj_rest",
    )(x, xs, g_pre, w_in_t)


def _mem_kv_kernel(x_ref, g_ref, w_ref, gk_ref, k_ref, v_ref, kb_ref, vb_ref):
    h = _rms(x_ref[...], g_ref[...]).astype(BF16)
    z = _dot(h, w_ref[...])
    for hh in range(M_HEADS):
        sl = slice(hh * M_HEAD_DIM, (hh + 1) * M_HEAD_DIM)
        kh = _rms(z[:, sl], gk_ref[...])
        k_ref[:, sl] = kh
        kb_ref[:, sl] = kh.astype(BF16)
    v = z[:, M_WIDTH:]
    v_ref[...] = v
    vb_ref[...] = v.astype(BF16)


def _mem_kv(mem, g_mem, w_mem, g_mk):
    n = mem.shape[0]
    blk = pl.BlockSpec((N_MEM, M_WIDTH), lambda i: (i, 0))
    return pl.pallas_call(
        _mem_kv_kernel,
        grid=(n // N_MEM,),
        in_specs=[pl.BlockSpec((N_MEM, D_MODEL), lambda i: (i, 0)), _const_spec((1, D_MODEL)),
                  _const_spec((D_MODEL, 2 * M_WIDTH)), _const_spec((1, M_HEAD_DIM))],
        out_specs=[blk, blk, blk, blk],
        out_shape=[jax.ShapeDtypeStruct((n, M_WIDTH), dt) for dt in (F32, F32, BF16, BF16)],
        compiler_params=_cparams(("arbitrary",)),
        name="mem_kv",
    )(mem, g_mem, w_mem, g_mk)


KEY_NEG_INF = np.int32(-0x7F800000)


def _key_to_f32(key):
    return pltpu.bitcast(jnp.where(key >= 0, key, INT_MIN - key), F32)


def _select_params(count, topk, pos_bits, amax, n_valid, t_ref, j_ref):
    kf = float(topk)
    hi0 = amax * 1.000001 + 1e-30
    all_selected = (n_valid <= kf).astype(I32)

    def bisect_body(_, state):
        lo, hi, t, done = state
        mid = 0.5 * lo + 0.5 * hi
        n_ge = count(lambda s, p: s >= mid)
        hit = (n_ge == kf) & (done == 0)
        return (jnp.where(n_ge >= kf, mid, lo), jnp.where(n_ge >= kf, hi, mid),
                jnp.where(hit, mid, t), jnp.where(hit, 1, done))

    _, _, t_bis, done = lax.fori_loop(
        0, BISECT_STEPS, bisect_body, (-hi0, hi0, jnp.full(hi0.shape, NEG_INF, F32), all_selected))
    t_ref[...] = t_bis
    j_ref[...] = jnp.full(j_ref.shape, INT_MAX, I32)

    @pl.when(jnp.min(done) == 0)
    def _():
        t0 = jnp.where(count(lambda s, p: s >= 0.0) >= kf, jnp.int32(0), INT_MIN)

        def bit_body(b, t):
            cand = t + lax.shift_left(jnp.int32(1), 30 - b)
            cand_f = _key_to_f32(cand)
            return jnp.where(count(lambda s, p: s >= cand_f) >= kf, cand, t)

        t = _key_to_f32(jnp.maximum(lax.fori_loop(0, 31, bit_body, t0), KEY_NEG_INF))
        t_ref[...] = t
        tie = (count(lambda s, p: s >= t) > kf) & (t > NEG_INF)

        @pl.when(jnp.max(tie.astype(I32)) > 0)
        def _():
            n_gt = count(lambda s, p: s > t)

            def pos_body(b, p_lo):
                cand = p_lo + lax.shift_left(jnp.int32(1), pos_bits - 1 - b)
                n_eq = count(lambda s, p: (s == t) & (p < cand))
                return jnp.where(n_gt + n_eq < kf, cand, p_lo)

            p_sel = lax.fori_loop(0, pos_bits, pos_body, jnp.zeros(t.shape, I32))
            j_ref[...] = jnp.where(tie, p_sel, INT_MAX)

    return jnp.maximum(t_ref[...], -hi0), j_ref[...]


def _fori_by_two(n, body, init):
    carry = lax.fori_loop(0, n // 2, lambda j, c: body(2 * j + 1, body(2 * j, c)), init)
    return lax.cond(n % 2 == 1, lambda c: body(n - 1, c), lambda c: c, carry)


def _dsa_prompt_kernel(qi_ref, wit_ref, q_ref, ki_ref, k_ref, v_ref, o_ref,
                       sc_ref, bias_ref, vt_ref, t_ref, j_ref, s_ref, acc_ref, *, topk, kc, pos_bits):
    i = pl.program_id(1)
    nck = (i * Q_BLOCK + Q_BLOCK + kc - 1) // kc
    nchunks = vt_ref.shape[0]
    rep = A_HEADS // A_KV_HEADS
    q_pos = i * Q_BLOCK + lax.broadcasted_iota(I32, (1, Q_BLOCK), 1)
    sub = lax.broadcasted_iota(I32, (kc, 1), 0)

    @pl.when(i == 0)
    def _():
        for c in range(nchunks):
            for g in range(A_KV_HEADS):
                gsl = slice(g * A_HEAD_DIM, (g + 1) * A_HEAD_DIM)
                vt_ref[c, gsl, :] = v_ref[c * kc:(c + 1) * kc, gsl].astype(F32).T.astype(BF16)

    def chunk_rows(c):
        return pl.ds(pl.multiple_of(c * kc, kc), kc)

    qi = qi_ref[...]
    wit = wit_ref[...]
    qi_pairs = [jnp.concatenate([qi[:, (2 * j) * IDX_DIM:(2 * j + 1) * IDX_DIM],
                                 qi[:, (2 * j + 1) * IDX_DIM:(2 * j + 2) * IDX_DIM]], axis=0)
                for j in range(IDX_HEADS // 2)]

    def score_body(c, amax):
        kic = ki_ref[chunk_rows(c), :]
        acc = jnp.zeros((kc, Q_BLOCK), F32)
        for j in range(IDX_HEADS // 2):
            d = _dot_nt(kic, qi_pairs[j])
            acc = acc + jnp.maximum(d[:, :Q_BLOCK], 0.0) * wit[2 * j:2 * j + 1, :]
            acc = acc + jnp.maximum(d[:, Q_BLOCK:], 0.0) * wit[2 * j + 1:2 * j + 2, :]
        sc_ref[c] = jnp.where(c * kc + sub <= q_pos, acc, NEG_INF)
        return jnp.maximum(amax, jnp.max(jnp.abs(acc).reshape(kc // 64, 64, Q_BLOCK), axis=0))

    amax = jnp.max(_fori_by_two(nck, score_body, jnp.zeros((64, Q_BLOCK), F32)), axis=0, keepdims=True)

    def count(pred):
        def body(c, acc):
            part = jnp.where(pred(sc_ref[c], c * kc + sub), 1.0, 0.0)
            return acc + jnp.sum(part.reshape(kc // 64, 64, Q_BLOCK), axis=0)

        acc = lax.fori_loop(0, nck, body, jnp.zeros((64, Q_BLOCK), F32))
        return jnp.sum(acc, axis=0, keepdims=True)

    thr, jsel = _select_params(count, topk, pos_bits, amax, (q_pos + 1).astype(F32), t_ref, j_ref)

    def bias_body(c, carry):
        s = sc_ref[c]
        tie_thr = jnp.where(c * kc + sub <= jsel, thr, jnp.inf)
        bias_ref[c] = jnp.where((s > thr) | (s >= tie_thr), 0.0, NEG_INF)
        return carry

    lax.fori_loop(0, nck, bias_body, 0)

    gsl = lambda g: slice(g * A_HEAD_DIM, (g + 1) * A_HEAD_DIM)
    row0 = lambda v: jnp.full((1, rep * Q_BLOCK), v, F32)
    groups = tuple(range(A_KV_HEADS))
    qgs = [jnp.concatenate([q_ref[:, gsl(g * rep + r)] for r in range(rep)], axis=0) for g in groups]


    def qk_body(c, ms):
        b = bias_ref[c]
        bias = jnp.concatenate([b] * rep, axis=1)
        out = []
        for g in groups:
            s = _dot_nt(k_ref[chunk_rows(c), gsl(g)], qgs[g]) + bias
            s_ref[c, g] = s
            out.append(jnp.maximum(ms[g], jnp.max(s, axis=0, keepdims=True)))
        return tuple(out)

    ms = _fori_by_two(nck, qk_body, (row0(NEG_INF),) * len(groups))
    acc_ref[...] = jnp.zeros(acc_ref.shape, F32)

    def pv_body(c, ls):
        out = []
        for g in groups:
            p = jnp.exp2(s_ref[c, g] - ms[g])
            out.append(ls[g] + jnp.sum(p, axis=0, keepdims=True))
            acc_ref[g] += _dot(vt_ref[c, gsl(g), :], p.astype(BF16))
        return tuple(out)

    ls = _fori_by_two(nck, pv_body, (row0(0.0),) * len(groups))
    for g in groups:
        o = acc_ref[g] / ls[g]
        for r in range(rep):
            o_ref[:, gsl(g * rep + r)] = o[:, r * Q_BLOCK:(r + 1) * Q_BLOCK].T.astype(BF16)


def _dsa_prompt(qi, wit, q, ki_b, k_b, v_b, batch, seq, topk):
    nqb = seq // Q_BLOCK
    rep = A_HEADS // A_KV_HEADS
    kc = min(512, seq)
    nchunks = seq // kc
    pos_bits = max(1, int(np.ceil(np.log2(seq))))
    qrow = lambda w: pl.BlockSpec((Q_BLOCK, w), lambda b, i: (b * nqb + i, 0))
    seqblk = lambda w: pl.BlockSpec((seq, w), lambda b, i: (b, 0), pipeline_mode=pl.Buffered(1))
    return pl.pallas_call(
        functools.partial(_dsa_prompt_kernel, topk=topk, kc=kc, pos_bits=pos_bits),
        grid=(batch, nqb),
        in_specs=[qrow(IDX_HEADS * IDX_DIM), pl.BlockSpec((IDX_HEADS, Q_BLOCK), lambda b, i: (0, b * nqb + i)),
                  qrow(A_WIDTH), seqblk(IDX_DIM), seqblk(A_KV_WIDTH), seqblk(A_KV_WIDTH)],
        out_specs=qrow(A_WIDTH),
        out_shape=jax.ShapeDtypeStruct((batch * seq, A_WIDTH), BF16),
        scratch_shapes=[pltpu.VMEM((nchunks, kc, Q_BLOCK), F32), pltpu.VMEM((nchunks, kc, Q_BLOCK), F32),
                        pltpu.VMEM((nchunks, A_KV_WIDTH, kc), BF16),
                        pltpu.VMEM((1, Q_BLOCK), F32), pltpu.VMEM((1, Q_BLOCK), I32),
                        pltpu.VMEM((nchunks, A_KV_HEADS, kc, rep * Q_BLOCK), F32),
                        pltpu.VMEM((A_KV_HEADS, A_HEAD_DIM, rep * Q_BLOCK), F32)],
        compiler_params=_cparams(("arbitrary", "arbitrary")),
        name="dsa_prompt",
    )(qi, wit, q, ki_b, k_b, v_b)


def _fold_heads(e, nq):
    acc = e[0:8]
    for t in range(1, e.shape[0] // 8):
        acc = acc + e[8 * t:8 * (t + 1)]
    return acc + pltpu.roll(acc, nq, 0)


def _tree(op, parts):
    parts = list(parts)
    while len(parts) > 1:
        parts = [op(a, b) for a, b in zip(parts[::2], parts[1::2])] + parts[len(parts) & ~1:]
    return parts[0]


def _lane_fold(op, x):
    return _tree(op, [x[:, t * 128:(t + 1) * 128] for t in range(x.shape[1] // 128)])


def _dsa_sample_kernel(pt_ref, qp_ref, wc_ref, kin_ref, qs_ref, kn_ref, vn_ref, kidx_hbm, k_hbm, v_hbm, o_ref,
                       sc_ref, scn_ref, bias_ref, biasn_ref, t_ref, j_ref, s_ref, sn_ref, m_ref, mfin_ref, l_ref, acc_ref,
                       idx_raw, kv_raw, idx_sem, kv_sem, *, ipages, pages, npages, nq, topk, pos_bits):
    b_idx, s_idx = pl.program_id(0), pl.program_id(1)
    ikw, kw = ipages * PAGE, pages * PAGE
    isteps, steps = npages // ipages, npages // pages
    nsteps = isteps + 2 * steps
    n_past = npages * PAGE
    rows_q = qs_ref.shape[1]
    dup = lambda x8: jnp.concatenate([x8] * (rows_q // 8), axis=0)
    gsl = lambda g: slice(g * A_HEAD_DIM, (g + 1) * A_HEAD_DIM)

    n = b_idx * nsteps + s_idx
    slot = n % 2
    phases = ((kidx_hbm, idx_raw, idx_sem, 0, isteps, ipages),
              (k_hbm, kv_raw, kv_sem, isteps, steps, pages),
              (v_hbm, kv_raw, kv_sem, isteps + steps, steps, pages))

    def block_copies(seq, step, sl, start):
        for src, dst, sem, first, count, per_step in phases:
            @pl.when((step >= first) & (step < first + count))
            def _(src=src, dst=dst, sem=sem, first=first, per_step=per_step):
                for p in range(per_step):
                    page = pt_ref[seq * npages + (step - first) * per_step + p] if start else 0
                    copy = pltpu.make_async_copy(src.at[page], dst.at[sl, p], sem.at[sl])
                    copy.start() if start else copy.wait()

    @pl.when(n == 0)
    def _():
        block_copies(b_idx, s_idx, slot, True)

    @pl.when(n + 1 < pl.num_programs(0) * nsteps)
    def _():
        wrap = s_idx + 1 == nsteps
        block_copies(jnp.where(wrap, b_idx + 1, b_idx), jnp.where(wrap, 0, s_idx + 1), 1 - slot, True)

    block_copies(b_idx, s_idx, slot, False)

    def head_rows(g):
        rows = pl.ds(g, PAGE, stride=A_KV_HEADS)
        return jnp.concatenate([kv_raw[slot, p, rows, :] for p in range(pages)], axis=0).astype(BF16)

    @pl.when(s_idx < isteps)
    def _():
        qp = qp_ref[...]
        wc = wc_ref[...]
        kpt = jnp.concatenate([idx_raw[slot, p] for p in range(ipages)], axis=1).astype(BF16)
        sc_ref[s_idx] = _fold_heads(jnp.maximum(_dot(qp, kpt), 0.0) * wc, nq)

        @pl.when(s_idx == isteps - 1)
        def _():
            qrow = lax.broadcasted_iota(I32, (8, 1), 0) % nq
            lane = lax.broadcasted_iota(I32, (1, ikw), 1)
            lane_n = lax.broadcasted_iota(I32, (1, PAGE), 1)
            kin = jnp.concatenate([kin_ref[...], jnp.zeros((PAGE - S_PAD, IDX_DIM), BF16)], axis=0)
            scn = _fold_heads(jnp.maximum(_dot_nt(qp, kin), 0.0) * wc, nq)
            scn_ref[...] = jnp.where(lane_n <= qrow, scn, NEG_INF)

            def count(pred):
                ones = _tree(jnp.add, [jnp.where(pred(sc_ref[t], t * ikw + lane), 1.0, 0.0) for t in range(isteps)])
                cn = jnp.where(pred(scn_ref[...], n_past + lane_n), 1.0, 0.0)
                return (jnp.sum(_lane_fold(jnp.add, ones), axis=-1, keepdims=True)
                        + jnp.sum(cn, axis=-1, keepdims=True))

            amax = jnp.maximum(
                jnp.max(_lane_fold(jnp.maximum, _tree(jnp.maximum, [jnp.abs(sc_ref[t]) for t in range(isteps)])),
                        axis=-1, keepdims=True),
                jnp.max(jnp.abs(scn), axis=-1, keepdims=True))
            n_valid = (n_past + 1 + qrow).astype(F32)
            thr, jsel = _select_params(count, topk, pos_bits, amax, n_valid, t_ref, j_ref)
            for t in range(steps):
                first = t * kw
                sp = sc_ref[first // ikw][:, first % ikw:first % ikw + kw]
                pos = first + lane[:, :kw]
                bias_ref[t] = jnp.where((sp > thr) | ((sp == thr) & (pos <= jsel)), 0.0, NEG_INF)
            sn = scn_ref[...]
            seln = ((sn > thr) | ((sn == thr) & (n_past + lane_n <= jsel))) & (lane_n <= qrow)
            biasn_ref[...] = jnp.where(seln, 0.0, NEG_INF)

    @pl.when((s_idx >= isteps) & (s_idx < isteps + steps))
    def _():
        t = s_idx - isteps

        @pl.when(t == 0)
        def _():
            m_ref[...] = jnp.full(m_ref.shape, NEG_INF, F32)

        bias = dup(bias_ref[t])
        for g in range(A_KV_HEADS):
            s = _dot_nt(qs_ref[g], head_rows(g)) + bias
            s_ref[g, t] = s
            m_ref[g] = jnp.maximum(m_ref[g], _lane_fold(jnp.maximum, s))

        @pl.when(t == steps - 1)
        def _():
            bn = dup(biasn_ref[:, :S_PAD])
            for g in range(A_KV_HEADS):
                sn = _dot_nt(qs_ref[g], kn_ref[:, gsl(g)]) + bn
                sn_ref[g] = sn
                mfin_ref[g] = jnp.maximum(jnp.max(m_ref[g], axis=-1, keepdims=True),
                                          jnp.max(sn, axis=-1, keepdims=True))

    @pl.when(s_idx >= isteps + steps)
    def _():
        t = s_idx - isteps - steps

        @pl.when(t == 0)
        def _():
            l_ref[...] = jnp.zeros(l_ref.shape, F32)
            acc_ref[...] = jnp.zeros(acc_ref.shape, F32)

        for g in range(A_KV_HEADS):
            p = jnp.exp2(s_ref[g, t] - mfin_ref[g])
            l_ref[g] += _lane_fold(jnp.add, p)
            acc_ref[g] += _dot(p.astype(BF16), head_rows(g))

        @pl.when(t == steps - 1)
        def _():
            for g in range(A_KV_HEADS):
                pn = jnp.exp2(sn_ref[g] - mfin_ref[g])
                l = jnp.sum(l_ref[g], axis=-1, keepdims=True) + jnp.sum(pn, axis=-1, keepdims=True)
                o_ref[g] = (acc_ref[g] + _dot(pn.astype(BF16), vn_ref[:, gsl(g)])) / l


def _dsa_sample(page_flat, qp, wcol, ki_new_b, qs, k_new_b, v_new_b, cache_kidx_t, cache_k, cache_v,
                nseq, npages, ipages, pages, nq, topk):
    isteps, steps = npages // ipages, npages // pages
    n_past = npages * PAGE
    ikw, kw = ipages * PAGE, pages * PAGE
    rows = IDX_HEADS * nq
    rows_q = qs.shape[2]
    pos_bits = int(np.ceil(np.log2(n_past + PAGE)))

    seq3 = lambda shape: pl.BlockSpec((None,) + shape, lambda b, s, pt: (b,) + (0,) * len(shape))
    new_rows = lambda w: pl.BlockSpec((S_PAD, w), lambda b, s, pt: (b, 0))
    in_hbm = pl.BlockSpec(memory_space=pl.ANY)
    grid_spec = pltpu.PrefetchScalarGridSpec(
        num_scalar_prefetch=1,
        grid=(nseq, isteps + 2 * steps),
        in_specs=[seq3((rows, IDX_DIM)), seq3((rows, 1)), new_rows(IDX_DIM),
                  seq3((A_KV_HEADS, rows_q, A_HEAD_DIM)), new_rows(A_KV_WIDTH), new_rows(A_KV_WIDTH),
                  in_hbm, in_hbm, in_hbm],
        out_specs=seq3((A_KV_HEADS, rows_q, A_HEAD_DIM)),
        scratch_shapes=[pltpu.VMEM((isteps, 8, ikw), F32), pltpu.VMEM((8, PAGE), F32),
                        pltpu.VMEM((steps, 8, kw), F32), pltpu.VMEM((8, PAGE), F32),
                        pltpu.VMEM((8, 1), F32), pltpu.VMEM((8, 1), I32),
                        pltpu.VMEM((A_KV_HEADS, steps, rows_q, kw), F32),
                        pltpu.VMEM((A_KV_HEADS, rows_q, S_PAD), F32),
                        pltpu.VMEM((A_KV_HEADS, rows_q, 128), F32),
                        pltpu.VMEM((A_KV_HEADS, rows_q, 1), F32),
                        pltpu.VMEM((A_KV_HEADS, rows_q, 128), F32),
                        pltpu.VMEM((A_KV_HEADS, rows_q, A_HEAD_DIM), F32),
                        pltpu.VMEM((2, ipages, IDX_DIM, PAGE), F32),
                        pltpu.VMEM((2, pages, PAGE * A_KV_HEADS, A_HEAD_DIM), F32),
                        pltpu.SemaphoreType.DMA((2,)), pltpu.SemaphoreType.DMA((2,))],
    )
    return pl.pallas_call(
        functools.partial(_dsa_sample_kernel, ipages=ipages, pages=pages, npages=npages, nq=nq, topk=topk,
                          pos_bits=pos_bits),
        grid_spec=grid_spec,
        out_shape=jax.ShapeDtypeStruct((nseq, A_KV_HEADS, rows_q, A_HEAD_DIM), F32),
        compiler_params=_cparams(("arbitrary", "arbitrary")),
        name="dsa_sample",
    )(page_flat, qp, wcol, ki_new_b, qs, k_new_b, v_new_b, cache_kidx_t, cache_k, cache_v)


def _merge_kernel(x_ref, oa_ref, ag_ref, bu_ref, bv_ref, bg_ref, cq_ref, cg_ref, ra_ref, rb_ref, rc_ref,
                  mk_ref, mv_ref, ws_ref, bs_ref, gsgu_ref, gmq_ref, wpa_ref, wpb_ref, wpc_ref, wout_ref,
                  y_ref, *maybe_vn_ref, tm, chunk, mem_groups):
    f32 = lambda r: r[...].astype(F32)
    silu = lambda t: t * jax.nn.sigmoid(t)

    vn = _rms(f32(bv_ref), gsgu_ref[...])
    if maybe_vn_ref:
        maybe_vn_ref[0][...] = vn
    vnb = vn.astype(BF16)
    bu = f32(bu_ref)
    tril = (lax.broadcasted_iota(I32, (chunk, chunk), 1) <= lax.broadcasted_iota(I32, (chunk, chunk), 0))
    ob_cols = []
    for g in range(B_GROUPS):
        wg = jnp.where(tril, ws_ref[g], 0.0).astype(BF16)
        gsl = slice(g * B_GROUP_DIM, (g + 1) * B_GROUP_DIM)
        parts = [_dot(wg, vnb[c * chunk:(c + 1) * chunk, gsl]) + bs_ref[:, g:g + 1] for c in range(tm // chunk)]
        ob_cols.append(parts[0] if len(parts) == 1 else jnp.concatenate(parts, axis=0))
    ob = bu * jnp.concatenate(ob_cols, axis=1)
    pb = _dot((ob * silu(f32(bg_ref))).astype(BF16), wpb_ref[...])

    cq = f32(cq_ref)
    rows_g = tm // mem_groups
    oc_cols = []
    for hh in range(M_HEADS):
        hsl = slice(hh * M_HEAD_DIM, (hh + 1) * M_HEAD_DIM)
        qn = (_rms(cq[:, hsl], gmq_ref[...]) * (M_HEAD_DIM ** -0.5)).astype(BF16)
        oc_rows = []
        for u in range(mem_groups):
            msl = slice(u * N_MEM, (u + 1) * N_MEM)
            s = _dot_nt(qn[u * rows_g:(u + 1) * rows_g], mk_ref[msl, hsl])
            p = jnp.exp(s - jnp.max(s, axis=-1, keepdims=True))
            oc_rows.append(_dot(p.astype(BF16), mv_ref[msl, hsl]) / jnp.sum(p, axis=-1, keepdims=True))
        oc_cols.append(oc_rows[0] if mem_groups == 1 else jnp.concatenate(oc_rows, axis=0))
    oc = jnp.concatenate(oc_cols, axis=1)
    pc = _dot((oc * silu(f32(cg_ref))).astype(BF16), wpc_ref[...])

    pa = _dot((f32(oa_ref) * silu(f32(ag_ref))).astype(BF16), wpa_ref[...])
    sig = jax.nn.sigmoid
    m = sig(f32(ra_ref)) * pa + sig(f32(rb_ref)) * pb + sig(f32(rc_ref)) * pc
    y_ref[...] = x_ref[...] + _dot(m.astype(BF16), wout_ref[...])


def _merge(x, oa, zr, mk_b, mv_b, ws, bs_t, g_sgu, g_mq, w_pa, w_pb, w_pc, w_out, tm, chunk, mem_groups, mem_map,
           emit_vn):
    n = x.shape[0]
    col = lambda w, j: pl.BlockSpec((tm, w), lambda i, j=j: (i, j))
    mem = pl.BlockSpec((mem_groups * N_MEM, M_WIDTH), mem_map)
    in_specs = [col(D_MODEL, 0), col(A_WIDTH, 0),
                col(1024, 0), col(1024, 1), col(1024, 2), col(1024, 3), col(1024, 4), col(1024, 5),
                col(2048, 3), col(2048, 4), col(2048, 5),
                mem, mem,
                _const_spec((B_GROUPS, chunk, chunk)), _const_spec((chunk, B_GROUPS)),
                _const_spec((1, B_WIDTH)), _const_spec((1, M_HEAD_DIM)),
                _const_spec((A_WIDTH, D_MODEL)), _const_spec((B_WIDTH, D_MODEL)),
                _const_spec((M_WIDTH, D_MODEL)), _const_spec((D_MODEL, D_MODEL))]
    out_specs = [col(D_MODEL, 0)]
    out_shape = [jax.ShapeDtypeStruct((n, D_MODEL), F32)]
    if emit_vn:
        out_specs.append(col(B_WIDTH, 0))
        out_shape.append(jax.ShapeDtypeStruct((n, B_WIDTH), F32))
    return pl.pallas_call(
        functools.partial(_merge_kernel, tm=tm, chunk=chunk, mem_groups=mem_groups),
        grid=(n // tm,),
        in_specs=in_specs,
        out_specs=out_specs,
        out_shape=out_shape,
        compiler_params=_cparams(("arbitrary",)),
        name="merge",
    )(x, oa, *([zr] * 9), mk_b, mv_b, ws, bs_t, g_sgu, g_mq, w_pa, w_pb, w_pc, w_out)


def _rope_tables(pos):
    freq = lambda half: ROPE_THETA ** (-jnp.arange(half, dtype=F32) / half)
    f_idx = freq(IDX_DIM // 2)
    ang = pos.astype(F32)[:, None] * jnp.concatenate([freq(A_HEAD_DIM // 2), f_idx, f_idx])[None, :]
    return jnp.cos(ang), jnp.sin(ang)


def kernel(x_prompt, x_sample, cache_k, cache_v, cache_kidx, cache_mem_k, cache_mem_v, page_table,
           mem_prompt, g_pre, w_in, g_q, g_k, g_mq, g_mk, g_mem, w_mem_kv, g_sgu, w_s, b_s,
           w_pa, w_pb, w_pc, w_out):
    batch, seq, _ = x_prompt.shape
    nseq, nq, _ = x_sample.shape
    assert nq == 4 and nq <= S_PAD
    npages = page_table.shape[1]
    n_past = npages * PAGE
    n_pool = cache_k.shape[0]
    row2 = lambda a: a.reshape(1, -1)

    w_in_t = w_in.T
    w_a = w_in_t[:A_COLS].astype(BF16)
    w_pa_b, w_pb_b, w_pc_b, w_out_b = (w.astype(BF16) for w in (w_pa, w_pb, w_pc, w_out))
    w_mem_b = w_mem_kv.astype(BF16)
    g_pre2, g_q2, g_k2, g_mq2, g_mk2, g_mem2, g_sgu2 = map(row2, (g_pre, g_q, g_k, g_mq, g_mk, g_mem, g_sgu))

    xp = x_prompt.reshape(batch * seq, D_MODEL)
    tm_a = min(256, seq)
    nblk = seq // tm_a
    tabs_p = _rope_tables(jnp.arange(seq))
    q, k_p, v_p, k_b, v_b, qi, ki_p, ki_b, wi = _proj_a(
        xp, g_pre2, w_a, g_q2, g_k2, tabs_p, tm_a, lambda i: (i % nblk, 0))
    xs = jnp.pad(x_sample, ((0, 0), (0, S_PAD - nq), (0, 0))).reshape(nseq * S_PAD, D_MODEL)
    rows_s = nseq * S_PAD
    zr, zr_s = _proj_rest(xp, xs, g_pre2, w_in_t, min(1024, seq), PROJ_TN)
    mk_p, mv_p, mk_b, mv_b = _mem_kv(mem_prompt.reshape(batch * N_MEM, D_MODEL), g_mem2, w_mem_b, g_mk2)
    oa = _dsa_prompt(qi, wi.T, q, ki_b, k_b, v_b, batch, seq, min(TOPK_MAX, seq // 4))
    tm_m = min(256, seq)
    nblk_m = seq // tm_m
    (y_p,) = _merge(xp, oa, zr, mk_b, mv_b, w_s, b_s.T, g_sgu2, g_mq2, w_pa_b, w_pb_b, w_pc_b, w_out_b,
                    tm_m, CHUNK, 1, lambda i: (i // nblk_m, 0), False)

    tabs_s = tuple(jnp.tile(t, (nseq, 1)) for t in _rope_tables(n_past + jnp.arange(S_PAD)))
    q_s, k_s, v_s, k_sb, v_sb, qi_s, ki_s, ki_sb, wi_s = _proj_a(
        xs, g_pre2, w_a, g_q2, g_k2, tabs_s, rows_s, lambda i: (0, 0))

    qp = (qi_s.reshape(nseq, S_PAD, IDX_HEADS, IDX_DIM)[:, :nq]
          .transpose(0, 2, 1, 3).reshape(nseq, IDX_HEADS * nq, IDX_DIM))
    wcol = wi_s.reshape(nseq, S_PAD, IDX_HEADS)[:, :nq].transpose(0, 2, 1).reshape(nseq, IDX_HEADS * nq, 1)
    rep = A_HEADS // A_KV_HEADS
    qs = (q_s.reshape(nseq, S_PAD, A_KV_HEADS, rep, A_HEAD_DIM)[:, :nq]
          .transpose(0, 2, 3, 1, 4).reshape(nseq, A_KV_HEADS, rep * nq, A_HEAD_DIM))
    qs = jnp.concatenate([qs, qs], axis=2)
    page_flat = page_table.reshape(-1)
    pages, ipages = min(SAMPLE_PAGES, npages), min(SAMPLE_IDX_PAGES, npages)
    o_s = _dsa_sample(page_flat, qp, wcol, ki_sb, qs, k_sb, v_sb, jnp.swapaxes(cache_kidx, 1, 2),
                      cache_k.reshape(n_pool, PAGE * A_KV_HEADS, A_HEAD_DIM),
                      cache_v.reshape(n_pool, PAGE * A_KV_HEADS, A_HEAD_DIM),
                      nseq, npages, ipages, pages, nq, min(TOPK_MAX, (n_past + nq) // 4))
    oa_s = (o_s[:, :, :rep * nq].reshape(nseq, A_KV_HEADS, rep, nq, A_HEAD_DIM)
            .transpose(0, 3, 1, 2, 4).reshape(nseq, nq, A_WIDTH))
    oa_s = jnp.pad(oa_s, ((0, 0), (0, S_PAD - nq), (0, 0))).reshape(rows_s, A_WIDTH).astype(BF16)
    mk_s = cache_mem_k.reshape(nseq * N_MEM, M_WIDTH).astype(BF16)
    mv_s = cache_mem_v.reshape(nseq * N_MEM, M_WIDTH).astype(BF16)
    y_s, vn_s = _merge(xs, oa_s, zr_s, mk_s, mv_s, w_s[:, :S_PAD, :S_PAD], b_s[:, :S_PAD].T, g_sgu2, g_mq2,
                       w_pa_b, w_pb_b, w_pc_b, w_out_b, rows_s, S_PAD, nseq, lambda i: (0, 0), True)

    take = lambda a, shape: a.reshape(nseq, S_PAD, -1)[:, :nq].reshape(shape)
    return (y_p.reshape(batch, seq, D_MODEL),
            take(y_s, (nseq, nq, D_MODEL)),
            k_p.reshape(batch, seq, A_KV_HEADS, A_HEAD_DIM),
            v_p.reshape(batch, seq, A_KV_HEADS, A_HEAD_DIM),
            ki_p.reshape(batch, seq, IDX_DIM),
            mk_p.reshape(batch, N_MEM, M_HEADS, M_HEAD_DIM),
            mv_p.reshape(batch, N_MEM, M_HEADS, M_HEAD_DIM),
            take(k_s, (nseq, nq, A_KV_HEADS, A_HEAD_DIM)),
            take(v_s, (nseq, nq, A_KV_HEADS, A_HEAD_DIM)),
            take(ki_s, (nseq, nq, IDX_DIM)),
            take(vn_s, (nseq, nq, B_GROUPS, B_GROUP_DIM)))
```

```python
import functools

import numpy as np
import jax
import jax.numpy as jnp
from jax import lax
from jax.experimental import pallas as pl
from jax.experimental.pallas import tpu as pltpu

F32 = jnp.float32
BF16 = jnp.bfloat16
I32 = jnp.int32

D_MODEL = 2048
PAGE = 128
A_HEADS = 8
A_KV_HEADS = 4
A_HEAD_DIM = 128
A_WIDTH = A_HEADS * A_HEAD_DIM
A_KV_WIDTH = A_KV_HEADS * A_HEAD_DIM
IDX_HEADS = 16
IDX_DIM = 64
TOPK_MAX = 256
Q_BLOCK = 128
ROPE_THETA = 10000.0
CHUNK = 128
B_GROUPS = 8
B_GROUP_DIM = 128
B_WIDTH = B_GROUPS * B_GROUP_DIM
N_MEM = 256
M_HEADS = 4
M_HEAD_DIM = 256
M_WIDTH = M_HEADS * M_HEAD_DIM
EPS = 1e-6

OFF_K = A_WIDTH
OFF_V = OFF_K + A_KV_WIDTH
OFF_QI = OFF_V + A_KV_WIDTH
OFF_KI = OFF_QI + IDX_HEADS * IDX_DIM
OFF_WI = OFF_KI + IDX_DIM
OFF_REST = OFF_WI + IDX_HEADS
A_COLS = 3200
REST_COLS = A_WIDTH + 3 * B_WIDTH + 2 * M_WIDTH + 3 * D_MODEL

Q_SCALE = float(np.log2(np.e)) * A_HEAD_DIM ** -0.5
S_PAD = 16
PROJ_TN = 1024
BISECT_STEPS = 24
SAMPLE_PAGES = 16
SAMPLE_IDX_PAGES = 64
DMA_SLOTS = 3
INT_MIN = np.int32(-2 ** 31)
INT_MAX = np.int32(2 ** 31 - 1)
NEG_INF = float("-inf")

V7X_VMEM_LIMIT = 56 * 1024 * 1024


def _cparams(sem):
    return pltpu.CompilerParams(dimension_semantics=sem, vmem_limit_bytes=V7X_VMEM_LIMIT)


def _dot(a, b):
    return jnp.dot(a, b, preferred_element_type=F32)


def _dot_nt(a, b):
    return lax.dot_general(a, b, (((1,), (1,)), ((), ())), preferred_element_type=F32)


def _rms(x, g):
    return x * lax.rsqrt(jnp.mean(x * x, axis=-1, keepdims=True) + EPS) * g


def _const_spec(shape):
    nd = len(shape)
    return pl.BlockSpec(shape, lambda *_: (0,) * nd, pipeline_mode=pl.Buffered(1))


def _proj_a_kernel(x_ref, g_ref, w_ref, gq_ref, gk_ref, cos_ref, sin_ref,
                   q_ref, k_ref, v_ref, kb_ref, vb_ref, qi_ref, ki_ref, kib_ref, wi_ref):
    h = _rms(x_ref[...], g_ref[...]).astype(BF16)
    z = _dot_nt(h, w_ref[...])
    tm = z.shape[0]
    c, s = cos_ref[...], sin_ref[...]
    c_sw, s_sw = pltpu.roll(c, 64, 1), pltpu.roll(s, 64, 1)
    lane = lax.broadcasted_iota(I32, (1, 128), 1)
    low = lane < 64
    cq, sq = jnp.where(low, c, c_sw), jnp.where(low, -s, s_sw)
    ci, si = jnp.where(low, c_sw, c), jnp.where(low, s_sw, s)
    first_half = lane % IDX_DIM < IDX_DIM // 2
    sia, sib = jnp.where(first_half, -si, 0.0), jnp.where(first_half, 0.0, si)

    def norm_rope(zz, g):
        n = _rms(zz, g)
        return n * cq + pltpu.roll(n, A_HEAD_DIM // 2, 1) * sq

    def rope_idx(zz):
        return zz * ci + pltpu.roll(zz, 96, 1) * sia + pltpu.roll(zz, 32, 1) * sib

    for hh in range(A_HEADS):
        sl = slice(hh * A_HEAD_DIM, (hh + 1) * A_HEAD_DIM)
        q_ref[:, sl] = (norm_rope(z[:, sl], gq_ref[...]) * Q_SCALE).astype(BF16)
    for hh in range(A_KV_HEADS):
        sl = slice(hh * A_HEAD_DIM, (hh + 1) * A_HEAD_DIM)
        kh = norm_rope(z[:, OFF_K + hh * A_HEAD_DIM:OFF_K + (hh + 1) * A_HEAD_DIM], gk_ref[...])
        vh = z[:, OFF_V + hh * A_HEAD_DIM:OFF_V + (hh + 1) * A_HEAD_DIM]
        head_rows = pl.ds(hh, tm, stride=A_KV_HEADS)
        k_ref[head_rows, :] = kh
        v_ref[head_rows, :] = vh
        kb_ref[:, sl] = kh.astype(BF16)
        vb_ref[:, sl] = vh.astype(BF16)
    for t in range(IDX_HEADS * IDX_DIM // 128):
        sl = slice(t * 128, (t + 1) * 128)
        qi_ref[:, sl] = rope_idx(z[:, OFF_QI + t * 128:OFF_QI + (t + 1) * 128]).astype(BF16)
    last = z[:, OFF_KI:OFF_KI + 128]
    ki = rope_idx(last)[:, :IDX_DIM]
    ki_ref[...] = ki
    kib_ref[...] = ki.astype(BF16)
    wi_ref[...] = last[:, IDX_DIM:IDX_DIM + IDX_HEADS] * ((IDX_HEADS ** -0.5) * (IDX_DIM ** -0.5))


def _proj_a(x, g_pre, w_a, g_q, g_k, tabs, tm, tab_map):
    n = x.shape[0]
    row = lambda w: pl.BlockSpec((tm, w), lambda i: (i, 0))
    tab = pl.BlockSpec((tm, 128), tab_map)
    outs = [(1, A_WIDTH, BF16), (A_KV_HEADS, A_HEAD_DIM, F32), (A_KV_HEADS, A_HEAD_DIM, F32),
            (1, A_KV_WIDTH, BF16), (1, A_KV_WIDTH, BF16),
            (1, IDX_HEADS * IDX_DIM, BF16), (1, IDX_DIM, F32), (1, IDX_DIM, BF16), (1, IDX_HEADS, F32)]
    return pl.pallas_call(
        _proj_a_kernel,
        grid=(n // tm,),
        in_specs=[row(D_MODEL), _const_spec((1, D_MODEL)), _const_spec((A_COLS, D_MODEL)),
                  _const_spec((1, A_HEAD_DIM)), _const_spec((1, A_HEAD_DIM)), tab, tab],
        out_specs=[pl.BlockSpec((tm * r, w), lambda i: (i, 0)) for r, w, _ in outs],
        out_shape=[jax.ShapeDtypeStruct((n * r, w), dt) for r, w, dt in outs],
        compiler_params=_cparams(("arbitrary",)),
        name="proj_a",
    )(x, g_pre, w_a, g_q, g_k, *tabs)


def _proj_rest_kernel(x_ref, xs_ref, g_ref, w_ref, o_ref, os_ref, h_ref, hs_ref):
    i, j = pl.program_id(0), pl.program_id(1)

    @pl.when(j == 0)
    def _():
        h_ref[...] = _rms(x_ref[...], g_ref[...]).astype(BF16)

    @pl.when((i == 0) & (j == 0))
    def _():
        hs_ref[...] = _rms(xs_ref[...], g_ref[...]).astype(BF16)

    w = w_ref[...].astype(BF16)
    o_ref[...] = _dot_nt(h_ref[...], w).astype(BF16)

    @pl.when(i == 0)
    def _():
        os_ref[...] = _dot_nt(hs_ref[...], w).astype(BF16)


def _proj_rest(x, xs, g_pre, w_in_t, tm, tn):
    n, ns = x.shape[0], xs.shape[0]
    ncols = REST_COLS // tn
    xs_cols = lambda i, j: (0, jnp.where(i == 0, j, ncols - 1))
    return pl.pallas_call(
        _proj_rest_kernel,
        grid=(n // tm, ncols),
        in_specs=[pl.BlockSpec((tm, D_MODEL), lambda i, j: (i, 0)),
                  pl.BlockSpec((ns, D_MODEL), lambda i, j: (0, 0), pipeline_mode=pl.Buffered(1)),
                  pl.BlockSpec((1, D_MODEL), lambda i, j: (0, 0)),
                  pl.BlockSpec((pl.Element(tn), pl.Element(D_MODEL)),
                               lambda i, j: (pl.multiple_of(OFF_REST + j * tn, 16), 0))],
        out_specs=[pl.BlockSpec((tm, tn), lambda i, j: (i, j)), pl.BlockSpec((ns, tn), xs_cols)],
        out_shape=[jax.ShapeDtypeStruct((n, REST_COLS), BF16), jax.ShapeDtypeStruct((ns, REST_COLS), BF16)],
        scratch_shapes=[pltpu.VMEM((tm, D_MODEL), BF16), pltpu.VMEM((ns, D_MODEL), BF16)],
        compiler_params=_cparams(("arbitrary", "arbitrary")),
        name="proj_rest",
    )(x, xs, g_pre, w_in_t)


def _mem_kv_kernel(x_ref, g_ref, w_ref, gk_ref, k_ref, v_ref, kb_ref, vb_ref):
    h = _rms(x_ref[...], g_ref[...]).astype(BF16)
    z = _dot(h, w_ref[...])
    for hh in range(M_HEADS):
        sl = slice(hh * M_HEAD_DIM, (hh + 1) * M_HEAD_DIM)
        kh = _rms(z[:, sl], gk_ref[...])
        k_ref[:, sl] = kh
        kb_ref[:, sl] = kh.astype(BF16)
    v = z[:, M_WIDTH:]
    v_ref[...] = v
    vb_ref[...] = v.astype(BF16)


def _mem_kv(mem, g_mem, w_mem, g_mk):
    n = mem.shape[0]
    blk = pl.BlockSpec((N_MEM, M_WIDTH), lambda i: (i, 0))
    return pl.pallas_call(
        _mem_kv_kernel,
        grid=(n // N_MEM,),
        in_specs=[pl.BlockSpec((N_MEM, D_MODEL), lambda i: (i, 0)), _const_spec((1, D_MODEL)),
                  _const_spec((D_MODEL, 2 * M_WIDTH)), _const_spec((1, M_HEAD_DIM))],
        out_specs=[blk, blk, blk, blk],
        out_shape=[jax.ShapeDtypeStruct((n, M_WIDTH), dt) for dt in (F32, F32, BF16, BF16)],
        compiler_params=_cparams(("arbitrary",)),
        name="mem_kv",
    )(mem, g_mem, w_mem, g_mk)


KEY_NEG_INF = np.int32(-0x7F800000)


def _key_to_f32(key):
    return pltpu.bitcast(jnp.where(key >= 0, key, INT_MIN - key), F32)


def _select_params(count, topk, pos_bits, amax, n_valid, t_ref, j_ref):
    kf = float(topk)
    hi0 = amax * 1.000001 + 1e-30
    all_selected = (n_valid <= kf).astype(I32)

    def bisect_body(_, state):
        lo, hi, t, done = state
        mid = 0.5 * lo + 0.5 * hi
        n_ge = count(lambda s, p: s >= mid)
        hit = (n_ge == kf) & (done == 0)
        return (jnp.where(n_ge >= kf, mid, lo), jnp.where(n_ge >= kf, hi, mid),
                jnp.where(hit, mid, t), jnp.where(hit, 1, done))

    _, _, t_bis, done = lax.fori_loop(
        0, BISECT_STEPS, bisect_body, (-hi0, hi0, jnp.full(hi0.shape, NEG_INF, F32), all_selected))
    t_ref[...] = t_bis
    j_ref[...] = jnp.full(j_ref.shape, INT_MAX, I32)

    @pl.when(jnp.min(done) == 0)
    def _():
        t0 = jnp.where(count(lambda s, p: s >= 0.0) >= kf, jnp.int32(0), INT_MIN)

        def bit_body(b, t):
            cand = t + lax.shift_left(jnp.int32(1), 30 - b)
            cand_f = _key_to_f32(cand)
            return jnp.where(count(lambda s, p: s >= cand_f) >= kf, cand, t)

        t = _key_to_f32(jnp.maximum(lax.fori_loop(0, 31, bit_body, t0), KEY_NEG_INF))
        t_ref[...] = t
        tie = (count(lambda s, p: s >= t) > kf) & (t > NEG_INF)

        @pl.when(jnp.max(tie.astype(I32)) > 0)
        def _():
            n_gt = count(lambda s, p: s > t)

            def pos_body(b, p_lo):
                cand = p_lo + lax.shift_left(jnp.int32(1), pos_bits - 1 - b)
                n_eq = count(lambda s, p: (s == t) & (p < cand))
                return jnp.where(n_gt + n_eq < kf, cand, p_lo)

            p_sel = lax.fori_loop(0, pos_bits, pos_body, jnp.zeros(t.shape, I32))
            j_ref[...] = jnp.where(tie, p_sel, INT_MAX)

    return jnp.maximum(t_ref[...], -hi0), j_ref[...]


def _fori_by_two(n, body, init):
    carry = lax.fori_loop(0, n // 2, lambda j, c: body(2 * j + 1, body(2 * j, c)), init)
    return lax.cond(n % 2 == 1, lambda c: body(n - 1, c), lambda c: c, carry)


def _dsa_prompt_kernel(qi_ref, wit_ref, q_ref, ki_ref, k_ref, v_ref, o_ref,
                       sc_ref, bias_ref, vt_ref, t_ref, j_ref, s_ref, acc_ref, *, topk, kc, pos_bits):
    i = pl.program_id(1)
    nck = (i * Q_BLOCK + Q_BLOCK + kc - 1) // kc
    nchunks = vt_ref.shape[0]
    rep = A_HEADS // A_KV_HEADS
    q_pos = i * Q_BLOCK + lax.broadcasted_iota(I32, (1, Q_BLOCK), 1)
    sub = lax.broadcasted_iota(I32, (kc, 1), 0)

    @pl.when(i == 0)
    def _():
        for c in range(nchunks):
            for g in range(A_KV_HEADS):
                gsl = slice(g * A_HEAD_DIM, (g + 1) * A_HEAD_DIM)
                vt_ref[c, gsl, :] = v_ref[c * kc:(c + 1) * kc, gsl].astype(F32).T.astype(BF16)

    def chunk_rows(c):
        return pl.ds(pl.multiple_of(c * kc, kc), kc)

    qi = qi_ref[...]
    wit = wit_ref[...]
    qi_pairs = [jnp.concatenate([qi[:, (2 * j) * IDX_DIM:(2 * j + 1) * IDX_DIM],
                                 qi[:, (2 * j + 1) * IDX_DIM:(2 * j + 2) * IDX_DIM]], axis=0)
                for j in range(IDX_HEADS // 2)]

    def score_body(c, amax):
        kic = ki_ref[chunk_rows(c), :]
        acc = jnp.zeros((kc, Q_BLOCK), F32)
        for j in range(IDX_HEADS // 2):
            d = _dot_nt(kic, qi_pairs[j])
            acc = acc + jnp.maximum(d[:, :Q_BLOCK], 0.0) * wit[2 * j:2 * j + 1, :]
            acc = acc + jnp.maximum(d[:, Q_BLOCK:], 0.0) * wit[2 * j + 1:2 * j + 2, :]
        sc_ref[c] = jnp.where(c * kc + sub <= q_pos, acc, NEG_INF)
        return jnp.maximum(amax, jnp.max(jnp.abs(acc).reshape(kc // 64, 64, Q_BLOCK), axis=0))

    amax = jnp.max(_fori_by_two(nck, score_body, jnp.zeros((64, Q_BLOCK), F32)), axis=0, keepdims=True)

    def count(pred):
        def body(c, acc):
            part = jnp.where(pred(sc_ref[c], c * kc + sub), 1.0, 0.0)
            return acc + jnp.sum(part.reshape(kc // 64, 64, Q_BLOCK), axis=0)

        acc = lax.fori_loop(0, nck, body, jnp.zeros((64, Q_BLOCK), F32))
        return jnp.sum(acc, axis=0, keepdims=True)

    thr, jsel = _select_params(count, topk, pos_bits, amax, (q_pos + 1).astype(F32), t_ref, j_ref)

    def bias_body(c, carry):
        s = sc_ref[c]
        tie_thr = jnp.where(c * kc + sub <= jsel, thr, jnp.inf)
        bias_ref[c] = jnp.where((s > thr) | (s >= tie_thr), 0.0, NEG_INF)
        return carry

    lax.fori_loop(0, nck, bias_body, 0)

    gsl = lambda g: slice(g * A_HEAD_DIM, (g + 1) * A_HEAD_DIM)
    row0 = lambda v: jnp.full((1, rep * Q_BLOCK), v, F32)
    groups = tuple(range(A_KV_HEADS))
    qgs = [jnp.concatenate([q_ref[:, gsl(g * rep + r)] for r in range(rep)], axis=0) for g in groups]


    def qk_body(c, ms):
        b = bias_ref[c]
        bias = jnp.concatenate([b] * rep, axis=1)
        out = []
        for g in groups:
            s = _dot_nt(k_ref[chunk_rows(c), gsl(g)], qgs[g]) + bias
            s_ref[c, g] = s
            out.append(jnp.maximum(ms[g], jnp.max(s, axis=0, keepdims=True)))
        return tuple(out)

    ms = _fori_by_two(nck, qk_body, (row0(NEG_INF),) * len(groups))
    acc_ref[...] = jnp.zeros(acc_ref.shape, F32)

    def pv_body(c, ls):
        out = []
        for g in groups:
            p = jnp.exp2(s_ref[c, g] - ms[g])
            out.append(ls[g] + jnp.sum(p, axis=0, keepdims=True))
            acc_ref[g] += _dot(vt_ref[c, gsl(g), :], p.astype(BF16))
        return tuple(out)

    ls = _fori_by_two(nck, pv_body, (row0(0.0),) * len(groups))
    for g in groups:
        o = acc_ref[g] / ls[g]
        for r in range(rep):
            o_ref[:, gsl(g * rep + r)] = o[:, r * Q_BLOCK:(r + 1) * Q_BLOCK].T.astype(BF16)


def _dsa_prompt(qi, wit, q, ki_b, k_b, v_b, batch, seq, topk):
    nqb = seq // Q_BLOCK
    rep = A_HEADS // A_KV_HEADS
    kc = min(512, seq)
    nchunks = seq // kc
    pos_bits = max(1, int(np.ceil(np.log2(seq))))
    qrow = lambda w: pl.BlockSpec((Q_BLOCK, w), lambda b, i: (b * nqb + i, 0))
    seqblk = lambda w: pl.BlockSpec((seq, w), lambda b, i: (b, 0), pipeline_mode=pl.Buffered(1))
    return pl.pallas_call(
        functools.partial(_dsa_prompt_kernel, topk=topk, kc=kc, pos_bits=pos_bits),
        grid=(batch, nqb),
        in_specs=[qrow(IDX_HEADS * IDX_DIM), pl.BlockSpec((IDX_HEADS, Q_BLOCK), lambda b, i: (0, b * nqb + i)),
                  qrow(A_WIDTH), seqblk(IDX_DIM), seqblk(A_KV_WIDTH), seqblk(A_KV_WIDTH)],
        out_specs=qrow(A_WIDTH),
        out_shape=jax.ShapeDtypeStruct((batch * seq, A_WIDTH), BF16),
        scratch_shapes=[pltpu.VMEM((nchunks, kc, Q_BLOCK), F32), pltpu.VMEM((nchunks, kc, Q_BLOCK), F32),
                        pltpu.VMEM((nchunks, A_KV_WIDTH, kc), BF16),
                        pltpu.VMEM((1, Q_BLOCK), F32), pltpu.VMEM((1, Q_BLOCK), I32),
                        pltpu.VMEM((nchunks, A_KV_HEADS, kc, rep * Q_BLOCK), F32),
                        pltpu.VMEM((A_KV_HEADS, A_HEAD_DIM, rep * Q_BLOCK), F32)],
        compiler_params=_cparams(("arbitrary", "arbitrary")),
        name="dsa_prompt",
    )(qi, wit, q, ki_b, k_b, v_b)


def _fold_heads(e, nq):
    acc = e[0:8]
    for t in range(1, e.shape[0] // 8):
        acc = acc + e[8 * t:8 * (t + 1)]
    return acc + pltpu.roll(acc, nq, 0)


def _tree(op, parts):
    parts = list(parts)
    while len(parts) > 1:
        parts = [op(a, b) for a, b in zip(parts[::2], parts[1::2])] + parts[len(parts) & ~1:]
    return parts[0]


def _lane_fold(op, x):
    return _tree(op, [x[:, t * 128:(t + 1) * 128] for t in range(x.shape[1] // 128)])


def _dsa_sample_kernel(pt_ref, qp_ref, wc_ref, kin_ref, qs_ref, kn_ref, vn_ref, kidx_hbm, k_hbm, v_hbm, o_ref,
                       sc_ref, scn_ref, bias_ref, biasn_ref, t_ref, j_ref, s_ref, sn_ref, m_ref, mfin_ref, l_ref, acc_ref,
                       idx_raw, kv_raw, idx_sem, kv_sem, *, ipages, pages, npages, nq, topk, pos_bits):
    b_idx, s_idx = pl.program_id(0), pl.program_id(1)
    ikw, kw = ipages * PAGE, pages * PAGE
    isteps, steps = npages // ipages, npages // pages
    nsteps = isteps + 2 * steps
    n_past = npages * PAGE
    rows_q = qs_ref.shape[1]
    dup = lambda x8: jnp.concatenate([x8] * (rows_q // 8), axis=0)
    gsl = lambda g: slice(g * A_HEAD_DIM, (g + 1) * A_HEAD_DIM)

    ahead = DMA_SLOTS - 1
    n = b_idx * nsteps + s_idx
    slot = n % DMA_SLOTS
    phases = ((kidx_hbm, idx_raw, idx_sem, 0, isteps, ipages),
              (k_hbm, kv_raw, kv_sem, isteps, steps, pages),
              (v_hbm, kv_raw, kv_sem, isteps + steps, steps, pages))

    def block_copies(seq, step, sl, start):
        for src, dst, sem, first, count, per_step in phases:
            @pl.when((step >= first) & (step < first + count))
            def _(src=src, dst=dst, sem=sem, first=first, per_step=per_step):
                for p in range(per_step):
                    page = pt_ref[seq * npages + (step - first) * per_step + p] if start else 0
                    copy = pltpu.make_async_copy(src.at[page], dst.at[sl, p], sem.at[sl])
                    copy.start() if start else copy.wait()

    @pl.when(n == 0)
    def _():
        for d in range(ahead):
            block_copies(jnp.int32(0), jnp.int32(d), d, True)

    @pl.when(n + ahead < pl.num_programs(0) * nsteps)
    def _():
        wrap = (s_idx + ahead >= nsteps).astype(I32)
        to_slot = slot + ahead
        block_copies(b_idx + wrap, s_idx + ahead - wrap * nsteps,
                     jnp.where(to_slot >= DMA_SLOTS, to_slot - DMA_SLOTS, to_slot), True)

    block_copies(b_idx, s_idx, slot, False)

    def head_rows(g):
        rows = pl.ds(g, PAGE, stride=A_KV_HEADS)
        return jnp.concatenate([kv_raw[slot, p, rows, :] for p in range(pages)], axis=0).astype(BF16)

    @pl.when(s_idx < isteps)
    def _():
        qp = qp_ref[...]
        wc = wc_ref[...]
        kpt = jnp.concatenate([idx_raw[slot, p] for p in range(ipages)], axis=1).astype(BF16)
        sc_ref[s_idx] = _fold_heads(jnp.maximum(_dot(qp, kpt), 0.0) * wc, nq)

        @pl.when(s_idx == isteps - 1)
        def _():
            qrow = lax.broadcasted_iota(I32, (8, 1), 0) % nq
            lane = lax.broadcasted_iota(I32, (1, ikw), 1)
            lane_n = lax.broadcasted_iota(I32, (1, PAGE), 1)
            kin = jnp.concatenate([kin_ref[...], jnp.zeros((PAGE - S_PAD, IDX_DIM), BF16)], axis=0)
            scn = _fold_heads(jnp.maximum(_dot_nt(qp, kin), 0.0) * wc, nq)
            scn_ref[...] = jnp.where(lane_n <= qrow, scn, NEG_INF)

            def count(pred):
                ones = _tree(jnp.add, [jnp.where(pred(sc_ref[t], t * ikw + lane), 1.0, 0.0) for t in range(isteps)])
                cn = jnp.where(pred(scn_ref[...], n_past + lane_n), 1.0, 0.0)
                return (jnp.sum(_lane_fold(jnp.add, ones), axis=-1, keepdims=True)
                        + jnp.sum(cn, axis=-1, keepdims=True))

            amax = jnp.maximum(
                jnp.max(_lane_fold(jnp.maximum, _tree(jnp.maximum, [jnp.abs(sc_ref[t]) for t in range(isteps)])),
                        axis=-1, keepdims=True),
                jnp.max(jnp.abs(scn), axis=-1, keepdims=True))
            n_valid = (n_past + 1 + qrow).astype(F32)
            thr, jsel = _select_params(count, topk, pos_bits, amax, n_valid, t_ref, j_ref)
            for t in range(steps):
                first = t * kw
                sp = sc_ref[first // ikw][:, first % ikw:first % ikw + kw]
                pos = first + lane[:, :kw]
                bias_ref[t] = jnp.where((sp > thr) | ((sp == thr) & (pos <= jsel)), 0.0, NEG_INF)
            sn = scn_ref[...]
            seln = ((sn > thr) | ((sn == thr) & (n_past + lane_n <= jsel))) & (lane_n <= qrow)
            biasn_ref[...] = jnp.where(seln, 0.0, NEG_INF)

    @pl.when((s_idx >= isteps) & (s_idx < isteps + steps))
    def _():
        t = s_idx - isteps

        @pl.when(t == 0)
        def _():
            m_ref[...] = jnp.full(m_ref.shape, NEG_INF, F32)

        bias = dup(bias_ref[t])
        for g in range(A_KV_HEADS):
            s = _dot_nt(qs_ref[g], head_rows(g)) + bias
            s_ref[g, t] = s
            m_ref[g] = jnp.maximum(m_ref[g], _lane_fold(jnp.maximum, s))

        @pl.when(t == steps - 1)
        def _():
            bn = dup(biasn_ref[:, :S_PAD])
            for g in range(A_KV_HEADS):
                sn = _dot_nt(qs_ref[g], kn_ref[:, gsl(g)]) + bn
                sn_ref[g] = sn
                mfin_ref[g] = jnp.maximum(jnp.max(m_ref[g], axis=-1, keepdims=True),
                                          jnp.max(sn, axis=-1, keepdims=True))

    @pl.when(s_idx >= isteps + steps)
    def _():
        t = s_idx - isteps - steps

        @pl.when(t == 0)
        def _():
            l_ref[...] = jnp.zeros(l_ref.shape, F32)
            acc_ref[...] = jnp.zeros(acc_ref.shape, F32)

        for g in range(A_KV_HEADS):
            p = jnp.exp2(s_ref[g, t] - mfin_ref[g])
            l_ref[g] += _lane_fold(jnp.add, p)
            acc_ref[g] += _dot(p.astype(BF16), head_rows(g))

        @pl.when(t == steps - 1)
        def _():
            for g in range(A_KV_HEADS):
                pn = jnp.exp2(sn_ref[g] - mfin_ref[g])
                l = jnp.sum(l_ref[g], axis=-1, keepdims=True) + jnp.sum(pn, axis=-1, keepdims=True)
                o_ref[g] = (acc_ref[g] + _dot(pn.astype(BF16), vn_ref[:, gsl(g)])) / l


def _dsa_sample(page_flat, qp, wcol, ki_new_b, qs, k_new_b, v_new_b, cache_kidx_t, cache_k, cache_v,
                nseq, npages, ipages, pages, nq, topk):
    isteps, steps = npages // ipages, npages // pages
    n_past = npages * PAGE
    ikw, kw = ipages * PAGE, pages * PAGE
    rows = IDX_HEADS * nq
    rows_q = qs.shape[2]
    pos_bits = int(np.ceil(np.log2(n_past + PAGE)))

    seq3 = lambda shape: pl.BlockSpec((None,) + shape, lambda b, s, pt: (b,) + (0,) * len(shape))
    new_rows = lambda w: pl.BlockSpec((S_PAD, w), lambda b, s, pt: (b, 0))
    in_hbm = pl.BlockSpec(memory_space=pl.ANY)
    grid_spec = pltpu.PrefetchScalarGridSpec(
        num_scalar_prefetch=1,
        grid=(nseq, isteps + 2 * steps),
        in_specs=[seq3((rows, IDX_DIM)), seq3((rows, 1)), new_rows(IDX_DIM),
                  seq3((A_KV_HEADS, rows_q, A_HEAD_DIM)), new_rows(A_KV_WIDTH), new_rows(A_KV_WIDTH),
                  in_hbm, in_hbm, in_hbm],
        out_specs=seq3((A_KV_HEADS, rows_q, A_HEAD_DIM)),
        scratch_shapes=[pltpu.VMEM((isteps, 8, ikw), F32), pltpu.VMEM((8, PAGE), F32),
                        pltpu.VMEM((steps, 8, kw), F32), pltpu.VMEM((8, PAGE), F32),
                        pltpu.VMEM((8, 1), F32), pltpu.VMEM((8, 1), I32),
                        pltpu.VMEM((A_KV_HEADS, steps, rows_q, kw), F32),
                        pltpu.VMEM((A_KV_HEADS, rows_q, S_PAD), F32),
                        pltpu.VMEM((A_KV_HEADS, rows_q, 128), F32),
                        pltpu.VMEM((A_KV_HEADS, rows_q, 1), F32),
                        pltpu.VMEM((A_KV_HEADS, rows_q, 128), F32),
                        pltpu.VMEM((A_KV_HEADS, rows_q, A_HEAD_DIM), F32),
                        pltpu.VMEM((DMA_SLOTS, ipages, IDX_DIM, PAGE), F32),
                        pltpu.VMEM((DMA_SLOTS, pages, PAGE * A_KV_HEADS, A_HEAD_DIM), F32),
                        pltpu.SemaphoreType.DMA((DMA_SLOTS,)), pltpu.SemaphoreType.DMA((DMA_SLOTS,))],
    )
    return pl.pallas_call(
        functools.partial(_dsa_sample_kernel, ipages=ipages, pages=pages, npages=npages, nq=nq, topk=topk,
                          pos_bits=pos_bits),
        grid_spec=grid_spec,
        out_shape=jax.ShapeDtypeStruct((nseq, A_KV_HEADS, rows_q, A_HEAD_DIM), F32),
        compiler_params=_cparams(("arbitrary", "arbitrary")),
        name="dsa_sample",
    )(page_flat, qp, wcol, ki_new_b, qs, k_new_b, v_new_b, cache_kidx_t, cache_k, cache_v)


def _merge_kernel(x_ref, oa_ref, ag_ref, bu_ref, bv_ref, bg_ref, cq_ref, cg_ref, ra_ref, rb_ref, rc_ref,
                  mk_ref, mv_ref, ws_ref, bs_ref, gsgu_ref, gmq_ref, wpa_ref, wpb_ref, wpc_ref, wout_ref,
                  y_ref, *maybe_vn_ref, tm, chunk, mem_groups):
    f32 = lambda r: r[...].astype(F32)
    silu = lambda t: t * jax.nn.sigmoid(t)

    vn = _rms(f32(bv_ref), gsgu_ref[...])
    if maybe_vn_ref:
        maybe_vn_ref[0][...] = vn
    vnb = vn.astype(BF16)
    bu = f32(bu_ref)
    tril = (lax.broadcasted_iota(I32, (chunk, chunk), 1) <= lax.broadcasted_iota(I32, (chunk, chunk), 0))
    ob_cols = []
    for g in range(B_GROUPS):
        wg = jnp.where(tril, ws_ref[g], 0.0).astype(BF16)
        gsl = slice(g * B_GROUP_DIM, (g + 1) * B_GROUP_DIM)
        parts = [_dot(wg, vnb[c * chunk:(c + 1) * chunk, gsl]) + bs_ref[:, g:g + 1] for c in range(tm // chunk)]
        ob_cols.append(parts[0] if len(parts) == 1 else jnp.concatenate(parts, axis=0))
    ob = bu * jnp.concatenate(ob_cols, axis=1)
    pb = _dot((ob * silu(f32(bg_ref))).astype(BF16), wpb_ref[...])

    cq = f32(cq_ref)
    rows_g = tm // mem_groups
    oc_cols = []
    for hh in range(M_HEADS):
        hsl = slice(hh * M_HEAD_DIM, (hh + 1) * M_HEAD_DIM)
        qn = (_rms(cq[:, hsl], gmq_ref[...]) * (M_HEAD_DIM ** -0.5)).astype(BF16)
        oc_rows = []
        for u in range(mem_groups):
            msl = slice(u * N_MEM, (u + 1) * N_MEM)
            s = _dot_nt(qn[u * rows_g:(u + 1) * rows_g], mk_ref[msl, hsl])
            p = jnp.exp(s - jnp.max(s, axis=-1, keepdims=True))
            oc_rows.append(_dot(p.astype(BF16), mv_ref[msl, hsl]) / jnp.sum(p, axis=-1, keepdims=True))
        oc_cols.append(oc_rows[0] if mem_groups == 1 else jnp.concatenate(oc_rows, axis=0))
    oc = jnp.concatenate(oc_cols, axis=1)
    pc = _dot((oc * silu(f32(cg_ref))).astype(BF16), wpc_ref[...])

    pa = _dot((f32(oa_ref) * silu(f32(ag_ref))).astype(BF16), wpa_ref[...])
    sig = jax.nn.sigmoid
    m = sig(f32(ra_ref)) * pa + sig(f32(rb_ref)) * pb + sig(f32(rc_ref)) * pc
    y_ref[...] = x_ref[...] + _dot(m.astype(BF16), wout_ref[...])


def _merge(x, oa, zr, mk_b, mv_b, ws, bs_t, g_sgu, g_mq, w_pa, w_pb, w_pc, w_out, tm, chunk, mem_groups, mem_map,
           emit_vn):
    n = x.shape[0]
    col = lambda w, j: pl.BlockSpec((tm, w), lambda i, j=j: (i, j))
    mem = pl.BlockSpec((mem_groups * N_MEM, M_WIDTH), mem_map)
    in_specs = [col(D_MODEL, 0), col(A_WIDTH, 0),
                col(1024, 0), col(1024, 1), col(1024, 2), col(1024, 3), col(1024, 4), col(1024, 5),
                col(2048, 3), col(2048, 4), col(2048, 5),
                mem, mem,
                _const_spec((B_GROUPS, chunk, chunk)), _const_spec((chunk, B_GROUPS)),
                _const_spec((1, B_WIDTH)), _const_spec((1, M_HEAD_DIM)),
                _const_spec((A_WIDTH, D_MODEL)), _const_spec((B_WIDTH, D_MODEL)),
                _const_spec((M_WIDTH, D_MODEL)), _const_spec((D_MODEL, D_MODEL))]
    out_specs = [col(D_MODEL, 0)]
    out_shape = [jax.ShapeDtypeStruct((n, D_MODEL), F32)]
    if emit_vn:
        out_specs.append(col(B_WIDTH, 0))
        out_shape.append(jax.ShapeDtypeStruct((n, B_WIDTH), F32))
    return pl.pallas_call(
        functools.partial(_merge_kernel, tm=tm, chunk=chunk, mem_groups=mem_groups),
        grid=(n // tm,),
        in_specs=in_specs,
        out_specs=out_specs,
        out_shape=out_shape,
        compiler_params=_cparams(("arbitrary",)),
        name="merge",
    )(x, oa, *([zr] * 9), mk_b, mv_b, ws, bs_t, g_sgu, g_mq, w_pa, w_pb, w_pc, w_out)


def _rope_tables(pos):
    freq = lambda half: ROPE_THETA ** (-jnp.arange(half, dtype=F32) / half)
    f_idx = freq(IDX_DIM // 2)
    ang = pos.astype(F32)[:, None] * jnp.concatenate([freq(A_HEAD_DIM // 2), f_idx, f_idx])[None, :]
    return jnp.cos(ang), jnp.sin(ang)


def kernel(x_prompt, x_sample, cache_k, cache_v, cache_kidx, cache_mem_k, cache_mem_v, page_table,
           mem_prompt, g_pre, w_in, g_q, g_k, g_mq, g_mk, g_mem, w_mem_kv, g_sgu, w_s, b_s,
           w_pa, w_pb, w_pc, w_out):
    batch, seq, _ = x_prompt.shape
    nseq, nq, _ = x_sample.shape
    assert nq == 4 and nq <= S_PAD
    npages = page_table.shape[1]
    n_past = npages * PAGE
    n_pool = cache_k.shape[0]
    row2 = lambda a: a.reshape(1, -1)

    w_in_t = w_in.T
    w_a = w_in_t[:A_COLS].astype(BF16)
    w_pa_b, w_pb_b, w_pc_b, w_out_b = (w.astype(BF16) for w in (w_pa, w_pb, w_pc, w_out))
    w_mem_b = w_mem_kv.astype(BF16)
    g_pre2, g_q2, g_k2, g_mq2, g_mk2, g_mem2, g_sgu2 = map(row2, (g_pre, g_q, g_k, g_mq, g_mk, g_mem, g_sgu))

    xp = x_prompt.reshape(batch * seq, D_MODEL)
    tm_a = min(256, seq)
    nblk = seq // tm_a
    tabs_p = _rope_tables(jnp.arange(seq))
    q, k_p, v_p, k_b, v_b, qi, ki_p, ki_b, wi = _proj_a(
        xp, g_pre2, w_a, g_q2, g_k2, tabs_p, tm_a, lambda i: (i % nblk, 0))
    xs = jnp.pad(x_sample, ((0, 0), (0, S_PAD - nq), (0, 0))).reshape(nseq * S_PAD, D_MODEL)
    rows_s = nseq * S_PAD
    zr, zr_s = _proj_rest(xp, xs, g_pre2, w_in_t, min(1024, seq), PROJ_TN)
    mk_p, mv_p, mk_b, mv_b = _mem_kv(mem_prompt.reshape(batch * N_MEM, D_MODEL), g_mem2, w_mem_b, g_mk2)
    oa = _dsa_prompt(qi, wi.T, q, ki_b, k_b, v_b, batch, seq, min(TOPK_MAX, seq // 4))
    tm_m = min(256, seq)
    nblk_m = seq // tm_m
    (y_p,) = _merge(xp, oa, zr, mk_b, mv_b, w_s, b_s.T, g_sgu2, g_mq2, w_pa_b, w_pb_b, w_pc_b, w_out_b,
                    tm_m, CHUNK, 1, lambda i: (i // nblk_m, 0), False)

    tabs_s = tuple(jnp.tile(t, (nseq, 1)) for t in _rope_tables(n_past + jnp.arange(S_PAD)))
    q_s, k_s, v_s, k_sb, v_sb, qi_s, ki_s, ki_sb, wi_s = _proj_a(
        xs, g_pre2, w_a, g_q2, g_k2, tabs_s, rows_s, lambda i: (0, 0))

    qp = (qi_s.reshape(nseq, S_PAD, IDX_HEADS, IDX_DIM)[:, :nq]
          .transpose(0, 2, 1, 3).reshape(nseq, IDX_HEADS * nq, IDX_DIM))
    wcol = wi_s.reshape(nseq, S_PAD, IDX_HEADS)[:, :nq].transpose(0, 2, 1).reshape(nseq, IDX_HEADS * nq, 1)
    rep = A_HEADS // A_KV_HEADS
    qs = (q_s.reshape(nseq, S_PAD, A_KV_HEADS, rep, A_HEAD_DIM)[:, :nq]
          .transpose(0, 2, 3, 1, 4).reshape(nseq, A_KV_HEADS, rep * nq, A_HEAD_DIM))
    qs = jnp.concatenate([qs, qs], axis=2)
    page_flat = page_table.reshape(-1)
    pages, ipages = min(SAMPLE_PAGES, npages), min(SAMPLE_IDX_PAGES, npages)
    o_s = _dsa_sample(page_flat, qp, wcol, ki_sb, qs, k_sb, v_sb, jnp.swapaxes(cache_kidx, 1, 2),
                      cache_k.reshape(n_pool, PAGE * A_KV_HEADS, A_HEAD_DIM),
                      cache_v.reshape(n_pool, PAGE * A_KV_HEADS, A_HEAD_DIM),
                      nseq, npages, ipages, pages, nq, min(TOPK_MAX, (n_past + nq) // 4))
    oa_s = (o_s[:, :, :rep * nq].reshape(nseq, A_KV_HEADS, rep, nq, A_HEAD_DIM)
            .transpose(0, 3, 1, 2, 4).reshape(nseq, nq, A_WIDTH))
    oa_s = jnp.pad(oa_s, ((0, 0), (0, S_PAD - nq), (0, 0))).reshape(rows_s, A_WIDTH).astype(BF16)
    mk_s = cache_mem_k.reshape(nseq * N_MEM, M_WIDTH).astype(BF16)
    mv_s = cache_mem_v.reshape(nseq * N_MEM, M_WIDTH).astype(BF16)
    y_s, vn_s = _merge(xs, oa_s, zr_s, mk_s, mv_s, w_s[:, :S_PAD, :S_PAD], b_s[:, :S_PAD].T, g_sgu2, g_mq2,
                       w_pa_b, w_pb_b, w_pc_b, w_out_b, rows_s, S_PAD, nseq, lambda i: (0, 0), True)

    take = lambda a, shape: a.reshape(nseq, S_PAD, -1)[:, :nq].reshape(shape)
    return (y_p.reshape(batch, seq, D_MODEL),
            take(y_s, (nseq, nq, D_MODEL)),
            k_p.reshape(batch, seq, A_KV_HEADS, A_HEAD_DIM),
            v_p.reshape(batch, seq, A_KV_HEADS, A_HEAD_DIM),
            ki_p.reshape(batch, seq, IDX_DIM),
            mk_p.reshape(batch, N_MEM, M_HEADS, M_HEAD_DIM),
            mv_p.reshape(batch, N_MEM, M_HEADS, M_HEAD_DIM),
            take(k_s, (nseq, nq, A_KV_HEADS, A_HEAD_DIM)),
            take(v_s, (nseq, nq, A_KV_HEADS, A_HEAD_DIM)),
            take(ki_s, (nseq, nq, IDX_DIM)),
            take(vn_s, (nseq, nq, B_GROUPS, B_GROUP_DIM)))
```

```python
import functools

import numpy as np
import jax
import jax.numpy as jnp
from jax import lax
from jax.experimental import pallas as pl
from jax.experimental.pallas import tpu as pltpu

F32 = jnp.float32
BF16 = jnp.bfloat16
I32 = jnp.int32

D_MODEL = 2048
PAGE = 128
A_HEADS = 8
A_KV_HEADS = 4
A_HEAD_DIM = 128
A_WIDTH = A_HEADS * A_HEAD_DIM
A_KV_WIDTH = A_KV_HEADS * A_HEAD_DIM
IDX_HEADS = 16
IDX_DIM = 64
TOPK_MAX = 256
Q_BLOCK = 128
ROPE_THETA = 10000.0
CHUNK = 128
B_GROUPS = 8
B_GROUP_DIM = 128
B_WIDTH = B_GROUPS * B_GROUP_DIM
N_MEM = 256
M_HEADS = 4
M_HEAD_DIM = 256
M_WIDTH = M_HEADS * M_HEAD_DIM
EPS = 1e-6

OFF_K = A_WIDTH
OFF_V = OFF_K + A_KV_WIDTH
OFF_QI = OFF_V + A_KV_WIDTH
OFF_KI = OFF_QI + IDX_HEADS * IDX_DIM
OFF_WI = OFF_KI + IDX_DIM
OFF_REST = OFF_WI + IDX_HEADS
A_COLS = 3200
REST_COLS = A_WIDTH + 3 * B_WIDTH + 2 * M_WIDTH + 3 * D_MODEL

Q_SCALE = float(np.log2(np.e)) * A_HEAD_DIM ** -0.5
S_PAD = 16
PROJ_TN = 1024
BISECT_STEPS = 24
SAMPLE_PAGES = 16
SAMPLE_IDX_PAGES = 64
DMA_SLOTS = 4
INT_MIN = np.int32(-2 ** 31)
INT_MAX = np.int32(2 ** 31 - 1)
NEG_INF = float("-inf")

V7X_VMEM_LIMIT = 56 * 1024 * 1024


def _cparams(sem):
    return pltpu.CompilerParams(dimension_semantics=sem, vmem_limit_bytes=V7X_VMEM_LIMIT)


def _dot(a, b):
    return jnp.dot(a, b, preferred_element_type=F32)


def _dot_nt(a, b):
    return lax.dot_general(a, b, (((1,), (1,)), ((), ())), preferred_element_type=F32)


def _rms(x, g):
    return x * lax.rsqrt(jnp.mean(x * x, axis=-1, keepdims=True) + EPS) * g


def _const_spec(shape):
    nd = len(shape)
    return pl.BlockSpec(shape, lambda *_: (0,) * nd, pipeline_mode=pl.Buffered(1))


def _proj_a_kernel(x_ref, g_ref, w_ref, gq_ref, gk_ref, cos_ref, sin_ref,
                   q_ref, k_ref, v_ref, kb_ref, vb_ref, qi_ref, ki_ref, kib_ref, wi_ref):
    h = _rms(x_ref[...], g_ref[...]).astype(BF16)
    z = _dot_nt(h, w_ref[...])
    tm = z.shape[0]
    c, s = cos_ref[...], sin_ref[...]
    c_sw, s_sw = pltpu.roll(c, 64, 1), pltpu.roll(s, 64, 1)
    lane = lax.broadcasted_iota(I32, (1, 128), 1)
    low = lane < 64
    cq, sq = jnp.where(low, c, c_sw), jnp.where(low, -s, s_sw)
    ci, si = jnp.where(low, c_sw, c), jnp.where(low, s_sw, s)
    first_half = lane % IDX_DIM < IDX_DIM // 2
    sia, sib = jnp.where(first_half, -si, 0.0), jnp.where(first_half, 0.0, si)

    def norm_rope(zz, g):
        n = _rms(zz, g)
        return n * cq + pltpu.roll(n, A_HEAD_DIM // 2, 1) * sq

    def rope_idx(zz):
        return zz * ci + pltpu.roll(zz, 96, 1) * sia + pltpu.roll(zz, 32, 1) * sib

    for hh in range(A_HEADS):
        sl = slice(hh * A_HEAD_DIM, (hh + 1) * A_HEAD_DIM)
        q_ref[:, sl] = (norm_rope(z[:, sl], gq_ref[...]) * Q_SCALE).astype(BF16)
    for hh in range(A_KV_HEADS):
        sl = slice(hh * A_HEAD_DIM, (hh + 1) * A_HEAD_DIM)
        kh = norm_rope(z[:, OFF_K + hh * A_HEAD_DIM:OFF_K + (hh + 1) * A_HEAD_DIM], gk_ref[...])
        vh = z[:, OFF_V + hh * A_HEAD_DIM:OFF_V + (hh + 1) * A_HEAD_DIM]
        head_rows = pl.ds(hh, tm, stride=A_KV_HEADS)
        k_ref[head_rows, :] = kh
        v_ref[head_rows, :] = vh
        kb_ref[:, sl] = kh.astype(BF16)
        vb_ref[:, sl] = vh.astype(BF16)
    for t in range(IDX_HEADS * IDX_DIM // 128):
        sl = slice(t * 128, (t + 1) * 128)
        qi_ref[:, sl] = rope_idx(z[:, OFF_QI + t * 128:OFF_QI + (t + 1) * 128]).astype(BF16)
    last = z[:, OFF_KI:OFF_KI + 128]
    ki = rope_idx(last)[:, :IDX_DIM]
    ki_ref[...] = ki
    kib_ref[...] = ki.astype(BF16)
    wi_ref[...] = last[:, IDX_DIM:IDX_DIM + IDX_HEADS] * ((IDX_HEADS ** -0.5) * (IDX_DIM ** -0.5))


def _proj_a(x, g_pre, w_a, g_q, g_k, tabs, tm, tab_map):
    n = x.shape[0]
    row = lambda w: pl.BlockSpec((tm, w), lambda i: (i, 0))
    tab = pl.BlockSpec((tm, 128), tab_map)
    outs = [(1, A_WIDTH, BF16), (A_KV_HEADS, A_HEAD_DIM, F32), (A_KV_HEADS, A_HEAD_DIM, F32),
            (1, A_KV_WIDTH, BF16), (1, A_KV_WIDTH, BF16),
            (1, IDX_HEADS * IDX_DIM, BF16), (1, IDX_DIM, F32), (1, IDX_DIM, BF16), (1, IDX_HEADS, F32)]
    return pl.pallas_call(
        _proj_a_kernel,
        grid=(n // tm,),
        in_specs=[row(D_MODEL), _const_spec((1, D_MODEL)), _const_spec((A_COLS, D_MODEL)),
                  _const_spec((1, A_HEAD_DIM)), _const_spec((1, A_HEAD_DIM)), tab, tab],
        out_specs=[pl.BlockSpec((tm * r, w), lambda i: (i, 0)) for r, w, _ in outs],
        out_shape=[jax.ShapeDtypeStruct((n * r, w), dt) for r, w, dt in outs],
        compiler_params=_cparams(("arbitrary",)),
        name="proj_a",
    )(x, g_pre, w_a, g_q, g_k, *tabs)


def _proj_rest_kernel(x_ref, xs_ref, g_ref, w_ref, o_ref, os_ref, h_ref, hs_ref):
    i, j = pl.program_id(0), pl.program_id(1)

    @pl.when(j == 0)
    def _():
        h_ref[...] = _rms(x_ref[...], g_ref[...]).astype(BF16)

    @pl.when((i == 0) & (j == 0))
    def _():
        hs_ref[...] = _rms(xs_ref[...], g_ref[...]).astype(BF16)

    w = w_ref[...].astype(BF16)
    o_ref[...] = _dot_nt(h_ref[...], w).astype(BF16)

    @pl.when(i == 0)
    def _():
        os_ref[...] = _dot_nt(hs_ref[...], w).astype(BF16)


def _proj_rest(x, xs, g_pre, w_in_t, tm, tn):
    n, ns = x.shape[0], xs.shape[0]
    ncols = REST_COLS // tn
    xs_cols = lambda i, j: (0, jnp.where(i == 0, j, ncols - 1))
    return pl.pallas_call(
        _proj_rest_kernel,
        grid=(n // tm, ncols),
        in_specs=[pl.BlockSpec((tm, D_MODEL), lambda i, j: (i, 0)),
                  pl.BlockSpec((ns, D_MODEL), lambda i, j: (0, 0), pipeline_mode=pl.Buffered(1)),
                  pl.BlockSpec((1, D_MODEL), lambda i, j: (0, 0)),
                  pl.BlockSpec((pl.Element(tn), pl.Element(D_MODEL)),
                               lambda i, j: (pl.multiple_of(OFF_REST + j * tn, 16), 0))],
        out_specs=[pl.BlockSpec((tm, tn), lambda i, j: (i, j)), pl.BlockSpec((ns, tn), xs_cols)],
        out_shape=[jax.ShapeDtypeStruct((n, REST_COLS), BF16), jax.ShapeDtypeStruct((ns, REST_COLS), BF16)],
        scratch_shapes=[pltpu.VMEM((tm, D_MODEL), BF16), pltpu.VMEM((ns, D_MODEL), BF16)],
        compiler_params=_cparams(("arbitrary", "arbitrary")),
        name="proj_rest",
    )(x, xs, g_pre, w_in_t)


def _mem_kv_kernel(x_ref, g_ref, w_ref, gk_ref, k_ref, v_ref, kb_ref, vb_ref):
    h = _rms(x_ref[...], g_ref[...]).astype(BF16)
    z = _dot(h, w_ref[...])
    for hh in range(M_HEADS):
        sl = slice(hh * M_HEAD_DIM, (hh + 1) * M_HEAD_DIM)
        kh = _rms(z[:, sl], gk_ref[...])
        k_ref[:, sl] = kh
        kb_ref[:, sl] = kh.astype(BF16)
    v = z[:, M_WIDTH:]
    v_ref[...] = v
    vb_ref[...] = v.astype(BF16)


def _mem_kv(mem, g_mem, w_mem, g_mk):
    n = mem.shape[0]
    blk = pl.BlockSpec((N_MEM, M_WIDTH), lambda i: (i, 0))
    return pl.pallas_call(
        _mem_kv_kernel,
        grid=(n // N_MEM,),
        in_specs=[pl.BlockSpec((N_MEM, D_MODEL), lambda i: (i, 0)), _const_spec((1, D_MODEL)),
                  _const_spec((D_MODEL, 2 * M_WIDTH)), _const_spec((1, M_HEAD_DIM))],
        out_specs=[blk, blk, blk, blk],
        out_shape=[jax.ShapeDtypeStruct((n, M_WIDTH), dt) for dt in (F32, F32, BF16, BF16)],
        compiler_params=_cparams(("arbitrary",)),
        name="mem_kv",
    )(mem, g_mem, w_mem, g_mk)


KEY_NEG_INF = np.int32(-0x7F800000)


def _key_to_f32(key):
    return pltpu.bitcast(jnp.where(key >= 0, key, INT_MIN - key), F32)


def _select_params(count, topk, pos_bits, amax, n_valid, t_ref, j_ref):
    kf = float(topk)
    hi0 = amax * 1.000001 + 1e-30
    all_selected = (n_valid <= kf).astype(I32)

    def bisect_body(_, state):
        lo, hi, t, done = state
        mid = 0.5 * lo + 0.5 * hi
        n_ge = count(lambda s, p: s >= mid)
        hit = (n_ge == kf) & (done == 0)
        return (jnp.where(n_ge >= kf, mid, lo), jnp.where(n_ge >= kf, hi, mid),
                jnp.where(hit, mid, t), jnp.where(hit, 1, done))

    _, _, t_bis, done = lax.fori_loop(
        0, BISECT_STEPS, bisect_body, (-hi0, hi0, jnp.full(hi0.shape, NEG_INF, F32), all_selected))
    t_ref[...] = t_bis
    j_ref[...] = jnp.full(j_ref.shape, INT_MAX, I32)

    @pl.when(jnp.min(done) == 0)
    def _():
        t0 = jnp.where(count(lambda s, p: s >= 0.0) >= kf, jnp.int32(0), INT_MIN)

        def bit_body(b, t):
            cand = t + lax.shift_left(jnp.int32(1), 30 - b)
            cand_f = _key_to_f32(cand)
            return jnp.where(count(lambda s, p: s >= cand_f) >= kf, cand, t)

        t = _key_to_f32(jnp.maximum(lax.fori_loop(0, 31, bit_body, t0), KEY_NEG_INF))
        t_ref[...] = t
        tie = (count(lambda s, p: s >= t) > kf) & (t > NEG_INF)

        @pl.when(jnp.max(tie.astype(I32)) > 0)
        def _():
            n_gt = count(lambda s, p: s > t)

            def pos_body(b, p_lo):
                cand = p_lo + lax.shift_left(jnp.int32(1), pos_bits - 1 - b)
                n_eq = count(lambda s, p: (s == t) & (p < cand))
                return jnp.where(n_gt + n_eq < kf, cand, p_lo)

            p_sel = lax.fori_loop(0, pos_bits, pos_body, jnp.zeros(t.shape, I32))
            j_ref[...] = jnp.where(tie, p_sel, INT_MAX)

    return jnp.maximum(t_ref[...], -hi0), j_ref[...]


def _fori_by_two(n, body, init):
    carry = lax.fori_loop(0, n // 2, lambda j, c: body(2 * j + 1, body(2 * j, c)), init)
    return lax.cond(n % 2 == 1, lambda c: body(n - 1, c), lambda c: c, carry)


def _dsa_prompt_kernel(qi_ref, wit_ref, q_ref, ki_ref, k_ref, v_ref, o_ref,
                       sc_ref, bias_ref, vt_ref, t_ref, j_ref, s_ref, acc_ref, *, topk, kc, pos_bits):
    i = pl.program_id(1)
    nck = (i * Q_BLOCK + Q_BLOCK + kc - 1) // kc
    nchunks = vt_ref.shape[0]
    rep = A_HEADS // A_KV_HEADS
    q_pos = i * Q_BLOCK + lax.broadcasted_iota(I32, (1, Q_BLOCK), 1)
    sub = lax.broadcasted_iota(I32, (kc, 1), 0)

    @pl.when(i == 0)
    def _():
        for c in range(nchunks):
            for g in range(A_KV_HEADS):
                gsl = slice(g * A_HEAD_DIM, (g + 1) * A_HEAD_DIM)
                vt_ref[c, gsl, :] = v_ref[c * kc:(c + 1) * kc, gsl].astype(F32).T.astype(BF16)

    def chunk_rows(c):
        return pl.ds(pl.multiple_of(c * kc, kc), kc)

    qi = qi_ref[...]
    wit = wit_ref[...]
    qi_pairs = [jnp.concatenate([qi[:, (2 * j) * IDX_DIM:(2 * j + 1) * IDX_DIM],
                                 qi[:, (2 * j + 1) * IDX_DIM:(2 * j + 2) * IDX_DIM]], axis=0)
                for j in range(IDX_HEADS // 2)]

    def score_body(c, amax):
        kic = ki_ref[chunk_rows(c), :]
        acc = jnp.zeros((kc, Q_BLOCK), F32)
        for j in range(IDX_HEADS // 2):
            d = _dot_nt(kic, qi_pairs[j])
            acc = acc + jnp.maximum(d[:, :Q_BLOCK], 0.0) * wit[2 * j:2 * j + 1, :]
            acc = acc + jnp.maximum(d[:, Q_BLOCK:], 0.0) * wit[2 * j + 1:2 * j + 2, :]
        sc_ref[c] = jnp.where(c * kc + sub <= q_pos, acc, NEG_INF)
        return jnp.maximum(amax, jnp.max(jnp.abs(acc).reshape(kc // 64, 64, Q_BLOCK), axis=0))

    amax = jnp.max(_fori_by_two(nck, score_body, jnp.zeros((64, Q_BLOCK), F32)), axis=0, keepdims=True)

    def count(pred):
        def body(c, acc):
            part = jnp.where(pred(sc_ref[c], c * kc + sub), 1.0, 0.0)
            return acc + jnp.sum(part.reshape(kc // 64, 64, Q_BLOCK), axis=0)

        acc = lax.fori_loop(0, nck, body, jnp.zeros((64, Q_BLOCK), F32))
        return jnp.sum(acc, axis=0, keepdims=True)

    thr, jsel = _select_params(count, topk, pos_bits, amax, (q_pos + 1).astype(F32), t_ref, j_ref)

    def bias_body(c, carry):
        s = sc_ref[c]
        tie_thr = jnp.where(c * kc + sub <= jsel, thr, jnp.inf)
        bias_ref[c] = jnp.where((s > thr) | (s >= tie_thr), 0.0, NEG_INF)
        return carry

    lax.fori_loop(0, nck, bias_body, 0)

    gsl = lambda g: slice(g * A_HEAD_DIM, (g + 1) * A_HEAD_DIM)
    row0 = lambda v: jnp.full((1, rep * Q_BLOCK), v, F32)
    groups = tuple(range(A_KV_HEADS))
    qgs = [jnp.concatenate([q_ref[:, gsl(g * rep + r)] for r in range(rep)], axis=0) for g in groups]


    def qk_body(c, ms):
        b = bias_ref[c]
        bias = jnp.concatenate([b] * rep, axis=1)
        out = []
        for g in groups:
            s = _dot_nt(k_ref[chunk_rows(c), gsl(g)], qgs[g]) + bias
            s_ref[c, g] = s
            out.append(jnp.maximum(ms[g], jnp.max(s, axis=0, keepdims=True)))
        return tuple(out)

    ms = _fori_by_two(nck, qk_body, (row0(NEG_INF),) * len(groups))
    acc_ref[...] = jnp.zeros(acc_ref.shape, F32)

    def pv_body(c, ls):
        out = []
        for g in groups:
            p = jnp.exp2(s_ref[c, g] - ms[g])
            out.append(ls[g] + jnp.sum(p, axis=0, keepdims=True))
            acc_ref[g] += _dot(vt_ref[c, gsl(g), :], p.astype(BF16))
        return tuple(out)

    ls = _fori_by_two(nck, pv_body, (row0(0.0),) * len(groups))
    for g in groups:
        o = acc_ref[g] / ls[g]
        for r in range(rep):
            o_ref[:, gsl(g * rep + r)] = o[:, r * Q_BLOCK:(r + 1) * Q_BLOCK].T.astype(BF16)


def _dsa_prompt(qi, wit, q, ki_b, k_b, v_b, batch, seq, topk):
    nqb = seq // Q_BLOCK
    rep = A_HEADS // A_KV_HEADS
    kc = min(512, seq)
    nchunks = seq // kc
    pos_bits = max(1, int(np.ceil(np.log2(seq))))
    qrow = lambda w: pl.BlockSpec((Q_BLOCK, w), lambda b, i: (b * nqb + i, 0))
    seqblk = lambda w: pl.BlockSpec((seq, w), lambda b, i: (b, 0), pipeline_mode=pl.Buffered(1))
    return pl.pallas_call(
        functools.partial(_dsa_prompt_kernel, topk=topk, kc=kc, pos_bits=pos_bits),
        grid=(batch, nqb),
        in_specs=[qrow(IDX_HEADS * IDX_DIM), pl.BlockSpec((IDX_HEADS, Q_BLOCK), lambda b, i: (0, b * nqb + i)),
                  qrow(A_WIDTH), seqblk(IDX_DIM), seqblk(A_KV_WIDTH), seqblk(A_KV_WIDTH)],
        out_specs=qrow(A_WIDTH),
        out_shape=jax.ShapeDtypeStruct((batch * seq, A_WIDTH), BF16),
        scratch_shapes=[pltpu.VMEM((nchunks, kc, Q_BLOCK), F32), pltpu.VMEM((nchunks, kc, Q_BLOCK), F32),
                        pltpu.VMEM((nchunks, A_KV_WIDTH, kc), BF16),
                        pltpu.VMEM((1, Q_BLOCK), F32), pltpu.VMEM((1, Q_BLOCK), I32),
                        pltpu.VMEM((nchunks, A_KV_HEADS, kc, rep * Q_BLOCK), F32),
                        pltpu.VMEM((A_KV_HEADS, A_HEAD_DIM, rep * Q_BLOCK), F32)],
        compiler_params=_cparams(("arbitrary", "arbitrary")),
        name="dsa_prompt",
    )(qi, wit, q, ki_b, k_b, v_b)


def _fold_heads(e, nq):
    acc = e[0:8]
    for t in range(1, e.shape[0] // 8):
        acc = acc + e[8 * t:8 * (t + 1)]
    return acc + pltpu.roll(acc, nq, 0)


def _tree(op, parts):
    parts = list(parts)
    while len(parts) > 1:
        parts = [op(a, b) for a, b in zip(parts[::2], parts[1::2])] + parts[len(parts) & ~1:]
    return parts[0]


def _lane_fold(op, x):
    return _tree(op, [x[:, t * 128:(t + 1) * 128] for t in range(x.shape[1] // 128)])


def _dsa_sample_kernel(pt_ref, qp_ref, wc_ref, kin_ref, qs_ref, kn_ref, vn_ref, kidx_hbm, k_hbm, v_hbm, o_ref,
                       sc_ref, scn_ref, bias_ref, biasn_ref, t_ref, j_ref, s_ref, sn_ref, m_ref, mfin_ref, l_ref, acc_ref,
                       idx_raw, kv_raw, idx_sem, kv_sem, *, ipages, pages, npages, nq, topk, pos_bits):
    b_idx, s_idx = pl.program_id(0), pl.program_id(1)
    ikw, kw = ipages * PAGE, pages * PAGE
    isteps, steps = npages // ipages, npages // pages
    nsteps = isteps + 2 * steps
    n_past = npages * PAGE
    rows_q = qs_ref.shape[1]
    dup = lambda x8: jnp.concatenate([x8] * (rows_q // 8), axis=0)
    gsl = lambda g: slice(g * A_HEAD_DIM, (g + 1) * A_HEAD_DIM)

    ahead = DMA_SLOTS - 1
    n = b_idx * nsteps + s_idx
    slot = n % DMA_SLOTS
    phases = ((kidx_hbm, idx_raw, idx_sem, 0, isteps, ipages),
              (k_hbm, kv_raw, kv_sem, isteps, steps, pages),
              (v_hbm, kv_raw, kv_sem, isteps + steps, steps, pages))

    def block_copies(seq, step, sl, start):
        for src, dst, sem, first, count, per_step in phases:
            @pl.when((step >= first) & (step < first + count))
            def _(src=src, dst=dst, sem=sem, first=first, per_step=per_step):
                for p in range(per_step):
                    page = pt_ref[seq * npages + (step - first) * per_step + p] if start else 0
                    copy = pltpu.make_async_copy(src.at[page], dst.at[sl, p], sem.at[sl])
                    copy.start() if start else copy.wait()

    @pl.when(n == 0)
    def _():
        for d in range(ahead):
            block_copies(jnp.int32(0), jnp.int32(d), d, True)

    @pl.when(n + ahead < pl.num_programs(0) * nsteps)
    def _():
        wrap = (s_idx + ahead >= nsteps).astype(I32)
        to_slot = slot + ahead
        block_copies(b_idx + wrap, s_idx + ahead - wrap * nsteps,
                     jnp.where(to_slot >= DMA_SLOTS, to_slot - DMA_SLOTS, to_slot), True)

    block_copies(b_idx, s_idx, slot, False)

    def head_rows(g):
        rows = pl.ds(g, PAGE, stride=A_KV_HEADS)
        return jnp.concatenate([kv_raw[slot, p, rows, :] for p in range(pages)], axis=0).astype(BF16)

    @pl.when(s_idx < isteps)
    def _():
        qp = qp_ref[...]
        wc = wc_ref[...]
        kpt = jnp.concatenate([idx_raw[slot, p] for p in range(ipages)], axis=1).astype(BF16)
        sc_ref[s_idx] = _fold_heads(jnp.maximum(_dot(qp, kpt), 0.0) * wc, nq)

        @pl.when(s_idx == isteps - 1)
        def _():
            qrow = lax.broadcasted_iota(I32, (8, 1), 0) % nq
            lane = lax.broadcasted_iota(I32, (1, ikw), 1)
            lane_n = lax.broadcasted_iota(I32, (1, PAGE), 1)
            kin = jnp.concatenate([kin_ref[...], jnp.zeros((PAGE - S_PAD, IDX_DIM), BF16)], axis=0)
            scn = _fold_heads(jnp.maximum(_dot_nt(qp, kin), 0.0) * wc, nq)
            scn_ref[...] = jnp.where(lane_n <= qrow, scn, NEG_INF)

            def count(pred):
                ones = _tree(jnp.add, [jnp.where(pred(sc_ref[t], t * ikw + lane), 1.0, 0.0) for t in range(isteps)])
                cn = jnp.where(pred(scn_ref[...], n_past + lane_n), 1.0, 0.0)
                return (jnp.sum(_lane_fold(jnp.add, ones), axis=-1, keepdims=True)
                        + jnp.sum(cn, axis=-1, keepdims=True))

            amax = jnp.maximum(
                jnp.max(_lane_fold(jnp.maximum, _tree(jnp.maximum, [jnp.abs(sc_ref[t]) for t in range(isteps)])),
                        axis=-1, keepdims=True),
                jnp.max(jnp.abs(scn), axis=-1, keepdims=True))
            n_valid = (n_past + 1 + qrow).astype(F32)
            thr, jsel = _select_params(count, topk, pos_bits, amax, n_valid, t_ref, j_ref)
            for t in range(steps):
                first = t * kw
                sp = sc_ref[first // ikw][:, first % ikw:first % ikw + kw]
                pos = first + lane[:, :kw]
                bias_ref[t] = jnp.where((sp > thr) | ((sp == thr) & (pos <= jsel)), 0.0, NEG_INF)
            sn = scn_ref[...]
            seln = ((sn > thr) | ((sn == thr) & (n_past + lane_n <= jsel))) & (lane_n <= qrow)
            biasn_ref[...] = jnp.where(seln, 0.0, NEG_INF)

    @pl.when((s_idx >= isteps) & (s_idx < isteps + steps))
    def _():
        t = s_idx - isteps

        @pl.when(t == 0)
        def _():
            m_ref[...] = jnp.full(m_ref.shape, NEG_INF, F32)

        bias = dup(bias_ref[t])
        for g in range(A_KV_HEADS):
            s = _dot_nt(qs_ref[g], head_rows(g)) + bias
            s_ref[g, t] = s
            m_ref[g] = jnp.maximum(m_ref[g], _lane_fold(jnp.maximum, s))

        @pl.when(t == steps - 1)
        def _():
            bn = dup(biasn_ref[:, :S_PAD])
            for g in range(A_KV_HEADS):
                sn = _dot_nt(qs_ref[g], kn_ref[:, gsl(g)]) + bn
                sn_ref[g] = sn
                mfin_ref[g] = jnp.maximum(jnp.max(m_ref[g], axis=-1, keepdims=True),
                                          jnp.max(sn, axis=-1, keepdims=True))

    @pl.when(s_idx >= isteps + steps)
    def _():
        t = s_idx - isteps - steps

        @pl.when(t == 0)
        def _():
            l_ref[...] = jnp.zeros(l_ref.shape, F32)
            acc_ref[...] = jnp.zeros(acc_ref.shape, F32)

        for g in range(A_KV_HEADS):
            p = jnp.exp2(s_ref[g, t] - mfin_ref[g])
            l_ref[g] += _lane_fold(jnp.add, p)
            acc_ref[g] += _dot(p.astype(BF16), head_rows(g))

        @pl.when(t == steps - 1)
        def _():
            for g in range(A_KV_HEADS):
                pn = jnp.exp2(sn_ref[g] - mfin_ref[g])
                l = jnp.sum(l_ref[g], axis=-1, keepdims=True) + jnp.sum(pn, axis=-1, keepdims=True)
                o_ref[g] = (acc_ref[g] + _dot(pn.astype(BF16), vn_ref[:, gsl(g)])) / l


def _dsa_sample(page_flat, qp, wcol, ki_new_b, qs, k_new_b, v_new_b, cache_kidx_t, cache_k, cache_v,
                nseq, npages, ipages, pages, nq, topk):
    isteps, steps = npages // ipages, npages // pages
    n_past = npages * PAGE
    ikw, kw = ipages * PAGE, pages * PAGE
    rows = IDX_HEADS * nq
    rows_q = qs.shape[2]
    pos_bits = int(np.ceil(np.log2(n_past + PAGE)))

    seq3 = lambda shape: pl.BlockSpec((None,) + shape, lambda b, s, pt: (b,) + (0,) * len(shape))
    new_rows = lambda w: pl.BlockSpec((S_PAD, w), lambda b, s, pt: (b, 0))
    in_hbm = pl.BlockSpec(memory_space=pl.ANY)
    grid_spec = pltpu.PrefetchScalarGridSpec(
        num_scalar_prefetch=1,
        grid=(nseq, isteps + 2 * steps),
        in_specs=[seq3((rows, IDX_DIM)), seq3((rows, 1)), new_rows(IDX_DIM),
                  seq3((A_KV_HEADS, rows_q, A_HEAD_DIM)), new_rows(A_KV_WIDTH), new_rows(A_KV_WIDTH),
                  in_hbm, in_hbm, in_hbm],
        out_specs=seq3((A_KV_HEADS, rows_q, A_HEAD_DIM)),
        scratch_shapes=[pltpu.VMEM((isteps, 8, ikw), F32), pltpu.VMEM((8, PAGE), F32),
                        pltpu.VMEM((steps, 8, kw), F32), pltpu.VMEM((8, PAGE), F32),
                        pltpu.VMEM((8, 1), F32), pltpu.VMEM((8, 1), I32),
                        pltpu.VMEM((A_KV_HEADS, steps, rows_q, kw), F32),
                        pltpu.VMEM((A_KV_HEADS, rows_q, S_PAD), F32),
                        pltpu.VMEM((A_KV_HEADS, rows_q, 128), F32),
                        pltpu.VMEM((A_KV_HEADS, rows_q, 1), F32),
                        pltpu.VMEM((A_KV_HEADS, rows_q, 128), F32),
                        pltpu.VMEM((A_KV_HEADS, rows_q, A_HEAD_DIM), F32),
                        pltpu.VMEM((DMA_SLOTS, ipages, IDX_DIM, PAGE), F32),
                        pltpu.VMEM((DMA_SLOTS, pages, PAGE * A_KV_HEADS, A_HEAD_DIM), F32),
                        pltpu.SemaphoreType.DMA((DMA_SLOTS,)), pltpu.SemaphoreType.DMA((DMA_SLOTS,))],
    )
    return pl.pallas_call(
        functools.partial(_dsa_sample_kernel, ipages=ipages, pages=pages, npages=npages, nq=nq, topk=topk,
                          pos_bits=pos_bits),
        grid_spec=grid_spec,
        out_shape=jax.ShapeDtypeStruct((nseq, A_KV_HEADS, rows_q, A_HEAD_DIM), F32),
        compiler_params=_cparams(("arbitrary", "arbitrary")),
        name="dsa_sample",
    )(page_flat, qp, wcol, ki_new_b, qs, k_new_b, v_new_b, cache_kidx_t, cache_k, cache_v)


def _merge_kernel(x_ref, oa_ref, ag_ref, bu_ref, bv_ref, bg_ref, cq_ref, cg_ref, ra_ref, rb_ref, rc_ref,
                  mk_ref, mv_ref, ws_ref, bs_ref, gsgu_ref, gmq_ref, wpa_ref, wpb_ref, wpc_ref, wout_ref,
                  y_ref, *maybe_vn_ref, tm, chunk, mem_groups):
    f32 = lambda r: r[...].astype(F32)
    silu = lambda t: t * jax.nn.sigmoid(t)

    vn = _rms(f32(bv_ref), gsgu_ref[...])
    if maybe_vn_ref:
        maybe_vn_ref[0][...] = vn
    vnb = vn.astype(BF16)
    bu = f32(bu_ref)
    tril = (lax.broadcasted_iota(I32, (chunk, chunk), 1) <= lax.broadcasted_iota(I32, (chunk, chunk), 0))
    ob_cols = []
    for g in range(B_GROUPS):
        wg = jnp.where(tril, ws_ref[g], 0.0).astype(BF16)
        gsl = slice(g * B_GROUP_DIM, (g + 1) * B_GROUP_DIM)
        parts = [_dot(wg, vnb[c * chunk:(c + 1) * chunk, gsl]) + bs_ref[:, g:g + 1] for c in range(tm // chunk)]
        ob_cols.append(parts[0] if len(parts) == 1 else jnp.concatenate(parts, axis=0))
    ob = bu * jnp.concatenate(ob_cols, axis=1)
    pb = _dot((ob * silu(f32(bg_ref))).astype(BF16), wpb_ref[...])

    cq = f32(cq_ref)
    rows_g = tm // mem_groups
    oc_cols = []
    for hh in range(M_HEADS):
        hsl = slice(hh * M_HEAD_DIM, (hh + 1) * M_HEAD_DIM)
        qn = (_rms(cq[:, hsl], gmq_ref[...]) * (M_HEAD_DIM ** -0.5)).astype(BF16)
        oc_rows = []
        for u in range(mem_groups):
            msl = slice(u * N_MEM, (u + 1) * N_MEM)
            s = _dot_nt(qn[u * rows_g:(u + 1) * rows_g], mk_ref[msl, hsl])
            p = jnp.exp(s - jnp.max(s, axis=-1, keepdims=True))
            oc_rows.append(_dot(p.astype(BF16), mv_ref[msl, hsl]) / jnp.sum(p, axis=-1, keepdims=True))
        oc_cols.append(oc_rows[0] if mem_groups == 1 else jnp.concatenate(oc_rows, axis=0))
    oc = jnp.concatenate(oc_cols, axis=1)
    pc = _dot((oc * silu(f32(cg_ref))).astype(BF16), wpc_ref[...])

    pa = _dot((f32(oa_ref) * silu(f32(ag_ref))).astype(BF16), wpa_ref[...])
    sig = jax.nn.sigmoid
    m = sig(f32(ra_ref)) * pa + sig(f32(rb_ref)) * pb + sig(f32(rc_ref)) * pc
    y_ref[...] = x_ref[...] + _dot(m.astype(BF16), wout_ref[...])


def _merge(x, oa, zr, mk_b, mv_b, ws, bs_t, g_sgu, g_mq, w_pa, w_pb, w_pc, w_out, tm, chunk, mem_groups, mem_map,
           emit_vn):
    n = x.shape[0]
    col = lambda w, j: pl.BlockSpec((tm, w), lambda i, j=j: (i, j))
    mem = pl.BlockSpec((mem_groups * N_MEM, M_WIDTH), mem_map)
    in_specs = [col(D_MODEL, 0), col(A_WIDTH, 0),
                col(1024, 0), col(1024, 1), col(1024, 2), col(1024, 3), col(1024, 4), col(1024, 5),
                col(2048, 3), col(2048, 4), col(2048, 5),
                mem, mem,
                _const_spec((B_GROUPS, chunk, chunk)), _const_spec((chunk, B_GROUPS)),
                _const_spec((1, B_WIDTH)), _const_spec((1, M_HEAD_DIM)),
                _const_spec((A_WIDTH, D_MODEL)), _const_spec((B_WIDTH, D_MODEL)),
                _const_spec((M_WIDTH, D_MODEL)), _const_spec((D_MODEL, D_MODEL))]
    out_specs = [col(D_MODEL, 0)]
    out_shape = [jax.ShapeDtypeStruct((n, D_MODEL), F32)]
    if emit_vn:
        out_specs.append(col(B_WIDTH, 0))
        out_shape.append(jax.ShapeDtypeStruct((n, B_WIDTH), F32))
    return pl.pallas_call(
        functools.partial(_merge_kernel, tm=tm, chunk=chunk, mem_groups=mem_groups),
        grid=(n // tm,),
        in_specs=in_specs,
        out_specs=out_specs,
        out_shape=out_shape,
        compiler_params=_cparams(("arbitrary",)),
        name="merge",
    )(x, oa, *([zr] * 9), mk_b, mv_b, ws, bs_t, g_sgu, g_mq, w_pa, w_pb, w_pc, w_out)


def _rope_tables(pos):
    freq = lambda half: ROPE_THETA ** (-jnp.arange(half, dtype=F32) / half)
    f_idx = freq(IDX_DIM // 2)
    ang = pos.astype(F32)[:, None] * jnp.concatenate([freq(A_HEAD_DIM // 2), f_idx, f_idx])[None, :]
    return jnp.cos(ang), jnp.sin(ang)


def kernel(x_prompt, x_sample, cache_k, cache_v, cache_kidx, cache_mem_k, cache_mem_v, page_table,
           mem_prompt, g_pre, w_in, g_q, g_k, g_mq, g_mk, g_mem, w_mem_kv, g_sgu, w_s, b_s,
           w_pa, w_pb, w_pc, w_out):
    batch, seq, _ = x_prompt.shape
    nseq, nq, _ = x_sample.shape
    assert nq == 4 and nq <= S_PAD
    npages = page_table.shape[1]
    n_past = npages * PAGE
    n_pool = cache_k.shape[0]
    row2 = lambda a: a.reshape(1, -1)

    w_in_t = w_in.T
    w_a = w_in_t[:A_COLS].astype(BF16)
    w_pa_b, w_pb_b, w_pc_b, w_out_b = (w.astype(BF16) for w in (w_pa, w_pb, w_pc, w_out))
    w_mem_b = w_mem_kv.astype(BF16)
    g_pre2, g_q2, g_k2, g_mq2, g_mk2, g_mem2, g_sgu2 = map(row2, (g_pre, g_q, g_k, g_mq, g_mk, g_mem, g_sgu))

    xp = x_prompt.reshape(batch * seq, D_MODEL)
    tm_a = min(256, seq)
    nblk = seq // tm_a
    tabs_p = _rope_tables(jnp.arange(seq))
    q, k_p, v_p, k_b, v_b, qi, ki_p, ki_b, wi = _proj_a(
        xp, g_pre2, w_a, g_q2, g_k2, tabs_p, tm_a, lambda i: (i % nblk, 0))
    xs = jnp.pad(x_sample, ((0, 0), (0, S_PAD - nq), (0, 0))).reshape(nseq * S_PAD, D_MODEL)
    rows_s = nseq * S_PAD
    zr, zr_s = _proj_rest(xp, xs, g_pre2, w_in_t, min(1024, seq), PROJ_TN)
    mk_p, mv_p, mk_b, mv_b = _mem_kv(mem_prompt.reshape(batch * N_MEM, D_MODEL), g_mem2, w_mem_b, g_mk2)
    oa = _dsa_prompt(qi, wi.T, q, ki_b, k_b, v_b, batch, seq, min(TOPK_MAX, seq // 4))
    tm_m = min(256, seq)
    nblk_m = seq // tm_m
    (y_p,) = _merge(xp, oa, zr, mk_b, mv_b, w_s, b_s.T, g_sgu2, g_mq2, w_pa_b, w_pb_b, w_pc_b, w_out_b,
                    tm_m, CHUNK, 1, lambda i: (i // nblk_m, 0), False)

    tabs_s = tuple(jnp.tile(t, (nseq, 1)) for t in _rope_tables(n_past + jnp.arange(S_PAD)))
    q_s, k_s, v_s, k_sb, v_sb, qi_s, ki_s, ki_sb, wi_s = _proj_a(
        xs, g_pre2, w_a, g_q2, g_k2, tabs_s, rows_s, lambda i: (0, 0))

    qp = (qi_s.reshape(nseq, S_PAD, IDX_HEADS, IDX_DIM)[:, :nq]
          .transpose(0, 2, 1, 3).reshape(nseq, IDX_HEADS * nq, IDX_DIM))
    wcol = wi_s.reshape(nseq, S_PAD, IDX_HEADS)[:, :nq].transpose(0, 2, 1).reshape(nseq, IDX_HEADS * nq, 1)
    rep = A_HEADS // A_KV_HEADS
    qs = (q_s.reshape(nseq, S_PAD, A_KV_HEADS, rep, A_HEAD_DIM)[:, :nq]
          .transpose(0, 2, 3, 1, 4).reshape(nseq, A_KV_HEADS, rep * nq, A_HEAD_DIM))
    qs = jnp.concatenate([qs, qs], axis=2)
    page_flat = page_table.reshape(-1)
    pages, ipages = min(SAMPLE_PAGES, npages), min(SAMPLE_IDX_PAGES, npages)
    o_s = _dsa_sample(page_flat, qp, wcol, ki_sb, qs, k_sb, v_sb, jnp.swapaxes(cache_kidx, 1, 2),
                      cache_k.reshape(n_pool, PAGE * A_KV_HEADS, A_HEAD_DIM),
                      cache_v.reshape(n_pool, PAGE * A_KV_HEADS, A_HEAD_DIM),
                      nseq, npages, ipages, pages, nq, min(TOPK_MAX, (n_past + nq) // 4))
    oa_s = (o_s[:, :, :rep * nq].reshape(nseq, A_KV_HEADS, rep, nq, A_HEAD_DIM)
            .transpose(0, 3, 1, 2, 4).reshape(nseq, nq, A_WIDTH))
    oa_s = jnp.pad(oa_s, ((0, 0), (0, S_PAD - nq), (0, 0))).reshape(rows_s, A_WIDTH).astype(BF16)
    mk_s = cache_mem_k.reshape(nseq * N_MEM, M_WIDTH).astype(BF16)
    mv_s = cache_mem_v.reshape(nseq * N_MEM, M_WIDTH).astype(BF16)
    y_s, vn_s = _merge(xs, oa_s, zr_s, mk_s, mv_s, w_s[:, :S_PAD, :S_PAD], b_s[:, :S_PAD].T, g_sgu2, g_mq2,
                       w_pa_b, w_pb_b, w_pc_b, w_out_b, rows_s, S_PAD, nseq, lambda i: (0, 0), True)

    take = lambda a, shape: a.reshape(nseq, S_PAD, -1)[:, :nq].reshape(shape)
    return (y_p.reshape(batch, seq, D_MODEL),
            take(y_s, (nseq, nq, D_MODEL)),
            k_p.reshape(batch, seq, A_KV_HEADS, A_HEAD_DIM),
            v_p.reshape(batch, seq, A_KV_HEADS, A_HEAD_DIM),
            ki_p.reshape(batch, seq, IDX_DIM),
            mk_p.reshape(batch, N_MEM, M_HEADS, M_HEAD_DIM),
            mv_p.reshape(batch, N_MEM, M_HEADS, M_HEAD_DIM),
            take(k_s, (nseq, nq, A_KV_HEADS, A_HEAD_DIM)),
            take(v_s, (nseq, nq, A_KV_HEADS, A_HEAD_DIM)),
            take(ki_s, (nseq, nq, IDX_DIM)),
            take(vn_s, (nseq, nq, B_GROUPS, B_GROUP_DIM)))
```

```python
import functools

import numpy as np
import jax
import jax.numpy as jnp
from jax import lax
from jax.experimental import pallas as pl
from jax.experimental.pallas import tpu as pltpu

F32 = jnp.float32
BF16 = jnp.bfloat16
I32 = jnp.int32

D_MODEL = 2048
PAGE = 128
A_HEADS = 8
A_KV_HEADS = 4
A_HEAD_DIM = 128
A_WIDTH = A_HEADS * A_HEAD_DIM
A_KV_WIDTH = A_KV_HEADS * A_HEAD_DIM
IDX_HEADS = 16
IDX_DIM = 64
TOPK_MAX = 256
Q_BLOCK = 128
ROPE_THETA = 10000.0
CHUNK = 128
B_GROUPS = 8
B_GROUP_DIM = 128
B_WIDTH = B_GROUPS * B_GROUP_DIM
N_MEM = 256
M_HEADS = 4
M_HEAD_DIM = 256
M_WIDTH = M_HEADS * M_HEAD_DIM
EPS = 1e-6

OFF_K = A_WIDTH
OFF_V = OFF_K + A_KV_WIDTH
OFF_QI = OFF_V + A_KV_WIDTH
OFF_KI = OFF_QI + IDX_HEADS * IDX_DIM
OFF_WI = OFF_KI + IDX_DIM
OFF_REST = OFF_WI + IDX_HEADS
A_COLS = 3200
REST_COLS = A_WIDTH + 3 * B_WIDTH + 2 * M_WIDTH + 3 * D_MODEL

Q_SCALE = float(np.log2(np.e)) * A_HEAD_DIM ** -0.5
S_PAD = 16
PROJ_TN = 1024
BISECT_STEPS = 24
SAMPLE_PAGES = 32
SAMPLE_IDX_PAGES = 64
DMA_SLOTS = 3
INT_MIN = np.int32(-2 ** 31)
INT_MAX = np.int32(2 ** 31 - 1)
NEG_INF = float("-inf")

V7X_VMEM_LIMIT = 56 * 1024 * 1024


def _cparams(sem):
    return pltpu.CompilerParams(dimension_semantics=sem, vmem_limit_bytes=V7X_VMEM_LIMIT)


def _dot(a, b):
    return jnp.dot(a, b, preferred_element_type=F32)


def _dot_nt(a, b):
    return lax.dot_general(a, b, (((1,), (1,)), ((), ())), preferred_element_type=F32)


def _rms(x, g):
    return x * lax.rsqrt(jnp.mean(x * x, axis=-1, keepdims=True) + EPS) * g


def _const_spec(shape):
    nd = len(shape)
    return pl.BlockSpec(shape, lambda *_: (0,) * nd, pipeline_mode=pl.Buffered(1))


def _proj_a_kernel(x_ref, g_ref, w_ref, gq_ref, gk_ref, cos_ref, sin_ref,
                   q_ref, k_ref, v_ref, kb_ref, vb_ref, qi_ref, ki_ref, kib_ref, wi_ref):
    h = _rms(x_ref[...], g_ref[...]).astype(BF16)
    z = _dot_nt(h, w_ref[...])
    tm = z.shape[0]
    c, s = cos_ref[...], sin_ref[...]
    c_sw, s_sw = pltpu.roll(c, 64, 1), pltpu.roll(s, 64, 1)
    lane = lax.broadcasted_iota(I32, (1, 128), 1)
    low = lane < 64
    cq, sq = jnp.where(low, c, c_sw), jnp.where(low, -s, s_sw)
    ci, si = jnp.where(low, c_sw, c), jnp.where(low, s_sw, s)
    first_half = lane % IDX_DIM < IDX_DIM // 2
    sia, sib = jnp.where(first_half, -si, 0.0), jnp.where(first_half, 0.0, si)

    def norm_rope(zz, g):
        n = _rms(zz, g)
        return n * cq + pltpu.roll(n, A_HEAD_DIM // 2, 1) * sq

    def rope_idx(zz):
        return zz * ci + pltpu.roll(zz, 96, 1) * sia + pltpu.roll(zz, 32, 1) * sib

    for hh in range(A_HEADS):
        sl = slice(hh * A_HEAD_DIM, (hh + 1) * A_HEAD_DIM)
        q_ref[:, sl] = (norm_rope(z[:, sl], gq_ref[...]) * Q_SCALE).astype(BF16)
    for hh in range(A_KV_HEADS):
        sl = slice(hh * A_HEAD_DIM, (hh + 1) * A_HEAD_DIM)
        kh = norm_rope(z[:, OFF_K + hh * A_HEAD_DIM:OFF_K + (hh + 1) * A_HEAD_DIM], gk_ref[...])
        vh = z[:, OFF_V + hh * A_HEAD_DIM:OFF_V + (hh + 1) * A_HEAD_DIM]
        head_rows = pl.ds(hh, tm, stride=A_KV_HEADS)
        k_ref[head_rows, :] = kh
        v_ref[head_rows, :] = vh
        kb_ref[:, sl] = kh.astype(BF16)
        vb_ref[:, sl] = vh.astype(BF16)
    for t in range(IDX_HEADS * IDX_DIM // 128):
        sl = slice(t * 128, (t + 1) * 128)
        qi_ref[:, sl] = rope_idx(z[:, OFF_QI + t * 128:OFF_QI + (t + 1) * 128]).astype(BF16)
    last = z[:, OFF_KI:OFF_KI + 128]
    ki = rope_idx(last)[:, :IDX_DIM]
    ki_ref[...] = ki
    kib_ref[...] = ki.astype(BF16)
    wi_ref[...] = last[:, IDX_DIM:IDX_DIM + IDX_HEADS] * ((IDX_HEADS ** -0.5) * (IDX_DIM ** -0.5))


def _proj_a(x, g_pre, w_a, g_q, g_k, tabs, tm, tab_map):
    n = x.shape[0]
    row = lambda w: pl.BlockSpec((tm, w), lambda i: (i, 0))
    tab = pl.BlockSpec((tm, 128), tab_map)
    outs = [(1, A_WIDTH, BF16), (A_KV_HEADS, A_HEAD_DIM, F32), (A_KV_HEADS, A_HEAD_DIM, F32),
            (1, A_KV_WIDTH, BF16), (1, A_KV_WIDTH, BF16),
            (1, IDX_HEADS * IDX_DIM, BF16), (1, IDX_DIM, F32), (1, IDX_DIM, BF16), (1, IDX_HEADS, F32)]
    return pl.pallas_call(
        _proj_a_kernel,
        grid=(n // tm,),
        in_specs=[row(D_MODEL), _const_spec((1, D_MODEL)), _const_spec((A_COLS, D_MODEL)),
                  _const_spec((1, A_HEAD_DIM)), _const_spec((1, A_HEAD_DIM)), tab, tab],
        out_specs=[pl.BlockSpec((tm * r, w), lambda i: (i, 0)) for r, w, _ in outs],
        out_shape=[jax.ShapeDtypeStruct((n * r, w), dt) for r, w, dt in outs],
        compiler_params=_cparams(("arbitrary",)),
        name="proj_a",
    )(x, g_pre, w_a, g_q, g_k, *tabs)


def _proj_rest_kernel(x_ref, xs_ref, g_ref, w_ref, o_ref, os_ref, h_ref, hs_ref):
    i, j = pl.program_id(0), pl.program_id(1)

    @pl.when(j == 0)
    def _():
        h_ref[...] = _rms(x_ref[...], g_ref[...]).astype(BF16)

    @pl.when((i == 0) & (j == 0))
    def _():
        hs_ref[...] = _rms(xs_ref[...], g_ref[...]).astype(BF16)

    w = w_ref[...].astype(BF16)
    o_ref[...] = _dot_nt(h_ref[...], w).astype(BF16)

    @pl.when(i == 0)
    def _():
        os_ref[...] = _dot_nt(hs_ref[...], w).astype(BF16)


def _proj_rest(x, xs, g_pre, w_in_t, tm, tn):
    n, ns = x.shape[0], xs.shape[0]
    ncols = REST_COLS // tn
    xs_cols = lambda i, j: (0, jnp.where(i == 0, j, ncols - 1))
    return pl.pallas_call(
        _proj_rest_kernel,
        grid=(n // tm, ncols),
        in_specs=[pl.BlockSpec((tm, D_MODEL), lambda i, j: (i, 0)),
                  pl.BlockSpec((ns, D_MODEL), lambda i, j: (0, 0), pipeline_mode=pl.Buffered(1)),
                  pl.BlockSpec((1, D_MODEL), lambda i, j: (0, 0)),
                  pl.BlockSpec((pl.Element(tn), pl.Element(D_MODEL)),
                               lambda i, j: (pl.multiple_of(OFF_REST + j * tn, 16), 0))],
        out_specs=[pl.BlockSpec((tm, tn), lambda i, j: (i, j)), pl.BlockSpec((ns, tn), xs_cols)],
        out_shape=[jax.ShapeDtypeStruct((n, REST_COLS), BF16), jax.ShapeDtypeStruct((ns, REST_COLS), BF16)],
        scratch_shapes=[pltpu.VMEM((tm, D_MODEL), BF16), pltpu.VMEM((ns, D_MODEL), BF16)],
        compiler_params=_cparams(("arbitrary", "arbitrary")),
        name="proj_rest",
    )(x, xs, g_pre, w_in_t)


def _mem_kv_kernel(x_ref, g_ref, w_ref, gk_ref, k_ref, v_ref, kb_ref, vb_ref):
    h = _rms(x_ref[...], g_ref[...]).astype(BF16)
    z = _dot(h, w_ref[...])
    for hh in range(M_HEADS):
        sl = slice(hh * M_HEAD_DIM, (hh + 1) * M_HEAD_DIM)
        kh = _rms(z[:, sl], gk_ref[...])
        k_ref[:, sl] = kh
        kb_ref[:, sl] = kh.astype(BF16)
    v = z[:, M_WIDTH:]
    v_ref[...] = v
    vb_ref[...] = v.astype(BF16)


def _mem_kv(mem, g_mem, w_mem, g_mk):
    n = mem.shape[0]
    blk = pl.BlockSpec((N_MEM, M_WIDTH), lambda i: (i, 0))
    return pl.pallas_call(
        _mem_kv_kernel,
        grid=(n // N_MEM,),
        in_specs=[pl.BlockSpec((N_MEM, D_MODEL), lambda i: (i, 0)), _const_spec((1, D_MODEL)),
                  _const_spec((D_MODEL, 2 * M_WIDTH)), _const_spec((1, M_HEAD_DIM))],
        out_specs=[blk, blk, blk, blk],
        out_shape=[jax.ShapeDtypeStruct((n, M_WIDTH), dt) for dt in (F32, F32, BF16, BF16)],
        compiler_params=_cparams(("arbitrary",)),
        name="mem_kv",
    )(mem, g_mem, w_mem, g_mk)


KEY_NEG_INF = np.int32(-0x7F800000)


def _key_to_f32(key):
    return pltpu.bitcast(jnp.where(key >= 0, key, INT_MIN - key), F32)


def _select_params(count, topk, pos_bits, amax, n_valid, t_ref, j_ref):
    kf = float(topk)
    hi0 = amax * 1.000001 + 1e-30
    all_selected = (n_valid <= kf).astype(I32)

    def bisect_body(_, state):
        lo, hi, t, done = state
        mid = 0.5 * lo + 0.5 * hi
        n_ge = count(lambda s, p: s >= mid)
        hit = (n_ge == kf) & (done == 0)
        return (jnp.where(n_ge >= kf, mid, lo), jnp.where(n_ge >= kf, hi, mid),
                jnp.where(hit, mid, t), jnp.where(hit, 1, done))

    _, _, t_bis, done = lax.fori_loop(
        0, BISECT_STEPS, bisect_body, (-hi0, hi0, jnp.full(hi0.shape, NEG_INF, F32), all_selected))
    t_ref[...] = t_bis
    j_ref[...] = jnp.full(j_ref.shape, INT_MAX, I32)

    @pl.when(jnp.min(done) == 0)
    def _():
        t0 = jnp.where(count(lambda s, p: s >= 0.0) >= kf, jnp.int32(0), INT_MIN)

        def bit_body(b, t):
            cand = t + lax.shift_left(jnp.int32(1), 30 - b)
            cand_f = _key_to_f32(cand)
            return jnp.where(count(lambda s, p: s >= cand_f) >= kf, cand, t)

        t = _key_to_f32(jnp.maximum(lax.fori_loop(0, 31, bit_body, t0), KEY_NEG_INF))
        t_ref[...] = t
        tie = (count(lambda s, p: s >= t) > kf) & (t > NEG_INF)

        @pl.when(jnp.max(tie.astype(I32)) > 0)
        def _():
            n_gt = count(lambda s, p: s > t)

            def pos_body(b, p_lo):
                cand = p_lo + lax.shift_left(jnp.int32(1), pos_bits - 1 - b)
                n_eq = count(lambda s, p: (s == t) & (p < cand))
                return jnp.where(n_gt + n_eq < kf, cand, p_lo)

            p_sel = lax.fori_loop(0, pos_bits, pos_body, jnp.zeros(t.shape, I32))
            j_ref[...] = jnp.where(tie, p_sel, INT_MAX)

    return jnp.maximum(t_ref[...], -hi0), j_ref[...]


def _fori_by_two(n, body, init):
    carry = lax.fori_loop(0, n // 2, lambda j, c: body(2 * j + 1, body(2 * j, c)), init)
    return lax.cond(n % 2 == 1, lambda c: body(n - 1, c), lambda c: c, carry)


def _dsa_prompt_kernel(qi_ref, wit_ref, q_ref, ki_ref, k_ref, v_ref, o_ref,
                       sc_ref, bias_ref, vt_ref, t_ref, j_ref, s_ref, acc_ref, *, topk, kc, pos_bits):
    i = pl.program_id(1)
    nck = (i * Q_BLOCK + Q_BLOCK + kc - 1) // kc
    nchunks = vt_ref.shape[0]
    rep = A_HEADS // A_KV_HEADS
    q_pos = i * Q_BLOCK + lax.broadcasted_iota(I32, (1, Q_BLOCK), 1)
    sub = lax.broadcasted_iota(I32, (kc, 1), 0)

    @pl.when(i == 0)
    def _():
        for c in range(nchunks):
            for g in range(A_KV_HEADS):
                gsl = slice(g * A_HEAD_DIM, (g + 1) * A_HEAD_DIM)
                vt_ref[c, gsl, :] = v_ref[c * kc:(c + 1) * kc, gsl].astype(F32).T.astype(BF16)

    def chunk_rows(c):
        return pl.ds(pl.multiple_of(c * kc, kc), kc)

    qi = qi_ref[...]
    wit = wit_ref[...]
    qi_pairs = [jnp.concatenate([qi[:, (2 * j) * IDX_DIM:(2 * j + 1) * IDX_DIM],
                                 qi[:, (2 * j + 1) * IDX_DIM:(2 * j + 2) * IDX_DIM]], axis=0)
                for j in range(IDX_HEADS // 2)]

    def score_body(c, amax):
        kic = ki_ref[chunk_rows(c), :]
        acc = jnp.zeros((kc, Q_BLOCK), F32)
        for j in range(IDX_HEADS // 2):
            d = _dot_nt(kic, qi_pairs[j])
            acc = acc + jnp.maximum(d[:, :Q_BLOCK], 0.0) * wit[2 * j:2 * j + 1, :]
            acc = acc + jnp.maximum(d[:, Q_BLOCK:], 0.0) * wit[2 * j + 1:2 * j + 2, :]
        sc_ref[c] = jnp.where(c * kc + sub <= q_pos, acc, NEG_INF)
        return jnp.maximum(amax, jnp.max(jnp.abs(acc).reshape(kc // 64, 64, Q_BLOCK), axis=0))

    amax = jnp.max(_fori_by_two(nck, score_body, jnp.zeros((64, Q_BLOCK), F32)), axis=0, keepdims=True)

    def count(pred):
        def body(c, acc):
            part = jnp.where(pred(sc_ref[c], c * kc + sub), 1.0, 0.0)
            return acc + jnp.sum(part.reshape(kc // 64, 64, Q_BLOCK), axis=0)

        acc = lax.fori_loop(0, nck, body, jnp.zeros((64, Q_BLOCK), F32))
        return jnp.sum(acc, axis=0, keepdims=True)

    thr, jsel = _select_params(count, topk, pos_bits, amax, (q_pos + 1).astype(F32), t_ref, j_ref)

    def bias_body(c, carry):
        s = sc_ref[c]
        tie_thr = jnp.where(c * kc + sub <= jsel, thr, jnp.inf)
        bias_ref[c] = jnp.where((s > thr) | (s >= tie_thr), 0.0, NEG_INF)
        return carry

    lax.fori_loop(0, nck, bias_body, 0)

    gsl = lambda g: slice(g * A_HEAD_DIM, (g + 1) * A_HEAD_DIM)
    row0 = lambda v: jnp.full((1, rep * Q_BLOCK), v, F32)
    groups = tuple(range(A_KV_HEADS))
    qgs = [jnp.concatenate([q_ref[:, gsl(g * rep + r)] for r in range(rep)], axis=0) for g in groups]


    def qk_body(c, ms):
        b = bias_ref[c]
        bias = jnp.concatenate([b] * rep, axis=1)
        out = []
        for g in groups:
            s = _dot_nt(k_ref[chunk_rows(c), gsl(g)], qgs[g]) + bias
            s_ref[c, g] = s
            out.append(jnp.maximum(ms[g], jnp.max(s, axis=0, keepdims=True)))
        return tuple(out)

    ms = _fori_by_two(nck, qk_body, (row0(NEG_INF),) * len(groups))
    acc_ref[...] = jnp.zeros(acc_ref.shape, F32)

    def pv_body(c, ls):
        out = []
        for g in groups:
            p = jnp.exp2(s_ref[c, g] - ms[g])
            out.append(ls[g] + jnp.sum(p, axis=0, keepdims=True))
            acc_ref[g] += _dot(vt_ref[c, gsl(g), :], p.astype(BF16))
        return tuple(out)

    ls = _fori_by_two(nck, pv_body, (row0(0.0),) * len(groups))
    for g in groups:
        o = acc_ref[g] / ls[g]
        for r in range(rep):
            o_ref[:, gsl(g * rep + r)] = o[:, r * Q_BLOCK:(r + 1) * Q_BLOCK].T.astype(BF16)


def _dsa_prompt(qi, wit, q, ki_b, k_b, v_b, batch, seq, topk):
    nqb = seq // Q_BLOCK
    rep = A_HEADS // A_KV_HEADS
    kc = min(512, seq)
    nchunks = seq // kc
    pos_bits = max(1, int(np.ceil(np.log2(seq))))
    qrow = lambda w: pl.BlockSpec((Q_BLOCK, w), lambda b, i: (b * nqb + i, 0))
    seqblk = lambda w: pl.BlockSpec((seq, w), lambda b, i: (b, 0), pipeline_mode=pl.Buffered(1))
    return pl.pallas_call(
        functools.partial(_dsa_prompt_kernel, topk=topk, kc=kc, pos_bits=pos_bits),
        grid=(batch, nqb),
        in_specs=[qrow(IDX_HEADS * IDX_DIM), pl.BlockSpec((IDX_HEADS, Q_BLOCK), lambda b, i: (0, b * nqb + i)),
                  qrow(A_WIDTH), seqblk(IDX_DIM), seqblk(A_KV_WIDTH), seqblk(A_KV_WIDTH)],
        out_specs=qrow(A_WIDTH),
        out_shape=jax.ShapeDtypeStruct((batch * seq, A_WIDTH), BF16),
        scratch_shapes=[pltpu.VMEM((nchunks, kc, Q_BLOCK), F32), pltpu.VMEM((nchunks, kc, Q_BLOCK), F32),
                        pltpu.VMEM((nchunks, A_KV_WIDTH, kc), BF16),
                        pltpu.VMEM((1, Q_BLOCK), F32), pltpu.VMEM((1, Q_BLOCK), I32),
                        pltpu.VMEM((nchunks, A_KV_HEADS, kc, rep * Q_BLOCK), F32),
                        pltpu.VMEM((A_KV_HEADS, A_HEAD_DIM, rep * Q_BLOCK), F32)],
        compiler_params=_cparams(("arbitrary", "arbitrary")),
        name="dsa_prompt",
    )(qi, wit, q, ki_b, k_b, v_b)


def _fold_heads(e, nq):
    acc = e[0:8]
    for t in range(1, e.shape[0] // 8):
        acc = acc + e[8 * t:8 * (t + 1)]
    return acc + pltpu.roll(acc, nq, 0)


def _tree(op, parts):
    parts = list(parts)
    while len(parts) > 1:
        parts = [op(a, b) for a, b in zip(parts[::2], parts[1::2])] + parts[len(parts) & ~1:]
    return parts[0]


def _lane_fold(op, x):
    return _tree(op, [x[:, t * 128:(t + 1) * 128] for t in range(x.shape[1] // 128)])


def _dsa_sample_kernel(pt_ref, qp_ref, wc_ref, kin_ref, qs_ref, kn_ref, vn_ref, kidx_hbm, k_hbm, v_hbm, o_ref,
                       sc_ref, scn_ref, bias_ref, biasn_ref, t_ref, j_ref, s_ref, sn_ref, m_ref, mfin_ref, l_ref, acc_ref,
                       idx_raw, kv_raw, idx_sem, kv_sem, *, ipages, pages, npages, nq, topk, pos_bits):
    b_idx, s_idx = pl.program_id(0), pl.program_id(1)
    ikw, kw = ipages * PAGE, pages * PAGE
    isteps, steps = npages // ipages, npages // pages
    nsteps = isteps + 2 * steps
    n_past = npages * PAGE
    rows_q = qs_ref.shape[1]
    dup = lambda x8: jnp.concatenate([x8] * (rows_q // 8), axis=0)
    gsl = lambda g: slice(g * A_HEAD_DIM, (g + 1) * A_HEAD_DIM)

    ahead = DMA_SLOTS - 1
    n = b_idx * nsteps + s_idx
    slot = n % DMA_SLOTS
    phases = ((kidx_hbm, idx_raw, idx_sem, 0, isteps, ipages),
              (k_hbm, kv_raw, kv_sem, isteps, steps, pages),
              (v_hbm, kv_raw, kv_sem, isteps + steps, steps, pages))

    def block_copies(seq, step, sl, start):
        for src, dst, sem, first, count, per_step in phases:
            @pl.when((step >= first) & (step < first + count))
            def _(src=src, dst=dst, sem=sem, first=first, per_step=per_step):
                for p in range(per_step):
                    page = pt_ref[seq * npages + (step - first) * per_step + p] if start else 0
                    copy = pltpu.make_async_copy(src.at[page], dst.at[sl, p], sem.at[sl])
                    copy.start() if start else copy.wait()

    @pl.when(n == 0)
    def _():
        for d in range(ahead):
            block_copies(jnp.int32(0), jnp.int32(d), d, True)

    @pl.when(n + ahead < pl.num_programs(0) * nsteps)
    def _():
        wrap = (s_idx + ahead >= nsteps).astype(I32)
        to_slot = slot + ahead
        block_copies(b_idx + wrap, s_idx + ahead - wrap * nsteps,
                     jnp.where(to_slot >= DMA_SLOTS, to_slot - DMA_SLOTS, to_slot), True)

    block_copies(b_idx, s_idx, slot, False)

    def head_rows(g):
        rows = pl.ds(g, PAGE, stride=A_KV_HEADS)
        return jnp.concatenate([kv_raw[slot, p, rows, :] for p in range(pages)], axis=0).astype(BF16)

    @pl.when(s_idx < isteps)
    def _():
        qp = qp_ref[...]
        wc = wc_ref[...]
        kpt = jnp.concatenate([idx_raw[slot, p] for p in range(ipages)], axis=1).astype(BF16)
        sc_ref[s_idx] = _fold_heads(jnp.maximum(_dot(qp, kpt), 0.0) * wc, nq)

        @pl.when(s_idx == isteps - 1)
        def _():
            qrow = lax.broadcasted_iota(I32, (8, 1), 0) % nq
            lane = lax.broadcasted_iota(I32, (1, ikw), 1)
            lane_n = lax.broadcasted_iota(I32, (1, PAGE), 1)
            kin = jnp.concatenate([kin_ref[...], jnp.zeros((PAGE - S_PAD, IDX_DIM), BF16)], axis=0)
            scn = _fold_heads(jnp.maximum(_dot_nt(qp, kin), 0.0) * wc, nq)
            scn_ref[...] = jnp.where(lane_n <= qrow, scn, NEG_INF)

            def count(pred):
                ones = _tree(jnp.add, [jnp.where(pred(sc_ref[t], t * ikw + lane), 1.0, 0.0) for t in range(isteps)])
                cn = jnp.where(pred(scn_ref[...], n_past + lane_n), 1.0, 0.0)
                return (jnp.sum(_lane_fold(jnp.add, ones), axis=-1, keepdims=True)
                        + jnp.sum(cn, axis=-1, keepdims=True))

            amax = jnp.maximum(
                jnp.max(_lane_fold(jnp.maximum, _tree(jnp.maximum, [jnp.abs(sc_ref[t]) for t in range(isteps)])),
                        axis=-1, keepdims=True),
                jnp.max(jnp.abs(scn), axis=-1, keepdims=True))
            n_valid = (n_past + 1 + qrow).astype(F32)
            thr, jsel = _select_params(count, topk, pos_bits, amax, n_valid, t_ref, j_ref)
            for t in range(steps):
                first = t * kw
                sp = sc_ref[first // ikw][:, first % ikw:first % ikw + kw]
                pos = first + lane[:, :kw]
                bias_ref[t] = jnp.where((sp > thr) | ((sp == thr) & (pos <= jsel)), 0.0, NEG_INF)
            sn = scn_ref[...]
            seln = ((sn > thr) | ((sn == thr) & (n_past + lane_n <= jsel))) & (lane_n <= qrow)
            biasn_ref[...] = jnp.where(seln, 0.0, NEG_INF)

    @pl.when((s_idx >= isteps) & (s_idx < isteps + steps))
    def _():
        t = s_idx - isteps

        @pl.when(t == 0)
        def _():
            m_ref[...] = jnp.full(m_ref.shape, NEG_INF, F32)

        bias = dup(bias_ref[t])
        for g in range(A_KV_HEADS):
            s = _dot_nt(qs_ref[g], head_rows(g)) + bias
            s_ref[g, t] = s
            m_ref[g] = jnp.maximum(m_ref[g], _lane_fold(jnp.maximum, s))

        @pl.when(t == steps - 1)
        def _():
            bn = dup(biasn_ref[:, :S_PAD])
            for g in range(A_KV_HEADS):
                sn = _dot_nt(qs_ref[g], kn_ref[:, gsl(g)]) + bn
                sn_ref[g] = sn
                mfin_ref[g] = jnp.maximum(jnp.max(m_ref[g], axis=-1, keepdims=True),
                                          jnp.max(sn, axis=-1, keepdims=True))

    @pl.when(s_idx >= isteps + steps)
    def _():
        t = s_idx - isteps - steps

        @pl.when(t == 0)
        def _():
            l_ref[...] = jnp.zeros(l_ref.shape, F32)
            acc_ref[...] = jnp.zeros(acc_ref.shape, F32)

        for g in range(A_KV_HEADS):
            p = jnp.exp2(s_ref[g, t] - mfin_ref[g])
            l_ref[g] += _lane_fold(jnp.add, p)
            acc_ref[g] += _dot(p.astype(BF16), head_rows(g))

        @pl.when(t == steps - 1)
        def _():
            for g in range(A_KV_HEADS):
                pn = jnp.exp2(sn_ref[g] - mfin_ref[g])
                l = jnp.sum(l_ref[g], axis=-1, keepdims=True) + jnp.sum(pn, axis=-1, keepdims=True)
                o_ref[g] = (acc_ref[g] + _dot(pn.astype(BF16), vn_ref[:, gsl(g)])) / l


def _dsa_sample(page_flat, qp, wcol, ki_new_b, qs, k_new_b, v_new_b, cache_kidx_t, cache_k, cache_v,
                nseq, npages, ipages, pages, nq, topk):
    isteps, steps = npages // ipages, npages // pages
    n_past = npages * PAGE
    ikw, kw = ipages * PAGE, pages * PAGE
    rows = IDX_HEADS * nq
    rows_q = qs.shape[2]
    pos_bits = int(np.ceil(np.log2(n_past + PAGE)))

    seq3 = lambda shape: pl.BlockSpec((None,) + shape, lambda b, s, pt: (b,) + (0,) * len(shape))
    new_rows = lambda w: pl.BlockSpec((S_PAD, w), lambda b, s, pt: (b, 0))
    in_hbm = pl.BlockSpec(memory_space=pl.ANY)
    grid_spec = pltpu.PrefetchScalarGridSpec(
        num_scalar_prefetch=1,
        grid=(nseq, isteps + 2 * steps),
        in_specs=[seq3((rows, IDX_DIM)), seq3((rows, 1)), new_rows(IDX_DIM),
                  seq3((A_KV_HEADS, rows_q, A_HEAD_DIM)), new_rows(A_KV_WIDTH), new_rows(A_KV_WIDTH),
                  in_hbm, in_hbm, in_hbm],
        out_specs=seq3((A_KV_HEADS, rows_q, A_HEAD_DIM)),
        scratch_shapes=[pltpu.VMEM((isteps, 8, ikw), F32), pltpu.VMEM((8, PAGE), F32),
                        pltpu.VMEM((steps, 8, kw), F32), pltpu.VMEM((8, PAGE), F32),
                        pltpu.VMEM((8, 1), F32), pltpu.VMEM((8, 1), I32),
                        pltpu.VMEM((A_KV_HEADS, steps, rows_q, kw), F32),
                        pltpu.VMEM((A_KV_HEADS, rows_q, S_PAD), F32),
                        pltpu.VMEM((A_KV_HEADS, rows_q, 128), F32),
                        pltpu.VMEM((A_KV_HEADS, rows_q, 1), F32),
                        pltpu.VMEM((A_KV_HEADS, rows_q, 128), F32),
                        pltpu.VMEM((A_KV_HEADS, rows_q, A_HEAD_DIM), F32),
                        pltpu.VMEM((DMA_SLOTS, ipages, IDX_DIM, PAGE), F32),
                        pltpu.VMEM((DMA_SLOTS, pages, PAGE * A_KV_HEADS, A_HEAD_DIM), F32),
                        pltpu.SemaphoreType.DMA((DMA_SLOTS,)), pltpu.SemaphoreType.DMA((DMA_SLOTS,))],
    )
    return pl.pallas_call(
        functools.partial(_dsa_sample_kernel, ipages=ipages, pages=pages, npages=npages, nq=nq, topk=topk,
                          pos_bits=pos_bits),
        grid_spec=grid_spec,
        out_shape=jax.ShapeDtypeStruct((nseq, A_KV_HEADS, rows_q, A_HEAD_DIM), F32),
        compiler_params=_cparams(("arbitrary", "arbitrary")),
        name="dsa_sample",
    )(page_flat, qp, wcol, ki_new_b, qs, k_new_b, v_new_b, cache_kidx_t, cache_k, cache_v)


def _merge_kernel(x_ref, oa_ref, ag_ref, bu_ref, bv_ref, bg_ref, cq_ref, cg_ref, ra_ref, rb_ref, rc_ref,
                  mk_ref, mv_ref, ws_ref, bs_ref, gsgu_ref, gmq_ref, wpa_ref, wpb_ref, wpc_ref, wout_ref,
                  y_ref, *maybe_vn_ref, tm, chunk, mem_groups):
    f32 = lambda r: r[...].astype(F32)
    silu = lambda t: t * jax.nn.sigmoid(t)

    vn = _rms(f32(bv_ref), gsgu_ref[...])
    if maybe_vn_ref:
        maybe_vn_ref[0][...] = vn
    vnb = vn.astype(BF16)
    bu = f32(bu_ref)
    tril = (lax.broadcasted_iota(I32, (chunk, chunk), 1) <= lax.broadcasted_iota(I32, (chunk, chunk), 0))
    ob_cols = []
    for g in range(B_GROUPS):
        wg = jnp.where(tril, ws_ref[g], 0.0).astype(BF16)
        gsl = slice(g * B_GROUP_DIM, (g + 1) * B_GROUP_DIM)
        parts = [_dot(wg, vnb[c * chunk:(c + 1) * chunk, gsl]) + bs_ref[:, g:g + 1] for c in range(tm // chunk)]
        ob_cols.append(parts[0] if len(parts) == 1 else jnp.concatenate(parts, axis=0))
    ob = bu * jnp.concatenate(ob_cols, axis=1)
    pb = _dot((ob * silu(f32(bg_ref))).astype(BF16), wpb_ref[...])

    cq = f32(cq_ref)
    rows_g = tm // mem_groups
    oc_cols = []
    for hh in range(M_HEADS):
        hsl = slice(hh * M_HEAD_DIM, (hh + 1) * M_HEAD_DIM)
        qn = (_rms(cq[:, hsl], gmq_ref[...]) * (M_HEAD_DIM ** -0.5)).astype(BF16)
        oc_rows = []
        for u in range(mem_groups):
            msl = slice(u * N_MEM, (u + 1) * N_MEM)
            s = _dot_nt(qn[u * rows_g:(u + 1) * rows_g], mk_ref[msl, hsl])
            p = jnp.exp(s - jnp.max(s, axis=-1, keepdims=True))
            oc_rows.append(_dot(p.astype(BF16), mv_ref[msl, hsl]) / jnp.sum(p, axis=-1, keepdims=True))
        oc_cols.append(oc_rows[0] if mem_groups == 1 else jnp.concatenate(oc_rows, axis=0))
    oc = jnp.concatenate(oc_cols, axis=1)
    pc = _dot((oc * silu(f32(cg_ref))).astype(BF16), wpc_ref[...])

    pa = _dot((f32(oa_ref) * silu(f32(ag_ref))).astype(BF16), wpa_ref[...])
    sig = jax.nn.sigmoid
    m = sig(f32(ra_ref)) * pa + sig(f32(rb_ref)) * pb + sig(f32(rc_ref)) * pc
    y_ref[...] = x_ref[...] + _dot(m.astype(BF16), wout_ref[...])


def _merge(x, oa, zr, mk_b, mv_b, ws, bs_t, g_sgu, g_mq, w_pa, w_pb, w_pc, w_out, tm, chunk, mem_groups, mem_map,
           emit_vn):
    n = x.shape[0]
    col = lambda w, j: pl.BlockSpec((tm, w), lambda i, j=j: (i, j))
    mem = pl.BlockSpec((mem_groups * N_MEM, M_WIDTH), mem_map)
    in_specs = [col(D_MODEL, 0), col(A_WIDTH, 0),
                col(1024, 0), col(1024, 1), col(1024, 2), col(1024, 3), col(1024, 4), col(1024, 5),
                col(2048, 3), col(2048, 4), col(2048, 5),
                mem, mem,
                _const_spec((B_GROUPS, chunk, chunk)), _const_spec((chunk, B_GROUPS)),
                _const_spec((1, B_WIDTH)), _const_spec((1, M_HEAD_DIM)),
                _const_spec((A_WIDTH, D_MODEL)), _const_spec((B_WIDTH, D_MODEL)),
                _const_spec((M_WIDTH, D_MODEL)), _const_spec((D_MODEL, D_MODEL))]
    out_specs = [col(D_MODEL, 0)]
    out_shape = [jax.ShapeDtypeStruct((n, D_MODEL), F32)]
    if emit_vn:
        out_specs.append(col(B_WIDTH, 0))
        out_shape.append(jax.ShapeDtypeStruct((n, B_WIDTH), F32))
    return pl.pallas_call(
        functools.partial(_merge_kernel, tm=tm, chunk=chunk, mem_groups=mem_groups),
        grid=(n // tm,),
        in_specs=in_specs,
        out_specs=out_specs,
        out_shape=out_shape,
        compiler_params=_cparams(("arbitrary",)),
        name="merge",
    )(x, oa, *([zr] * 9), mk_b, mv_b, ws, bs_t, g_sgu, g_mq, w_pa, w_pb, w_pc, w_out)


def _rope_tables(pos):
    freq = lambda half: ROPE_THETA ** (-jnp.arange(half, dtype=F32) / half)
    f_idx = freq(IDX_DIM // 2)
    ang = pos.astype(F32)[:, None] * jnp.concatenate([freq(A_HEAD_DIM // 2), f_idx, f_idx])[None, :]
    return jnp.cos(ang), jnp.sin(ang)


def kernel(x_prompt, x_sample, cache_k, cache_v, cache_kidx, cache_mem_k, cache_mem_v, page_table,
           mem_prompt, g_pre, w_in, g_q, g_k, g_mq, g_mk, g_mem, w_mem_kv, g_sgu, w_s, b_s,
           w_pa, w_pb, w_pc, w_out):
    batch, seq, _ = x_prompt.shape
    nseq, nq, _ = x_sample.shape
    assert nq == 4 and nq <= S_PAD
    npages = page_table.shape[1]
    n_past = npages * PAGE
    n_pool = cache_k.shape[0]
    row2 = lambda a: a.reshape(1, -1)

    w_in_t = w_in.T
    w_a = w_in_t[:A_COLS].astype(BF16)
    w_pa_b, w_pb_b, w_pc_b, w_out_b = (w.astype(BF16) for w in (w_pa, w_pb, w_pc, w_out))
    w_mem_b = w_mem_kv.astype(BF16)
    g_pre2, g_q2, g_k2, g_mq2, g_mk2, g_mem2, g_sgu2 = map(row2, (g_pre, g_q, g_k, g_mq, g_mk, g_mem, g_sgu))

    xp = x_prompt.reshape(batch * seq, D_MODEL)
    tm_a = min(256, seq)
    nblk = seq // tm_a
    tabs_p = _rope_tables(jnp.arange(seq))
    q, k_p, v_p, k_b, v_b, qi, ki_p, ki_b, wi = _proj_a(
        xp, g_pre2, w_a, g_q2, g_k2, tabs_p, tm_a, lambda i: (i % nblk, 0))
    xs = jnp.pad(x_sample, ((0, 0), (0, S_PAD - nq), (0, 0))).reshape(nseq * S_PAD, D_MODEL)
    rows_s = nseq * S_PAD
    zr, zr_s = _proj_rest(xp, xs, g_pre2, w_in_t, min(1024, seq), PROJ_TN)
    mk_p, mv_p, mk_b, mv_b = _mem_kv(mem_prompt.reshape(batch * N_MEM, D_MODEL), g_mem2, w_mem_b, g_mk2)
    oa = _dsa_prompt(qi, wi.T, q, ki_b, k_b, v_b, batch, seq, min(TOPK_MAX, seq // 4))
    tm_m = min(256, seq)
    nblk_m = seq // tm_m
    (y_p,) = _merge(xp, oa, zr, mk_b, mv_b, w_s, b_s.T, g_sgu2, g_mq2, w_pa_b, w_pb_b, w_pc_b, w_out_b,
                    tm_m, CHUNK, 1, lambda i: (i // nblk_m, 0), False)

    tabs_s = tuple(jnp.tile(t, (nseq, 1)) for t in _rope_tables(n_past + jnp.arange(S_PAD)))
    q_s, k_s, v_s, k_sb, v_sb, qi_s, ki_s, ki_sb, wi_s = _proj_a(
        xs, g_pre2, w_a, g_q2, g_k2, tabs_s, rows_s, lambda i: (0, 0))

    qp = (qi_s.reshape(nseq, S_PAD, IDX_HEADS, IDX_DIM)[:, :nq]
          .transpose(0, 2, 1, 3).reshape(nseq, IDX_HEADS * nq, IDX_DIM))
    wcol = wi_s.reshape(nseq, S_PAD, IDX_HEADS)[:, :nq].transpose(0, 2, 1).reshape(nseq, IDX_HEADS * nq, 1)
    rep = A_HEADS // A_KV_HEADS
    qs = (q_s.reshape(nseq, S_PAD, A_KV_HEADS, rep, A_HEAD_DIM)[:, :nq]
          .transpose(0, 2, 3, 1, 4).reshape(nseq, A_KV_HEADS, rep * nq, A_HEAD_DIM))
    qs = jnp.concatenate([qs, qs], axis=2)
    page_flat = page_table.reshape(-1)
    pages, ipages = min(SAMPLE_PAGES, npages), min(SAMPLE_IDX_PAGES, npages)
    o_s = _dsa_sample(page_flat, qp, wcol, ki_sb, qs, k_sb, v_sb, jnp.swapaxes(cache_kidx, 1, 2),
                      cache_k.reshape(n_pool, PAGE * A_KV_HEADS, A_HEAD_DIM),
                      cache_v.reshape(n_pool, PAGE * A_KV_HEADS, A_HEAD_DIM),
                      nseq, npages, ipages, pages, nq, min(TOPK_MAX, (n_past + nq) // 4))
    oa_s = (o_s[:, :, :rep * nq].reshape(nseq, A_KV_HEADS, rep, nq, A_HEAD_DIM)
            .transpose(0, 3, 1, 2, 4).reshape(nseq, nq, A_WIDTH))
    oa_s = jnp.pad(oa_s, ((0, 0), (0, S_PAD - nq), (0, 0))).reshape(rows_s, A_WIDTH).astype(BF16)
    mk_s = cache_mem_k.reshape(nseq * N_MEM, M_WIDTH).astype(BF16)
    mv_s = cache_mem_v.reshape(nseq * N_MEM, M_WIDTH).astype(BF16)
    y_s, vn_s = _merge(xs, oa_s, zr_s, mk_s, mv_s, w_s[:, :S_PAD, :S_PAD], b_s[:, :S_PAD].T, g_sgu2, g_mq2,
                       w_pa_b, w_pb_b, w_pc_b, w_out_b, rows_s, S_PAD, nseq, lambda i: (0, 0), True)

    take = lambda a, shape: a.reshape(nseq, S_PAD, -1)[:, :nq].reshape(shape)
    return (y_p.reshape(batch, seq, D_MODEL),
            take(y_s, (nseq, nq, D_MODEL)),
            k_p.reshape(batch, seq, A_KV_HEADS, A_HEAD_DIM),
            v_p.reshape(batch, seq, A_KV_HEADS, A_HEAD_DIM),
            ki_p.reshape(batch, seq, IDX_DIM),
            mk_p.reshape(batch, N_MEM, M_HEADS, M_HEAD_DIM),
            mv_p.reshape(batch, N_MEM, M_HEADS, M_HEAD_DIM),
            take(k_s, (nseq, nq, A_KV_HEADS, A_HEAD_DIM)),
            take(v_s, (nseq, nq, A_KV_HEADS, A_HEAD_DIM)),
            take(ki_s, (nseq, nq, IDX_DIM)),
            take(vn_s, (nseq, nq, B_GROUPS, B_GROUP_DIM)))
```

```python
import functools

import numpy as np
import jax
import jax.numpy as jnp
from jax import lax
from jax.experimental import pallas as pl
from jax.experimental.pallas import tpu as pltpu

F32 = jnp.float32
BF16 = jnp.bfloat16
I32 = jnp.int32

D_MODEL = 2048
PAGE = 128
A_HEADS = 8
A_KV_HEADS = 4
A_HEAD_DIM = 128
A_WIDTH = A_HEADS * A_HEAD_DIM
A_KV_WIDTH = A_KV_HEADS * A_HEAD_DIM
IDX_HEADS = 16
IDX_DIM = 64
TOPK_MAX = 256
Q_BLOCK = 128
ROPE_THETA = 10000.0
CHUNK = 128
B_GROUPS = 8
B_GROUP_DIM = 128
B_WIDTH = B_GROUPS * B_GROUP_DIM
N_MEM = 256
M_HEADS = 4
M_HEAD_DIM = 256
M_WIDTH = M_HEADS * M_HEAD_DIM
EPS = 1e-6

OFF_K = A_WIDTH
OFF_V = OFF_K + A_KV_WIDTH
OFF_QI = OFF_V + A_KV_WIDTH
OFF_KI = OFF_QI + IDX_HEADS * IDX_DIM
OFF_WI = OFF_KI + IDX_DIM
OFF_REST = OFF_WI + IDX_HEADS
A_COLS = 3200
REST_COLS = A_WIDTH + 3 * B_WIDTH + 2 * M_WIDTH + 3 * D_MODEL

Q_SCALE = float(np.log2(np.e)) * A_HEAD_DIM ** -0.5
S_PAD = 16
PROJ_TN = 1024
BISECT_STEPS = 24
SAMPLE_PAGES = 32
SAMPLE_IDX_PAGES = 64
DMA_SLOTS = 4
INT_MIN = np.int32(-2 ** 31)
INT_MAX = np.int32(2 ** 31 - 1)
NEG_INF = float("-inf")

V7X_VMEM_LIMIT = 56 * 1024 * 1024


def _cparams(sem):
    return pltpu.CompilerParams(dimension_semantics=sem, vmem_limit_bytes=V7X_VMEM_LIMIT)


def _dot(a, b):
    return jnp.dot(a, b, preferred_element_type=F32)


def _dot_nt(a, b):
    return lax.dot_general(a, b, (((1,), (1,)), ((), ())), preferred_element_type=F32)


def _rms(x, g):
    return x * lax.rsqrt(jnp.mean(x * x, axis=-1, keepdims=True) + EPS) * g


def _const_spec(shape):
    nd = len(shape)
    return pl.BlockSpec(shape, lambda *_: (0,) * nd, pipeline_mode=pl.Buffered(1))


def _proj_a_kernel(x_ref, g_ref, w_ref, gq_ref, gk_ref, cos_ref, sin_ref,
                   q_ref, k_ref, v_ref, kb_ref, vb_ref, qi_ref, ki_ref, kib_ref, wi_ref):
    h = _rms(x_ref[...], g_ref[...]).astype(BF16)
    z = _dot_nt(h, w_ref[...])
    tm = z.shape[0]
    c, s = cos_ref[...], sin_ref[...]
    c_sw, s_sw = pltpu.roll(c, 64, 1), pltpu.roll(s, 64, 1)
    lane = lax.broadcasted_iota(I32, (1, 128), 1)
    low = lane < 64
    cq, sq = jnp.where(low, c, c_sw), jnp.where(low, -s, s_sw)
    ci, si = jnp.where(low, c_sw, c), jnp.where(low, s_sw, s)
    first_half = lane % IDX_DIM < IDX_DIM // 2
    sia, sib = jnp.where(first_half, -si, 0.0), jnp.where(first_half, 0.0, si)

    def norm_rope(zz, g):
        n = _rms(zz, g)
        return n * cq + pltpu.roll(n, A_HEAD_DIM // 2, 1) * sq

    def rope_idx(zz):
        return zz * ci + pltpu.roll(zz, 96, 1) * sia + pltpu.roll(zz, 32, 1) * sib

    for hh in range(A_HEADS):
        sl = slice(hh * A_HEAD_DIM, (hh + 1) * A_HEAD_DIM)
        q_ref[:, sl] = (norm_rope(z[:, sl], gq_ref[...]) * Q_SCALE).astype(BF16)
    for hh in range(A_KV_HEADS):
        sl = slice(hh * A_HEAD_DIM, (hh + 1) * A_HEAD_DIM)
        kh = norm_rope(z[:, OFF_K + hh * A_HEAD_DIM:OFF_K + (hh + 1) * A_HEAD_DIM], gk_ref[...])
        vh = z[:, OFF_V + hh * A_HEAD_DIM:OFF_V + (hh + 1) * A_HEAD_DIM]
        head_rows = pl.ds(hh, tm, stride=A_KV_HEADS)
        k_ref[head_rows, :] = kh
        v_ref[head_rows, :] = vh
        kb_ref[:, sl] = kh.astype(BF16)
        vb_ref[:, sl] = vh.astype(BF16)
    for t in range(IDX_HEADS * IDX_DIM // 128):
        sl = slice(t * 128, (t + 1) * 128)
        qi_ref[:, sl] = rope_idx(z[:, OFF_QI + t * 128:OFF_QI + (t + 1) * 128]).astype(BF16)
    last = z[:, OFF_KI:OFF_KI + 128]
    ki = rope_idx(last)[:, :IDX_DIM]
    ki_ref[...] = ki
    kib_ref[...] = ki.astype(BF16)
    wi_ref[...] = last[:, IDX_DIM:IDX_DIM + IDX_HEADS] * ((IDX_HEADS ** -0.5) * (IDX_DIM ** -0.5))


def _proj_a(x, g_pre, w_a, g_q, g_k, tabs, tm, tab_map):
    n = x.shape[0]
    row = lambda w: pl.BlockSpec((tm, w), lambda i: (i, 0))
    tab = pl.BlockSpec((tm, 128), tab_map)
    outs = [(1, A_WIDTH, BF16), (A_KV_HEADS, A_HEAD_DIM, F32), (A_KV_HEADS, A_HEAD_DIM, F32),
            (1, A_KV_WIDTH, BF16), (1, A_KV_WIDTH, BF16),
            (1, IDX_HEADS * IDX_DIM, BF16), (1, IDX_DIM, F32), (1, IDX_DIM, BF16), (1, IDX_HEADS, F32)]
    return pl.pallas_call(
        _proj_a_kernel,
        grid=(n // tm,),
        in_specs=[row(D_MODEL), _const_spec((1, D_MODEL)), _const_spec((A_COLS, D_MODEL)),
                  _const_spec((1, A_HEAD_DIM)), _const_spec((1, A_HEAD_DIM)), tab, tab],
        out_specs=[pl.BlockSpec((tm * r, w), lambda i: (i, 0)) for r, w, _ in outs],
        out_shape=[jax.ShapeDtypeStruct((n * r, w), dt) for r, w, dt in outs],
        compiler_params=_cparams(("arbitrary",)),
        name="proj_a",
    )(x, g_pre, w_a, g_q, g_k, *tabs)


def _proj_rest_kernel(x_ref, xs_ref, g_ref, w_ref, o_ref, os_ref, h_ref, hs_ref):
    i, j = pl.program_id(0), pl.program_id(1)

    @pl.when(j == 0)
    def _():
        h_ref[...] = _rms(x_ref[...], g_ref[...]).astype(BF16)

    @pl.when((i == 0) & (j == 0))
    def _():
        hs_ref[...] = _rms(xs_ref[...], g_ref[...]).astype(BF16)

    w = w_ref[...].astype(BF16)
    o_ref[...] = _dot_nt(h_ref[...], w).astype(BF16)

    @pl.when(i == 0)
    def _():
        os_ref[...] = _dot_nt(hs_ref[...], w).astype(BF16)


def _proj_rest(x, xs, g_pre, w_in_t, tm, tn):
    n, ns = x.shape[0], xs.shape[0]
    ncols = REST_COLS // tn
    xs_cols = lambda i, j: (0, jnp.where(i == 0, j, ncols - 1))
    return pl.pallas_call(
        _proj_rest_kernel,
        grid=(n // tm, ncols),
        in_specs=[pl.BlockSpec((tm, D_MODEL), lambda i, j: (i, 0)),
                  pl.BlockSpec((ns, D_MODEL), lambda i, j: (0, 0), pipeline_mode=pl.Buffered(1)),
                  pl.BlockSpec((1, D_MODEL), lambda i, j: (0, 0)),
                  pl.BlockSpec((pl.Element(tn), pl.Element(D_MODEL)),
                               lambda i, j: (pl.multiple_of(OFF_REST + j * tn, 16), 0))],
        out_specs=[pl.BlockSpec((tm, tn), lambda i, j: (i, j)), pl.BlockSpec((ns, tn), xs_cols)],
        out_shape=[jax.ShapeDtypeStruct((n, REST_COLS), BF16), jax.ShapeDtypeStruct((ns, REST_COLS), BF16)],
        scratch_shapes=[pltpu.VMEM((tm, D_MODEL), BF16), pltpu.VMEM((ns, D_MODEL), BF16)],
        compiler_params=_cparams(("arbitrary", "arbitrary")),
        name="proj_rest",
    )(x, xs, g_pre, w_in_t)


def _mem_kv_kernel(x_ref, g_ref, w_ref, gk_ref, k_ref, v_ref, kb_ref, vb_ref):
    h = _rms(x_ref[...], g_ref[...]).astype(BF16)
    z = _dot(h, w_ref[...])
    for hh in range(M_HEADS):
        sl = slice(hh * M_HEAD_DIM, (hh + 1) * M_HEAD_DIM)
        kh = _rms(z[:, sl], gk_ref[...])
        k_ref[:, sl] = kh
        kb_ref[:, sl] = kh.astype(BF16)
    v = z[:, M_WIDTH:]
    v_ref[...] = v
    vb_ref[...] = v.astype(BF16)


def _mem_kv(mem, g_mem, w_mem, g_mk):
    n = mem.shape[0]
    blk = pl.BlockSpec((N_MEM, M_WIDTH), lambda i: (i, 0))
    return pl.pallas_call(
        _mem_kv_kernel,
        grid=(n // N_MEM,),
        in_specs=[pl.BlockSpec((N_MEM, D_MODEL), lambda i: (i, 0)), _const_spec((1, D_MODEL)),
                  _const_spec((D_MODEL, 2 * M_WIDTH)), _const_spec((1, M_HEAD_DIM))],
        out_specs=[blk, blk, blk, blk],
        out_shape=[jax.ShapeDtypeStruct((n, M_WIDTH), dt) for dt in (F32, F32, BF16, BF16)],
        compiler_params=_cparams(("arbitrary",)),
        name="mem_kv",
    )(mem, g_mem, w_mem, g_mk)


KEY_NEG_INF = np.int32(-0x7F800000)


def _key_to_f32(key):
    return pltpu.bitcast(jnp.where(key >= 0, key, INT_MIN - key), F32)


def _select_params(count, topk, pos_bits, amax, n_valid, t_ref, j_ref):
    kf = float(topk)
    hi0 = amax * 1.000001 + 1e-30
    all_selected = (n_valid <= kf).astype(I32)

    def bisect_body(_, state):
        lo, hi, t, done = state
        mid = 0.5 * lo + 0.5 * hi
        n_ge = count(lambda s, p: s >= mid)
        hit = (n_ge == kf) & (done == 0)
        return (jnp.where(n_ge >= kf, mid, lo), jnp.where(n_ge >= kf, hi, mid),
                jnp.where(hit, mid, t), jnp.where(hit, 1, done))

    _, _, t_bis, done = lax.fori_loop(
        0, BISECT_STEPS, bisect_body, (-hi0, hi0, jnp.full(hi0.shape, NEG_INF, F32), all_selected))
    t_ref[...] = t_bis
    j_ref[...] = jnp.full(j_ref.shape, INT_MAX, I32)

    @pl.when(jnp.min(done) == 0)
    def _():
        t0 = jnp.where(count(lambda s, p: s >= 0.0) >= kf, jnp.int32(0), INT_MIN)

        def bit_body(b, t):
            cand = t + lax.shift_left(jnp.int32(1), 30 - b)
            cand_f = _key_to_f32(cand)
            return jnp.where(count(lambda s, p: s >= cand_f) >= kf, cand, t)

        t = _key_to_f32(jnp.maximum(lax.fori_loop(0, 31, bit_body, t0), KEY_NEG_INF))
        t_ref[...] = t
        tie = (count(lambda s, p: s >= t) > kf) & (t > NEG_INF)

        @pl.when(jnp.max(tie.astype(I32)) > 0)
        def _():
            n_gt = count(lambda s, p: s > t)

            def pos_body(b, p_lo):
                cand = p_lo + lax.shift_left(jnp.int32(1), pos_bits - 1 - b)
                n_eq = count(lambda s, p: (s == t) & (p < cand))
                return jnp.where(n_gt + n_eq < kf, cand, p_lo)

            p_sel = lax.fori_loop(0, pos_bits, pos_body, jnp.zeros(t.shape, I32))
            j_ref[...] = jnp.where(tie, p_sel, INT_MAX)

    return jnp.maximum(t_ref[...], -hi0), j_ref[...]


def _fori_by_two(n, body, init):
    carry = lax.fori_loop(0, n // 2, lambda j, c: body(2 * j + 1, body(2 * j, c)), init)
    return lax.cond(n % 2 == 1, lambda c: body(n - 1, c), lambda c: c, carry)


def _dsa_prompt_kernel(qi_ref, wit_ref, q_ref, ki_ref, k_ref, v_ref, o_ref,
                       sc_ref, bias_ref, vt_ref, t_ref, j_ref, s_ref, acc_ref, *, topk, kc, pos_bits):
    i = pl.program_id(1)
    nck = (i * Q_BLOCK + Q_BLOCK + kc - 1) // kc
    nchunks = vt_ref.shape[0]
    rep = A_HEADS // A_KV_HEADS
    q_pos = i * Q_BLOCK + lax.broadcasted_iota(I32, (1, Q_BLOCK), 1)
    sub = lax.broadcasted_iota(I32, (kc, 1), 0)

    @pl.when(i == 0)
    def _():
        for c in range(nchunks):
            for g in range(A_KV_HEADS):
                gsl = slice(g * A_HEAD_DIM, (g + 1) * A_HEAD_DIM)
                vt_ref[c, gsl, :] = v_ref[c * kc:(c + 1) * kc, gsl].astype(F32).T.astype(BF16)

    def chunk_rows(c):
        return pl.ds(pl.multiple_of(c * kc, kc), kc)

    qi = qi_ref[...]
    wit = wit_ref[...]
    qi_pairs = [jnp.concatenate([qi[:, (2 * j) * IDX_DIM:(2 * j + 1) * IDX_DIM],
                                 qi[:, (2 * j + 1) * IDX_DIM:(2 * j + 2) * IDX_DIM]], axis=0)
                for j in range(IDX_HEADS // 2)]

    def score_body(c, amax):
        kic = ki_ref[chunk_rows(c), :]
        acc = jnp.zeros((kc, Q_BLOCK), F32)
        for j in range(IDX_HEADS // 2):
            d = _dot_nt(kic, qi_pairs[j])
            acc = acc + jnp.maximum(d[:, :Q_BLOCK], 0.0) * wit[2 * j:2 * j + 1, :]
            acc = acc + jnp.maximum(d[:, Q_BLOCK:], 0.0) * wit[2 * j + 1:2 * j + 2, :]
        sc_ref[c] = jnp.where(c * kc + sub <= q_pos, acc, NEG_INF)
        return jnp.maximum(amax, jnp.max(jnp.abs(acc).reshape(kc // 64, 64, Q_BLOCK), axis=0))

    amax = jnp.max(_fori_by_two(nck, score_body, jnp.zeros((64, Q_BLOCK), F32)), axis=0, keepdims=True)

    def count(pred):
        def body(c, acc):
            part = jnp.where(pred(sc_ref[c], c * kc + sub), 1.0, 0.0)
            return acc + jnp.sum(part.reshape(kc // 64, 64, Q_BLOCK), axis=0)

        acc = lax.fori_loop(0, nck, body, jnp.zeros((64, Q_BLOCK), F32))
        return jnp.sum(acc, axis=0, keepdims=True)

    thr, jsel = _select_params(count, topk, pos_bits, amax, (q_pos + 1).astype(F32), t_ref, j_ref)

    def bias_body(c, carry):
        s = sc_ref[c]
        tie_thr = jnp.where(c * kc + sub <= jsel, thr, jnp.inf)
        bias_ref[c] = jnp.where((s > thr) | (s >= tie_thr), 0.0, NEG_INF)
        return carry

    lax.fori_loop(0, nck, bias_body, 0)

    gsl = lambda g: slice(g * A_HEAD_DIM, (g + 1) * A_HEAD_DIM)
    row0 = lambda v: jnp.full((1, rep * Q_BLOCK), v, F32)
    groups = tuple(range(A_KV_HEADS))
    qgs = [jnp.concatenate([q_ref[:, gsl(g * rep + r)] for r in range(rep)], axis=0) for g in groups]


    def qk_body(c, ms):
        b = bias_ref[c]
        bias = jnp.concatenate([b] * rep, axis=1)
        out = []
        for g in groups:
            s = _dot_nt(k_ref[chunk_rows(c), gsl(g)], qgs[g]) + bias
            s_ref[c, g] = s
            out.append(jnp.maximum(ms[g], jnp.max(s, axis=0, keepdims=True)))
        return tuple(out)

    ms = _fori_by_two(nck, qk_body, (row0(NEG_INF),) * len(groups))
    acc_ref[...] = jnp.zeros(acc_ref.shape, F32)

    def pv_body(c, ls):
        out = []
        for g in groups:
            p = jnp.exp2(s_ref[c, g] - ms[g])
            out.append(ls[g] + jnp.sum(p, axis=0, keepdims=True))
            acc_ref[g] += _dot(vt_ref[c, gsl(g), :], p.astype(BF16))
        return tuple(out)

    ls = _fori_by_two(nck, pv_body, (row0(0.0),) * len(groups))
    for g in groups:
        o = acc_ref[g] / ls[g]
        for r in range(rep):
            o_ref[:, gsl(g * rep + r)] = o[:, r * Q_BLOCK:(r + 1) * Q_BLOCK].T.astype(BF16)


def _dsa_prompt(qi, wit, q, ki_b, k_b, v_b, batch, seq, topk):
    nqb = seq // Q_BLOCK
    rep = A_HEADS // A_KV_HEADS
    kc = min(512, seq)
    nchunks = seq // kc
    pos_bits = max(1, int(np.ceil(np.log2(seq))))
    qrow = lambda w: pl.BlockSpec((Q_BLOCK, w), lambda b, i: (b * nqb + i, 0))
    seqblk = lambda w: pl.BlockSpec((seq, w), lambda b, i: (b, 0), pipeline_mode=pl.Buffered(1))
    return pl.pallas_call(
        functools.partial(_dsa_prompt_kernel, topk=topk, kc=kc, pos_bits=pos_bits),
        grid=(batch, nqb),
        in_specs=[qrow(IDX_HEADS * IDX_DIM), pl.BlockSpec((IDX_HEADS, Q_BLOCK), lambda b, i: (0, b * nqb + i)),
                  qrow(A_WIDTH), seqblk(IDX_DIM), seqblk(A_KV_WIDTH), seqblk(A_KV_WIDTH)],
        out_specs=qrow(A_WIDTH),
        out_shape=jax.ShapeDtypeStruct((batch * seq, A_WIDTH), BF16),
        scratch_shapes=[pltpu.VMEM((nchunks, kc, Q_BLOCK), F32), pltpu.VMEM((nchunks, kc, Q_BLOCK), F32),
                        pltpu.VMEM((nchunks, A_KV_WIDTH, kc), BF16),
                        pltpu.VMEM((1, Q_BLOCK), F32), pltpu.VMEM((1, Q_BLOCK), I32),
                        pltpu.VMEM((nchunks, A_KV_HEADS, kc, rep * Q_BLOCK), F32),
                        pltpu.VMEM((A_KV_HEADS, A_HEAD_DIM, rep * Q_BLOCK), F32)],
        compiler_params=_cparams(("arbitrary", "arbitrary")),
        name="dsa_prompt",
    )(qi, wit, q, ki_b, k_b, v_b)


def _fold_heads(e, nq):
    acc = e[0:8]
    for t in range(1, e.shape[0] // 8):
        acc = acc + e[8 * t:8 * (t + 1)]
    return acc + pltpu.roll(acc, nq, 0)


def _tree(op, parts):
    parts = list(parts)
    while len(parts) > 1:
        parts = [op(a, b) for a, b in zip(parts[::2], parts[1::2])] + parts[len(parts) & ~1:]
    return parts[0]


def _lane_fold(op, x):
    return _tree(op, [x[:, t * 128:(t + 1) * 128] for t in range(x.shape[1] // 128)])


def _dsa_sample_kernel(pt_ref, qp_ref, wc_ref, kin_ref, qs_ref, kn_ref, vn_ref, kidx_hbm, k_hbm, v_hbm, o_ref,
                       sc_ref, scn_ref, bias_ref, biasn_ref, t_ref, j_ref, s_ref, sn_ref, m_ref, mfin_ref, l_ref, acc_ref,
                       idx_raw, kv_raw, idx_sem, kv_sem, *, ipages, pages, npages, nq, topk, pos_bits):
    b_idx, s_idx = pl.program_id(0), pl.program_id(1)
    ikw, kw = ipages * PAGE, pages * PAGE
    isteps, steps = npages // ipages, npages // pages
    nsteps = isteps + 2 * steps
    n_past = npages * PAGE
    rows_q = qs_ref.shape[1]
    dup = lambda x8: jnp.concatenate([x8] * (rows_q // 8), axis=0)
    gsl = lambda g: slice(g * A_HEAD_DIM, (g + 1) * A_HEAD_DIM)

    ahead = DMA_SLOTS - 1
    n = b_idx * nsteps + s_idx
    slot = n % DMA_SLOTS
    phases = ((kidx_hbm, idx_raw, idx_sem, 0, isteps, ipages),
              (k_hbm, kv_raw, kv_sem, isteps, steps, pages),
              (v_hbm, kv_raw, kv_sem, isteps + steps, steps, pages))

    def block_copies(seq, step, sl, start):
        for src, dst, sem, first, count, per_step in phases:
            @pl.when((step >= first) & (step < first + count))
            def _(src=src, dst=dst, sem=sem, first=first, per_step=per_step):
                for p in range(per_step):
                    page = pt_ref[seq * npages + (step - first) * per_step + p] if start else 0
                    copy = pltpu.make_async_copy(src.at[page], dst.at[sl, p], sem.at[sl])
                    copy.start() if start else copy.wait()

    @pl.when(n == 0)
    def _():
        for d in range(ahead):
            block_copies(jnp.int32(0), jnp.int32(d), d, True)

    @pl.when(n + ahead < pl.num_programs(0) * nsteps)
    def _():
        wrap = (s_idx + ahead >= nsteps).astype(I32)
        to_slot = slot + ahead
        block_copies(b_idx + wrap, s_idx + ahead - wrap * nsteps,
                     jnp.where(to_slot >= DMA_SLOTS, to_slot - DMA_SLOTS, to_slot), True)

    block_copies(b_idx, s_idx, slot, False)

    def head_rows(g):
        rows = pl.ds(g, PAGE, stride=A_KV_HEADS)
        return jnp.concatenate([kv_raw[slot, p, rows, :] for p in range(pages)], axis=0).astype(BF16)

    @pl.when(s_idx < isteps)
    def _():
        qp = qp_ref[...]
        wc = wc_ref[...]
        kpt = jnp.concatenate([idx_raw[slot, p] for p in range(ipages)], axis=1).astype(BF16)
        sc_ref[s_idx] = _fold_heads(jnp.maximum(_dot(qp, kpt), 0.0) * wc, nq)

        @pl.when(s_idx == isteps - 1)
        def _():
            qrow = lax.broadcasted_iota(I32, (8, 1), 0) % nq
            lane = lax.broadcasted_iota(I32, (1, ikw), 1)
            lane_n = lax.broadcasted_iota(I32, (1, PAGE), 1)
            kin = jnp.concatenate([kin_ref[...], jnp.zeros((PAGE - S_PAD, IDX_DIM), BF16)], axis=0)
            scn = _fold_heads(jnp.maximum(_dot_nt(qp, kin), 0.0) * wc, nq)
            scn_ref[...] = jnp.where(lane_n <= qrow, scn, NEG_INF)

            def count(pred):
                ones = _tree(jnp.add, [jnp.where(pred(sc_ref[t], t * ikw + lane), 1.0, 0.0) for t in range(isteps)])
                cn = jnp.where(pred(scn_ref[...], n_past + lane_n), 1.0, 0.0)
                return (jnp.sum(_lane_fold(jnp.add, ones), axis=-1, keepdims=True)
                        + jnp.sum(cn, axis=-1, keepdims=True))

            amax = jnp.maximum(
                jnp.max(_lane_fold(jnp.maximum, _tree(jnp.maximum, [jnp.abs(sc_ref[t]) for t in range(isteps)])),
                        axis=-1, keepdims=True),
                jnp.max(jnp.abs(scn), axis=-1, keepdims=True))
            n_valid = (n_past + 1 + qrow).astype(F32)
            thr, jsel = _select_params(count, topk, pos_bits, amax, n_valid, t_ref, j_ref)
            for t in range(steps):
                first = t * kw
                sp = sc_ref[first // ikw][:, first % ikw:first % ikw + kw]
                pos = first + lane[:, :kw]
                bias_ref[t] = jnp.where((sp > thr) | ((sp == thr) & (pos <= jsel)), 0.0, NEG_INF)
            sn = scn_ref[...]
            seln = ((sn > thr) | ((sn == thr) & (n_past + lane_n <= jsel))) & (lane_n <= qrow)
            biasn_ref[...] = jnp.where(seln, 0.0, NEG_INF)

    @pl.when((s_idx >= isteps) & (s_idx < isteps + steps))
    def _():
        t = s_idx - isteps

        @pl.when(t == 0)
        def _():
            m_ref[...] = jnp.full(m_ref.shape, NEG_INF, F32)

        bias = dup(bias_ref[t])
        for g in range(A_KV_HEADS):
            s = _dot_nt(qs_ref[g], head_rows(g)) + bias
            s_ref[g, t] = s
            m_ref[g] = jnp.maximum(m_ref[g], _lane_fold(jnp.maximum, s))

        @pl.when(t == steps - 1)
        def _():
            bn = dup(biasn_ref[:, :S_PAD])
            for g in range(A_KV_HEADS):
                sn = _dot_nt(qs_ref[g], kn_ref[:, gsl(g)]) + bn
                sn_ref[g] = sn
                mfin_ref[g] = jnp.maximum(jnp.max(m_ref[g], axis=-1, keepdims=True),
                                          jnp.max(sn, axis=-1, keepdims=True))

    @pl.when(s_idx >= isteps + steps)
    def _():
        t = s_idx - isteps - steps

        @pl.when(t == 0)
        def _():
            l_ref[...] = jnp.zeros(l_ref.shape, F32)
            acc_ref[...] = jnp.zeros(acc_ref.shape, F32)

        for g in range(A_KV_HEADS):
            p = jnp.exp2(s_ref[g, t] - mfin_ref[g])
            l_ref[g] += _lane_fold(jnp.add, p)
            acc_ref[g] += _dot(p.astype(BF16), head_rows(g))

        @pl.when(t == steps - 1)
        def _():
            for g in range(A_KV_HEADS):
                pn = jnp.exp2(sn_ref[g] - mfin_ref[g])
                l = jnp.sum(l_ref[g], axis=-1, keepdims=True) + jnp.sum(pn, axis=-1, keepdims=True)
                o_ref[g] = (acc_ref[g] + _dot(pn.astype(BF16), vn_ref[:, gsl(g)])) / l


def _dsa_sample(page_flat, qp, wcol, ki_new_b, qs, k_new_b, v_new_b, cache_kidx_t, cache_k, cache_v,
                nseq, npages, ipages, pages, nq, topk):
    isteps, steps = npages // ipages, npages // pages
    n_past = npages * PAGE
    ikw, kw = ipages * PAGE, pages * PAGE
    rows = IDX_HEADS * nq
    rows_q = qs.shape[2]
    pos_bits = int(np.ceil(np.log2(n_past + PAGE)))

    seq3 = lambda shape: pl.BlockSpec((None,) + shape, lambda b, s, pt: (b,) + (0,) * len(shape))
    new_rows = lambda w: pl.BlockSpec((S_PAD, w), lambda b, s, pt: (b, 0))
    in_hbm = pl.BlockSpec(memory_space=pl.ANY)
    grid_spec = pltpu.PrefetchScalarGridSpec(
        num_scalar_prefetch=1,
        grid=(nseq, isteps + 2 * steps),
        in_specs=[seq3((rows, IDX_DIM)), seq3((rows, 1)), new_rows(IDX_DIM),
                  seq3((A_KV_HEADS, rows_q, A_HEAD_DIM)), new_rows(A_KV_WIDTH), new_rows(A_KV_WIDTH),
                  in_hbm, in_hbm, in_hbm],
        out_specs=seq3((A_KV_HEADS, rows_q, A_HEAD_DIM)),
        scratch_shapes=[pltpu.VMEM((isteps, 8, ikw), F32), pltpu.VMEM((8, PAGE), F32),
                        pltpu.VMEM((steps, 8, kw), F32), pltpu.VMEM((8, PAGE), F32),
                        pltpu.VMEM((8, 1), F32), pltpu.VMEM((8, 1), I32),
                        pltpu.VMEM((A_KV_HEADS, steps, rows_q, kw), F32),
                        pltpu.VMEM((A_KV_HEADS, rows_q, S_PAD), F32),
                        pltpu.VMEM((A_KV_HEADS, rows_q, 128), F32),
                        pltpu.VMEM((A_KV_HEADS, rows_q, 1), F32),
                        pltpu.VMEM((A_KV_HEADS, rows_q, 128), F32),
                        pltpu.VMEM((A_KV_HEADS, rows_q, A_HEAD_DIM), F32),
                        pltpu.VMEM((DMA_SLOTS, ipages, IDX_DIM, PAGE), F32),
                        pltpu.VMEM((DMA_SLOTS, pages, PAGE * A_KV_HEADS, A_HEAD_DIM), F32),
                        pltpu.SemaphoreType.DMA((DMA_SLOTS,)), pltpu.SemaphoreType.DMA((DMA_SLOTS,))],
    )
    return pl.pallas_call(
        functools.partial(_dsa_sample_kernel, ipages=ipages, pages=pages, npages=npages, nq=nq, topk=topk,
                          pos_bits=pos_bits),
        grid_spec=grid_spec,
        out_shape=jax.ShapeDtypeStruct((nseq, A_KV_HEADS, rows_q, A_HEAD_DIM), F32),
        compiler_params=_cparams(("arbitrary", "arbitrary")),
        name="dsa_sample",
    )(page_flat, qp, wcol, ki_new_b, qs, k_new_b, v_new_b, cache_kidx_t, cache_k, cache_v)


def _merge_kernel(x_ref, oa_ref, ag_ref, bu_ref, bv_ref, bg_ref, cq_ref, cg_ref, ra_ref, rb_ref, rc_ref,
                  mk_ref, mv_ref, ws_ref, bs_ref, gsgu_ref, gmq_ref, wpa_ref, wpb_ref, wpc_ref, wout_ref,
                  y_ref, *maybe_vn_ref, tm, chunk, mem_groups):
    f32 = lambda r: r[...].astype(F32)
    silu = lambda t: t * jax.nn.sigmoid(t)

    vn = _rms(f32(bv_ref), gsgu_ref[...])
    if maybe_vn_ref:
        maybe_vn_ref[0][...] = vn
    vnb = vn.astype(BF16)
    bu = f32(bu_ref)
    tril = (lax.broadcasted_iota(I32, (chunk, chunk), 1) <= lax.broadcasted_iota(I32, (chunk, chunk), 0))
    ob_cols = []
    for g in range(B_GROUPS):
        wg = jnp.where(tril, ws_ref[g], 0.0).astype(BF16)
        gsl = slice(g * B_GROUP_DIM, (g + 1) * B_GROUP_DIM)
        parts = [_dot(wg, vnb[c * chunk:(c + 1) * chunk, gsl]) + bs_ref[:, g:g + 1] for c in range(tm // chunk)]
        ob_cols.append(parts[0] if len(parts) == 1 else jnp.concatenate(parts, axis=0))
    ob = bu * jnp.concatenate(ob_cols, axis=1)
    pb = _dot((ob * silu(f32(bg_ref))).astype(BF16), wpb_ref[...])

    cq = f32(cq_ref)
    rows_g = tm // mem_groups
    oc_cols = []
    for hh in range(M_HEADS):
        hsl = slice(hh * M_HEAD_DIM, (hh + 1) * M_HEAD_DIM)
        qn = (_rms(cq[:, hsl], gmq_ref[...]) * (M_HEAD_DIM ** -0.5)).astype(BF16)
        oc_rows = []
        for u in range(mem_groups):
            msl = slice(u * N_MEM, (u + 1) * N_MEM)
            s = _dot_nt(qn[u * rows_g:(u + 1) * rows_g], mk_ref[msl, hsl])
            p = jnp.exp(s - jnp.max(s, axis=-1, keepdims=True))
            oc_rows.append(_dot(p.astype(BF16), mv_ref[msl, hsl]) / jnp.sum(p, axis=-1, keepdims=True))
        oc_cols.append(oc_rows[0] if mem_groups == 1 else jnp.concatenate(oc_rows, axis=0))
    oc = jnp.concatenate(oc_cols, axis=1)
    pc = _dot((oc * silu(f32(cg_ref))).astype(BF16), wpc_ref[...])

    pa = _dot((f32(oa_ref) * silu(f32(ag_ref))).astype(BF16), wpa_ref[...])
    sig = jax.nn.sigmoid
    m = sig(f32(ra_ref)) * pa + sig(f32(rb_ref)) * pb + sig(f32(rc_ref)) * pc
    y_ref[...] = x_ref[...] + _dot(m.astype(BF16), wout_ref[...])


def _merge(x, oa, zr, mk_b, mv_b, ws, bs_t, g_sgu, g_mq, w_pa, w_pb, w_pc, w_out, tm, chunk, mem_groups, mem_map,
           emit_vn):
    n = x.shape[0]
    col = lambda w, j: pl.BlockSpec((tm, w), lambda i, j=j: (i, j))
    mem = pl.BlockSpec((mem_groups * N_MEM, M_WIDTH), mem_map)
    in_specs = [col(D_MODEL, 0), col(A_WIDTH, 0),
                col(1024, 0), col(1024, 1), col(1024, 2), col(1024, 3), col(1024, 4), col(1024, 5),
                col(2048, 3), col(2048, 4), col(2048, 5),
                mem, mem,
                _const_spec((B_GROUPS, chunk, chunk)), _const_spec((chunk, B_GROUPS)),
                _const_spec((1, B_WIDTH)), _const_spec((1, M_HEAD_DIM)),
                _const_spec((A_WIDTH, D_MODEL)), _const_spec((B_WIDTH, D_MODEL)),
                _const_spec((M_WIDTH, D_MODEL)), _const_spec((D_MODEL, D_MODEL))]
    out_specs = [col(D_MODEL, 0)]
    out_shape = [jax.ShapeDtypeStruct((n, D_MODEL), F32)]
    if emit_vn:
        out_specs.append(col(B_WIDTH, 0))
        out_shape.append(jax.ShapeDtypeStruct((n, B_WIDTH), F32))
    return pl.pallas_call(
        functools.partial(_merge_kernel, tm=tm, chunk=chunk, mem_groups=mem_groups),
        grid=(n // tm,),
        in_specs=in_specs,
        out_specs=out_specs,
        out_shape=out_shape,
        compiler_params=_cparams(("arbitrary",)),
        name="merge",
    )(x, oa, *([zr] * 9), mk_b, mv_b, ws, bs_t, g_sgu, g_mq, w_pa, w_pb, w_pc, w_out)


def _rope_tables(pos):
    freq = lambda half: ROPE_THETA ** (-jnp.arange(half, dtype=F32) / half)
    f_idx = freq(IDX_DIM // 2)
    ang = pos.astype(F32)[:, None] * jnp.concatenate([freq(A_HEAD_DIM // 2), f_idx, f_idx])[None, :]
    return jnp.cos(ang), jnp.sin(ang)


def kernel(x_prompt, x_sample, cache_k, cache_v, cache_kidx, cache_mem_k, cache_mem_v, page_table,
           mem_prompt, g_pre, w_in, g_q, g_k, g_mq, g_mk, g_mem, w_mem_kv, g_sgu, w_s, b_s,
           w_pa, w_pb, w_pc, w_out):
    batch, seq, _ = x_prompt.shape
    nseq, nq, _ = x_sample.shape
    assert nq == 4 and nq <= S_PAD
    npages = page_table.shape[1]
    n_past = npages * PAGE
    n_pool = cache_k.shape[0]
    row2 = lambda a: a.reshape(1, -1)

    w_in_t = w_in.T
    w_a = w_in_t[:A_COLS].astype(BF16)
    w_pa_b, w_pb_b, w_pc_b, w_out_b = (w.astype(BF16) for w in (w_pa, w_pb, w_pc, w_out))
    w_mem_b = w_mem_kv.astype(BF16)
    g_pre2, g_q2, g_k2, g_mq2, g_mk2, g_mem2, g_sgu2 = map(row2, (g_pre, g_q, g_k, g_mq, g_mk, g_mem, g_sgu))

    xp = x_prompt.reshape(batch * seq, D_MODEL)
    tm_a = min(256, seq)
    nblk = seq // tm_a
    tabs_p = _rope_tables(jnp.arange(seq))
    q, k_p, v_p, k_b, v_b, qi, ki_p, ki_b, wi = _proj_a(
        xp, g_pre2, w_a, g_q2, g_k2, tabs_p, tm_a, lambda i: (i % nblk, 0))
    xs = jnp.pad(x_sample, ((0, 0), (0, S_PAD - nq), (0, 0))).reshape(nseq * S_PAD, D_MODEL)
    rows_s = nseq * S_PAD
    zr, zr_s = _proj_rest(xp, xs, g_pre2, w_in_t, min(1024, seq), PROJ_TN)
    mk_p, mv_p, mk_b, mv_b = _mem_kv(mem_prompt.reshape(batch * N_MEM, D_MODEL), g_mem2, w_mem_b, g_mk2)
    oa = _dsa_prompt(qi, wi.T, q, ki_b, k_b, v_b, batch, seq, min(TOPK_MAX, seq // 4))
    tm_m = min(256, seq)
    nblk_m = seq // tm_m
    (y_p,) = _merge(xp, oa, zr, mk_b, mv_b, w_s, b_s.T, g_sgu2, g_mq2, w_pa_b, w_pb_b, w_pc_b, w_out_b,
                    tm_m, CHUNK, 1, lambda i: (i // nblk_m, 0), False)

    tabs_s = tuple(jnp.tile(t, (nseq, 1)) for t in _rope_tables(n_past + jnp.arange(S_PAD)))
    q_s, k_s, v_s, k_sb, v_sb, qi_s, ki_s, ki_sb, wi_s = _proj_a(
        xs, g_pre2, w_a, g_q2, g_k2, tabs_s, rows_s, lambda i: (0, 0))

    qp = (qi_s.reshape(nseq, S_PAD, IDX_HEADS, IDX_DIM)[:, :nq]
          .transpose(0, 2, 1, 3).reshape(nseq, IDX_HEADS * nq, IDX_DIM))
    wcol = wi_s.reshape(nseq, S_PAD, IDX_HEADS)[:, :nq].transpose(0, 2, 1).reshape(nseq, IDX_HEADS * nq, 1)
    rep = A_HEADS // A_KV_HEADS
    qs = (q_s.reshape(nseq, S_PAD, A_KV_HEADS, rep, A_HEAD_DIM)[:, :nq]
          .transpose(0, 2, 3, 1, 4).reshape(nseq, A_KV_HEADS, rep * nq, A_HEAD_DIM))
    qs = jnp.concatenate([qs, qs], axis=2)
    page_flat = page_table.reshape(-1)
    pages, ipages = min(SAMPLE_PAGES, npages), min(SAMPLE_IDX_PAGES, npages)
    o_s = _dsa_sample(page_flat, qp, wcol, ki_sb, qs, k_sb, v_sb, jnp.swapaxes(cache_kidx, 1, 2),
                      cache_k.reshape(n_pool, PAGE * A_KV_HEADS, A_HEAD_DIM),
                      cache_v.reshape(n_pool, PAGE * A_KV_HEADS, A_HEAD_DIM),
                      nseq, npages, ipages, pages, nq, min(TOPK_MAX, (n_past + nq) // 4))
    oa_s = (o_s[:, :, :rep * nq].reshape(nseq, A_KV_HEADS, rep, nq, A_HEAD_DIM)
            .transpose(0, 3, 1, 2, 4).reshape(nseq, nq, A_WIDTH))
    oa_s = jnp.pad(oa_s, ((0, 0), (0, S_PAD - nq), (0, 0))).reshape(rows_s, A_WIDTH).astype(BF16)
    mk_s = cache_mem_k.reshape(nseq * N_MEM, M_WIDTH).astype(BF16)
    mv_s = cache_mem_v.reshape(nseq * N_MEM, M_WIDTH).astype(BF16)
    y_s, vn_s = _merge(xs, oa_s, zr_s, mk_s, mv_s, w_s[:, :S_PAD, :S_PAD], b_s[:, :S_PAD].T, g_sgu2, g_mq2,
                       w_pa_b, w_pb_b, w_pc_b, w_out_b, rows_s, S_PAD, nseq, lambda i: (0, 0), True)

    take = lambda a, shape: a.reshape(nseq, S_PAD, -1)[:, :nq].reshape(shape)
    return (y_p.reshape(batch, seq, D_MODEL),
            take(y_s, (nseq, nq, D_MODEL)),
            k_p.reshape(batch, seq, A_KV_HEADS, A_HEAD_DIM),
            v_p.reshape(batch, seq, A_KV_HEADS, A_HEAD_DIM),
            ki_p.reshape(batch, seq, IDX_DIM),
            mk_p.reshape(batch, N_MEM, M_HEADS, M_HEAD_DIM),
            mv_p.reshape(batch, N_MEM, M_HEADS, M_HEAD_DIM),
            take(k_s, (nseq, nq, A_KV_HEADS, A_HEAD_DIM)),
            take(v_s, (nseq, nq, A_KV_HEADS, A_HEAD_DIM)),
            take(ki_s, (nseq, nq, IDX_DIM)),
            take(vn_s, (nseq, nq, B_GROUPS, B_GROUP_DIM)))
```

```python
import functools

import numpy as np
import jax
import jax.numpy as jnp
from jax import lax
from jax.experimental import pallas as pl
from jax.experimental.pallas import tpu as pltpu

F32 = jnp.float32
BF16 = jnp.bfloat16
I32 = jnp.int32

D_MODEL = 2048
PAGE = 128
A_HEADS = 8
A_KV_HEADS = 4
A_HEAD_DIM = 128
A_WIDTH = A_HEADS * A_HEAD_DIM
A_KV_WIDTH = A_KV_HEADS * A_HEAD_DIM
IDX_HEADS = 16
IDX_DIM = 64
TOPK_MAX = 256
Q_BLOCK = 128
ROPE_THETA = 10000.0
CHUNK = 128
B_GROUPS = 8
B_GROUP_DIM = 128
B_WIDTH = B_GROUPS * B_GROUP_DIM
N_MEM = 256
M_HEADS = 4
M_HEAD_DIM = 256
M_WIDTH = M_HEADS * M_HEAD_DIM
EPS = 1e-6

OFF_K = A_WIDTH
OFF_V = OFF_K + A_KV_WIDTH
OFF_QI = OFF_V + A_KV_WIDTH
OFF_KI = OFF_QI + IDX_HEADS * IDX_DIM
OFF_WI = OFF_KI + IDX_DIM
OFF_REST = OFF_WI + IDX_HEADS
A_COLS = 3200
REST_COLS = A_WIDTH + 3 * B_WIDTH + 2 * M_WIDTH + 3 * D_MODEL

Q_SCALE = float(np.log2(np.e)) * A_HEAD_DIM ** -0.5
S_PAD = 16
PROJ_TN = 1024
BISECT_STEPS = 24
SAMPLE_PAGES = 32
SAMPLE_IDX_PAGES = 64
DMA_SLOTS = 4
INT_MIN = np.int32(-2 ** 31)
INT_MAX = np.int32(2 ** 31 - 1)
NEG_INF = float("-inf")

V7X_VMEM_LIMIT = 56 * 1024 * 1024


def _cparams(sem):
    return pltpu.CompilerParams(dimension_semantics=sem, vmem_limit_bytes=V7X_VMEM_LIMIT)


def _dot(a, b):
    return jnp.dot(a, b, preferred_element_type=F32)


def _dot_nt(a, b):
    return lax.dot_general(a, b, (((1,), (1,)), ((), ())), preferred_element_type=F32)


def _rms(x, g):
    return x * lax.rsqrt(jnp.mean(x * x, axis=-1, keepdims=True) + EPS) * g


def _const_spec(shape):
    nd = len(shape)
    return pl.BlockSpec(shape, lambda *_: (0,) * nd, pipeline_mode=pl.Buffered(1))


def _proj_a_kernel(x_ref, g_ref, w_ref, gq_ref, gk_ref, cos_ref, sin_ref,
                   q_ref, k_ref, v_ref, kb_ref, vb_ref, qi_ref, ki_ref, kib_ref, wi_ref):
    h = _rms(x_ref[...], g_ref[...]).astype(BF16)
    z = _dot_nt(h, w_ref[...])
    tm = z.shape[0]
    c, s = cos_ref[...], sin_ref[...]
    c_sw, s_sw = pltpu.roll(c, 64, 1), pltpu.roll(s, 64, 1)
    lane = lax.broadcasted_iota(I32, (1, 128), 1)
    low = lane < 64
    cq, sq = jnp.where(low, c, c_sw), jnp.where(low, -s, s_sw)
    ci, si = jnp.where(low, c_sw, c), jnp.where(low, s_sw, s)
    first_half = lane % IDX_DIM < IDX_DIM // 2
    sia, sib = jnp.where(first_half, -si, 0.0), jnp.where(first_half, 0.0, si)

    def norm_rope(zz, g):
        n = _rms(zz, g)
        return n * cq + pltpu.roll(n, A_HEAD_DIM // 2, 1) * sq

    def rope_idx(zz):
        return zz * ci + pltpu.roll(zz, 96, 1) * sia + pltpu.roll(zz, 32, 1) * sib

    for hh in range(A_HEADS):
        sl = slice(hh * A_HEAD_DIM, (hh + 1) * A_HEAD_DIM)
        q_ref[:, sl] = (norm_rope(z[:, sl], gq_ref[...]) * Q_SCALE).astype(BF16)
    for hh in range(A_KV_HEADS):
        sl = slice(hh * A_HEAD_DIM, (hh + 1) * A_HEAD_DIM)
        kh = norm_rope(z[:, OFF_K + hh * A_HEAD_DIM:OFF_K + (hh + 1) * A_HEAD_DIM], gk_ref[...])
        vh = z[:, OFF_V + hh * A_HEAD_DIM:OFF_V + (hh + 1) * A_HEAD_DIM]
        head_rows = pl.ds(hh, tm, stride=A_KV_HEADS)
        k_ref[head_rows, :] = kh
        v_ref[head_rows, :] = vh
        kb_ref[:, sl] = kh.astype(BF16)
        vb_ref[:, sl] = vh.astype(BF16)
    for t in range(IDX_HEADS * IDX_DIM // 128):
        sl = slice(t * 128, (t + 1) * 128)
        qi_ref[:, sl] = rope_idx(z[:, OFF_QI + t * 128:OFF_QI + (t + 1) * 128]).astype(BF16)
    last = z[:, OFF_KI:OFF_KI + 128]
    ki = rope_idx(last)[:, :IDX_DIM]
    ki_ref[...] = ki
    kib_ref[...] = ki.astype(BF16)
    wi_ref[...] = last[:, IDX_DIM:IDX_DIM + IDX_HEADS] * ((IDX_HEADS ** -0.5) * (IDX_DIM ** -0.5))


def _proj_a(x, g_pre, w_a, g_q, g_k, tabs, tm, tab_map):
    n = x.shape[0]
    row = lambda w: pl.BlockSpec((tm, w), lambda i: (i, 0))
    tab = pl.BlockSpec((tm, 128), tab_map)
    outs = [(1, A_WIDTH, BF16), (A_KV_HEADS, A_HEAD_DIM, F32), (A_KV_HEADS, A_HEAD_DIM, F32),
            (1, A_KV_WIDTH, BF16), (1, A_KV_WIDTH, BF16),
            (1, IDX_HEADS * IDX_DIM, BF16), (1, IDX_DIM, F32), (1, IDX_DIM, BF16), (1, IDX_HEADS, F32)]
    return pl.pallas_call(
        _proj_a_kernel,
        grid=(n // tm,),
        in_specs=[row(D_MODEL), _const_spec((1, D_MODEL)), _const_spec((A_COLS, D_MODEL)),
                  _const_spec((1, A_HEAD_DIM)), _const_spec((1, A_HEAD_DIM)), tab, tab],
        out_specs=[pl.BlockSpec((tm * r, w), lambda i: (i, 0)) for r, w, _ in outs],
        out_shape=[jax.ShapeDtypeStruct((n * r, w), dt) for r, w, dt in outs],
        compiler_params=_cparams(("arbitrary",)),
        name="proj_a",
    )(x, g_pre, w_a, g_q, g_k, *tabs)


def _proj_rest_kernel(x_ref, xs_ref, g_ref, w_ref, o_ref, os_ref, h_ref, hs_ref):
    i, j = pl.program_id(0), pl.program_id(1)

    @pl.when(j == 0)
    def _():
        h_ref[...] = _rms(x_ref[...], g_ref[...]).astype(BF16)

    @pl.when((i == 0) & (j == 0))
    def _():
        hs_ref[...] = _rms(xs_ref[...], g_ref[...]).astype(BF16)

    w = w_ref[...].astype(BF16)
    o_ref[...] = _dot_nt(h_ref[...], w).astype(BF16)

    @pl.when(i == 0)
    def _():
        os_ref[...] = _dot_nt(hs_ref[...], w).astype(BF16)


def _proj_rest(x, xs, g_pre, w_in_t, tm, tn):
    n, ns = x.shape[0], xs.shape[0]
    ncols = REST_COLS // tn
    xs_cols = lambda i, j: (0, jnp.where(i == 0, j, ncols - 1))
    return pl.pallas_call(
        _proj_rest_kernel,
        grid=(n // tm, ncols),
        in_specs=[pl.BlockSpec((tm, D_MODEL), lambda i, j: (i, 0)),
                  pl.BlockSpec((ns, D_MODEL), lambda i, j: (0, 0), pipeline_mode=pl.Buffered(1)),
                  pl.BlockSpec((1, D_MODEL), lambda i, j: (0, 0)),
                  pl.BlockSpec((pl.Element(tn), pl.Element(D_MODEL)),
                               lambda i, j: (pl.multiple_of(OFF_REST + j * tn, 16), 0))],
        out_specs=[pl.BlockSpec((tm, tn), lambda i, j: (i, j)), pl.BlockSpec((ns, tn), xs_cols)],
        out_shape=[jax.ShapeDtypeStruct((n, REST_COLS), BF16), jax.ShapeDtypeStruct((ns, REST_COLS), BF16)],
        scratch_shapes=[pltpu.VMEM((tm, D_MODEL), BF16), pltpu.VMEM((ns, D_MODEL), BF16)],
        compiler_params=_cparams(("arbitrary", "arbitrary")),
        name="proj_rest",
    )(x, xs, g_pre, w_in_t)


def _mem_kv_kernel(x_ref, g_ref, w_ref, gk_ref, k_ref, v_ref, kb_ref, vb_ref):
    h = _rms(x_ref[...], g_ref[...]).astype(BF16)
    z = _dot(h, w_ref[...])
    for hh in range(M_HEADS):
        sl = slice(hh * M_HEAD_DIM, (hh + 1) * M_HEAD_DIM)
        kh = _rms(z[:, sl], gk_ref[...])
        k_ref[:, sl] = kh
        kb_ref[:, sl] = kh.astype(BF16)
    v = z[:, M_WIDTH:]
    v_ref[...] = v
    vb_ref[...] = v.astype(BF16)


def _mem_kv(mem, g_mem, w_mem, g_mk):
    n = mem.shape[0]
    blk = pl.BlockSpec((N_MEM, M_WIDTH), lambda i: (i, 0))
    return pl.pallas_call(
        _mem_kv_kernel,
        grid=(n // N_MEM,),
        in_specs=[pl.BlockSpec((N_MEM, D_MODEL), lambda i: (i, 0)), _const_spec((1, D_MODEL)),
                  _const_spec((D_MODEL, 2 * M_WIDTH)), _const_spec((1, M_HEAD_DIM))],
        out_specs=[blk, blk, blk, blk],
        out_shape=[jax.ShapeDtypeStruct((n, M_WIDTH), dt) for dt in (F32, F32, BF16, BF16)],
        compiler_params=_cparams(("arbitrary",)),
        name="mem_kv",
    )(mem, g_mem, w_mem, g_mk)


KEY_NEG_INF = np.int32(-0x7F800000)


def _key_to_f32(key):
    return pltpu.bitcast(jnp.where(key >= 0, key, INT_MIN - key), F32)


def _select_params(count, topk, pos_bits, amax, n_valid, t_ref, j_ref):
    kf = float(topk)
    hi0 = amax * 1.000001 + 1e-30
    all_selected = (n_valid <= kf).astype(I32)

    def bisect_body(_, state):
        lo, hi, t, done = state
        mid = 0.5 * lo + 0.5 * hi
        n_ge = count(lambda s, p: s >= mid)
        hit = (n_ge == kf) & (done == 0)
        return (jnp.where(n_ge >= kf, mid, lo), jnp.where(n_ge >= kf, hi, mid),
                jnp.where(hit, mid, t), jnp.where(hit, 1, done))

    _, _, t_bis, done = lax.fori_loop(
        0, BISECT_STEPS, bisect_body, (-hi0, hi0, jnp.full(hi0.shape, NEG_INF, F32), all_selected))
    t_ref[...] = t_bis
    j_ref[...] = jnp.full(j_ref.shape, INT_MAX, I32)

    @pl.when(jnp.min(done) == 0)
    def _():
        t0 = jnp.where(count(lambda s, p: s >= 0.0) >= kf, jnp.int32(0), INT_MIN)

        def bit_body(b, t):
            cand = t + lax.shift_left(jnp.int32(1), 30 - b)
            cand_f = _key_to_f32(cand)
            return jnp.where(count(lambda s, p: s >= cand_f) >= kf, cand, t)

        t = _key_to_f32(jnp.maximum(lax.fori_loop(0, 31, bit_body, t0), KEY_NEG_INF))
        t_ref[...] = t
        tie = (count(lambda s, p: s >= t) > kf) & (t > NEG_INF)

        @pl.when(jnp.max(tie.astype(I32)) > 0)
        def _():
            n_gt = count(lambda s, p: s > t)

            def pos_body(b, p_lo):
                cand = p_lo + lax.shift_left(jnp.int32(1), pos_bits - 1 - b)
                n_eq = count(lambda s, p: (s == t) & (p < cand))
                return jnp.where(n_gt + n_eq < kf, cand, p_lo)

            p_sel = lax.fori_loop(0, pos_bits, pos_body, jnp.zeros(t.shape, I32))
            j_ref[...] = jnp.where(tie, p_sel, INT_MAX)

    return jnp.maximum(t_ref[...], -hi0), j_ref[...]


def _fori_by_two(n, body, init):
    carry = lax.fori_loop(0, n // 2, lambda j, c: body(2 * j + 1, body(2 * j, c)), init)
    return lax.cond(n % 2 == 1, lambda c: body(n - 1, c), lambda c: c, carry)


def _dsa_prompt_kernel(qi_ref, wit_ref, q_ref, ki_ref, k_ref, v_ref, o_ref,
                       sc_ref, bias_ref, vt_ref, t_ref, j_ref, s_ref, acc_ref, *, topk, kc, pos_bits):
    i = pl.program_id(1)
    nck = (i * Q_BLOCK + Q_BLOCK + kc - 1) // kc
    nchunks = vt_ref.shape[0]
    rep = A_HEADS // A_KV_HEADS
    q_pos = i * Q_BLOCK + lax.broadcasted_iota(I32, (1, Q_BLOCK), 1)
    sub = lax.broadcasted_iota(I32, (kc, 1), 0)

    @pl.when(i == 0)
    def _():
        for c in range(nchunks):
            for g in range(A_KV_HEADS):
                gsl = slice(g * A_HEAD_DIM, (g + 1) * A_HEAD_DIM)
                vt_ref[c, gsl, :] = v_ref[c * kc:(c + 1) * kc, gsl].astype(F32).T.astype(BF16)

    def chunk_rows(c):
        return pl.ds(pl.multiple_of(c * kc, kc), kc)

    qi = qi_ref[...]
    wit = wit_ref[...]
    qi_pairs = [jnp.concatenate([qi[:, (2 * j) * IDX_DIM:(2 * j + 1) * IDX_DIM],
                                 qi[:, (2 * j + 1) * IDX_DIM:(2 * j + 2) * IDX_DIM]], axis=0)
                for j in range(IDX_HEADS // 2)]

    def score_body(c, amax):
        kic = ki_ref[chunk_rows(c), :]
        acc = jnp.zeros((kc, Q_BLOCK), F32)
        for j in range(IDX_HEADS // 2):
            d = _dot_nt(kic, qi_pairs[j])
            acc = acc + jnp.maximum(d[:, :Q_BLOCK], 0.0) * wit[2 * j:2 * j + 1, :]
            acc = acc + jnp.maximum(d[:, Q_BLOCK:], 0.0) * wit[2 * j + 1:2 * j + 2, :]
        sc_ref[c] = jnp.where(c * kc + sub <= q_pos, acc, NEG_INF)
        return jnp.maximum(amax, jnp.max(jnp.abs(acc).reshape(kc // 64, 64, Q_BLOCK), axis=0))

    amax = jnp.max(_fori_by_two(nck, score_body, jnp.zeros((64, Q_BLOCK), F32)), axis=0, keepdims=True)

    def count(pred):
        def body(c, acc):
            part = jnp.where(pred(sc_ref[c], c * kc + sub), 1.0, 0.0)
            return acc + jnp.sum(part.reshape(kc // 64, 64, Q_BLOCK), axis=0)

        acc = lax.fori_loop(0, nck, body, jnp.zeros((64, Q_BLOCK), F32))
        return jnp.sum(acc, axis=0, keepdims=True)

    thr, jsel = _select_params(count, topk, pos_bits, amax, (q_pos + 1).astype(F32), t_ref, j_ref)

    def bias_body(c, carry):
        s = sc_ref[c]
        tie_thr = jnp.where(c * kc + sub <= jsel, thr, jnp.inf)
        bias_ref[c] = jnp.where((s > thr) | (s >= tie_thr), 0.0, NEG_INF)
        return carry

    lax.fori_loop(0, nck, bias_body, 0)

    gsl = lambda g: slice(g * A_HEAD_DIM, (g + 1) * A_HEAD_DIM)
    row0 = lambda v: jnp.full((1, rep * Q_BLOCK), v, F32)
    groups = tuple(range(A_KV_HEADS))
    qgs = [jnp.concatenate([q_ref[:, gsl(g * rep + r)] for r in range(rep)], axis=0) for g in groups]


    def qk_body(c, ms):
        b = bias_ref[c]
        bias = jnp.concatenate([b] * rep, axis=1)
        out = []
        for g in groups:
            s = _dot_nt(k_ref[chunk_rows(c), gsl(g)], qgs[g]) + bias
            s_ref[c, g] = s
            out.append(jnp.maximum(ms[g], jnp.max(s, axis=0, keepdims=True)))
        return tuple(out)

    ms = _fori_by_two(nck, qk_body, (row0(NEG_INF),) * len(groups))
    acc_ref[...] = jnp.zeros(acc_ref.shape, F32)

    def pv_body(c, ls):
        out = []
        for g in groups:
            p = jnp.exp2(s_ref[c, g] - ms[g])
            out.append(ls[g] + jnp.sum(p, axis=0, keepdims=True))
            acc_ref[g] += _dot(vt_ref[c, gsl(g), :], p.astype(BF16))
        return tuple(out)

    ls = _fori_by_two(nck, pv_body, (row0(0.0),) * len(groups))
    for g in groups:
        o = acc_ref[g] / ls[g]
        for r in range(rep):
            o_ref[:, gsl(g * rep + r)] = o[:, r * Q_BLOCK:(r + 1) * Q_BLOCK].T.astype(BF16)


def _dsa_prompt(qi, wit, q, ki_b, k_b, v_b, batch, seq, topk):
    nqb = seq // Q_BLOCK
    rep = A_HEADS // A_KV_HEADS
    kc = min(512, seq)
    nchunks = seq // kc
    pos_bits = max(1, int(np.ceil(np.log2(seq))))
    qrow = lambda w: pl.BlockSpec((Q_BLOCK, w), lambda b, i: (b * nqb + i, 0))
    seqblk = lambda w: pl.BlockSpec((seq, w), lambda b, i: (b, 0), pipeline_mode=pl.Buffered(1))
    return pl.pallas_call(
        functools.partial(_dsa_prompt_kernel, topk=topk, kc=kc, pos_bits=pos_bits),
        grid=(batch, nqb),
        in_specs=[qrow(IDX_HEADS * IDX_DIM), pl.BlockSpec((IDX_HEADS, Q_BLOCK), lambda b, i: (0, b * nqb + i)),
                  qrow(A_WIDTH), seqblk(IDX_DIM), seqblk(A_KV_WIDTH), seqblk(A_KV_WIDTH)],
        out_specs=qrow(A_WIDTH),
        out_shape=jax.ShapeDtypeStruct((batch * seq, A_WIDTH), BF16),
        scratch_shapes=[pltpu.VMEM((nchunks, kc, Q_BLOCK), F32), pltpu.VMEM((nchunks, kc, Q_BLOCK), F32),
                        pltpu.VMEM((nchunks, A_KV_WIDTH, kc), BF16),
                        pltpu.VMEM((1, Q_BLOCK), F32), pltpu.VMEM((1, Q_BLOCK), I32),
                        pltpu.VMEM((nchunks, A_KV_HEADS, kc, rep * Q_BLOCK), F32),
                        pltpu.VMEM((A_KV_HEADS, A_HEAD_DIM, rep * Q_BLOCK), F32)],
        compiler_params=_cparams(("arbitrary", "arbitrary")),
        name="dsa_prompt",
    )(qi, wit, q, ki_b, k_b, v_b)


def _fold_heads(e, nq):
    acc = e[0:8]
    for t in range(1, e.shape[0] // 8):
        acc = acc + e[8 * t:8 * (t + 1)]
    return acc + pltpu.roll(acc, nq, 0)


def _tree(op, parts):
    parts = list(parts)
    while len(parts) > 1:
        parts = [op(a, b) for a, b in zip(parts[::2], parts[1::2])] + parts[len(parts) & ~1:]
    return parts[0]


def _lane_fold(op, x):
    return _tree(op, [x[:, t * 128:(t + 1) * 128] for t in range(x.shape[1] // 128)])


def _dsa_sample_kernel(pt_ref, qp_ref, wc_ref, kin_ref, qs_ref, kn_ref, vn_ref, kidx_hbm, k_hbm, v_hbm, o_ref,
                       sc_ref, scn_ref, bias_ref, biasn_ref, t_ref, j_ref, s_ref, sn_ref, m_ref, mfin_ref, l_ref, acc_ref,
                       idx_raw, kv_raw, idx_sem, kv_sem, *, ipages, pages, npages, nq, topk, pos_bits):
    b_idx, s_idx = pl.program_id(0), pl.program_id(1)
    ikw, kw = ipages * PAGE, pages * PAGE
    isteps, steps = npages // ipages, npages // pages
    nsteps = isteps + 2 * steps
    n_past = npages * PAGE
    rows_q = qs_ref.shape[1]
    dup = lambda x8: jnp.concatenate([x8] * (rows_q // 8), axis=0)
    gsl = lambda g: slice(g * A_HEAD_DIM, (g + 1) * A_HEAD_DIM)

    ahead = DMA_SLOTS - 1
    n = b_idx * nsteps + s_idx
    slot = n % DMA_SLOTS
    phases = ((kidx_hbm, idx_raw, idx_sem, 0, isteps, ipages),
              (k_hbm, kv_raw, kv_sem, isteps, steps, pages),
              (v_hbm, kv_raw, kv_sem, isteps + steps, steps, pages))

    def block_copies(seq, step, sl, start):
        for src, dst, sem, first, count, per_step in phases:
            @pl.when((step >= first) & (step < first + count))
            def _(src=src, dst=dst, sem=sem, first=first, per_step=per_step):
                for p in range(per_step):
                    page = pt_ref[seq * npages + (step - first) * per_step + p] if start else 0
                    copy = pltpu.make_async_copy(src.at[page], dst.at[sl, p], sem.at[sl])
                    copy.start() if start else copy.wait()

    @pl.when(n == 0)
    def _():
        for d in range(ahead):
            block_copies(jnp.int32(0), jnp.int32(d), d, True)

    @pl.when(n + ahead < pl.num_programs(0) * nsteps)
    def _():
        wrap = (s_idx + ahead >= nsteps).astype(I32)
        to_slot = slot + ahead
        block_copies(b_idx + wrap, s_idx + ahead - wrap * nsteps,
                     jnp.where(to_slot >= DMA_SLOTS, to_slot - DMA_SLOTS, to_slot), True)

    block_copies(b_idx, s_idx, slot, False)

    def head_rows(g):
        rows = pl.ds(g, PAGE, stride=A_KV_HEADS)
        return jnp.concatenate([kv_raw[slot, p, rows, :] for p in range(pages)], axis=0).astype(BF16)

    @pl.when(s_idx < isteps)
    def _():
        qp = qp_ref[...]
        wc = wc_ref[...]
        kpt = jnp.concatenate([idx_raw[slot, p] for p in range(ipages)], axis=1).astype(BF16)
        sc_ref[s_idx] = _fold_heads(jnp.maximum(_dot(qp, kpt), 0.0) * wc, nq)

        @pl.when(s_idx == isteps - 1)
        def _():
            qrow = lax.broadcasted_iota(I32, (8, 1), 0) % nq
            lane = lax.broadcasted_iota(I32, (1, ikw), 1)
            lane_n = lax.broadcasted_iota(I32, (1, PAGE), 1)
            kin = jnp.concatenate([kin_ref[...], jnp.zeros((PAGE - S_PAD, IDX_DIM), BF16)], axis=0)
            scn = _fold_heads(jnp.maximum(_dot_nt(qp, kin), 0.0) * wc, nq)
            scn_ref[...] = jnp.where(lane_n <= qrow, scn, NEG_INF)

            def count(pred):
                ones = _tree(jnp.add, [jnp.where(pred(sc_ref[t], t * ikw + lane), 1.0, 0.0) for t in range(isteps)])
                cn = jnp.where(pred(scn_ref[...], n_past + lane_n), 1.0, 0.0)
                return (jnp.sum(_lane_fold(jnp.add, ones), axis=-1, keepdims=True)
                        + jnp.sum(cn, axis=-1, keepdims=True))

            amax = jnp.maximum(
                jnp.max(_lane_fold(jnp.maximum, _tree(jnp.maximum, [jnp.abs(sc_ref[t]) for t in range(isteps)])),
                        axis=-1, keepdims=True),
                jnp.max(jnp.abs(scn), axis=-1, keepdims=True))
            n_valid = (n_past + 1 + qrow).astype(F32)
            thr, jsel = _select_params(count, topk, pos_bits, amax, n_valid, t_ref, j_ref)
            for t in range(steps):
                first = t * kw
                sp = sc_ref[first // ikw][:, first % ikw:first % ikw + kw]
                pos = first + lane[:, :kw]
                bias_ref[t] = jnp.where((sp > thr) | ((sp == thr) & (pos <= jsel)), 0.0, NEG_INF)
            sn = scn_ref[...]
            seln = ((sn > thr) | ((sn == thr) & (n_past + lane_n <= jsel))) & (lane_n <= qrow)
            biasn_ref[...] = jnp.where(seln, 0.0, NEG_INF)

    @pl.when((s_idx >= isteps) & (s_idx < isteps + steps))
    def _():
        t = s_idx - isteps

        @pl.when(t == 0)
        def _():
            m_ref[...] = jnp.full(m_ref.shape, NEG_INF, F32)

        bias = dup(bias_ref[t])
        for g in range(A_KV_HEADS):
            s = _dot_nt(qs_ref[g], head_rows(g)) + bias
            s_ref[g, t] = s
            m_ref[g] = jnp.maximum(m_ref[g], _lane_fold(jnp.maximum, s))

        @pl.when(t == steps - 1)
        def _():
            bn = dup(biasn_ref[:, :S_PAD])
            for g in range(A_KV_HEADS):
                sn = _dot_nt(qs_ref[g], kn_ref[:, gsl(g)]) + bn
                sn_ref[g] = sn
                mfin_ref[g] = jnp.maximum(jnp.max(m_ref[g], axis=-1, keepdims=True),
                                          jnp.max(sn, axis=-1, keepdims=True))

    @pl.when(s_idx >= isteps + steps)
    def _():
        t = s_idx - isteps - steps

        @pl.when(t == 0)
        def _():
            l_ref[...] = jnp.zeros(l_ref.shape, F32)
            acc_ref[...] = jnp.zeros(acc_ref.shape, F32)

        for g in range(A_KV_HEADS):
            p = jnp.exp2(s_ref[g, t] - mfin_ref[g])
            l_ref[g] += _lane_fold(jnp.add, p)
            acc_ref[g] += _dot(p.astype(BF16), head_rows(g))

        @pl.when(t == steps - 1)
        def _():
            for g in range(A_KV_HEADS):
                pn = jnp.exp2(sn_ref[g] - mfin_ref[g])
                l = jnp.sum(l_ref[g], axis=-1, keepdims=True) + jnp.sum(pn, axis=-1, keepdims=True)
                o_ref[g] = (acc_ref[g] + _dot(pn.astype(BF16), vn_ref[:, gsl(g)])) / l


def _dsa_sample(page_flat, qp, wcol, ki_new_b, qs, k_new_b, v_new_b, cache_kidx_t, cache_k, cache_v,
                nseq, npages, ipages, pages, nq, topk):
    isteps, steps = npages // ipages, npages // pages
    n_past = npages * PAGE
    ikw, kw = ipages * PAGE, pages * PAGE
    rows = IDX_HEADS * nq
    rows_q = qs.shape[2]
    pos_bits = int(np.ceil(np.log2(n_past + PAGE)))

    seq3 = lambda shape: pl.BlockSpec((None,) + shape, lambda b, s, pt: (b,) + (0,) * len(shape))
    new_rows = lambda w: pl.BlockSpec((S_PAD, w), lambda b, s, pt: (b, 0))
    in_hbm = pl.BlockSpec(memory_space=pl.ANY)
    grid_spec = pltpu.PrefetchScalarGridSpec(
        num_scalar_prefetch=1,
        grid=(nseq, isteps + 2 * steps),
        in_specs=[seq3((rows, IDX_DIM)), seq3((rows, 1)), new_rows(IDX_DIM),
                  seq3((A_KV_HEADS, rows_q, A_HEAD_DIM)), new_rows(A_KV_WIDTH), new_rows(A_KV_WIDTH),
                  in_hbm, in_hbm, in_hbm],
        out_specs=seq3((A_KV_HEADS, rows_q, A_HEAD_DIM)),
        scratch_shapes=[pltpu.VMEM((isteps, 8, ikw), F32), pltpu.VMEM((8, PAGE), F32),
                        pltpu.VMEM((steps, 8, kw), F32), pltpu.VMEM((8, PAGE), F32),
                        pltpu.VMEM((8, 1), F32), pltpu.VMEM((8, 1), I32),
                        pltpu.VMEM((A_KV_HEADS, steps, rows_q, kw), F32),
                        pltpu.VMEM((A_KV_HEADS, rows_q, S_PAD), F32),
                        pltpu.VMEM((A_KV_HEADS, rows_q, 128), F32),
                        pltpu.VMEM((A_KV_HEADS, rows_q, 1), F32),
                        pltpu.VMEM((A_KV_HEADS, rows_q, 128), F32),
                        pltpu.VMEM((A_KV_HEADS, rows_q, A_HEAD_DIM), F32),
                        pltpu.VMEM((DMA_SLOTS, ipages, IDX_DIM, PAGE), F32),
                        pltpu.VMEM((DMA_SLOTS, pages, PAGE * A_KV_HEADS, A_HEAD_DIM), F32),
                        pltpu.SemaphoreType.DMA((DMA_SLOTS,)), pltpu.SemaphoreType.DMA((DMA_SLOTS,))],
    )
    return pl.pallas_call(
        functools.partial(_dsa_sample_kernel, ipages=ipages, pages=pages, npages=npages, nq=nq, topk=topk,
                          pos_bits=pos_bits),
        grid_spec=grid_spec,
        out_shape=jax.ShapeDtypeStruct((nseq, A_KV_HEADS, rows_q, A_HEAD_DIM), F32),
        compiler_params=_cparams(("arbitrary", "arbitrary")),
        name="dsa_sample",
    )(page_flat, qp, wcol, ki_new_b, qs, k_new_b, v_new_b, cache_kidx_t, cache_k, cache_v)


def _merge_kernel(x_ref, oa_ref, ag_ref, bu_ref, bv_ref, bg_ref, cq_ref, cg_ref, ra_ref, rb_ref, rc_ref,
                  mk_ref, mv_ref, ws_ref, bs_ref, gsgu_ref, gmq_ref, wpa_ref, wpb_ref, wpc_ref, wout_ref,
                  y_ref, *maybe_vn_ref, tm, chunk, mem_groups):
    f32 = lambda r: r[...].astype(F32)
    silu = lambda t: t * jax.nn.sigmoid(t)

    vn = _rms(f32(bv_ref), gsgu_ref[...])
    if maybe_vn_ref:
        maybe_vn_ref[0][...] = vn
    vnb = vn.astype(BF16)
    bu = f32(bu_ref)
    tril = (lax.broadcasted_iota(I32, (chunk, chunk), 1) <= lax.broadcasted_iota(I32, (chunk, chunk), 0))
    ob_cols = []
    for g in range(B_GROUPS):
        wg = jnp.where(tril, ws_ref[g], 0.0).astype(BF16)
        gsl = slice(g * B_GROUP_DIM, (g + 1) * B_GROUP_DIM)
        parts = [_dot(wg, vnb[c * chunk:(c + 1) * chunk, gsl]) + bs_ref[:, g:g + 1] for c in range(tm // chunk)]
        ob_cols.append(parts[0] if len(parts) == 1 else jnp.concatenate(parts, axis=0))
    ob = bu * jnp.concatenate(ob_cols, axis=1)
    pb = _dot((ob * silu(f32(bg_ref))).astype(BF16), wpb_ref[...])

    cq = f32(cq_ref)
    rows_g = tm // mem_groups
    oc_cols = []
    for hh in range(M_HEADS):
        hsl = slice(hh * M_HEAD_DIM, (hh + 1) * M_HEAD_DIM)
        qn = (_rms(cq[:, hsl], gmq_ref[...]) * (M_HEAD_DIM ** -0.5)).astype(BF16)
        oc_rows = []
        for u in range(mem_groups):
            msl = slice(u * N_MEM, (u + 1) * N_MEM)
            s = _dot_nt(qn[u * rows_g:(u + 1) * rows_g], mk_ref[msl, hsl])
            p = jnp.exp(s - jnp.max(s, axis=-1, keepdims=True))
            oc_rows.append(_dot(p.astype(BF16), mv_ref[msl, hsl]) / jnp.sum(p, axis=-1, keepdims=True))
        oc_cols.append(oc_rows[0] if mem_groups == 1 else jnp.concatenate(oc_rows, axis=0))
    oc = jnp.concatenate(oc_cols, axis=1)
    pc = _dot((oc * silu(f32(cg_ref))).astype(BF16), wpc_ref[...])

    pa = _dot((f32(oa_ref) * silu(f32(ag_ref))).astype(BF16), wpa_ref[...])
    sig = jax.nn.sigmoid
    m = sig(f32(ra_ref)) * pa + sig(f32(rb_ref)) * pb + sig(f32(rc_ref)) * pc
    y_ref[...] = x_ref[...] + _dot(m.astype(BF16), wout_ref[...])


def _merge(x, oa, zr, mk_b, mv_b, ws, bs_t, g_sgu, g_mq, w_pa, w_pb, w_pc, w_out, tm, chunk, mem_groups, mem_map,
           emit_vn):
    n = x.shape[0]
    col = lambda w, j: pl.BlockSpec((tm, w), lambda i, j=j: (i, j))
    mem = pl.BlockSpec((mem_groups * N_MEM, M_WIDTH), mem_map)
    in_specs = [col(D_MODEL, 0), col(A_WIDTH, 0),
                col(1024, 0), col(1024, 1), col(1024, 2), col(1024, 3), col(1024, 4), col(1024, 5),
                col(2048, 3), col(2048, 4), col(2048, 5),
                mem, mem,
                _const_spec((B_GROUPS, chunk, chunk)), _const_spec((chunk, B_GROUPS)),
                _const_spec((1, B_WIDTH)), _const_spec((1, M_HEAD_DIM)),
                _const_spec((A_WIDTH, D_MODEL)), _const_spec((B_WIDTH, D_MODEL)),
                _const_spec((M_WIDTH, D_MODEL)), _const_spec((D_MODEL, D_MODEL))]
    out_specs = [col(D_MODEL, 0)]
    out_shape = [jax.ShapeDtypeStruct((n, D_MODEL), F32)]
    if emit_vn:
        out_specs.append(col(B_WIDTH, 0))
        out_shape.append(jax.ShapeDtypeStruct((n, B_WIDTH), F32))
    return pl.pallas_call(
        functools.partial(_merge_kernel, tm=tm, chunk=chunk, mem_groups=mem_groups),
        grid=(n // tm,),
        in_specs=in_specs,
        out_specs=out_specs,
        out_shape=out_shape,
        compiler_params=_cparams(("arbitrary",)),
        name="merge",
    )(x, oa, *([zr] * 9), mk_b, mv_b, ws, bs_t, g_sgu, g_mq, w_pa, w_pb, w_pc, w_out)


def _rope_tables(pos):
    freq = lambda half: ROPE_THETA ** (-jnp.arange(half, dtype=F32) / half)
    f_idx = freq(IDX_DIM // 2)
    ang = pos.astype(F32)[:, None] * jnp.concatenate([freq(A_HEAD_DIM // 2), f_idx, f_idx])[None, :]
    return jnp.cos(ang), jnp.sin(ang)


def kernel(x_prompt, x_sample, cache_k, cache_v, cache_kidx, cache_mem_k, cache_mem_v, page_table,
           mem_prompt, g_pre, w_in, g_q, g_k, g_mq, g_mk, g_mem, w_mem_kv, g_sgu, w_s, b_s,
           w_pa, w_pb, w_pc, w_out):
    batch, seq, _ = x_prompt.shape
    nseq, nq, _ = x_sample.shape
    assert nq == 4 and nq <= S_PAD
    npages = page_table.shape[1]
    n_past = npages * PAGE
    n_pool = cache_k.shape[0]
    row2 = lambda a: a.reshape(1, -1)

    w_in_t = w_in.T
    w_a = w_in_t[:A_COLS].astype(BF16)
    w_pa_b, w_pb_b, w_pc_b, w_out_b = (w.astype(BF16) for w in (w_pa, w_pb, w_pc, w_out))
    w_mem_b = w_mem_kv.astype(BF16)
    g_pre2, g_q2, g_k2, g_mq2, g_mk2, g_mem2, g_sgu2 = map(row2, (g_pre, g_q, g_k, g_mq, g_mk, g_mem, g_sgu))

    xp = x_prompt.reshape(batch * seq, D_MODEL)
    tm_a = min(512, seq)
    nblk = seq // tm_a
    tabs_p = _rope_tables(jnp.arange(seq))
    q, k_p, v_p, k_b, v_b, qi, ki_p, ki_b, wi = _proj_a(
        xp, g_pre2, w_a, g_q2, g_k2, tabs_p, tm_a, lambda i: (i % nblk, 0))
    xs = jnp.pad(x_sample, ((0, 0), (0, S_PAD - nq), (0, 0))).reshape(nseq * S_PAD, D_MODEL)
    rows_s = nseq * S_PAD
    zr, zr_s = _proj_rest(xp, xs, g_pre2, w_in_t, min(1024, seq), PROJ_TN)
    mk_p, mv_p, mk_b, mv_b = _mem_kv(mem_prompt.reshape(batch * N_MEM, D_MODEL), g_mem2, w_mem_b, g_mk2)
    oa = _dsa_prompt(qi, wi.T, q, ki_b, k_b, v_b, batch, seq, min(TOPK_MAX, seq // 4))
    tm_m = min(256, seq)
    nblk_m = seq // tm_m
    (y_p,) = _merge(xp, oa, zr, mk_b, mv_b, w_s, b_s.T, g_sgu2, g_mq2, w_pa_b, w_pb_b, w_pc_b, w_out_b,
                    tm_m, CHUNK, 1, lambda i: (i // nblk_m, 0), False)

    tabs_s = tuple(jnp.tile(t, (nseq, 1)) for t in _rope_tables(n_past + jnp.arange(S_PAD)))
    q_s, k_s, v_s, k_sb, v_sb, qi_s, ki_s, ki_sb, wi_s = _proj_a(
        xs, g_pre2, w_a, g_q2, g_k2, tabs_s, rows_s, lambda i: (0, 0))

    qp = (qi_s.reshape(nseq, S_PAD, IDX_HEADS, IDX_DIM)[:, :nq]
          .transpose(0, 2, 1, 3).reshape(nseq, IDX_HEADS * nq, IDX_DIM))
    wcol = wi_s.reshape(nseq, S_PAD, IDX_HEADS)[:, :nq].transpose(0, 2, 1).reshape(nseq, IDX_HEADS * nq, 1)
    rep = A_HEADS // A_KV_HEADS
    qs = (q_s.reshape(nseq, S_PAD, A_KV_HEADS, rep, A_HEAD_DIM)[:, :nq]
          .transpose(0, 2, 3, 1, 4).reshape(nseq, A_KV_HEADS, rep * nq, A_HEAD_DIM))
    qs = jnp.concatenate([qs, qs], axis=2)
    page_flat = page_table.reshape(-1)
    pages, ipages = min(SAMPLE_PAGES, npages), min(SAMPLE_IDX_PAGES, npages)
    o_s = _dsa_sample(page_flat, qp, wcol, ki_sb, qs, k_sb, v_sb, jnp.swapaxes(cache_kidx, 1, 2),
                      cache_k.reshape(n_pool, PAGE * A_KV_HEADS, A_HEAD_DIM),
                      cache_v.reshape(n_pool, PAGE * A_KV_HEADS, A_HEAD_DIM),
                      nseq, npages, ipages, pages, nq, min(TOPK_MAX, (n_past + nq) // 4))
    oa_s = (o_s[:, :, :rep * nq].reshape(nseq, A_KV_HEADS, rep, nq, A_HEAD_DIM)
            .transpose(0, 3, 1, 2, 4).reshape(nseq, nq, A_WIDTH))
    oa_s = jnp.pad(oa_s, ((0, 0), (0, S_PAD - nq), (0, 0))).reshape(rows_s, A_WIDTH).astype(BF16)
    mk_s = cache_mem_k.reshape(nseq * N_MEM, M_WIDTH).astype(BF16)
    mv_s = cache_mem_v.reshape(nseq * N_MEM, M_WIDTH).astype(BF16)
    y_s, vn_s = _merge(xs, oa_s, zr_s, mk_s, mv_s, w_s[:, :S_PAD, :S_PAD], b_s[:, :S_PAD].T, g_sgu2, g_mq2,
                       w_pa_b, w_pb_b, w_pc_b, w_out_b, rows_s, S_PAD, nseq, lambda i: (0, 0), True)

    take = lambda a, shape: a.reshape(nseq, S_PAD, -1)[:, :nq].reshape(shape)
    return (y_p.reshape(batch, seq, D_MODEL),
            take(y_s, (nseq, nq, D_MODEL)),
            k_p.reshape(batch, seq, A_KV_HEADS, A_HEAD_DIM),
            v_p.reshape(batch, seq, A_KV_HEADS, A_HEAD_DIM),
            ki_p.reshape(batch, seq, IDX_DIM),
            mk_p.reshape(batch, N_MEM, M_HEADS, M_HEAD_DIM),
            mv_p.reshape(batch, N_MEM, M_HEADS, M_HEAD_DIM),
            take(k_s, (nseq, nq, A_KV_HEADS, A_HEAD_DIM)),
            take(v_s, (nseq, nq, A_KV_HEADS, A_HEAD_DIM)),
            take(ki_s, (nseq, nq, IDX_DIM)),
            take(vn_s, (nseq, nq, B_GROUPS, B_GROUP_DIM)))
```
